```python
import jax
import jax.numpy as jnp
from jax import lax
import numpy as np

D_MODEL = 1024
BATCH = 8
SEQ = 4096
DEPTH = 2

GRID_W = 64
CTX_LEN = 256
NORM_EPS = 1e-6
ROPE_THETA = 10000.0
Q_BLOCK = 128

GDN_HEADS = 8
GDN_DK = 64
GDN_DV = 64
GDN_CONV = 5
GDN_CHUNK = 64
GDN_QK_W = GDN_HEADS * GDN_DK
GDN_V_W = GDN_HEADS * GDN_DV

MLA_HEADS = 8
MLA_NOPE = 64
MLA_ROPE = 32
MLA_V = 64
MLA_Q_LORA = 256
MLA_KV_LORA = 128

NA_HEADS = 8
NA_DIM = 64
NA_WIN_H = 8
NA_WIN_W = 16
NA_W = NA_HEADS * NA_DIM

GQA_HEADS = 8
GQA_KV_HEADS = 2
GQA_DIM = 64

D_FF = -(-(8 * D_MODEL) // (3 * 256)) * 256

EVEN_SPLITS = (2 * GDN_QK_W + GDN_V_W, GDN_V_W, 2 * GDN_HEADS, 2 * GDN_HEADS, MLA_Q_LORA, MLA_KV_LORA, MLA_ROPE)
ODD_SPLITS = (NA_W, NA_W, NA_W, GQA_HEADS * GQA_DIM, GQA_KV_HEADS * GQA_DIM, GQA_KV_HEADS * GQA_DIM)
IN_EVEN = sum(EVEN_SPLITS)
IN_ODD = sum(ODD_SPLITS)
MIX_EVEN = GDN_V_W + MLA_HEADS * MLA_V
MIX_ODD = NA_W + GQA_HEADS * GQA_DIM
N_EVEN = (DEPTH + 1) // 2
N_ODD = DEPTH // 2

kernel_name = 'hybrid_gdn_mla_natten_gqa_prefix_dit'


def split_cols(z, sizes):
    offs = [int(o) for o in np.cumsum(sizes)[:-1]]
    return jnp.split(z, offs, axis=-1)


def to_heads(t, n):
    return t.reshape(t.shape[0], t.shape[1], n, -1)


def rms_norm(x, g):
    xf = x.astype(jnp.float32)
    y = xf * lax.rsqrt(jnp.mean(xf * xf, -1, keepdims=True) + NORM_EPS)
    return (y * g.astype(jnp.float32)).astype(x.dtype)


def l2_normalize(x):
    xf = x.astype(jnp.float32)
    return (xf * lax.rsqrt(jnp.sum(xf * xf, -1, keepdims=True) + NORM_EPS)).astype(x.dtype)


def modulate(h, shift, scale):
    return h * (1.0 + scale) + shift


def swiglu(u, w_gate, w_up, w_down):
    return (jax.nn.silu(u @ w_gate) * (u @ w_up)) @ w_down


def axial_rope(n_tokens, rot_dim):
    t = jnp.arange(n_tokens, dtype=jnp.int32)
    row = (t // GRID_W).astype(jnp.float32)
    col = (t % GRID_W).astype(jnp.float32)
    n_freq = rot_dim // 4
    inv_freq = ROPE_THETA ** (-jnp.arange(n_freq, dtype=jnp.float32) / n_freq)
    ang = jnp.concatenate([row[:, None] * inv_freq, col[:, None] * inv_freq], -1)
    return jnp.cos(ang), jnp.sin(ang)


def apply_rope(x, cos, sin):
    half = x.shape[-1] // 2
    x1, x2 = x[..., :half], x[..., half:]
    cs, sn = cos[None, :, None, :], sin[None, :, None, :]
    return jnp.concatenate([x1 * cs - x2 * sn, x2 * cs + x1 * sn], -1).astype(x.dtype)


def short_conv(x, w):
    n_tap = w.shape[0]
    pad = n_tap // 2
    t_len = x.shape[1]
    xp = jnp.pad(x, ((0, 0), (pad, pad), (0, 0)))
    out = xp[:, 0:t_len] * w[0]
    for j in range(1, n_tap):
        out = out + xp[:, j:j + t_len] * w[j]
    return out


def blocked_attention(q, k, v, scale):
    b, tq, h, d = q.shape
    hk = k.shape[2]
    grp = h // hk
    nb = tq // Q_BLOCK
    qb = q.reshape(b, nb, Q_BLOCK, hk, grp, d).transpose(1, 0, 2, 3, 4, 5)

    def one_block(q_i):
        s = jnp.einsum('bqkgd,bskd->bkgqs', q_i, k).astype(jnp.float32) * scale
        p = jax.nn.softmax(s, axis=-1).astype(v.dtype)
        return jnp.einsum('bkgqs,bskd->bqkgd', p, v)

    o = lax.map(one_block, qb)
    return o.transpose(1, 0, 2, 3, 4, 5).reshape(b, tq, h, v.shape[-1])


def delta_rule_chunked(q, k, v, g, beta, s0):
    f32 = jnp.float32
    b, t_len, h, dk = q.shape
    dv = v.shape[-1]
    n = t_len // GDN_CHUNK
    cl = GDN_CHUNK

    def chunks(t):
        return jnp.moveaxis(t.astype(f32).reshape(b, n, cl, h, *t.shape[3:]), 3, 1)

    qc, kc, vc, gch, bc = chunks(q), chunks(k), chunks(v), chunks(g), chunks(beta)
    gc = jnp.cumsum(gch, -1)
    tril = jnp.tril(jnp.ones((cl, cl), bool))
    strict = jnp.tril(jnp.ones((cl, cl), bool), -1)
    decay = jnp.exp(jnp.where(tril, gc[..., :, None] - gc[..., None, :], -jnp.inf))
    kb = kc * bc[..., None]
    vb = vc * bc[..., None]
    m = jnp.where(strict, jnp.einsum('bhncd,bhnsd->bhncs', kb, kc) * decay, 0.0)
    eye = jnp.eye(cl, dtype=f32)
    tmat = lax.linalg.triangular_solve(m + eye, jnp.broadcast_to(eye, m.shape), left_side=True,
                                       lower=True, unit_diagonal=True)
    u = jnp.einsum('bhncs,bhnsv->bhncv', tmat, vb)
    w = jnp.einsum('bhncs,bhnsk->bhnck', tmat, kb * jnp.exp(gc)[..., None])
    qk = jnp.einsum('bhncd,bhnsd->bhncs', qc, kc) * decay
    q_dec = qc * jnp.exp(gc)[..., None]
    k_dec = kc * jnp.exp(gc[..., -1:] - gc)[..., None]
    g_last = jnp.exp(gc[..., -1])
    xs = tuple(jnp.moveaxis(t, 2, 0) for t in (u, w, qk, q_dec, k_dec, g_last))

    def step(s, inp):
        u_i, w_i, qk_i, qd_i, kd_i, gl_i = inp
        v_new = u_i - jnp.einsum('bhck,bhkv->bhcv', w_i, s)
        o_i = jnp.einsum('bhck,bhkv->bhcv', qd_i, s) + jnp.einsum('bhcs,bhsv->bhcv', qk_i, v_new)
        s = s * gl_i[..., None, None] + jnp.einsum('bhck,bhcv->bhkv', kd_i, v_new)
        return s, o_i

    s_fin, o = lax.scan(step, s0.astype(f32), xs)
    return s_fin, o.transpose(1, 0, 3, 2, 4).reshape(b, t_len, h, dv)


def gdn_prepare(qkv, a, bt, conv_w, a_log, dt_bias):
    b, t_len, _ = qkv.shape
    qkv = jax.nn.silu(short_conv(qkv, conv_w))
    q, k, v = jnp.split(qkv, [GDN_QK_W, 2 * GDN_QK_W], axis=-1)
    q = l2_normalize(q.reshape(b, t_len, GDN_HEADS, GDN_DK)) * (GDN_DK ** -0.5)
    k = l2_normalize(k.reshape(b, t_len, GDN_HEADS, GDN_DK))
    v = v.reshape(b, t_len, GDN_HEADS, GDN_DV)
    a = a.reshape(b, t_len, 2, GDN_HEADS).astype(jnp.float32)
    bt = bt.reshape(b, t_len, 2, GDN_HEADS).astype(jnp.float32)
    g = -jnp.exp(a_log.astype(jnp.float32)) * jax.nn.softplus(a + dt_bias.astype(jnp.float32))
    return q, k, v, g, jax.nn.sigmoid(bt)


def gdn_bidirectional(lat, ctx):
    q_l, k_l, v_l, g_l, b_l = lat
    q_c, k_c, v_c, g_c, b_c = ctx
    s0 = jnp.zeros((q_l.shape[0], GDN_HEADS, GDN_DK, GDN_DV), jnp.float32)
    s_cf, oc_f = delta_rule_chunked(q_c, k_c, v_c, g_c[:, :, 0], b_c[:, :, 0], s0)
    _, ol_f = delta_rule_chunked(q_l, k_l, v_l, g_l[:, :, 0], b_l[:, :, 0], s_cf)
    fl = lambda t: jnp.flip(t, 1)
    s_cb, oc_b = delta_rule_chunked(fl(q_c), fl(k_c), fl(v_c), fl(g_c[:, :, 1]), fl(b_c[:, :, 1]), s0)
    _, ol_b = delta_rule_chunked(fl(q_l), fl(k_l), fl(v_l), fl(g_l[:, :, 1]), fl(b_l[:, :, 1]), s_cb)
    return ol_f + fl(ol_b), oc_f + fl(oc_b)


def gdn_output(o, gate, out_norm):
    b, t_len, h, dv = o.shape
    o = rms_norm(o.astype(gate.dtype), out_norm)
    return (o * jax.nn.silu(gate.reshape(b, t_len, h, dv))).reshape(b, t_len, h * dv)


def mla_q(q_down, q_norm, w_q_up, rope):
    b, t_len, _ = q_down.shape
    q = (rms_norm(q_down, q_norm) @ w_q_up).reshape(b, t_len, MLA_HEADS, MLA_NOPE + MLA_ROPE)
    if rope is None:
        return q
    return jnp.concatenate([q[..., :MLA_NOPE], apply_rope(q[..., MLA_NOPE:], *rope)], -1)


def mla_kv(kv_down, k_pe, kv_norm, w_kv_up, rope):
    b, t_len, _ = kv_down.shape
    kv = (rms_norm(kv_down, kv_norm) @ w_kv_up).reshape(b, t_len, MLA_HEADS, MLA_NOPE + MLA_V)
    k_pe = k_pe[:, :, None, :]
    if rope is not None:
        k_pe = apply_rope(k_pe, *rope)
    k = jnp.concatenate([kv[..., :MLA_NOPE], jnp.broadcast_to(k_pe, (b, t_len, MLA_HEADS, MLA_ROPE))], -1)
    return k, kv[..., MLA_NOPE:]


def neighbourhood_attention(q, k, v, k_ctx, v_ctx, rpb):
    b, s_len, h, d = q.shape
    rows = s_len // GRID_W
    kh = min(NA_WIN_H, rows)
    kw = NA_WIN_W
    nk = kh * kw
    r = jnp.arange(rows)
    col = jnp.arange(GRID_W)
    rs = jnp.clip(r - kh // 2, 0, rows - kh)
    cs = jnp.clip(col - kw // 2, 0, GRID_W - kw)
    kr = rs[:, None] + jnp.arange(kh)
    kc = cs[:, None] + jnp.arange(kw)
    idx = (kr[:, None, :, None] * GRID_W + kc[None, :, None, :]).reshape(rows, GRID_W * nk)
    dr = kr - r[:, None] + (NA_WIN_H - 1)
    dc = kc - col[:, None] + (NA_WIN_W - 1)
    bias = rpb[:, dr[:, None, :, None], dc[None, :, None, :]]
    bias = bias.reshape(h, rows, GRID_W, nk).transpose(1, 0, 2, 3).astype(jnp.float32)
    qr = q.reshape(b, rows, GRID_W, h, d).transpose(1, 0, 2, 3, 4)
    scale = d ** -0.5

    def row_block(inp):
        q_i, idx_i, b_i = inp
        kg = jnp.take(k, idx_i, axis=1).reshape(b, GRID_W, nk, h, d)
        vg = jnp.take(v, idx_i, axis=1).reshape(b, GRID_W, nk, h, d)
        s_loc = jnp.einsum('bqhd,bqnhd->bhqn', q_i, kg).astype(jnp.float32) * scale + b_i[None]
        s_ctx = jnp.einsum('bqhd,bshd->bhqs', q_i, k_ctx).astype(jnp.float32) * scale
        p = jax.nn.softmax(jnp.concatenate([s_loc, s_ctx], -1), axis=-1).astype(v.dtype)
        return (jnp.einsum('bhqn,bqnhd->bqhd', p[..., :nk], vg)
                + jnp.einsum('bhqs,bshd->bqhd', p[..., nk:], v_ctx))

    o = lax.map(row_block, (qr, idx, bias))
    return o.transpose(1, 0, 2, 3, 4).reshape(b, s_len, h, d)


def even_mixer(u_lat, u_ctx, need_ctx, w_in, w_out, conv_w, a_log, dt_bias, out_norm,
               q_norm, w_q_up, kv_norm, w_kv_up, rope):
    b, s_len, _ = u_lat.shape
    lat = split_cols(u_lat @ w_in, EVEN_SPLITS)
    ctx = split_cols(u_ctx @ w_in, EVEN_SPLITS)
    o_lat, o_ctx = gdn_bidirectional(gdn_prepare(lat[0], lat[2], lat[3], conv_w, a_log, dt_bias),
                                     gdn_prepare(ctx[0], ctx[2], ctx[3], conv_w, a_log, dt_bias))
    a_lat = gdn_output(o_lat, lat[1], out_norm)
    k_c, v_c = mla_kv(ctx[5], ctx[6], kv_norm, w_kv_up, None)
    k_l, v_l = mla_kv(lat[5], lat[6], kv_norm, w_kv_up, rope)
    q_l = mla_q(lat[4], q_norm, w_q_up, rope)
    scale = (MLA_NOPE + MLA_ROPE) ** -0.5
    b_lat = blocked_attention(q_l, jnp.concatenate([k_c, k_l], 1), jnp.concatenate([v_c, v_l], 1), scale)
    y_lat = jnp.concatenate([a_lat, b_lat.reshape(b, s_len, -1)], -1) @ w_out
    if not need_ctx:
        return y_lat, None
    l_len = u_ctx.shape[1]
    a_ctx = gdn_output(o_ctx, ctx[1], out_norm)
    b_ctx = blocked_attention(mla_q(ctx[4], q_norm, w_q_up, None), k_c, v_c, scale)
    y_ctx = jnp.concatenate([a_ctx, b_ctx.reshape(b, l_len, -1)], -1) @ w_out
    return y_lat, y_ctx


def odd_mixer(u_lat, u_ctx, need_ctx, w_in, w_out, rpb, q_norm, k_norm, rope):
    b, s_len, _ = u_lat.shape
    lat = split_cols(u_lat @ w_in, ODD_SPLITS)
    ctx = split_cols(u_ctx @ w_in, ODD_SPLITS)
    kn_c, vn_c = to_heads(ctx[1], NA_HEADS), to_heads(ctx[2], NA_HEADS)
    c_lat = neighbourhood_attention(to_heads(lat[0], NA_HEADS), to_heads(lat[1], NA_HEADS),
                                    to_heads(lat[2], NA_HEADS), kn_c, vn_c, rpb)
    kd_c = rms_norm(to_heads(ctx[4], GQA_KV_HEADS), k_norm)
    vd_c = to_heads(ctx[5], GQA_KV_HEADS)
    qd_l = apply_rope(rms_norm(to_heads(lat[3], GQA_HEADS), q_norm), *rope)
    kd_l = apply_rope(rms_norm(to_heads(lat[4], GQA_KV_HEADS), k_norm), *rope)
    scale = GQA_DIM ** -0.5
    d_lat = blocked_attention(qd_l, jnp.concatenate([kd_c, kd_l], 1),
                              jnp.concatenate([vd_c, to_heads(lat[5], GQA_KV_HEADS)], 1), scale)
    y_lat = jnp.concatenate([c_lat.reshape(b, s_len, -1), d_lat.reshape(b, s_len, -1)], -1) @ w_out
    if not need_ctx:
        return y_lat, None
    l_len = u_ctx.shape[1]
    c_ctx_o = blocked_attention(to_heads(ctx[0], NA_HEADS), kn_c, vn_c, NA_DIM ** -0.5)
    d_ctx = blocked_attention(rms_norm(to_heads(ctx[3], GQA_HEADS), q_norm), kd_c, vd_c, scale)
    y_ctx = jnp.concatenate([c_ctx_o.reshape(b, l_len, -1), d_ctx.reshape(b, l_len, -1)], -1) @ w_out
    return y_lat, y_ctx


def setup_inputs(seed: int = 0) -> dict:
    key = jax.random.key(seed)
    ks = iter(jax.random.split(key, 40))
    f32 = jnp.float32

    def nrm(shape, scale):
        return jax.random.normal(next(ks), shape, f32) * scale

    def gain(shape):
        return 1.0 + 0.05 * jax.random.normal(next(ks), shape, f32)

    d = D_MODEL
    x = nrm((BATCH, SEQ, d), 1.0)
    c = nrm((BATCH, d), 1.0)
    ctx = nrm((BATCH, CTX_LEN, d), 1.0)
    c_ctx = nrm((d,), 1.0)
    w_mod = nrm((DEPTH, d, 6 * d), 0.5 * d ** -0.5)
    b_mod = nrm((DEPTH, 6 * d), 0.01)
    g_pre_mix = gain((DEPTH, d))
    g_post_mix = gain((DEPTH, d))
    g_pre_ffn = gain((DEPTH, d))
    g_post_ffn = gain((DEPTH, d))
    w_ffn_gate = nrm((DEPTH, d, D_FF), d ** -0.5)
    w_ffn_up = nrm((DEPTH, d, D_FF), d ** -0.5)
    w_ffn_down = nrm((DEPTH, D_FF, d), D_FF ** -0.5)
    w_in_even = nrm((N_EVEN, d, IN_EVEN), d ** -0.5)
    w_out_even = nrm((N_EVEN, MIX_EVEN, d), MIX_EVEN ** -0.5)
    gdn_conv = nrm((N_EVEN, GDN_CONV, 2 * GDN_QK_W + GDN_V_W), GDN_CONV ** -0.5)
    gdn_a_log = jnp.log(jax.random.uniform(next(ks), (N_EVEN, 2, GDN_HEADS), f32, 1.0, 16.0))
    dt = jnp.exp(jax.random.uniform(next(ks), (N_EVEN, 2, GDN_HEADS), f32,
                                    float(np.log(1e-3)), float(np.log(1e-1))))
    gdn_dt_bias = dt + jnp.log(-jnp.expm1(-dt))
    gdn_out_norm = gain((N_EVEN, GDN_DV))
    mla_q_norm = gain((N_EVEN, MLA_Q_LORA))
    mla_w_q_up = nrm((N_EVEN, MLA_Q_LORA, MLA_HEADS * (MLA_NOPE + MLA_ROPE)), MLA_Q_LORA ** -0.5)
    mla_kv_norm = gain((N_EVEN, MLA_KV_LORA))
    mla_w_kv_up = nrm((N_EVEN, MLA_KV_LORA, MLA_HEADS * (MLA_NOPE + MLA_V)), MLA_KV_LORA ** -0.5)
    w_in_odd = nrm((N_ODD, d, IN_ODD), d ** -0.5)
    w_out_odd = nrm((N_ODD, MIX_ODD, d), MIX_ODD ** -0.5)
    na_rpb = nrm((N_ODD, NA_HEADS, 2 * NA_WIN_H - 1, 2 * NA_WIN_W - 1), 0.1)
    gqa_q_norm = gain((N_ODD, GQA_DIM))
    gqa_k_norm = gain((N_ODD, GQA_DIM))
    return {'x': x, 'c': c, 'ctx': ctx, 'c_ctx': c_ctx, 'w_mod': w_mod, 'b_mod': b_mod,
            'g_pre_mix': g_pre_mix, 'g_post_mix': g_post_mix, 'g_pre_ffn': g_pre_ffn,
            'g_post_ffn': g_post_ffn, 'w_ffn_gate': w_ffn_gate, 'w_ffn_up': w_ffn_up,
            'w_ffn_down': w_ffn_down, 'w_in_even': w_in_even, 'w_out_even': w_out_even,
            'gdn_conv': gdn_conv, 'gdn_a_log': gdn_a_log, 'gdn_dt_bias': gdn_dt_bias,
            'gdn_out_norm': gdn_out_norm, 'mla_q_norm': mla_q_norm, 'mla_w_q_up': mla_w_q_up,
            'mla_kv_norm': mla_kv_norm, 'mla_w_kv_up': mla_w_kv_up, 'w_in_odd': w_in_odd,
            'w_out_odd': w_out_odd, 'na_rpb': na_rpb, 'gqa_q_norm': gqa_q_norm, 'gqa_k_norm': gqa_k_norm}


def reference(x, c, ctx, c_ctx, w_mod, b_mod, g_pre_mix, g_post_mix, g_pre_ffn, g_post_ffn,
              w_ffn_gate, w_ffn_up, w_ffn_down, w_in_even, w_out_even, gdn_conv, gdn_a_log,
              gdn_dt_bias, gdn_out_norm, mla_q_norm, mla_w_q_up, mla_kv_norm, mla_w_kv_up,
              w_in_odd, w_out_odd, na_rpb, gqa_q_norm, gqa_k_norm):
    s_len = x.shape[1]
    rope_mla = axial_rope(s_len, MLA_ROPE)
    rope_gqa = axial_rope(s_len, GQA_DIM)
    h, hc = x, ctx
    for i in range(DEPTH):
        last = i == DEPTH - 1
        mod = (jax.nn.silu(c) @ w_mod[i] + b_mod[i])[:, None, :]
        mod_c = (jax.nn.silu(c_ctx) @ w_mod[i] + b_mod[i])[None, None, :]
        sh_m, sc_m, gt_m, sh_f, sc_f, gt_f = jnp.split(mod, 6, axis=-1)
        csh_m, csc_m, cgt_m, csh_f, csc_f, cgt_f = jnp.split(mod_c, 6, axis=-1)
        u = modulate(rms_norm(h, g_pre_mix[i]), sh_m, sc_m)
        uc = modulate(rms_norm(hc, g_pre_mix[i]), csh_m, csc_m)
        j = i // 2
        if i % 2 == 0:
            y, yc = even_mixer(u, uc, not last, w_in_even[j], w_out_even[j], gdn_conv[j], gdn_a_log[j],
                               gdn_dt_bias[j], gdn_out_norm[j], mla_q_norm[j], mla_w_q_up[j],
                               mla_kv_norm[j], mla_w_kv_up[j], rope_mla)
        else:
            y, yc = odd_mixer(u, uc, not last, w_in_odd[j], w_out_odd[j], na_rpb[j],
                              gqa_q_norm[j], gqa_k_norm[j], rope_gqa)
        h = h + gt_m * rms_norm(y, g_post_mix[i])
        f = swiglu(modulate(rms_norm(h, g_pre_ffn[i]), sh_f, sc_f), w_ffn_gate[i], w_ffn_up[i], w_ffn_down[i])
        h = h + gt_f * rms_norm(f, g_post_ffn[i])
        if not last:
            hc = hc + cgt_m * rms_norm(yc, g_post_mix[i])
            fc = swiglu(modulate(rms_norm(hc, g_pre_ffn[i]), csh_f, csc_f), w_ffn_gate[i], w_ffn_up[i], w_ffn_down[i])
            hc = hc + cgt_f * rms_norm(fc, g_post_ffn[i])
    return h
```

```python
import functools

import numpy as np
import jax
import jax.numpy as jnp
from jax import lax
from jax.experimental import pallas as pl
from jax.experimental.pallas import tpu as pltpu

F32 = jnp.float32
BF16 = jnp.bfloat16

GRID_W = 64
NORM_EPS = 1e-6
ROPE_THETA = 10000.0

GDN_HEADS = 8
GDN_DK = 64
GDN_DV = 64
GDN_CHUNK = 64
GDN_QK_W = GDN_HEADS * GDN_DK
GDN_V_W = GDN_HEADS * GDN_DV

MLA_HEADS = 8
MLA_NOPE = 64
MLA_ROPE = 32
MLA_V = 64
MLA_Q_LORA = 256
MLA_KV_LORA = 128

NA_HEADS = 8
NA_DIM = 64
NA_WIN_H = 8
NA_WIN_W = 16
NA_W = NA_HEADS * NA_DIM

GQA_HEADS = 8
GQA_KV_HEADS = 2
GQA_DIM = 64

LANES = 128
HEAD_W = 64
MASK_VALUE = -1e30
VMEM_LIMIT = 56 << 20

NT_DIMS = (((1,), (1,)), ((), ()))


def _params(n_grid, vmem=VMEM_LIMIT):
    return pltpu.CompilerParams(dimension_semantics=("arbitrary",) * n_grid, vmem_limit_bytes=vmem)


def _rms(x):
    return x * lax.rsqrt(jnp.mean(x * x, axis=-1, keepdims=True) + NORM_EPS)


def _mod_kernel(c_ref, w_ref, b_ref, o_ref):
    c = c_ref[...]
    a = (c * jax.nn.sigmoid(c)).astype(BF16)
    o_ref[...] = jnp.dot(a, w_ref[...], preferred_element_type=F32) + b_ref[...]


def _mod_call(c_rows, w, b):
    r, d = c_rows.shape
    n = w.shape[1]
    tn = 1024
    return pl.pallas_call(
        _mod_kernel,
        grid=(n // tn,),
        in_specs=[pl.BlockSpec((r, d), lambda j: (0, 0)),
                  pl.BlockSpec((d, tn), lambda j: (0, j)),
                  pl.BlockSpec((1, tn), lambda j: (0, j))],
        out_specs=pl.BlockSpec((r, tn), lambda j: (0, j)),
        out_shape=jax.ShapeDtypeStruct((r, n), F32),
        compiler_params=_params(1),
        name="mod",
    )(c_rows, w, b.reshape(1, n))


def _nmm_kernel(x_ref, g_ref, sh_ref, sc_ref, w_ref, *o_refs, offs):
    x = x_ref[0].astype(F32)
    u = (_rms(x) * g_ref[...] * (1.0 + sc_ref[0]) + sh_ref[0]).astype(BF16)
    for o_ref, (o, n) in zip(o_refs, offs):
        o_ref[0] = jnp.dot(u, w_ref[:, o:o + n], preferred_element_type=F32).astype(o_ref.dtype)


def _nmm_call(x, g, shift, scale, w, splits, dtypes, tm, name):
    b, t, din = x.shape
    n = w.shape[1]
    assert sum(splits) == n and t % tm == 0
    offs, o = [], 0
    for s in splits:
        assert s % LANES == 0
        offs.append((o, s))
        o += s
    if shift is None:
        shift = jnp.zeros((b, din), F32)
        scale = jnp.zeros((b, din), F32)
    return pl.pallas_call(
        functools.partial(_nmm_kernel, offs=tuple(offs)),
        grid=(b, t // tm),
        in_specs=[pl.BlockSpec((1, tm, din), lambda i, j: (i, j, 0)),
                  pl.BlockSpec((1, din), lambda i, j: (0, 0)),
                  pl.BlockSpec((1, 1, din), lambda i, j: (i, 0, 0)),
                  pl.BlockSpec((1, 1, din), lambda i, j: (i, 0, 0)),
                  pl.BlockSpec((din, n), lambda i, j: (0, 0))],
        out_specs=[pl.BlockSpec((1, tm, s), lambda i, j: (i, j, 0)) for s in splits],
        out_shape=[jax.ShapeDtypeStruct((b, t, s), dt) for s, dt in zip(splits, dtypes)],
        compiler_params=_params(2),
        name=name,
    )(x, g.reshape(1, din).astype(F32), shift.reshape(b, 1, din), scale.reshape(b, 1, din), w)


def _attn_kernel(q_ref, k_ref, v_ref, o_ref, *, dc, chunks):
    tq = q_ref.shape[1]
    lane = lax.broadcasted_iota(jnp.int32, (tq, LANES), 1)
    outs = []
    for hh in range(2):
        q = q_ref[0, :, hh * dc:(hh + 1) * dc]
        m = l = acc = None
        for s0, n in chunks:
            kk = k_ref[0, s0:s0 + n, :]
            vv = v_ref[0, s0:s0 + n, :]
            s = lax.dot_general(q, kk, NT_DIMS, preferred_element_type=F32)
            mc = jnp.max(s, axis=-1, keepdims=True)
            if m is None:
                m_new = mc
                p = jnp.exp(s - m_new)
                l = jnp.sum(p, axis=-1, keepdims=True)
                acc = jnp.dot(p.astype(BF16), vv, preferred_element_type=F32)
            else:
                m_new = jnp.maximum(m, mc)
                alpha = jnp.exp(m - m_new)
                p = jnp.exp(s - m_new)
                l = alpha * l + jnp.sum(p, axis=-1, keepdims=True)
                acc = alpha * acc + jnp.dot(p.astype(BF16), vv, preferred_element_type=F32)
            m = m_new
        outs.append(acc / l)
    o_ref[0] = jnp.where(lane < HEAD_W, outs[0], outs[1]).astype(o_ref.dtype)


def _attn_call(q, k, v, dc, n_pairs, n_groups, chunks, tq, name):
    b, t_q, _ = q.shape
    t_k = k.shape[1]
    per = n_pairs // n_groups
    return pl.pallas_call(
        functools.partial(_attn_kernel, dc=dc, chunks=tuple(chunks)),
        grid=(b, n_pairs, t_q // tq),
        in_specs=[pl.BlockSpec((1, tq, 2 * dc), lambda i, p, j: (i, j, p)),
                  pl.BlockSpec((1, t_k, dc), lambda i, p, j: (i, 0, p // per)),
                  pl.BlockSpec((1, t_k, LANES), lambda i, p, j: (i, 0, p // per))],
        out_specs=pl.BlockSpec((1, tq, LANES), lambda i, p, j: (i, j, p)),
        out_shape=jax.ShapeDtypeStruct((b, t_q, n_pairs * LANES), BF16),
        compiler_params=_params(3),
        name=name,
    )(q, k, v)


NA_QROWS = 4
NA_KROWS = 12


def _na_kernel(q_ref, k_ref, v_ref, kc_ref, vc_ref, bias_ref, o_ref):
    rb = pl.program_id(1)
    n_rows = k_ref.shape[1] // GRID_W
    base = jnp.clip(rb * NA_QROWS - NA_WIN_H // 2, 0, n_rows - NA_KROWS) * GRID_W
    base = pl.multiple_of(base, GRID_W)
    nq = NA_QROWS * GRID_W
    nk = NA_KROWS * GRID_W
    lane = lax.broadcasted_iota(jnp.int32, (nq, LANES), 1)
    for p in range(NA_HEADS // 2):
        cols = slice(p * LANES, (p + 1) * LANES)
        qp = q_ref[0, :, cols]
        kw = k_ref[0, pl.ds(base, nk), cols]
        vw = v_ref[0, pl.ds(base, nk), cols]
        kc = kc_ref[0, :, cols]
        vc = vc_ref[0, :, cols]
        outs = []
        for hh in range(2):
            sel = (lane < HEAD_W) if hh == 0 else (lane >= HEAD_W)
            qm = jnp.where(sel, qp, jnp.zeros_like(qp))
            s_loc = lax.dot_general(qm, kw, NT_DIMS, preferred_element_type=F32) + bias_ref[0, 2 * p + hh]
            s_ctx = lax.dot_general(qm, kc, NT_DIMS, preferred_element_type=F32)
            m = jnp.maximum(jnp.max(s_loc, axis=-1, keepdims=True), jnp.max(s_ctx, axis=-1, keepdims=True))
            p_loc = jnp.exp(s_loc - m)
            p_ctx = jnp.exp(s_ctx - m)
            l = jnp.sum(p_loc, axis=-1, keepdims=True) + jnp.sum(p_ctx, axis=-1, keepdims=True)
            acc = (jnp.dot(p_loc.astype(BF16), vw, preferred_element_type=F32)
                   + jnp.dot(p_ctx.astype(BF16), vc, preferred_element_type=F32))
            outs.append(acc / l)
        o_ref[0, :, cols] = jnp.where(lane < HEAD_W, outs[0], outs[1]).astype(o_ref.dtype)


def _na_bias_table(rpb, rows):
    n_blocks = rows // NA_QROWS
    variants = []
    for r0 in (0, NA_QROWS * (n_blocks // 2), rows - NA_QROWS):
        base = int(np.clip(r0 - NA_WIN_H // 2, 0, rows - NA_KROWS))
        r = r0 + np.arange(NA_QROWS)[:, None, None, None]
        c = np.arange(GRID_W)[None, :, None, None]
        kr = base + np.arange(NA_KROWS)[None, None, :, None]
        kc = np.arange(GRID_W)[None, None, None, :]
        rs = np.clip(r - NA_WIN_H // 2, 0, rows - NA_WIN_H)
        cs = np.clip(c - NA_WIN_W // 2, 0, GRID_W - NA_WIN_W)
        valid = (kr >= rs) & (kr < rs + NA_WIN_H) & (kc >= cs) & (kc < cs + NA_WIN_W)
        dr = np.clip(kr - r + (NA_WIN_H - 1), 0, 2 * NA_WIN_H - 2)
        dc = np.clip(kc - c + (NA_WIN_W - 1), 0, 2 * NA_WIN_W - 2)
        shape = (NA_QROWS, GRID_W, NA_KROWS, GRID_W)
        dr = np.broadcast_to(dr, shape).reshape(NA_QROWS * GRID_W, NA_KROWS * GRID_W)
        dc = np.broadcast_to(dc, shape).reshape(NA_QROWS * GRID_W, NA_KROWS * GRID_W)
        valid = np.broadcast_to(valid, shape).reshape(NA_QROWS * GRID_W, NA_KROWS * GRID_W)
        bias = rpb[:, dr, dc].astype(F32)
        variants.append(jnp.where(valid[None], bias, MASK_VALUE))
    return jnp.stack(variants, 0)


def _na_call(q, k, v, kc, vc, bias):
    b, t, w = q.shape
    n_blocks = t // (NA_QROWS * GRID_W)
    nq = NA_QROWS * GRID_W
    tc = kc.shape[1]

    def bias_map(i, r):
        return (jnp.where(r == 0, 0, jnp.where(r == n_blocks - 1, 2, 1)), 0, 0, 0)

    return pl.pallas_call(
        _na_kernel,
        grid=(b, n_blocks),
        in_specs=[pl.BlockSpec((1, nq, w), lambda i, r: (i, r, 0)),
                  pl.BlockSpec((1, t, w), lambda i, r: (i, 0, 0)),
                  pl.BlockSpec((1, t, w), lambda i, r: (i, 0, 0)),
                  pl.BlockSpec((1, tc, w), lambda i, r: (i, 0, 0)),
                  pl.BlockSpec((1, tc, w), lambda i, r: (i, 0, 0)),
                  pl.BlockSpec((1, NA_HEADS, nq, NA_KROWS * GRID_W), bias_map)],
        out_specs=pl.BlockSpec((1, nq, w), lambda i, r: (i, r, 0)),
        out_shape=jax.ShapeDtypeStruct((b, t, w), BF16),
        compiler_params=_params(2),
        name="na_attn",
    )(q, k, v, kc, vc, bias)


GDN_STEP_CHUNKS = 4


def _mm(a, b):
    return jnp.dot(a.astype(BF16), b.astype(BF16), preferred_element_type=F32)


def _mm_hi(a, b):
    return jnp.dot(a, b, preferred_element_type=F32, precision=lax.Precision.HIGHEST)


def _gdn_kernel(q_ref, k_ref, kt_ref, v_ref, gcc_ref, gcr_ref, b_ref, o_ref, s_ref):
    c_len = GDN_CHUNK

    @pl.when(pl.program_id(2) == 0)
    def _():
        s_ref[...] = jnp.zeros_like(s_ref)

    row = lax.broadcasted_iota(jnp.int32, (c_len, c_len), 0)
    col = lax.broadcasted_iota(jnp.int32, (c_len, c_len), 1)
    tril = row >= col
    strict = row > col
    eye = (row == col).astype(F32)
    s = s_ref[...]
    for c in range(GDN_STEP_CHUNKS):
        rows = slice(c * c_len, (c + 1) * c_len)
        q = q_ref[0, 0, rows, :]
        k = k_ref[0, 0, rows, :]
        v = v_ref[0, 0, rows, :]
        kt = kt_ref[0, 0, 0, c]
        gcc = gcc_ref[0, 0, rows, :]
        gcr = gcr_ref[0, 0, 0, c:c + 1, :]
        beta = b_ref[0, 0, rows, :]
        diff = gcc - gcr
        decay = jnp.where(tril, jnp.exp(jnp.where(tril, diff, 0.0)), 0.0)
        kb = k * beta
        vb = v * beta
        m = jnp.where(strict, _mm(kb, kt) * decay, 0.0)
        x = eye - m
        pw = _mm_hi(m, m)
        for it in range(5):
            x = x + _mm_hi(x, pw)
            if it < 4:
                pw = _mm_hi(pw, pw)
        egc = jnp.exp(gcc)
        u = _mm(x, vb)
        w = _mm(x, kb * egc)
        qk = _mm(q, kt) * decay
        qd = q * egc
        g_last = gcr[:, c_len - 1:c_len]
        kdt = kt * jnp.exp(g_last - gcr)
        v_new = u - _mm(w, s)
        o_ref[0, 0, rows, :] = _mm(qd, s) + _mm(qk, v_new)
        s = s * jnp.exp(g_last) + _mm(kdt, v_new)
    s_ref[...] = s


def _gdn_call(q, k, kt, v, gcc, gcr, beta):
    nb, h, t, dk = q.shape
    dv = v.shape[-1]
    tb = GDN_STEP_CHUNKS * GDN_CHUNK
    tok = lambda d: pl.BlockSpec((1, 1, tb, d), lambda i, j, s: (i, j, s, 0))
    return pl.pallas_call(
        _gdn_kernel,
        grid=(nb, h, t // tb),
        in_specs=[tok(dk), tok(dk),
                  pl.BlockSpec((1, 1, 1, GDN_STEP_CHUNKS, dk, GDN_CHUNK), lambda i, j, s: (i, j, s, 0, 0, 0)),
                  tok(dv), tok(1),
                  pl.BlockSpec((1, 1, 1, GDN_STEP_CHUNKS, GDN_CHUNK), lambda i, j, s: (i, j, s, 0, 0)),
                  tok(1)],
        out_specs=tok(dv),
        out_shape=jax.ShapeDtypeStruct((nb, h, t, dv), F32),
        scratch_shapes=[pltpu.VMEM((dk, dv), F32)],
        compiler_params=_params(3),
        name="gdn_scan",
    )(q, k, kt, v, gcc, gcr, beta)


def _mix_ffn_kernel(h_ref, mix_ref, gtm_ref, shf_ref, scf_ref, gtf_ref, gpm_ref, gpf_ref, gqf_ref,
                    wo_ref, wg_ref, wu_ref, wd_ref, o_ref, *, ff_chunks):
    y = jnp.dot(mix_ref[0], wo_ref[...], preferred_element_type=F32)
    h1 = h_ref[0] + gtm_ref[0] * (_rms(y) * gpm_ref[...])
    u = (_rms(h1) * gpf_ref[...] * (1.0 + scf_ref[0]) + shf_ref[0]).astype(BF16)
    f = None
    for o, n in ff_chunks:
        gg = jnp.dot(u, wg_ref[:, o:o + n], preferred_element_type=F32)
        uu = jnp.dot(u, wu_ref[:, o:o + n], preferred_element_type=F32)
        a = (gg * jax.nn.sigmoid(gg) * uu).astype(BF16)
        part = jnp.dot(a, wd_ref[o:o + n, :], preferred_element_type=F32)
        f = part if f is None else f + part
    o_ref[0] = h1 + gtf_ref[0] * (_rms(f) * gqf_ref[...])


def _mix_ffn_call(h, mix, gt_m, sh_f, sc_f, gt_f, g_post_mix, g_pre_ffn, g_post_ffn, wo, wg, wu, wd, tm, name):
    b, t, d = h.shape
    dm = mix.shape[-1]
    ff = wg.shape[1]
    half = ff // 2
    assert half % LANES == 0
    ff_chunks = ((0, half), (half, ff - half))
    tok = lambda n: pl.BlockSpec((1, tm, n), lambda i, j: (i, j, 0))
    per_b = pl.BlockSpec((1, 1, d), lambda i, j: (i, 0, 0))
    vec = pl.BlockSpec((1, d), lambda i, j: (0, 0))
    const = lambda shape: pl.BlockSpec(shape, lambda i, j: (0, 0), pipeline_mode=pl.Buffered(1))
    r3 = lambda a: a.reshape(b, 1, d)
    r2 = lambda a: a.reshape(1, d).astype(F32)
    return pl.pallas_call(
        functools.partial(_mix_ffn_kernel, ff_chunks=ff_chunks),
        grid=(b, t // tm),
        in_specs=[tok(d), tok(dm), per_b, per_b, per_b, per_b, vec, vec, vec,
                  const((dm, d)), const((d, ff)), const((d, ff)), const((ff, d))],
        out_specs=tok(d),
        out_shape=jax.ShapeDtypeStruct((b, t, d), F32),
        compiler_params=_params(2),
        name=name,
    )(h, mix, r3(gt_m), r3(sh_f), r3(sc_f), r3(gt_f), r2(g_post_mix), r2(g_pre_ffn), r2(g_post_ffn),
      wo, wg, wu, wd)


def _axial_rope(n_tokens, rot_dim):
    t = jnp.arange(n_tokens, dtype=jnp.int32)
    row = (t // GRID_W).astype(F32)
    col = (t % GRID_W).astype(F32)
    n_freq = rot_dim // 4
    inv_freq = ROPE_THETA ** (-jnp.arange(n_freq, dtype=F32) / n_freq)
    ang = jnp.concatenate([row[:, None] * inv_freq, col[:, None] * inv_freq], -1)
    return jnp.cos(ang), jnp.sin(ang)


def _apply_rope(x, cos, sin):
    half = x.shape[-1] // 2
    x1, x2 = x[..., :half], x[..., half:]
    cs, sn = cos[None, :, None, :], sin[None, :, None, :]
    return jnp.concatenate([x1 * cs - x2 * sn, x2 * cs + x1 * sn], -1)


def _short_conv(x, w):
    n_tap = w.shape[0]
    pad = n_tap // 2
    t_len = x.shape[1]
    xp = jnp.pad(x, ((0, 0), (pad, pad), (0, 0)))
    out = xp[:, 0:t_len] * w[0]
    for j in range(1, n_tap):
        out = out + xp[:, j:j + t_len] * w[j]
    return out


def _l2n(x):
    return x * lax.rsqrt(jnp.sum(x * x, -1, keepdims=True) + NORM_EPS)


def _gdn_prepare(qkv, a, bt, conv_w, a_log, dt_bias):
    b, t_len, _ = qkv.shape
    qkv = jax.nn.silu(_short_conv(qkv.astype(F32), conv_w))
    q, k, v = jnp.split(qkv, [GDN_QK_W, 2 * GDN_QK_W], axis=-1)
    q = _l2n(q.reshape(b, t_len, GDN_HEADS, GDN_DK)) * (GDN_DK ** -0.5)
    k = _l2n(k.reshape(b, t_len, GDN_HEADS, GDN_DK))
    v = v.reshape(b, t_len, GDN_HEADS, GDN_DV)
    a = a.reshape(b, t_len, 2, GDN_HEADS)
    bt = bt.reshape(b, t_len, 2, GDN_HEADS)
    g = -jnp.exp(a_log) * jax.nn.softplus(a + dt_bias)
    return q, k, v, g, jax.nn.sigmoid(bt)


def _gdn_bidirectional(lat, ctx):
    q_l, k_l, v_l, g_l, b_l = lat
    q_c, k_c, v_c, g_c, b_c = ctx
    bsz, s_len = q_l.shape[:2]
    l_len = q_c.shape[1]

    def seq(tc, tl, d):
        if d == 1:
            tc, tl = jnp.flip(tc, 1), jnp.flip(tl, 1)
        return jnp.concatenate([tc, tl], 1)

    def both(tc, tl):
        return jnp.concatenate([seq(tc, tl, 0), seq(tc, tl, 1)], 0)

    def both_dir(tc, tl):
        return jnp.concatenate([seq(tc[:, :, 0], tl[:, :, 0], 0), seq(tc[:, :, 1], tl[:, :, 1], 1)], 0)

    hm = lambda x: jnp.transpose(x, (0, 2, 1, 3))
    q, k, v = hm(both(q_c, q_l)), hm(both(k_c, k_l)), hm(both(v_c, v_l))
    g = jnp.transpose(both_dir(g_c, g_l), (0, 2, 1))
    beta = jnp.transpose(both_dir(b_c, b_l), (0, 2, 1))
    nb, h, t_all, dk = q.shape
    n_chunk = t_all // GDN_CHUNK
    n_step = n_chunk // GDN_STEP_CHUNKS
    gc = jnp.cumsum(g.reshape(nb, h, n_chunk, GDN_CHUNK), -1)
    gcc = gc.reshape(nb, h, t_all, 1)
    gcr = gc.reshape(nb, h, n_step, GDN_STEP_CHUNKS, GDN_CHUNK)
    kt = jnp.swapaxes(k.reshape(nb, h, n_step, GDN_STEP_CHUNKS, GDN_CHUNK, dk), -1, -2)
    o = _gdn_call(q, k, kt, v, gcc, gcr, beta.reshape(nb, h, t_all, 1))
    o = jnp.transpose(o, (0, 2, 1, 3))
    o_f, o_b = o[:bsz], o[bsz:]
    o_ctx = o_f[:, :l_len] + jnp.flip(o_b[:, :l_len], 1)
    o_lat = o_f[:, l_len:] + jnp.flip(o_b[:, l_len:], 1)
    return o_lat, o_ctx


def _gdn_output(o, gate, out_norm):
    b, t_len, h, dv = o.shape
    o = _rms(o) * out_norm
    return (o * jax.nn.silu(gate.astype(F32).reshape(b, t_len, h, dv))).reshape(b, t_len, h * dv)


def _pair_slots(x):
    b, t, h, d = x.shape
    z = jnp.zeros_like(x)
    even = jnp.concatenate([x, z], -1)
    odd = jnp.concatenate([z, x], -1)
    is_odd = (jnp.arange(h) % 2 == 1)[None, None, :, None]
    return jnp.where(is_odd, odd, even)


def _attn_chunks(l_len, s_len, size=512):
    chunks = [(0, l_len)] if l_len else []
    chunks += [(l_len + o, size) for o in range(0, s_len, size)]
    return chunks


def _even_mixer(h, hc, mods, mods_c, g_pre, w_in, conv_w, a_log, dt_bias, out_norm, q_norm, w_q_up,
                kv_norm, w_kv_up, rope):
    b, s_len, d = h.shape
    l_len = hc.shape[1]
    sh_m, sc_m = mods
    csh_m, csc_m = mods_c
    qkv_w = 2 * GDN_QK_W + GDN_V_W
    o0 = qkv_w + GDN_V_W
    w_a, w_bt = w_in[:, o0:o0 + 16], w_in[:, o0 + 16:o0 + 32]
    w_qd = w_in[:, o0 + 32:o0 + 32 + MLA_Q_LORA]
    w_kvd = w_in[:, o0 + 32 + MLA_Q_LORA:o0 + 32 + MLA_Q_LORA + MLA_KV_LORA]
    w_pe = w_in[:, o0 + 32 + MLA_Q_LORA + MLA_KV_LORA:]
    misc = jnp.concatenate([w_a, w_bt, w_pe, jnp.zeros((d, LANES - 64), F32)], 1)
    w_cat = jnp.concatenate([w_in[:, :o0], w_qd, w_kvd, misc], 1).astype(BF16)
    splits = (qkv_w, GDN_V_W, MLA_Q_LORA, MLA_KV_LORA, LANES)
    dts = (BF16, BF16, BF16, BF16, F32)
    lat = _nmm_call(h, g_pre, sh_m, sc_m, w_cat, splits, dts, 512, "in_even_lat")
    ctx = _nmm_call(hc, g_pre, csh_m, csc_m, w_cat, splits, dts, 256, "in_even_ctx")

    def prep(z):
        return _gdn_prepare(z[0], z[4][..., :16], z[4][..., 16:32], conv_w, a_log, dt_bias)

    o_lat, o_ctx = _gdn_bidirectional(prep(lat), prep(ctx))
    a_lat = _gdn_output(o_lat, lat[1], out_norm)
    a_ctx = _gdn_output(o_ctx, ctx[1], out_norm)

    scale = (MLA_NOPE + MLA_ROPE) ** -0.5
    cos, sin = rope
    wq = w_q_up.astype(BF16)
    wkv = w_kv_up.astype(BF16)

    def mla_q(z, tm, use_rope, name):
        t = z[2].shape[1]
        qf = _nmm_call(z[2], q_norm, None, None, wq, (wq.shape[1],), (F32,), tm, name)[0]
        qf = qf.reshape(b, t, MLA_HEADS, MLA_NOPE + MLA_ROPE)
        nope, rp = qf[..., :MLA_NOPE], qf[..., MLA_NOPE:]
        if use_rope:
            rp = _apply_rope(rp, cos, sin)
        qc = jnp.concatenate([_pair_slots(nope), rp, jnp.zeros((b, t, MLA_HEADS, LANES - MLA_ROPE), F32)], -1)
        return (qc * scale).astype(BF16).reshape(b, t, MLA_HEADS * 2 * LANES)

    def mla_kv(z, tm, use_rope, name):
        t = z[3].shape[1]
        kv = _nmm_call(z[3], kv_norm, None, None, wkv, (wkv.shape[1],), (F32,), tm, name)[0]
        kv = kv.reshape(b, t, MLA_HEADS, MLA_NOPE + MLA_V)
        k_nope, v = kv[..., :MLA_NOPE], kv[..., MLA_NOPE:]
        k_pe = z[4][..., 32:32 + MLA_ROPE][:, :, None, :]
        if use_rope:
            k_pe = _apply_rope(k_pe, cos, sin)
        n_pair = MLA_HEADS // 2
        kc = jnp.concatenate([k_nope.reshape(b, t, n_pair, LANES),
                              jnp.broadcast_to(k_pe, (b, t, n_pair, MLA_ROPE)),
                              jnp.zeros((b, t, n_pair, LANES - MLA_ROPE), F32)], -1)
        return (kc.astype(BF16).reshape(b, t, n_pair * 2 * LANES),
                v.astype(BF16).reshape(b, t, MLA_HEADS * MLA_V))

    q_l = mla_q(lat, 512, True, "mla_q_lat")
    q_c = mla_q(ctx, 256, False, "mla_q_ctx")
    k_l, v_l = mla_kv(lat, 512, True, "mla_kv_lat")
    k_c, v_c = mla_kv(ctx, 256, False, "mla_kv_ctx")
    k_all = jnp.concatenate([k_c, k_l], 1)
    v_all = jnp.concatenate([v_c, v_l], 1)
    n_pair = MLA_HEADS // 2
    b_lat = _attn_call(q_l, k_all, v_all, 2 * LANES, n_pair, n_pair, _attn_chunks(l_len, s_len), 512, "mla_attn_lat")
    b_ctx = _attn_call(q_c, k_c, v_c, 2 * LANES, n_pair, n_pair, [(0, l_len)], l_len, "mla_attn_ctx")
    mix_lat = jnp.concatenate([a_lat.astype(BF16), b_lat], -1)
    mix_ctx = jnp.concatenate([a_ctx.astype(BF16), b_ctx], -1)
    return mix_lat, mix_ctx


def _odd_mixer_last(h, hc, mods, mods_c, g_pre, w_in, rpb, q_norm, k_norm, rope):
    b, s_len, d = h.shape
    l_len = hc.shape[1]
    sh_m, sc_m = mods
    csh_m, csc_m = mods_c
    w_cat = w_in.astype(BF16)
    kv_w = GQA_KV_HEADS * GQA_DIM
    splits = (NA_W, NA_W, NA_W, GQA_HEADS * GQA_DIM, 2 * kv_w)
    dts = (BF16, BF16, BF16, F32, F32)
    lat = _nmm_call(h, g_pre, sh_m, sc_m, w_cat, splits, dts, 512, "in_odd_lat")
    ctx = _nmm_call(hc, g_pre, csh_m, csc_m, w_cat, splits, dts, 256, "in_odd_ctx")

    q_na = (lat[0].astype(F32) * (NA_DIM ** -0.5)).astype(BF16)
    bias = _na_bias_table(rpb, s_len // GRID_W)
    c_lat = _na_call(q_na, lat[1], lat[2], ctx[1], ctx[2], bias)

    cos, sin = rope
    scale = GQA_DIM ** -0.5
    qd = _rms(lat[3].reshape(b, s_len, GQA_HEADS, GQA_DIM)) * q_norm
    qd = _apply_rope(qd, cos, sin) * scale
    q_cat = _pair_slots(qd).astype(BF16).reshape(b, s_len, GQA_HEADS * LANES)

    def kv(z, use_rope):
        t = z[4].shape[1]
        k = _rms(z[4][..., :kv_w].reshape(b, t, GQA_KV_HEADS, GQA_DIM)) * k_norm
        if use_rope:
            k = _apply_rope(k, cos, sin)
        v = z[4][..., kv_w:].reshape(b, t, GQA_KV_HEADS, GQA_DIM)
        dup = lambda x: jnp.concatenate([x, x], -1).astype(BF16).reshape(b, t, GQA_KV_HEADS * LANES)
        return dup(k), dup(v)

    k_l, v_l = kv(lat, True)
    k_c, v_c = kv(ctx, False)
    k_all = jnp.concatenate([k_c, k_l], 1)
    v_all = jnp.concatenate([v_c, v_l], 1)
    d_lat = _attn_call(q_cat, k_all, v_all, LANES, GQA_HEADS // 2, GQA_KV_HEADS, _attn_chunks(l_len, s_len), 512,
                       "gqa_attn")
    return jnp.concatenate([c_lat, d_lat], -1)


def kernel(x, c, ctx, c_ctx, w_mod, b_mod, g_pre_mix, g_post_mix, g_pre_ffn, g_post_ffn, w_ffn_gate, w_ffn_up,
           w_ffn_down, w_in_even, w_out_even, gdn_conv, gdn_a_log, gdn_dt_bias, gdn_out_norm, mla_q_norm,
           mla_w_q_up, mla_kv_norm, mla_w_kv_up, w_in_odd, w_out_odd, na_rpb, gqa_q_norm, gqa_k_norm):
    bsz, s_len, d = x.shape
    depth = w_mod.shape[0]
    assert depth == 2, "layer 0 = even mixer with context update, layer 1 = odd mixer (last)"
    rope_mla = _axial_rope(s_len, MLA_ROPE)
    rope_gqa = _axial_rope(s_len, GQA_DIM)
    c_rows = jnp.concatenate([c, c_ctx[None, :], jnp.zeros((16 - bsz - 1, d), F32)], 0)
    h, hc = x, ctx
    for i in range(depth):
        mod_all = _mod_call(c_rows, w_mod[i].astype(BF16), b_mod[i])
        mod = jnp.split(mod_all[:bsz], 6, axis=-1)
        mod_c = [jnp.broadcast_to(m, (bsz, d)) for m in jnp.split(mod_all[bsz:bsz + 1], 6, axis=-1)]
        sh_m, sc_m, gt_m, sh_f, sc_f, gt_f = mod
        csh_m, csc_m, cgt_m, csh_f, csc_f, cgt_f = mod_c
        wg, wu, wd = w_ffn_gate[i].astype(BF16), w_ffn_up[i].astype(BF16), w_ffn_down[i].astype(BF16)
        if i == 0:
            mix, mix_c = _even_mixer(h, hc, (sh_m, sc_m), (csh_m, csc_m), g_pre_mix[i], w_in_even[0], gdn_conv[0],
                                     gdn_a_log[0], gdn_dt_bias[0], gdn_out_norm[0], mla_q_norm[0], mla_w_q_up[0],
                                     mla_kv_norm[0], mla_w_kv_up[0], rope_mla)
            wo = w_out_even[0].astype(BF16)
            hc = _mix_ffn_call(hc, mix_c, cgt_m, csh_f, csc_f, cgt_f, g_post_mix[i], g_pre_ffn[i], g_post_ffn[i],
                               wo, wg, wu, wd, 256, "mix_ffn_ctx")
        else:
            mix = _odd_mixer_last(h, hc, (sh_m, sc_m), (csh_m, csc_m), g_pre_mix[i], w_in_odd[0], na_rpb[0],
                                  gqa_q_norm[0], gqa_k_norm[0], rope_gqa)
            wo = w_out_odd[0].astype(BF16)
        h = _mix_ffn_call(h, mix, gt_m, sh_f, sc_f, gt_f, g_post_mix[i], g_pre_ffn[i], g_post_ffn[i],
                          wo, wg, wu, wd, 256, "mix_ffn_lat%d" % i)
    return h
```

```python
import functools

import numpy as np
import jax
import jax.numpy as jnp
from jax import lax
from jax.experimental import pallas as pl
from jax.experimental.pallas import tpu as pltpu

F32 = jnp.float32
BF16 = jnp.bfloat16

GRID_W = 64
NORM_EPS = 1e-6
ROPE_THETA = 10000.0

GDN_HEADS = 8
GDN_DK = 64
GDN_DV = 64
GDN_CHUNK = 64
GDN_QK_W = GDN_HEADS * GDN_DK
GDN_V_W = GDN_HEADS * GDN_DV

MLA_HEADS = 8
MLA_NOPE = 64
MLA_ROPE = 32
MLA_V = 64
MLA_Q_LORA = 256
MLA_KV_LORA = 128

NA_HEADS = 8
NA_DIM = 64
NA_WIN_H = 8
NA_WIN_W = 16
NA_W = NA_HEADS * NA_DIM

GQA_HEADS = 8
GQA_KV_HEADS = 2
GQA_DIM = 64

LANES = 128
HEAD_W = 64
MASK_VALUE = -1e30
VMEM_LIMIT = 56 << 20

NT_DIMS = (((1,), (1,)), ((), ()))


def _params(n_grid, vmem=VMEM_LIMIT):
    return pltpu.CompilerParams(dimension_semantics=("arbitrary",) * n_grid, vmem_limit_bytes=vmem)


def _rms(x):
    return x * lax.rsqrt(jnp.mean(x * x, axis=-1, keepdims=True) + NORM_EPS)


def _mod_kernel(c_ref, w_ref, b_ref, o_ref):
    c = c_ref[...]
    a = (c * jax.nn.sigmoid(c)).astype(BF16)
    o_ref[...] = jnp.dot(a, w_ref[...], preferred_element_type=F32) + b_ref[...]


def _mod_call(c_rows, w, b):
    r, d = c_rows.shape
    n = w.shape[1]
    tn = 1024
    return pl.pallas_call(
        _mod_kernel,
        grid=(n // tn,),
        in_specs=[pl.BlockSpec((r, d), lambda j: (0, 0)),
                  pl.BlockSpec((d, tn), lambda j: (0, j)),
                  pl.BlockSpec((1, tn), lambda j: (0, j))],
        out_specs=pl.BlockSpec((r, tn), lambda j: (0, j)),
        out_shape=jax.ShapeDtypeStruct((r, n), F32),
        compiler_params=_params(1),
        name="mod",
    )(c_rows, w, b.reshape(1, n))


def _nmm_kernel(x_ref, g_ref, sh_ref, sc_ref, w_ref, *o_refs, offs):
    x = x_ref[0].astype(F32)
    u = (_rms(x) * g_ref[...] * (1.0 + sc_ref[0]) + sh_ref[0]).astype(BF16)
    for o_ref, (o, n) in zip(o_refs, offs):
        o_ref[0] = jnp.dot(u, w_ref[:, o:o + n], preferred_element_type=F32).astype(o_ref.dtype)


def _nmm_call(x, g, shift, scale, w, splits, dtypes, tm, name):
    b, t, din = x.shape
    n = w.shape[1]
    assert sum(splits) == n and t % tm == 0
    offs, o = [], 0
    for s in splits:
        assert s % LANES == 0
        offs.append((o, s))
        o += s
    if shift is None:
        shift = jnp.zeros((b, din), F32)
        scale = jnp.zeros((b, din), F32)
    return pl.pallas_call(
        functools.partial(_nmm_kernel, offs=tuple(offs)),
        grid=(b, t // tm),
        in_specs=[pl.BlockSpec((1, tm, din), lambda i, j: (i, j, 0)),
                  pl.BlockSpec((1, din), lambda i, j: (0, 0)),
                  pl.BlockSpec((1, 1, din), lambda i, j: (i, 0, 0)),
                  pl.BlockSpec((1, 1, din), lambda i, j: (i, 0, 0)),
                  pl.BlockSpec((din, n), lambda i, j: (0, 0))],
        out_specs=[pl.BlockSpec((1, tm, s), lambda i, j: (i, j, 0)) for s in splits],
        out_shape=[jax.ShapeDtypeStruct((b, t, s), dt) for s, dt in zip(splits, dtypes)],
        compiler_params=_params(2),
        name=name,
    )(x, g.reshape(1, din).astype(F32), shift.reshape(b, 1, din), scale.reshape(b, 1, din), w)


def _attn_kernel(q_ref, k_ref, v_ref, o_ref, *, dc, chunks):
    tq = q_ref.shape[1]
    lane = lax.broadcasted_iota(jnp.int32, (tq, LANES), 1)
    outs = []
    for hh in range(2):
        q = q_ref[0, :, hh * dc:(hh + 1) * dc]
        m = l = acc = None
        for s0, n in chunks:
            kk = k_ref[0, s0:s0 + n, :]
            vv = v_ref[0, s0:s0 + n, :]
            s = lax.dot_general(q, kk, NT_DIMS, preferred_element_type=F32)
            mc = jnp.max(s, axis=-1, keepdims=True)
            if m is None:
                m_new = mc
                p = jnp.exp(s - m_new)
                l = jnp.sum(p, axis=-1, keepdims=True)
                acc = jnp.dot(p.astype(BF16), vv, preferred_element_type=F32)
            else:
                m_new = jnp.maximum(m, mc)
                alpha = jnp.exp(m - m_new)
                p = jnp.exp(s - m_new)
                l = alpha * l + jnp.sum(p, axis=-1, keepdims=True)
                acc = alpha * acc + jnp.dot(p.astype(BF16), vv, preferred_element_type=F32)
            m = m_new
        outs.append(acc / l)
    o_ref[0] = jnp.where(lane < HEAD_W, outs[0], outs[1]).astype(o_ref.dtype)


def _attn_call(q, k, v, dc, n_pairs, n_groups, chunks, tq, name):
    b, t_q, _ = q.shape
    t_k = k.shape[1]
    per = n_pairs // n_groups
    return pl.pallas_call(
        functools.partial(_attn_kernel, dc=dc, chunks=tuple(chunks)),
        grid=(b, n_pairs, t_q // tq),
        in_specs=[pl.BlockSpec((1, tq, 2 * dc), lambda i, p, j: (i, j, p)),
                  pl.BlockSpec((1, t_k, dc), lambda i, p, j: (i, 0, p // per)),
                  pl.BlockSpec((1, t_k, LANES), lambda i, p, j: (i, 0, p // per))],
        out_specs=pl.BlockSpec((1, tq, LANES), lambda i, p, j: (i, j, p)),
        out_shape=jax.ShapeDtypeStruct((b, t_q, n_pairs * LANES), BF16),
        compiler_params=_params(3),
        name=name,
    )(q, k, v)


NA_QROWS = 4
NA_KROWS = 12


def _na_kernel(q_ref, k_ref, v_ref, kc_ref, vc_ref, bias_ref, o_ref):
    rb = pl.program_id(1)
    n_rows = k_ref.shape[1] // GRID_W
    base = jnp.clip(rb * NA_QROWS - NA_WIN_H // 2, 0, n_rows - NA_KROWS) * GRID_W
    base = pl.multiple_of(base, GRID_W)
    nq = NA_QROWS * GRID_W
    nk = NA_KROWS * GRID_W
    lane = lax.broadcasted_iota(jnp.int32, (nq, LANES), 1)
    for p in range(NA_HEADS // 2):
        cols = slice(p * LANES, (p + 1) * LANES)
        qp = q_ref[0, :, cols]
        kw = k_ref[0, pl.ds(base, nk), cols]
        vw = v_ref[0, pl.ds(base, nk), cols]
        kc = kc_ref[0, :, cols]
        vc = vc_ref[0, :, cols]
        outs = []
        for hh in range(2):
            sel = (lane < HEAD_W) if hh == 0 else (lane >= HEAD_W)
            qm = jnp.where(sel, qp, jnp.zeros_like(qp))
            s_loc = lax.dot_general(qm, kw, NT_DIMS, preferred_element_type=F32) + bias_ref[0, 2 * p + hh]
            s_ctx = lax.dot_general(qm, kc, NT_DIMS, preferred_element_type=F32)
            m = jnp.maximum(jnp.max(s_loc, axis=-1, keepdims=True), jnp.max(s_ctx, axis=-1, keepdims=True))
            p_loc = jnp.exp(s_loc - m)
            p_ctx = jnp.exp(s_ctx - m)
            l = jnp.sum(p_loc, axis=-1, keepdims=True) + jnp.sum(p_ctx, axis=-1, keepdims=True)
            acc = (jnp.dot(p_loc.astype(BF16), vw, preferred_element_type=F32)
                   + jnp.dot(p_ctx.astype(BF16), vc, preferred_element_type=F32))
            outs.append(acc / l)
        o_ref[0, :, cols] = jnp.where(lane < HEAD_W, outs[0], outs[1]).astype(o_ref.dtype)


def _na_bias_table(rpb, rows):
    n_blocks = rows // NA_QROWS
    n_h = rpb.shape[0]
    c = np.arange(GRID_W)[:, None]
    kc = np.arange(GRID_W)[None, :]
    cs = np.clip(c - NA_WIN_W // 2, 0, GRID_W - NA_WIN_W)
    col_ok = (kc >= cs) & (kc < cs + NA_WIN_W)
    dc = np.clip(kc - c + (NA_WIN_W - 1), 0, 2 * NA_WIN_W - 2)
    pick = (np.arange(2 * NA_WIN_W - 1)[:, None] == dc.reshape(-1)[None, :]).astype(np.float32)
    variants = []
    for r0 in (0, NA_QROWS * (n_blocks // 2), rows - NA_QROWS):
        base = int(np.clip(r0 - NA_WIN_H // 2, 0, rows - NA_KROWS))
        r = r0 + np.arange(NA_QROWS)[:, None]
        kr = base + np.arange(NA_KROWS)[None, :]
        rs = np.clip(r - NA_WIN_H // 2, 0, rows - NA_WIN_H)
        row_ok = (kr >= rs) & (kr < rs + NA_WIN_H)
        dr = np.clip(kr - r + (NA_WIN_H - 1), 0, 2 * NA_WIN_H - 2)
        by_row = rpb[:, dr.reshape(-1), :].astype(F32)
        full = jnp.einsum("hxd,dm->hxm", by_row, jnp.asarray(pick), precision=lax.Precision.HIGHEST)
        full = full.reshape(n_h, NA_QROWS, NA_KROWS, GRID_W, GRID_W).transpose(0, 1, 3, 2, 4)
        full = full.reshape(n_h, NA_QROWS * GRID_W, NA_KROWS * GRID_W)
        valid = (row_ok[:, None, :, None] & col_ok[None, :, None, :]).reshape(full.shape[1:])
        variants.append(jnp.where(valid[None], full, MASK_VALUE))
    return jnp.stack(variants, 0)


def _na_call(q, k, v, kc, vc, bias):
    b, t, w = q.shape
    n_blocks = t // (NA_QROWS * GRID_W)
    nq = NA_QROWS * GRID_W
    tc = kc.shape[1]

    def bias_map(i, r):
        return (jnp.where(r == 0, 0, jnp.where(r == n_blocks - 1, 2, 1)), 0, 0, 0)

    return pl.pallas_call(
        _na_kernel,
        grid=(b, n_blocks),
        in_specs=[pl.BlockSpec((1, nq, w), lambda i, r: (i, r, 0)),
                  pl.BlockSpec((1, t, w), lambda i, r: (i, 0, 0)),
                  pl.BlockSpec((1, t, w), lambda i, r: (i, 0, 0)),
                  pl.BlockSpec((1, tc, w), lambda i, r: (i, 0, 0)),
                  pl.BlockSpec((1, tc, w), lambda i, r: (i, 0, 0)),
                  pl.BlockSpec((1, NA_HEADS, nq, NA_KROWS * GRID_W), bias_map)],
        out_specs=pl.BlockSpec((1, nq, w), lambda i, r: (i, r, 0)),
        out_shape=jax.ShapeDtypeStruct((b, t, w), BF16),
        compiler_params=_params(2),
        name="na_attn",
    )(q, k, v, kc, vc, bias)


GDN_STEP_CHUNKS = 4
GDN_BLOCK = GDN_STEP_CHUNKS * GDN_CHUNK
GDN_HALO = 16
GDN_PAIRS = GDN_HEADS // 2
TN_DIMS = (((0,), (0,)), ((), ()))


def _split3(x):
    hi = x.astype(BF16)
    r = x - hi.astype(F32)
    mid = r.astype(BF16)
    lo = (r - mid.astype(F32)).astype(BF16)
    return hi, mid, lo


def _bd(x):
    lane = lax.broadcasted_iota(jnp.int32, x.shape, 1)
    z = jnp.zeros_like(x)
    return jnp.concatenate([jnp.where(lane < HEAD_W, x, z), jnp.where(lane >= HEAD_W, x, z)], axis=0)


def _dotb(a, b):
    return jnp.dot(a.astype(BF16), b.astype(BF16), preferred_element_type=F32)


def _softplus(x):
    return jnp.maximum(x, 0.0) + jnp.log(1.0 + jnp.exp(-jnp.abs(x)))


def _gdn_kernel(cur_ref, prev_ref, next_ref, misc_ref, cw_ref, alog_ref, dtb_ref, eg_ref, eb_ref, ones_ref,
                tri3_ref, s0_ref, o_ref, sfin_ref, xs_ref, s_ref, *, rev, n_blk, n_tap):
    step = pl.program_id(1)
    blk = (n_blk - 1 - step) if rev else step
    c_len = GDN_CHUNK
    hw = GDN_HEADS * HEAD_W

    @pl.when(step == 0)
    def _():
        s_ref[...] = s0_ref[0]

    has_prev = (blk > 0).astype(F32)
    has_next = (blk < n_blk - 1).astype(F32)
    xs_ref[0:GDN_HALO, :] = prev_ref[0].astype(F32) * has_prev
    xs_ref[GDN_HALO:GDN_HALO + GDN_BLOCK, :] = cur_ref[0].astype(F32)
    xs_ref[GDN_HALO + GDN_BLOCK:, :] = next_ref[0].astype(F32) * has_next
    acc = None
    for j in range(n_tap):
        start = GDN_HALO - n_tap // 2 + j
        term = xs_ref[start:start + GDN_BLOCK, :] * cw_ref[j:j + 1, :]
        acc = term if acc is None else acc + term
    y = acc * jax.nn.sigmoid(acc)
    q, k, v = y[:, 0:hw], y[:, hw:2 * hw], y[:, 2 * hw:3 * hw]

    def l2n(x):
        x2 = x * x
        hi = x2.astype(BF16)
        mid = (x2 - hi.astype(F32)).astype(BF16)
        ss = jnp.dot(jnp.concatenate([hi, mid], axis=1), ones_ref[...], preferred_element_type=F32)
        return x * lax.rsqrt(ss + NORM_EPS)

    qn = l2n(q) * (GDN_DK ** -0.5)
    kn = l2n(k)

    misc = misc_ref[0]
    g_all = -jnp.exp(alog_ref[...]) * _softplus(misc + dtb_ref[...])
    b_all = jax.nn.sigmoid(misc)
    gx = jnp.dot(jnp.concatenate(_split3(g_all), axis=1), eg_ref[...], preferred_element_type=F32)
    bx = jnp.dot(jnp.concatenate(_split3(b_all), axis=1), eb_ref[...], preferred_element_type=F32)

    row = lax.broadcasted_iota(jnp.int32, (c_len, LANES), 0)
    colp = lax.broadcasted_iota(jnp.int32, (c_len, LANES), 1) & (HEAD_W - 1)
    tri = (row <= colp) if rev else (row >= colp)
    strict = (row < colp) if rev else (row > colp)
    eye_f = (row == colp).astype(F32)
    row_w = lax.broadcasted_iota(jnp.int32, (c_len, hw), 0)
    col_w = lax.broadcasted_iota(jnp.int32, (c_len, hw), 1) & (HEAD_W - 1)
    eye_w = row_w == col_w
    bd_row = lax.broadcasted_iota(jnp.int32, (LANES, LANES), 0)
    bd_col = lax.broadcasted_iota(jnp.int32, (LANES, LANES), 1)
    bd_mask = (bd_row < HEAD_W) == (bd_col < HEAD_W)
    ones8 = jnp.ones((8, 3 * c_len), BF16)

    keys = [(c, p) for c in range(GDN_STEP_CHUNKS) for p in range(GDN_PAIRS)]
    sl = [slice(p * LANES, (p + 1) * LANES) for p in range(GDN_PAIRS)]
    gtot, vb, kbg, qd, kd = {}, {}, {}, {}, {}
    ms, qks = {}, {}
    for c in range(GDN_STEP_CHUNKS):
        r = slice(c * c_len, (c + 1) * c_len)
        gc = jnp.dot(tri3_ref[...], jnp.concatenate(_split3(gx[r]), axis=0), preferred_element_type=F32)
        gtot[c] = gc[0:1] if rev else gc[c_len - 1:c_len]
        egc = jnp.exp(gc)
        kc = kn[r]
        kb = kc * bx[r]
        vb[c] = v[r] * bx[r]
        kbg[c] = kb * egc
        qd[c] = qn[r] * egc
        kd[c] = kc * jnp.exp(gtot[c] - gc)
        dg = jnp.where(eye_w, gc, 0.0)
        gr = jnp.dot(ones8, jnp.concatenate(_split3(dg), axis=0), preferred_element_type=F32)[0:1]
        for p in range(GDN_PAIRS):
            diff = gc[:, sl[p]] - gr[:, sl[p]]
            decay = jnp.where(tri, jnp.exp(jnp.where(tri, diff, 0.0)), 0.0)
            lhs = jnp.concatenate([kb[:, sl[p]], qn[r][:, sl[p]]], axis=0)
            aq = lax.dot_general(lhs.astype(BF16), _bd(kc[:, sl[p]]).astype(BF16), NT_DIMS,
                                 preferred_element_type=F32)
            ms[c, p] = jnp.where(strict, aq[0:c_len] * decay, 0.0)
            qks[c, p] = aq[c_len:2 * c_len] * decay

    xs = {key: eye_f - ms[key] for key in keys}
    pws = dict(ms)
    for it in range(6):
        for key in keys:
            bd_pw = _bd(pws[key]).astype(BF16)
            if it > 0:
                xs[key] = xs[key] + jnp.dot(xs[key].astype(BF16), bd_pw, preferred_element_type=F32)
            if it < 5:
                pws[key] = jnp.dot(pws[key].astype(BF16), bd_pw, preferred_element_type=F32)
    uws = {}
    for c, p in keys:
        rhs = jnp.concatenate([_bd(vb[c][:, sl[p]]), _bd(kbg[c][:, sl[p]])], axis=1)
        uws[c, p] = _dotb(xs[c, p], rhs)

    order = range(GDN_STEP_CHUNKS - 1, -1, -1) if rev else range(GDN_STEP_CHUNKS)
    state = [s_ref[p] for p in range(GDN_PAIRS)]
    for c in order:
        r = slice(c * c_len, (c + 1) * c_len)
        for p in range(GDN_PAIRS):
            u, w = uws[c, p][:, 0:LANES], uws[c, p][:, LANES:2 * LANES]
            wq = _dotb(jnp.concatenate([w, qd[c][:, sl[p]]], axis=0), state[p])
            v_new = u - wq[0:c_len]
            o_ref[0, r, sl[p]] = wq[c_len:2 * c_len] + _dotb(qks[c, p], _bd(v_new))
            upd = lax.dot_general(kd[c][:, sl[p]].astype(BF16), v_new.astype(BF16), TN_DIMS,
                                  preferred_element_type=F32)
            state[p] = state[p] * jnp.exp(gtot[c][:, sl[p]]) + jnp.where(bd_mask, upd, 0.0)
    for p in range(GDN_PAIRS):
        s_ref[p] = state[p]

    @pl.when(step == n_blk - 1)
    def _():
        for p in range(GDN_PAIRS):
            sfin_ref[0, p] = state[p]


def _gdn_consts(conv_w, a_log, dt_bias, d, rev):
    n_tap, w = conv_w.shape
    cw = jnp.concatenate([conv_w.astype(F32), jnp.zeros((8 - n_tap, w), F32)], 0)
    pad = jnp.zeros((LANES - 2 * GDN_HEADS,), F32)
    alog = jnp.concatenate([a_log.reshape(-1).astype(F32), pad]).reshape(1, LANES)
    dtb = jnp.concatenate([dt_bias.reshape(-1).astype(F32), pad]).reshape(1, LANES)
    hw = GDN_HEADS * HEAD_W
    head_of_lane = np.arange(hw) // HEAD_W
    src = np.arange(LANES)[:, None]
    eg = (src == d * GDN_HEADS + head_of_lane[None, :]).astype(np.float32)
    eb = (src == 2 * GDN_HEADS + d * GDN_HEADS + head_of_lane[None, :]).astype(np.float32)
    ones_bd = (head_of_lane[:, None] == head_of_lane[None, :]).astype(np.float32)
    i = np.arange(GDN_CHUNK)
    tri = (i[:, None] <= i[None, :]) if rev else (i[:, None] >= i[None, :])
    as_bf = lambda a, reps, ax: jnp.asarray(np.concatenate([a] * reps, axis=ax), BF16)
    return cw, alog, dtb, as_bf(eg, 3, 0), as_bf(eb, 3, 0), as_bf(ones_bd, 2, 0), as_bf(tri.astype(np.float32), 3, 1)


def _gdn_call(qkv, misc, consts, s0, rev, name):
    b, t, w = qkv.shape
    n_blk = t // GDN_BLOCK
    per = GDN_BLOCK // GDN_HALO
    n_halo = t // GDN_HALO
    cw, alog, dtb, eg, eb, ones_bd, tri3 = consts
    n_tap = 5
    hw = GDN_HEADS * HEAD_W
    blk_of = (lambda s: n_blk - 1 - s) if rev else (lambda s: s)
    const = lambda a: pl.BlockSpec(a.shape, lambda i, s: (0,) * a.ndim)
    s_spec = pl.BlockSpec((1, GDN_PAIRS, LANES, LANES), lambda i, s: (i, 0, 0, 0))
    return pl.pallas_call(
        functools.partial(_gdn_kernel, rev=rev, n_blk=n_blk, n_tap=n_tap),
        grid=(b, n_blk),
        in_specs=[pl.BlockSpec((1, GDN_BLOCK, w), lambda i, s: (i, blk_of(s), 0)),
                  pl.BlockSpec((1, GDN_HALO, w), lambda i, s: (i, jnp.maximum(blk_of(s) * per - 1, 0), 0)),
                  pl.BlockSpec((1, GDN_HALO, w), lambda i, s: (i, jnp.minimum((blk_of(s) + 1) * per, n_halo - 1), 0)),
                  pl.BlockSpec((1, GDN_BLOCK, LANES), lambda i, s: (i, blk_of(s), 0)),
                  const(cw), const(alog), const(dtb), const(eg), const(eb), const(ones_bd), const(tri3), s_spec],
        out_specs=[pl.BlockSpec((1, GDN_BLOCK, hw), lambda i, s: (i, blk_of(s), 0)), s_spec],
        out_shape=[jax.ShapeDtypeStruct((b, t, hw), F32),
                   jax.ShapeDtypeStruct((b, GDN_PAIRS, LANES, LANES), F32)],
        scratch_shapes=[pltpu.VMEM((GDN_BLOCK + 2 * GDN_HALO, w), F32),
                        pltpu.VMEM((GDN_PAIRS, LANES, LANES), F32)],
        compiler_params=_params(2),
        name=name,
    )(qkv, qkv, qkv, misc, cw, alog, dtb, eg, eb, ones_bd, tri3, s0)


def _mix_ffn_kernel(h_ref, mix_ref, gtm_ref, shf_ref, scf_ref, gtf_ref, gpm_ref, gpf_ref, gqf_ref,
                    wo_ref, wg_ref, wu_ref, wd_ref, o_ref, *, ff_chunks):
    y = jnp.dot(mix_ref[0], wo_ref[...], preferred_element_type=F32)
    h1 = h_ref[0] + gtm_ref[0] * (_rms(y) * gpm_ref[...])
    u = (_rms(h1) * gpf_ref[...] * (1.0 + scf_ref[0]) + shf_ref[0]).astype(BF16)
    f = None
    for o, n in ff_chunks:
        gg = jnp.dot(u, wg_ref[:, o:o + n], preferred_element_type=F32)
        uu = jnp.dot(u, wu_ref[:, o:o + n], preferred_element_type=F32)
        a = (gg * jax.nn.sigmoid(gg) * uu).astype(BF16)
        part = jnp.dot(a, wd_ref[o:o + n, :], preferred_element_type=F32)
        f = part if f is None else f + part
    o_ref[0] = h1 + gtf_ref[0] * (_rms(f) * gqf_ref[...])


def _mix_ffn_call(h, mix, gt_m, sh_f, sc_f, gt_f, g_post_mix, g_pre_ffn, g_post_ffn, wo, wg, wu, wd, tm, name):
    b, t, d = h.shape
    dm = mix.shape[-1]
    ff = wg.shape[1]
    half = ff // 2
    assert half % LANES == 0
    ff_chunks = ((0, half), (half, ff - half))
    tok = lambda n: pl.BlockSpec((1, tm, n), lambda i, j: (i, j, 0))
    per_b = pl.BlockSpec((1, 1, d), lambda i, j: (i, 0, 0))
    vec = pl.BlockSpec((1, d), lambda i, j: (0, 0))
    const = lambda shape: pl.BlockSpec(shape, lambda i, j: (0, 0), pipeline_mode=pl.Buffered(1))
    r3 = lambda a: a.reshape(b, 1, d)
    r2 = lambda a: a.reshape(1, d).astype(F32)
    return pl.pallas_call(
        functools.partial(_mix_ffn_kernel, ff_chunks=ff_chunks),
        grid=(b, t // tm),
        in_specs=[tok(d), tok(dm), per_b, per_b, per_b, per_b, vec, vec, vec,
                  const((dm, d)), const((d, ff)), const((d, ff)), const((ff, d))],
        out_specs=tok(d),
        out_shape=jax.ShapeDtypeStruct((b, t, d), F32),
        compiler_params=_params(2),
        name=name,
    )(h, mix, r3(gt_m), r3(sh_f), r3(sc_f), r3(gt_f), r2(g_post_mix), r2(g_pre_ffn), r2(g_post_ffn),
      wo, wg, wu, wd)


def _axial_rope(n_tokens, rot_dim):
    t = jnp.arange(n_tokens, dtype=jnp.int32)
    row = (t // GRID_W).astype(F32)
    col = (t % GRID_W).astype(F32)
    n_freq = rot_dim // 4
    inv_freq = ROPE_THETA ** (-jnp.arange(n_freq, dtype=F32) / n_freq)
    ang = jnp.concatenate([row[:, None] * inv_freq, col[:, None] * inv_freq], -1)
    return jnp.cos(ang), jnp.sin(ang)


def _apply_rope(x, cos, sin):
    half = x.shape[-1] // 2
    x1, x2 = x[..., :half], x[..., half:]
    cs, sn = cos[None, :, None, :], sin[None, :, None, :]
    return jnp.concatenate([x1 * cs - x2 * sn, x2 * cs + x1 * sn], -1)


def _gdn_bidirectional(lat, ctx, conv_w, a_log, dt_bias):
    bsz = lat[0].shape[0]
    s_zero = jnp.zeros((bsz, GDN_PAIRS, LANES, LANES), F32)
    outs = []
    for d, rev in ((0, False), (1, True)):
        consts = _gdn_consts(conv_w, a_log, dt_bias, d, rev)
        o_c, s_c = _gdn_call(ctx[0], ctx[4], consts, s_zero, rev, "gdn_ctx_%d" % d)
        o_l, _ = _gdn_call(lat[0], lat[4], consts, s_c, rev, "gdn_lat_%d" % d)
        outs.append((o_l, o_c))
    return outs[0][0] + outs[1][0], outs[0][1] + outs[1][1]


def _gdn_output(o, gate, out_norm):
    b, t_len, _ = o.shape
    h, dv = GDN_HEADS, GDN_DV
    o = o.reshape(b, t_len, h, dv)
    o = _rms(o) * out_norm
    return (o * jax.nn.silu(gate.astype(F32).reshape(b, t_len, h, dv))).reshape(b, t_len, h * dv)


def _pair_slots(x):
    b, t, h, d = x.shape
    z = jnp.zeros_like(x)
    even = jnp.concatenate([x, z], -1)
    odd = jnp.concatenate([z, x], -1)
    is_odd = (jnp.arange(h) % 2 == 1)[None, None, :, None]
    return jnp.where(is_odd, odd, even)


def _attn_chunks(l_len, s_len, size=512):
    chunks = [(0, l_len)] if l_len else []
    chunks += [(l_len + o, size) for o in range(0, s_len, size)]
    return chunks


def _even_mixer(h, hc, mods, mods_c, g_pre, w_in, conv_w, a_log, dt_bias, out_norm, q_norm, w_q_up,
                kv_norm, w_kv_up, rope):
    b, s_len, d = h.shape
    l_len = hc.shape[1]
    sh_m, sc_m = mods
    csh_m, csc_m = mods_c
    qkv_w = 2 * GDN_QK_W + GDN_V_W
    o0 = qkv_w + GDN_V_W
    w_a, w_bt = w_in[:, o0:o0 + 16], w_in[:, o0 + 16:o0 + 32]
    w_qd = w_in[:, o0 + 32:o0 + 32 + MLA_Q_LORA]
    w_kvd = w_in[:, o0 + 32 + MLA_Q_LORA:o0 + 32 + MLA_Q_LORA + MLA_KV_LORA]
    w_pe = w_in[:, o0 + 32 + MLA_Q_LORA + MLA_KV_LORA:]
    misc = jnp.concatenate([w_a, w_bt, w_pe, jnp.zeros((d, LANES - 64), F32)], 1)
    w_cat = jnp.concatenate([w_in[:, :o0], w_qd, w_kvd, misc], 1).astype(BF16)
    splits = (qkv_w, GDN_V_W, MLA_Q_LORA, MLA_KV_LORA, LANES)
    dts = (BF16, BF16, BF16, BF16, F32)
    lat = _nmm_call(h, g_pre, sh_m, sc_m, w_cat, splits, dts, 512, "in_even_lat")
    ctx = _nmm_call(hc, g_pre, csh_m, csc_m, w_cat, splits, dts, 256, "in_even_ctx")

    o_lat, o_ctx = _gdn_bidirectional(lat, ctx, conv_w, a_log, dt_bias)
    a_lat = _gdn_output(o_lat, lat[1], out_norm)
    a_ctx = _gdn_output(o_ctx, ctx[1], out_norm)

    scale = (MLA_NOPE + MLA_ROPE) ** -0.5
    cos, sin = rope
    wq = w_q_up.astype(BF16)
    wkv = w_kv_up.astype(BF16)

    def mla_q(z, tm, use_rope, name):
        t = z[2].shape[1]
        qf = _nmm_call(z[2], q_norm, None, None, wq, (wq.shape[1],), (F32,), tm, name)[0]
        qf = qf.reshape(b, t, MLA_HEADS, MLA_NOPE + MLA_ROPE)
        nope, rp = qf[..., :MLA_NOPE], qf[..., MLA_NOPE:]
        if use_rope:
            rp = _apply_rope(rp, cos, sin)
        qc = jnp.concatenate([_pair_slots(nope), rp, jnp.zeros((b, t, MLA_HEADS, LANES - MLA_ROPE), F32)], -1)
        return (qc * scale).astype(BF16).reshape(b, t, MLA_HEADS * 2 * LANES)

    def mla_kv(z, tm, use_rope, name):
        t = z[3].shape[1]
        kv = _nmm_call(z[3], kv_norm, None, None, wkv, (wkv.shape[1],), (F32,), tm, name)[0]
        kv = kv.reshape(b, t, MLA_HEADS, MLA_NOPE + MLA_V)
        k_nope, v = kv[..., :MLA_NOPE], kv[..., MLA_NOPE:]
        k_pe = z[4][..., 32:32 + MLA_ROPE][:, :, None, :]
        if use_rope:
            k_pe = _apply_rope(k_pe, cos, sin)
        n_pair = MLA_HEADS // 2
        kc = jnp.concatenate([k_nope.reshape(b, t, n_pair, LANES),
                              jnp.broadcast_to(k_pe, (b, t, n_pair, MLA_ROPE)),
                              jnp.zeros((b, t, n_pair, LANES - MLA_ROPE), F32)], -1)
        return (kc.astype(BF16).reshape(b, t, n_pair * 2 * LANES),
                v.astype(BF16).reshape(b, t, MLA_HEADS * MLA_V))

    q_l = mla_q(lat, 512, True, "mla_q_lat")
    q_c = mla_q(ctx, 256, False, "mla_q_ctx")
    k_l, v_l = mla_kv(lat, 512, True, "mla_kv_lat")
    k_c, v_c = mla_kv(ctx, 256, False, "mla_kv_ctx")
    k_all = jnp.concatenate([k_c, k_l], 1)
    v_all = jnp.concatenate([v_c, v_l], 1)
    n_pair = MLA_HEADS // 2
    b_lat = _attn_call(q_l, k_all, v_all, 2 * LANES, n_pair, n_pair, _attn_chunks(l_len, s_len), 512, "mla_attn_lat")
    b_ctx = _attn_call(q_c, k_c, v_c, 2 * LANES, n_pair, n_pair, [(0, l_len)], l_len, "mla_attn_ctx")
    mix_lat = jnp.concatenate([a_lat.astype(BF16), b_lat], -1)
    mix_ctx = jnp.concatenate([a_ctx.astype(BF16), b_ctx], -1)
    return mix_lat, mix_ctx


def _odd_mixer_last(h, hc, mods, mods_c, g_pre, w_in, rpb, q_norm, k_norm, rope):
    b, s_len, d = h.shape
    l_len = hc.shape[1]
    sh_m, sc_m = mods
    csh_m, csc_m = mods_c
    w_cat = w_in.astype(BF16)
    kv_w = GQA_KV_HEADS * GQA_DIM
    splits = (NA_W, NA_W, NA_W, GQA_HEADS * GQA_DIM, 2 * kv_w)
    dts = (BF16, BF16, BF16, F32, F32)
    lat = _nmm_call(h, g_pre, sh_m, sc_m, w_cat, splits, dts, 512, "in_odd_lat")
    ctx = _nmm_call(hc, g_pre, csh_m, csc_m, w_cat, splits, dts, 256, "in_odd_ctx")

    q_na = (lat[0].astype(F32) * (NA_DIM ** -0.5)).astype(BF16)
    bias = _na_bias_table(rpb, s_len // GRID_W)
    c_lat = _na_call(q_na, lat[1], lat[2], ctx[1], ctx[2], bias)

    cos, sin = rope
    scale = GQA_DIM ** -0.5
    qd = _rms(lat[3].reshape(b, s_len, GQA_HEADS, GQA_DIM)) * q_norm
    qd = _apply_rope(qd, cos, sin) * scale
    q_cat = _pair_slots(qd).astype(BF16).reshape(b, s_len, GQA_HEADS * LANES)

    def kv(z, use_rope):
        t = z[4].shape[1]
        k = _rms(z[4][..., :kv_w].reshape(b, t, GQA_KV_HEADS, GQA_DIM)) * k_norm
        if use_rope:
            k = _apply_rope(k, cos, sin)
        v = z[4][..., kv_w:].reshape(b, t, GQA_KV_HEADS, GQA_DIM)
        dup = lambda x: jnp.concatenate([x, x], -1).astype(BF16).reshape(b, t, GQA_KV_HEADS * LANES)
        return dup(k), dup(v)

    k_l, v_l = kv(lat, True)
    k_c, v_c = kv(ctx, False)
    k_all = jnp.concatenate([k_c, k_l], 1)
    v_all = jnp.concatenate([v_c, v_l], 1)
    d_lat = _attn_call(q_cat, k_all, v_all, LANES, GQA_HEADS // 2, GQA_KV_HEADS, _attn_chunks(l_len, s_len), 512,
                       "gqa_attn")
    return jnp.concatenate([c_lat, d_lat], -1)


def kernel(x, c, ctx, c_ctx, w_mod, b_mod, g_pre_mix, g_post_mix, g_pre_ffn, g_post_ffn, w_ffn_gate, w_ffn_up,
           w_ffn_down, w_in_even, w_out_even, gdn_conv, gdn_a_log, gdn_dt_bias, gdn_out_norm, mla_q_norm,
           mla_w_q_up, mla_kv_norm, mla_w_kv_up, w_in_odd, w_out_odd, na_rpb, gqa_q_norm, gqa_k_norm):
    bsz, s_len, d = x.shape
    depth = w_mod.shape[0]
    assert depth == 2, "layer 0 = even mixer with context update, layer 1 = odd mixer (last)"
    rope_mla = _axial_rope(s_len, MLA_ROPE)
    rope_gqa = _axial_rope(s_len, GQA_DIM)
    c_rows = jnp.concatenate([c, c_ctx[None, :], jnp.zeros((16 - bsz - 1, d), F32)], 0)
    h, hc = x, ctx
    for i in range(depth):
        mod_all = _mod_call(c_rows, w_mod[i].astype(BF16), b_mod[i])
        mod = jnp.split(mod_all[:bsz], 6, axis=-1)
        mod_c = [jnp.broadcast_to(m, (bsz, d)) for m in jnp.split(mod_all[bsz:bsz + 1], 6, axis=-1)]
        sh_m, sc_m, gt_m, sh_f, sc_f, gt_f = mod
        csh_m, csc_m, cgt_m, csh_f, csc_f, cgt_f = mod_c
        wg, wu, wd = w_ffn_gate[i].astype(BF16), w_ffn_up[i].astype(BF16), w_ffn_down[i].astype(BF16)
        if i == 0:
            mix, mix_c = _even_mixer(h, hc, (sh_m, sc_m), (csh_m, csc_m), g_pre_mix[i], w_in_even[0], gdn_conv[0],
                                     gdn_a_log[0], gdn_dt_bias[0], gdn_out_norm[0], mla_q_norm[0], mla_w_q_up[0],
                                     mla_kv_norm[0], mla_w_kv_up[0], rope_mla)
            wo = w_out_even[0].astype(BF16)
            hc = _mix_ffn_call(hc, mix_c, cgt_m, csh_f, csc_f, cgt_f, g_post_mix[i], g_pre_ffn[i], g_post_ffn[i],
                               wo, wg, wu, wd, 256, "mix_ffn_ctx")
        else:
            mix = _odd_mixer_last(h, hc, (sh_m, sc_m), (csh_m, csc_m), g_pre_mix[i], w_in_odd[0], na_rpb[0],
                                  gqa_q_norm[0], gqa_k_norm[0], rope_gqa)
            wo = w_out_odd[0].astype(BF16)
        h = _mix_ffn_call(h, mix, gt_m, sh_f, sc_f, gt_f, g_post_mix[i], g_pre_ffn[i], g_post_ffn[i],
                          wo, wg, wu, wd, 256, "mix_ffn_lat%d" % i)
    return h
```

```python
import functools

import numpy as np
import jax
import jax.numpy as jnp
from jax import lax
from jax.experimental import pallas as pl
from jax.experimental.pallas import tpu as pltpu

F32 = jnp.float32
BF16 = jnp.bfloat16

GRID_W = 64
NORM_EPS = 1e-6
ROPE_THETA = 10000.0

GDN_HEADS = 8
GDN_DK = 64
GDN_DV = 64
GDN_CHUNK = 64
GDN_QK_W = GDN_HEADS * GDN_DK
GDN_V_W = GDN_HEADS * GDN_DV

MLA_HEADS = 8
MLA_NOPE = 64
MLA_ROPE = 32
MLA_V = 64
MLA_Q_LORA = 256
MLA_KV_LORA = 128

NA_HEADS = 8
NA_DIM = 64
NA_WIN_H = 8
NA_WIN_W = 16
NA_W = NA_HEADS * NA_DIM

GQA_HEADS = 8
GQA_KV_HEADS = 2
GQA_DIM = 64

LANES = 128
HEAD_W = 64
MASK_VALUE = -1e30
VMEM_LIMIT = 56 << 20

NT_DIMS = (((1,), (1,)), ((), ()))


def _params(n_grid, vmem=VMEM_LIMIT):
    return pltpu.CompilerParams(dimension_semantics=("arbitrary",) * n_grid, vmem_limit_bytes=vmem)


def _rms(x):
    return x * lax.rsqrt(jnp.mean(x * x, axis=-1, keepdims=True) + NORM_EPS)


def _mod_kernel(c_ref, w_ref, b_ref, o_ref):
    c = c_ref[...]
    a = (c * jax.nn.sigmoid(c)).astype(BF16)
    o_ref[...] = jnp.dot(a, w_ref[...], preferred_element_type=F32) + b_ref[...]


def _mod_call(c_rows, w, b):
    r, d = c_rows.shape
    n = w.shape[1]
    tn = 1024
    return pl.pallas_call(
        _mod_kernel,
        grid=(n // tn,),
        in_specs=[pl.BlockSpec((r, d), lambda j: (0, 0)),
                  pl.BlockSpec((d, tn), lambda j: (0, j)),
                  pl.BlockSpec((1, tn), lambda j: (0, j))],
        out_specs=pl.BlockSpec((r, tn), lambda j: (0, j)),
        out_shape=jax.ShapeDtypeStruct((r, n), F32),
        compiler_params=_params(1),
        name="mod",
    )(c_rows, w, b.reshape(1, n))


def _nmm_kernel(x_ref, g_ref, sh_ref, sc_ref, w_ref, *o_refs, offs):
    x = x_ref[0].astype(F32)
    u = (_rms(x) * g_ref[...] * (1.0 + sc_ref[0]) + sh_ref[0]).astype(BF16)
    for o_ref, (o, n) in zip(o_refs, offs):
        o_ref[0] = jnp.dot(u, w_ref[:, o:o + n], preferred_element_type=F32).astype(o_ref.dtype)


def _nmm_call(x, g, shift, scale, w, splits, dtypes, tm, name):
    b, t, din = x.shape
    n = w.shape[1]
    assert sum(splits) == n and t % tm == 0
    offs, o = [], 0
    for s in splits:
        assert s % LANES == 0
        offs.append((o, s))
        o += s
    if shift is None:
        shift = jnp.zeros((b, din), F32)
        scale = jnp.zeros((b, din), F32)
    return pl.pallas_call(
        functools.partial(_nmm_kernel, offs=tuple(offs)),
        grid=(b, t // tm),
        in_specs=[pl.BlockSpec((1, tm, din), lambda i, j: (i, j, 0)),
                  pl.BlockSpec((1, din), lambda i, j: (0, 0)),
                  pl.BlockSpec((1, 1, din), lambda i, j: (i, 0, 0)),
                  pl.BlockSpec((1, 1, din), lambda i, j: (i, 0, 0)),
                  pl.BlockSpec((din, n), lambda i, j: (0, 0))],
        out_specs=[pl.BlockSpec((1, tm, s), lambda i, j: (i, j, 0)) for s in splits],
        out_shape=[jax.ShapeDtypeStruct((b, t, s), dt) for s, dt in zip(splits, dtypes)],
        compiler_params=_params(2),
        name=name,
    )(x, g.reshape(1, din).astype(F32), shift.reshape(b, 1, din), scale.reshape(b, 1, din), w)


def _attn_kernel(q_ref, k_ref, v_ref, o_ref, *, dc, chunks):
    tq = q_ref.shape[1]
    lane = lax.broadcasted_iota(jnp.int32, (tq, LANES), 1)
    accs = []
    for hh in range(2):
        q = q_ref[0, :, hh * dc:(hh + 1) * dc]
        m = acc = None
        for s0, n in chunks:
            kk = k_ref[0, s0:s0 + n, :]
            vv = v_ref[0, s0:s0 + n, hh * LANES:(hh + 1) * LANES]
            s = lax.dot_general(q, kk, NT_DIMS, preferred_element_type=F32)
            mc = jnp.max(s, axis=-1, keepdims=True)
            if m is None:
                m_new = mc
                acc = jnp.dot(jnp.exp2(s - m_new).astype(BF16), vv, preferred_element_type=F32)
            else:
                m_new = jnp.maximum(m, mc)
                acc = jnp.exp2(m - m_new) * acc + jnp.dot(jnp.exp2(s - m_new).astype(BF16), vv,
                                                          preferred_element_type=F32)
            m = m_new
        accs.append(acc)
    low = lane < HEAD_W
    num = jnp.where(low, accs[0], accs[1])
    den = pltpu.roll(jnp.where(low, accs[1], accs[0]), HEAD_W, 1)
    o_ref[0] = (num / den).astype(o_ref.dtype)


def _attn_call(q, k, v, dc, n_pairs, n_groups, chunks, tq, name):
    b, t_q, _ = q.shape
    t_k = k.shape[1]
    per = n_pairs // n_groups
    return pl.pallas_call(
        functools.partial(_attn_kernel, dc=dc, chunks=tuple(chunks)),
        grid=(b, n_pairs, t_q // tq),
        in_specs=[pl.BlockSpec((1, tq, 2 * dc), lambda i, p, j: (i, j, p)),
                  pl.BlockSpec((1, t_k, dc), lambda i, p, j: (i, 0, p // per)),
                  pl.BlockSpec((1, t_k, 2 * LANES), lambda i, p, j: (i, 0, p // per))],
        out_specs=pl.BlockSpec((1, tq, LANES), lambda i, p, j: (i, j, p)),
        out_shape=jax.ShapeDtypeStruct((b, t_q, n_pairs * LANES), BF16),
        compiler_params=_params(3),
        name=name,
    )(q, k, v)


NA_QROWS = 4
NA_KROWS = 12


def _na_kernel(q_ref, k_ref, v_ref, kc_ref, vc_ref, bias_ref, o_ref):
    rb = pl.program_id(1)
    n_rows = k_ref.shape[1] // GRID_W
    base = jnp.clip(rb * NA_QROWS - NA_WIN_H // 2, 0, n_rows - NA_KROWS) * GRID_W
    base = pl.multiple_of(base, GRID_W)
    nq = NA_QROWS * GRID_W
    nk = NA_KROWS * GRID_W
    lane = lax.broadcasted_iota(jnp.int32, (nq, LANES), 1)
    for p in range(NA_HEADS // 2):
        cols = slice(p * LANES, (p + 1) * LANES)
        qp = q_ref[0, :, cols]
        kw = k_ref[0, pl.ds(base, nk), cols]
        vw = v_ref[0, pl.ds(base, nk), cols]
        kc = kc_ref[0, :, cols]
        vc = vc_ref[0, :, cols]
        outs = []
        for hh in range(2):
            sel = (lane < HEAD_W) if hh == 0 else (lane >= HEAD_W)
            qm = jnp.where(sel, qp, jnp.zeros_like(qp))
            s_loc = lax.dot_general(qm, kw, NT_DIMS, preferred_element_type=F32) + bias_ref[0, 2 * p + hh]
            s_ctx = lax.dot_general(qm, kc, NT_DIMS, preferred_element_type=F32)
            m = jnp.maximum(jnp.max(s_loc, axis=-1, keepdims=True), jnp.max(s_ctx, axis=-1, keepdims=True))
            p_loc = jnp.exp(s_loc - m)
            p_ctx = jnp.exp(s_ctx - m)
            l = jnp.sum(p_loc, axis=-1, keepdims=True) + jnp.sum(p_ctx, axis=-1, keepdims=True)
            acc = (jnp.dot(p_loc.astype(BF16), vw, preferred_element_type=F32)
                   + jnp.dot(p_ctx.astype(BF16), vc, preferred_element_type=F32))
            outs.append(acc / l)
        o_ref[0, :, cols] = jnp.where(lane < HEAD_W, outs[0], outs[1]).astype(o_ref.dtype)


def _na_bias_table(rpb, rows):
    n_blocks = rows // NA_QROWS
    n_h = rpb.shape[0]
    c = np.arange(GRID_W)[:, None]
    kc = np.arange(GRID_W)[None, :]
    cs = np.clip(c - NA_WIN_W // 2, 0, GRID_W - NA_WIN_W)
    col_ok = (kc >= cs) & (kc < cs + NA_WIN_W)
    dc = np.clip(kc - c + (NA_WIN_W - 1), 0, 2 * NA_WIN_W - 2)
    pick = (np.arange(2 * NA_WIN_W - 1)[:, None] == dc.reshape(-1)[None, :]).astype(np.float32)
    variants = []
    for r0 in (0, NA_QROWS * (n_blocks // 2), rows - NA_QROWS):
        base = int(np.clip(r0 - NA_WIN_H // 2, 0, rows - NA_KROWS))
        r = r0 + np.arange(NA_QROWS)[:, None]
        kr = base + np.arange(NA_KROWS)[None, :]
        rs = np.clip(r - NA_WIN_H // 2, 0, rows - NA_WIN_H)
        row_ok = (kr >= rs) & (kr < rs + NA_WIN_H)
        dr = np.clip(kr - r + (NA_WIN_H - 1), 0, 2 * NA_WIN_H - 2)
        by_row = rpb[:, dr.reshape(-1), :].astype(F32)
        full = jnp.einsum("hxd,dm->hxm", by_row, jnp.asarray(pick), precision=lax.Precision.HIGHEST)
        full = full.reshape(n_h, NA_QROWS, NA_KROWS, GRID_W, GRID_W).transpose(0, 1, 3, 2, 4)
        full = full.reshape(n_h, NA_QROWS * GRID_W, NA_KROWS * GRID_W)
        valid = (row_ok[:, None, :, None] & col_ok[None, :, None, :]).reshape(full.shape[1:])
        variants.append(jnp.where(valid[None], full, MASK_VALUE))
    return jnp.stack(variants, 0)


def _na_call(q, k, v, kc, vc, bias):
    b, t, w = q.shape
    n_blocks = t // (NA_QROWS * GRID_W)
    nq = NA_QROWS * GRID_W
    tc = kc.shape[1]

    def bias_map(i, r):
        return (jnp.where(r == 0, 0, jnp.where(r == n_blocks - 1, 2, 1)), 0, 0, 0)

    return pl.pallas_call(
        _na_kernel,
        grid=(b, n_blocks),
        in_specs=[pl.BlockSpec((1, nq, w), lambda i, r: (i, r, 0)),
                  pl.BlockSpec((1, t, w), lambda i, r: (i, 0, 0)),
                  pl.BlockSpec((1, t, w), lambda i, r: (i, 0, 0)),
                  pl.BlockSpec((1, tc, w), lambda i, r: (i, 0, 0)),
                  pl.BlockSpec((1, tc, w), lambda i, r: (i, 0, 0)),
                  pl.BlockSpec((1, NA_HEADS, nq, NA_KROWS * GRID_W), bias_map)],
        out_specs=pl.BlockSpec((1, nq, w), lambda i, r: (i, r, 0)),
        out_shape=jax.ShapeDtypeStruct((b, t, w), BF16),
        compiler_params=_params(2),
        name="na_attn",
    )(q, k, v, kc, vc, bias)


GDN_STEP_CHUNKS = 4
GDN_BLOCK = GDN_STEP_CHUNKS * GDN_CHUNK
GDN_HALO = 16
GDN_PAIRS = GDN_HEADS // 2
TN_DIMS = (((0,), (0,)), ((), ()))


def _split3(x):
    hi = x.astype(BF16)
    r = x - hi.astype(F32)
    mid = r.astype(BF16)
    lo = (r - mid.astype(F32)).astype(BF16)
    return hi, mid, lo


def _bd(x):
    lane = lax.broadcasted_iota(jnp.int32, x.shape, 1)
    z = jnp.zeros_like(x)
    return jnp.concatenate([jnp.where(lane < HEAD_W, x, z), jnp.where(lane >= HEAD_W, x, z)], axis=0)


def _dotb(a, b):
    return jnp.dot(a.astype(BF16), b.astype(BF16), preferred_element_type=F32)


def _softplus(x):
    return jnp.maximum(x, 0.0) + jnp.log(1.0 + jnp.exp(-jnp.abs(x)))


def _gdn_kernel(cur_ref, prev_ref, next_ref, misc_ref, cw_ref, alog_ref, dtb_ref, eg_ref, eb_ref, ones_ref,
                tri3_ref, s0_ref, o_ref, sfin_ref, xs_ref, s_ref, *, rev, n_blk, n_tap):
    step = pl.program_id(1)
    blk = (n_blk - 1 - step) if rev else step
    c_len = GDN_CHUNK
    hw = GDN_HEADS * HEAD_W

    @pl.when(step == 0)
    def _():
        s_ref[...] = s0_ref[0]

    has_prev = (blk > 0).astype(F32)
    has_next = (blk < n_blk - 1).astype(F32)
    xs_ref[0:GDN_HALO, :] = prev_ref[0].astype(F32) * has_prev
    xs_ref[GDN_HALO:GDN_HALO + GDN_BLOCK, :] = cur_ref[0].astype(F32)
    xs_ref[GDN_HALO + GDN_BLOCK:, :] = next_ref[0].astype(F32) * has_next
    acc = None
    for j in range(n_tap):
        start = GDN_HALO - n_tap // 2 + j
        term = xs_ref[start:start + GDN_BLOCK, :] * cw_ref[j:j + 1, :]
        acc = term if acc is None else acc + term
    y = acc * jax.nn.sigmoid(acc)
    q, k, v = y[:, 0:hw], y[:, hw:2 * hw], y[:, 2 * hw:3 * hw]

    def l2n(x):
        x2 = x * x
        hi = x2.astype(BF16)
        mid = (x2 - hi.astype(F32)).astype(BF16)
        ss = jnp.dot(jnp.concatenate([hi, mid], axis=1), ones_ref[...], preferred_element_type=F32)
        return x * lax.rsqrt(ss + NORM_EPS)

    qn = l2n(q) * (GDN_DK ** -0.5)
    kn = l2n(k)

    misc = misc_ref[0]
    g_all = -jnp.exp(alog_ref[...]) * _softplus(misc + dtb_ref[...])
    b_all = jax.nn.sigmoid(misc)
    gx = jnp.dot(jnp.concatenate(_split3(g_all), axis=1), eg_ref[...], preferred_element_type=F32)
    bx = jnp.dot(jnp.concatenate(_split3(b_all), axis=1), eb_ref[...], preferred_element_type=F32)

    row = lax.broadcasted_iota(jnp.int32, (c_len, LANES), 0)
    colp = lax.broadcasted_iota(jnp.int32, (c_len, LANES), 1) & (HEAD_W - 1)
    tri = (row <= colp) if rev else (row >= colp)
    strict = (row < colp) if rev else (row > colp)
    eye_f = (row == colp).astype(F32)
    row_w = lax.broadcasted_iota(jnp.int32, (c_len, hw), 0)
    col_w = lax.broadcasted_iota(jnp.int32, (c_len, hw), 1) & (HEAD_W - 1)
    eye_w = row_w == col_w
    bd_row = lax.broadcasted_iota(jnp.int32, (LANES, LANES), 0)
    bd_col = lax.broadcasted_iota(jnp.int32, (LANES, LANES), 1)
    bd_mask = (bd_row < HEAD_W) == (bd_col < HEAD_W)
    ones8 = jnp.ones((8, 3 * c_len), BF16)

    keys = [(c, p) for c in range(GDN_STEP_CHUNKS) for p in range(GDN_PAIRS)]
    sl = [slice(p * LANES, (p + 1) * LANES) for p in range(GDN_PAIRS)]
    gtot, vb, kbg, qd, kd = {}, {}, {}, {}, {}
    ms, qks = {}, {}
    for c in range(GDN_STEP_CHUNKS):
        r = slice(c * c_len, (c + 1) * c_len)
        gc = jnp.dot(tri3_ref[...], jnp.concatenate(_split3(gx[r]), axis=0), preferred_element_type=F32)
        gtot[c] = gc[0:1] if rev else gc[c_len - 1:c_len]
        egc = jnp.exp(gc)
        kc = kn[r]
        kb = kc * bx[r]
        vb[c] = v[r] * bx[r]
        kbg[c] = kb * egc
        qd[c] = qn[r] * egc
        kd[c] = kc * jnp.exp(gtot[c] - gc)
        dg = jnp.where(eye_w, gc, 0.0)
        gr = jnp.dot(ones8, jnp.concatenate(_split3(dg), axis=0), preferred_element_type=F32)[0:1]
        for p in range(GDN_PAIRS):
            diff = gc[:, sl[p]] - gr[:, sl[p]]
            decay = jnp.where(tri, jnp.exp(jnp.where(tri, diff, 0.0)), 0.0)
            lhs = jnp.concatenate([kb[:, sl[p]], qn[r][:, sl[p]]], axis=0)
            aq = lax.dot_general(lhs.astype(BF16), _bd(kc[:, sl[p]]).astype(BF16), NT_DIMS,
                                 preferred_element_type=F32)
            ms[c, p] = jnp.where(strict, aq[0:c_len] * decay, 0.0)
            qks[c, p] = aq[c_len:2 * c_len] * decay

    xs = {key: eye_f - ms[key] for key in keys}
    pws = dict(ms)
    for it in range(6):
        for key in keys:
            bd_pw = _bd(pws[key]).astype(BF16)
            if it > 0:
                xs[key] = xs[key] + jnp.dot(xs[key].astype(BF16), bd_pw, preferred_element_type=F32)
            if it < 5:
                pws[key] = jnp.dot(pws[key].astype(BF16), bd_pw, preferred_element_type=F32)
    uws = {}
    for c, p in keys:
        rhs = jnp.concatenate([_bd(vb[c][:, sl[p]]), _bd(kbg[c][:, sl[p]])], axis=1)
        uws[c, p] = _dotb(xs[c, p], rhs)

    order = range(GDN_STEP_CHUNKS - 1, -1, -1) if rev else range(GDN_STEP_CHUNKS)
    state = [s_ref[p] for p in range(GDN_PAIRS)]
    for c in order:
        r = slice(c * c_len, (c + 1) * c_len)
        for p in range(GDN_PAIRS):
            u, w = uws[c, p][:, 0:LANES], uws[c, p][:, LANES:2 * LANES]
            wq = _dotb(jnp.concatenate([w, qd[c][:, sl[p]]], axis=0), state[p])
            v_new = u - wq[0:c_len]
            o_ref[0, r, sl[p]] = wq[c_len:2 * c_len] + _dotb(qks[c, p], _bd(v_new))
            upd = lax.dot_general(kd[c][:, sl[p]].astype(BF16), v_new.astype(BF16), TN_DIMS,
                                  preferred_element_type=F32)
            state[p] = state[p] * jnp.exp(gtot[c][:, sl[p]]) + jnp.where(bd_mask, upd, 0.0)
    for p in range(GDN_PAIRS):
        s_ref[p] = state[p]

    @pl.when(step == n_blk - 1)
    def _():
        for p in range(GDN_PAIRS):
            sfin_ref[0, p] = state[p]


def _gdn_consts(conv_w, a_log, dt_bias, d, rev):
    n_tap, w = conv_w.shape
    cw = jnp.concatenate([conv_w.astype(F32), jnp.zeros((8 - n_tap, w), F32)], 0)
    pad = jnp.zeros((LANES - 2 * GDN_HEADS,), F32)
    alog = jnp.concatenate([a_log.reshape(-1).astype(F32), pad]).reshape(1, LANES)
    dtb = jnp.concatenate([dt_bias.reshape(-1).astype(F32), pad]).reshape(1, LANES)
    hw = GDN_HEADS * HEAD_W
    head_of_lane = np.arange(hw) // HEAD_W
    src = np.arange(LANES)[:, None]
    eg = (src == d * GDN_HEADS + head_of_lane[None, :]).astype(np.float32)
    eb = (src == 2 * GDN_HEADS + d * GDN_HEADS + head_of_lane[None, :]).astype(np.float32)
    ones_bd = (head_of_lane[:, None] == head_of_lane[None, :]).astype(np.float32)
    i = np.arange(GDN_CHUNK)
    tri = (i[:, None] <= i[None, :]) if rev else (i[:, None] >= i[None, :])
    as_bf = lambda a, reps, ax: jnp.asarray(np.concatenate([a] * reps, axis=ax), BF16)
    return cw, alog, dtb, as_bf(eg, 3, 0), as_bf(eb, 3, 0), as_bf(ones_bd, 2, 0), as_bf(tri.astype(np.float32), 3, 1)


def _gdn_call(qkv, misc, consts, s0, rev, name):
    b, t, w = qkv.shape
    n_blk = t // GDN_BLOCK
    per = GDN_BLOCK // GDN_HALO
    n_halo = t // GDN_HALO
    cw, alog, dtb, eg, eb, ones_bd, tri3 = consts
    n_tap = 5
    hw = GDN_HEADS * HEAD_W
    blk_of = (lambda s: n_blk - 1 - s) if rev else (lambda s: s)
    const = lambda a: pl.BlockSpec(a.shape, lambda i, s: (0,) * a.ndim)
    s_spec = pl.BlockSpec((1, GDN_PAIRS, LANES, LANES), lambda i, s: (i, 0, 0, 0))
    return pl.pallas_call(
        functools.partial(_gdn_kernel, rev=rev, n_blk=n_blk, n_tap=n_tap),
        grid=(b, n_blk),
        in_specs=[pl.BlockSpec((1, GDN_BLOCK, w), lambda i, s: (i, blk_of(s), 0)),
                  pl.BlockSpec((1, GDN_HALO, w), lambda i, s: (i, jnp.maximum(blk_of(s) * per - 1, 0), 0)),
                  pl.BlockSpec((1, GDN_HALO, w), lambda i, s: (i, jnp.minimum((blk_of(s) + 1) * per, n_halo - 1), 0)),
                  pl.BlockSpec((1, GDN_BLOCK, LANES), lambda i, s: (i, blk_of(s), 0)),
                  const(cw), const(alog), const(dtb), const(eg), const(eb), const(ones_bd), const(tri3), s_spec],
        out_specs=[pl.BlockSpec((1, GDN_BLOCK, hw), lambda i, s: (i, blk_of(s), 0)), s_spec],
        out_shape=[jax.ShapeDtypeStruct((b, t, hw), F32),
                   jax.ShapeDtypeStruct((b, GDN_PAIRS, LANES, LANES), F32)],
        scratch_shapes=[pltpu.VMEM((GDN_BLOCK + 2 * GDN_HALO, w), F32),
                        pltpu.VMEM((GDN_PAIRS, LANES, LANES), F32)],
        compiler_params=_params(2),
        name=name,
    )(qkv, qkv, qkv, misc, cw, alog, dtb, eg, eb, ones_bd, tri3, s0)


def _mix_ffn_kernel(h_ref, mix_ref, gtm_ref, shf_ref, scf_ref, gtf_ref, gpm_ref, gpf_ref, gqf_ref,
                    wo_ref, wg_ref, wu_ref, wd_ref, o_ref, *, ff_chunks):
    y = jnp.dot(mix_ref[0], wo_ref[...], preferred_element_type=F32)
    h1 = h_ref[0] + gtm_ref[0] * (_rms(y) * gpm_ref[...])
    u = (_rms(h1) * gpf_ref[...] * (1.0 + scf_ref[0]) + shf_ref[0]).astype(BF16)
    f = None
    for o, n in ff_chunks:
        gg = jnp.dot(u, wg_ref[:, o:o + n], preferred_element_type=F32)
        uu = jnp.dot(u, wu_ref[:, o:o + n], preferred_element_type=F32)
        a = (gg * jax.nn.sigmoid(gg) * uu).astype(BF16)
        part = jnp.dot(a, wd_ref[o:o + n, :], preferred_element_type=F32)
        f = part if f is None else f + part
    o_ref[0] = h1 + gtf_ref[0] * (_rms(f) * gqf_ref[...])


def _mix_ffn_call(h, mix, gt_m, sh_f, sc_f, gt_f, g_post_mix, g_pre_ffn, g_post_ffn, wo, wg, wu, wd, tm, name):
    b, t, d = h.shape
    dm = mix.shape[-1]
    ff = wg.shape[1]
    half = ff // 2
    assert half % LANES == 0
    ff_chunks = ((0, half), (half, ff - half))
    tok = lambda n: pl.BlockSpec((1, tm, n), lambda i, j: (i, j, 0))
    per_b = pl.BlockSpec((1, 1, d), lambda i, j: (i, 0, 0))
    vec = pl.BlockSpec((1, d), lambda i, j: (0, 0))
    const = lambda shape: pl.BlockSpec(shape, lambda i, j: (0, 0), pipeline_mode=pl.Buffered(1))
    r3 = lambda a: a.reshape(b, 1, d)
    r2 = lambda a: a.reshape(1, d).astype(F32)
    return pl.pallas_call(
        functools.partial(_mix_ffn_kernel, ff_chunks=ff_chunks),
        grid=(b, t // tm),
        in_specs=[tok(d), tok(dm), per_b, per_b, per_b, per_b, vec, vec, vec,
                  const((dm, d)), const((d, ff)), const((d, ff)), const((ff, d))],
        out_specs=tok(d),
        out_shape=jax.ShapeDtypeStruct((b, t, d), F32),
        compiler_params=_params(2),
        name=name,
    )(h, mix, r3(gt_m), r3(sh_f), r3(sc_f), r3(gt_f), r2(g_post_mix), r2(g_pre_ffn), r2(g_post_ffn),
      wo, wg, wu, wd)


def _axial_rope(n_tokens, rot_dim):
    t = jnp.arange(n_tokens, dtype=jnp.int32)
    row = (t // GRID_W).astype(F32)
    col = (t % GRID_W).astype(F32)
    n_freq = rot_dim // 4
    inv_freq = ROPE_THETA ** (-jnp.arange(n_freq, dtype=F32) / n_freq)
    ang = jnp.concatenate([row[:, None] * inv_freq, col[:, None] * inv_freq], -1)
    return jnp.cos(ang), jnp.sin(ang)


def _apply_rope(x, cos, sin):
    half = x.shape[-1] // 2
    x1, x2 = x[..., :half], x[..., half:]
    cs, sn = cos[None, :, None, :], sin[None, :, None, :]
    return jnp.concatenate([x1 * cs - x2 * sn, x2 * cs + x1 * sn], -1)


def _gdn_bidirectional(lat, ctx, conv_w, a_log, dt_bias):
    bsz = lat[0].shape[0]
    s_zero = jnp.zeros((bsz, GDN_PAIRS, LANES, LANES), F32)
    outs = []
    for d, rev in ((0, False), (1, True)):
        consts = _gdn_consts(conv_w, a_log, dt_bias, d, rev)
        o_c, s_c = _gdn_call(ctx[0], ctx[4], consts, s_zero, rev, "gdn_ctx_%d" % d)
        o_l, _ = _gdn_call(lat[0], lat[4], consts, s_c, rev, "gdn_lat_%d" % d)
        outs.append((o_l, o_c))
    return outs[0][0] + outs[1][0], outs[0][1] + outs[1][1]


def _gdn_output(o, gate, out_norm):
    b, t_len, _ = o.shape
    h, dv = GDN_HEADS, GDN_DV
    o = o.reshape(b, t_len, h, dv)
    o = _rms(o) * out_norm
    return (o * jax.nn.silu(gate.astype(F32).reshape(b, t_len, h, dv))).reshape(b, t_len, h * dv)


def _pair_slots(x):
    b, t, h, d = x.shape
    z = jnp.zeros_like(x)
    even = jnp.concatenate([x, z], -1)
    odd = jnp.concatenate([z, x], -1)
    is_odd = (jnp.arange(h) % 2 == 1)[None, None, :, None]
    return jnp.where(is_odd, odd, even)


def _value_slots(v):
    b, t, g, d = v.shape
    one = jnp.ones_like(v)
    return jnp.concatenate([v, one, one, v], -1).astype(BF16).reshape(b, t, g * 4 * d)


def _value_slots_per_head(v):
    b, t, h, d = v.shape
    one = jnp.ones_like(v)
    is_odd = (jnp.arange(h) % 2 == 1)[None, None, :, None]
    out = jnp.where(is_odd, jnp.concatenate([one, v], -1), jnp.concatenate([v, one], -1))
    return out.astype(BF16).reshape(b, t, h * 2 * d)


LOG2E = 1.4426950408889634
ATTN_TQ = 1024


def _attn_chunks(l_len, s_len, size=512):
    chunks = [(0, l_len)] if l_len else []
    chunks += [(l_len + o, size) for o in range(0, s_len, size)]
    return chunks


def _even_mixer(h, hc, mods, mods_c, g_pre, w_in, conv_w, a_log, dt_bias, out_norm, q_norm, w_q_up,
                kv_norm, w_kv_up, rope):
    b, s_len, d = h.shape
    l_len = hc.shape[1]
    sh_m, sc_m = mods
    csh_m, csc_m = mods_c
    qkv_w = 2 * GDN_QK_W + GDN_V_W
    o0 = qkv_w + GDN_V_W
    w_a, w_bt = w_in[:, o0:o0 + 16], w_in[:, o0 + 16:o0 + 32]
    w_qd = w_in[:, o0 + 32:o0 + 32 + MLA_Q_LORA]
    w_kvd = w_in[:, o0 + 32 + MLA_Q_LORA:o0 + 32 + MLA_Q_LORA + MLA_KV_LORA]
    w_pe = w_in[:, o0 + 32 + MLA_Q_LORA + MLA_KV_LORA:]
    misc = jnp.concatenate([w_a, w_bt, w_pe, jnp.zeros((d, LANES - 64), F32)], 1)
    w_cat = jnp.concatenate([w_in[:, :o0], w_qd, w_kvd, misc], 1).astype(BF16)
    splits = (qkv_w, GDN_V_W, MLA_Q_LORA, MLA_KV_LORA, LANES)
    dts = (BF16, BF16, BF16, BF16, F32)
    lat = _nmm_call(h, g_pre, sh_m, sc_m, w_cat, splits, dts, 512, "in_even_lat")
    ctx = _nmm_call(hc, g_pre, csh_m, csc_m, w_cat, splits, dts, 256, "in_even_ctx")

    o_lat, o_ctx = _gdn_bidirectional(lat, ctx, conv_w, a_log, dt_bias)
    a_lat = _gdn_output(o_lat, lat[1], out_norm)
    a_ctx = _gdn_output(o_ctx, ctx[1], out_norm)

    scale = (MLA_NOPE + MLA_ROPE) ** -0.5
    cos, sin = rope
    wq = w_q_up.astype(BF16)
    wkv = w_kv_up.astype(BF16)

    def mla_q(z, tm, use_rope, name):
        t = z[2].shape[1]
        qf = _nmm_call(z[2], q_norm, None, None, wq, (wq.shape[1],), (F32,), tm, name)[0]
        qf = qf.reshape(b, t, MLA_HEADS, MLA_NOPE + MLA_ROPE)
        nope, rp = qf[..., :MLA_NOPE], qf[..., MLA_NOPE:]
        if use_rope:
            rp = _apply_rope(rp, cos, sin)
        qc = jnp.concatenate([_pair_slots(nope), rp, jnp.zeros((b, t, MLA_HEADS, LANES - MLA_ROPE), F32)], -1)
        return (qc * (scale * LOG2E)).astype(BF16).reshape(b, t, MLA_HEADS * 2 * LANES)

    def mla_kv(z, tm, use_rope, name):
        t = z[3].shape[1]
        kv = _nmm_call(z[3], kv_norm, None, None, wkv, (wkv.shape[1],), (F32,), tm, name)[0]
        kv = kv.reshape(b, t, MLA_HEADS, MLA_NOPE + MLA_V)
        k_nope, v = kv[..., :MLA_NOPE], kv[..., MLA_NOPE:]
        k_pe = z[4][..., 32:32 + MLA_ROPE][:, :, None, :]
        if use_rope:
            k_pe = _apply_rope(k_pe, cos, sin)
        n_pair = MLA_HEADS // 2
        kc = jnp.concatenate([k_nope.reshape(b, t, n_pair, LANES),
                              jnp.broadcast_to(k_pe, (b, t, n_pair, MLA_ROPE)),
                              jnp.zeros((b, t, n_pair, LANES - MLA_ROPE), F32)], -1)
        return (kc.astype(BF16).reshape(b, t, n_pair * 2 * LANES),
                _value_slots_per_head(v))

    q_l = mla_q(lat, 512, True, "mla_q_lat")
    q_c = mla_q(ctx, 256, False, "mla_q_ctx")
    k_l, v_l = mla_kv(lat, 512, True, "mla_kv_lat")
    k_c, v_c = mla_kv(ctx, 256, False, "mla_kv_ctx")
    k_all = jnp.concatenate([k_c, k_l], 1)
    v_all = jnp.concatenate([v_c, v_l], 1)
    n_pair = MLA_HEADS // 2
    b_lat = _attn_call(q_l, k_all, v_all, 2 * LANES, n_pair, n_pair, _attn_chunks(l_len, s_len), ATTN_TQ,
                       "mla_attn_lat")
    b_ctx = _attn_call(q_c, k_c, v_c, 2 * LANES, n_pair, n_pair, [(0, l_len)], l_len, "mla_attn_ctx")
    mix_lat = jnp.concatenate([a_lat.astype(BF16), b_lat], -1)
    mix_ctx = jnp.concatenate([a_ctx.astype(BF16), b_ctx], -1)
    return mix_lat, mix_ctx


def _odd_mixer_last(h, hc, mods, mods_c, g_pre, w_in, rpb, q_norm, k_norm, rope):
    b, s_len, d = h.shape
    l_len = hc.shape[1]
    sh_m, sc_m = mods
    csh_m, csc_m = mods_c
    w_cat = w_in.astype(BF16)
    kv_w = GQA_KV_HEADS * GQA_DIM
    splits = (NA_W, NA_W, NA_W, GQA_HEADS * GQA_DIM, 2 * kv_w)
    dts = (BF16, BF16, BF16, F32, F32)
    lat = _nmm_call(h, g_pre, sh_m, sc_m, w_cat, splits, dts, 512, "in_odd_lat")
    ctx = _nmm_call(hc, g_pre, csh_m, csc_m, w_cat, splits, dts, 256, "in_odd_ctx")

    q_na = (lat[0].astype(F32) * (NA_DIM ** -0.5)).astype(BF16)
    bias = _na_bias_table(rpb, s_len // GRID_W)
    c_lat = _na_call(q_na, lat[1], lat[2], ctx[1], ctx[2], bias)

    cos, sin = rope
    scale = GQA_DIM ** -0.5
    qd = _rms(lat[3].reshape(b, s_len, GQA_HEADS, GQA_DIM)) * q_norm
    qd = _apply_rope(qd, cos, sin) * (scale * LOG2E)
    q_cat = _pair_slots(qd).astype(BF16).reshape(b, s_len, GQA_HEADS * LANES)

    def kv(z, use_rope):
        t = z[4].shape[1]
        k = _rms(z[4][..., :kv_w].reshape(b, t, GQA_KV_HEADS, GQA_DIM)) * k_norm
        if use_rope:
            k = _apply_rope(k, cos, sin)
        v = z[4][..., kv_w:].reshape(b, t, GQA_KV_HEADS, GQA_DIM)
        k_dup = jnp.concatenate([k, k], -1).astype(BF16).reshape(b, t, GQA_KV_HEADS * LANES)
        return k_dup, _value_slots(v)

    k_l, v_l = kv(lat, True)
    k_c, v_c = kv(ctx, False)
    k_all = jnp.concatenate([k_c, k_l], 1)
    v_all = jnp.concatenate([v_c, v_l], 1)
    d_lat = _attn_call(q_cat, k_all, v_all, LANES, GQA_HEADS // 2, GQA_KV_HEADS, _attn_chunks(l_len, s_len), ATTN_TQ,
                       "gqa_attn")
    return jnp.concatenate([c_lat, d_lat], -1)


def kernel(x, c, ctx, c_ctx, w_mod, b_mod, g_pre_mix, g_post_mix, g_pre_ffn, g_post_ffn, w_ffn_gate, w_ffn_up,
           w_ffn_down, w_in_even, w_out_even, gdn_conv, gdn_a_log, gdn_dt_bias, gdn_out_norm, mla_q_norm,
           mla_w_q_up, mla_kv_norm, mla_w_kv_up, w_in_odd, w_out_odd, na_rpb, gqa_q_norm, gqa_k_norm):
    bsz, s_len, d = x.shape
    depth = w_mod.shape[0]
    assert depth == 2, "layer 0 = even mixer with context update, layer 1 = odd mixer (last)"
    rope_mla = _axial_rope(s_len, MLA_ROPE)
    rope_gqa = _axial_rope(s_len, GQA_DIM)
    c_rows = jnp.concatenate([c, c_ctx[None, :], jnp.zeros((16 - bsz - 1, d), F32)], 0)
    h, hc = x, ctx
    for i in range(depth):
        mod_all = _mod_call(c_rows, w_mod[i].astype(BF16), b_mod[i])
        mod = jnp.split(mod_all[:bsz], 6, axis=-1)
        mod_c = [jnp.broadcast_to(m, (bsz, d)) for m in jnp.split(mod_all[bsz:bsz + 1], 6, axis=-1)]
        sh_m, sc_m, gt_m, sh_f, sc_f, gt_f = mod
        csh_m, csc_m, cgt_m, csh_f, csc_f, cgt_f = mod_c
        wg, wu, wd = w_ffn_gate[i].astype(BF16), w_ffn_up[i].astype(BF16), w_ffn_down[i].astype(BF16)
        if i == 0:
            mix, mix_c = _even_mixer(h, hc, (sh_m, sc_m), (csh_m, csc_m), g_pre_mix[i], w_in_even[0], gdn_conv[0],
                                     gdn_a_log[0], gdn_dt_bias[0], gdn_out_norm[0], mla_q_norm[0], mla_w_q_up[0],
                                     mla_kv_norm[0], mla_w_kv_up[0], rope_mla)
            wo = w_out_even[0].astype(BF16)
            hc = _mix_ffn_call(hc, mix_c, cgt_m, csh_f, csc_f, cgt_f, g_post_mix[i], g_pre_ffn[i], g_post_ffn[i],
                               wo, wg, wu, wd, 256, "mix_ffn_ctx")
        else:
            mix = _odd_mixer_last(h, hc, (sh_m, sc_m), (csh_m, csc_m), g_pre_mix[i], w_in_odd[0], na_rpb[0],
                                  gqa_q_norm[0], gqa_k_norm[0], rope_gqa)
            wo = w_out_odd[0].astype(BF16)
        h = _mix_ffn_call(h, mix, gt_m, sh_f, sc_f, gt_f, g_post_mix[i], g_pre_ffn[i], g_post_ffn[i],
                          wo, wg, wu, wd, 256, "mix_ffn_lat%d" % i)
    return h
```

```python
import functools

import numpy as np
import jax
import jax.numpy as jnp
from jax import lax
from jax.experimental import pallas as pl
from jax.experimental.pallas import tpu as pltpu

F32 = jnp.float32
BF16 = jnp.bfloat16

GRID_W = 64
NORM_EPS = 1e-6
ROPE_THETA = 10000.0

GDN_HEADS = 8
GDN_DK = 64
GDN_DV = 64
GDN_CHUNK = 64
GDN_QK_W = GDN_HEADS * GDN_DK
GDN_V_W = GDN_HEADS * GDN_DV

MLA_HEADS = 8
MLA_NOPE = 64
MLA_ROPE = 32
MLA_V = 64
MLA_Q_LORA = 256
MLA_KV_LORA = 128

NA_HEADS = 8
NA_DIM = 64
NA_WIN_H = 8
NA_WIN_W = 16
NA_W = NA_HEADS * NA_DIM

GQA_HEADS = 8
GQA_KV_HEADS = 2
GQA_DIM = 64

LANES = 128
HEAD_W = 64
MASK_VALUE = -1e30
VMEM_LIMIT = 56 << 20

NT_DIMS = (((1,), (1,)), ((), ()))


def _params(n_grid, vmem=VMEM_LIMIT):
    return pltpu.CompilerParams(dimension_semantics=("arbitrary",) * n_grid, vmem_limit_bytes=vmem)


def _rms(x):
    return x * lax.rsqrt(jnp.mean(x * x, axis=-1, keepdims=True) + NORM_EPS)


def _mod_kernel(c_ref, w_ref, b_ref, o_ref):
    c = c_ref[...]
    a = (c * jax.nn.sigmoid(c)).astype(BF16)
    o_ref[...] = jnp.dot(a, w_ref[...], preferred_element_type=F32) + b_ref[...]


def _mod_call(c_rows, w, b):
    r, d = c_rows.shape
    n = w.shape[1]
    tn = 1024
    return pl.pallas_call(
        _mod_kernel,
        grid=(n // tn,),
        in_specs=[pl.BlockSpec((r, d), lambda j: (0, 0)),
                  pl.BlockSpec((d, tn), lambda j: (0, j)),
                  pl.BlockSpec((1, tn), lambda j: (0, j))],
        out_specs=pl.BlockSpec((r, tn), lambda j: (0, j)),
        out_shape=jax.ShapeDtypeStruct((r, n), F32),
        compiler_params=_params(1),
        name="mod",
    )(c_rows, w, b.reshape(1, n))


def _norm_mod(x_ref, g_ref, sh_ref, sc_ref):
    return (_rms(x_ref[0]) * g_ref[...] * (1.0 + sc_ref[0]) + sh_ref[0]).astype(BF16)


def _dotf(a, b):
    return jnp.dot(a, b, preferred_element_type=F32)


def _seg_mean_sq(x, ones_ref):
    x2 = x * x
    hi = x2.astype(BF16)
    mid = (x2 - hi.astype(F32)).astype(BF16)
    return _dotf(jnp.concatenate([hi, mid], axis=1), ones_ref[...]) * (1.0 / HEAD_W)


def _rotate(t, cos, sin_signed, half):
    lane = lax.broadcasted_iota(jnp.int32, t.shape, 1)
    n = t.shape[1]
    partner = jnp.where((lane & (2 * half - 1)) < half, pltpu.roll(t, n - half, 1), pltpu.roll(t, half, 1))
    return t * cos + partner * sin_signed


EVEN_QKV_W = 2 * GDN_QK_W + GDN_V_W
EVEN_GROUPS = (EVEN_QKV_W, GDN_V_W, MLA_Q_LORA, MLA_KV_LORA, LANES, LANES)
MLA_QC = 2 * LANES
MLA_PAIRS = MLA_HEADS // 2


def _in_even_kernel(x_ref, g_ref, sh_ref, sc_ref, w_ref, qg_ref, wq_ref, kvg_ref, wk_ref, wv_ref, vone_ref,
                    cos_ref, sin_ref, qkv_ref, gate_ref, misc_ref, q_ref, k_ref, v_ref):
    u = _norm_mod(x_ref, g_ref, sh_ref, sc_ref)
    offs = np.cumsum((0,) + EVEN_GROUPS)
    grp = lambda i: _dotf(u, w_ref[:, int(offs[i]):int(offs[i + 1])])
    qkv_ref[0] = grp(0).astype(BF16)
    gate_ref[0] = grp(1).astype(BF16)
    misc_ref[0] = grp(4)
    cos, sin = cos_ref[...], sin_ref[...]
    qn = (_rms(grp(2)) * qg_ref[...]).astype(BF16)
    for h in range(MLA_HEADS):
        blk = _dotf(qn, wq_ref[:, h * MLA_QC:(h + 1) * MLA_QC])
        q_ref[0, :, h * MLA_QC:h * MLA_QC + LANES] = blk[:, 0:LANES].astype(BF16)
        q_ref[0, :, h * MLA_QC + LANES:(h + 1) * MLA_QC] = _rotate(blk[:, LANES:], cos, sin,
                                                                    MLA_ROPE // 2).astype(BF16)
    kvn = (_rms(grp(3)) * kvg_ref[...]).astype(BF16)
    pe = _rotate(grp(5), cos, sin, MLA_ROPE // 2).astype(BF16)
    k_nope = _dotf(kvn, wk_ref[...])
    for p in range(MLA_PAIRS):
        k_ref[0, :, p * MLA_QC:p * MLA_QC + LANES] = k_nope[:, p * LANES:(p + 1) * LANES].astype(BF16)
        k_ref[0, :, p * MLA_QC + LANES:(p + 1) * MLA_QC] = pe
    v_ref[0] = (_dotf(kvn, wv_ref[...]) + vone_ref[...]).astype(BF16)


def _even_weights(w_in, q_norm, w_q_up, kv_norm, w_kv_up):
    d = w_in.shape[0]
    o0 = EVEN_QKV_W + GDN_V_W
    n_ab = 4 * GDN_HEADS
    o1 = o0 + n_ab
    o2 = o1 + MLA_Q_LORA
    o3 = o2 + MLA_KV_LORA
    zeros = lambda n, rows=d: jnp.zeros((rows, n), F32)
    w_cat = jnp.concatenate([w_in[:, :o0], w_in[:, o1:o2], w_in[:, o2:o3],
                             w_in[:, o0:o1], zeros(LANES - n_ab),
                             w_in[:, o3:], zeros(LANES - MLA_ROPE)], 1).astype(BF16)
    qh = MLA_NOPE + MLA_ROPE
    cols = []
    for h in range(MLA_HEADS):
        nope = w_q_up[:, h * qh:h * qh + MLA_NOPE]
        rp = w_q_up[:, h * qh + MLA_NOPE:(h + 1) * qh]
        z = zeros(HEAD_W, MLA_Q_LORA)
        cols += ([nope, z] if h % 2 == 0 else [z, nope]) + [rp, zeros(LANES - MLA_ROPE, MLA_Q_LORA)]
    wq = jnp.concatenate(cols, 1).astype(BF16)
    kvh = MLA_NOPE + MLA_V
    wk = jnp.concatenate([w_kv_up[:, h * kvh:h * kvh + MLA_NOPE] for h in range(MLA_HEADS)], 1).astype(BF16)
    vcols, ones = [], []
    for h in range(MLA_HEADS):
        vh = w_kv_up[:, h * kvh + MLA_NOPE:(h + 1) * kvh]
        z = zeros(HEAD_W, MLA_KV_LORA)
        vcols += [vh, z] if h % 2 == 0 else [z, vh]
        ones += [0.0, 1.0] if h % 2 == 0 else [1.0, 0.0]
    wv = jnp.concatenate(vcols, 1).astype(BF16)
    vone = jnp.asarray(np.repeat(np.asarray(ones, np.float32), HEAD_W)[None, :])
    q_gain = (q_norm * ((MLA_NOPE + MLA_ROPE) ** -0.5 * LOG2E)).reshape(1, -1).astype(F32)
    return w_cat, q_gain, wq, kv_norm.reshape(1, -1).astype(F32), wk, wv, vone


def _in_even_call(x, g, shift, scale, weights, cos, sin, tm, name):
    b, t, d = x.shape
    w_cat, q_gain, wq, kv_gain, wk, wv, vone = weights
    const = lambda a: pl.BlockSpec(a.shape, lambda i, j: (0,) * a.ndim)
    per_b = pl.BlockSpec((1, 1, d), lambda i, j: (i, 0, 0))
    tab = pl.BlockSpec((tm, LANES), lambda i, j: (j, 0))
    widths = (EVEN_QKV_W, GDN_V_W, LANES, MLA_HEADS * MLA_QC, MLA_PAIRS * MLA_QC, MLA_HEADS * LANES)
    dts = (BF16, BF16, F32, BF16, BF16, BF16)
    g2 = g.reshape(1, d).astype(F32)
    return pl.pallas_call(
        _in_even_kernel,
        grid=(b, t // tm),
        in_specs=[pl.BlockSpec((1, tm, d), lambda i, j: (i, j, 0)), const(g2), per_b, per_b, const(w_cat),
                  const(q_gain), const(wq), const(kv_gain), const(wk), const(wv), const(vone), tab, tab],
        out_specs=[pl.BlockSpec((1, tm, n), lambda i, j: (i, j, 0)) for n in widths],
        out_shape=[jax.ShapeDtypeStruct((b, t, n), dt) for n, dt in zip(widths, dts)],
        compiler_params=_params(2),
        name=name,
    )(x, g2, shift.reshape(b, 1, d), scale.reshape(b, 1, d), w_cat, q_gain, wq, kv_gain, wk, wv, vone, cos, sin)


GQA_KV_W = GQA_KV_HEADS * GQA_DIM


def _in_odd_kernel(x_ref, g_ref, sh_ref, sc_ref, w_ref, qg_ref, kg_ref, ones_q_ref, ones_k_ref, cos_ref, sin_ref,
                   naq_ref, nak_ref, nav_ref, gq_ref, gk_ref, gv_ref):
    u = _norm_mod(x_ref, g_ref, sh_ref, sc_ref)
    grp = lambda lo, n: _dotf(u, w_ref[:, lo:lo + n])
    naq_ref[0] = (grp(0, NA_W) * (NA_DIM ** -0.5)).astype(BF16)
    nak_ref[0] = grp(NA_W, NA_W).astype(BF16)
    nav_ref[0] = grp(2 * NA_W, NA_W).astype(BF16)
    cos, sin = cos_ref[...], sin_ref[...]
    o = 3 * NA_W
    q = grp(o, GQA_HEADS * GQA_DIM)
    qn = q * lax.rsqrt(_seg_mean_sq(q, ones_q_ref) + NORM_EPS) * qg_ref[...]
    for p in range(GQA_HEADS // 2):
        cols = slice(p * LANES, (p + 1) * LANES)
        gq_ref[0, :, cols] = _rotate(qn[:, cols], cos, sin, GQA_DIM // 2).astype(BF16)
    o += GQA_HEADS * GQA_DIM
    k = grp(o, GQA_KV_W)
    kr = _rotate(k * lax.rsqrt(_seg_mean_sq(k, ones_k_ref) + NORM_EPS) * kg_ref[...], cos, sin, GQA_DIM // 2)
    low = lax.broadcasted_iota(jnp.int32, k.shape, 1) < HEAD_W
    k_sw = pltpu.roll(kr, HEAD_W, 1)
    gk_ref[0, :, 0:LANES] = jnp.where(low, kr, k_sw).astype(BF16)
    gk_ref[0, :, LANES:2 * LANES] = jnp.where(low, k_sw, kr).astype(BF16)
    v = grp(o + GQA_KV_W, GQA_KV_W)
    v_sw = pltpu.roll(v, HEAD_W, 1)
    slots = (jnp.where(low, v, 1.0), jnp.where(low, 1.0, v_sw),
             jnp.where(low, v_sw, 1.0), jnp.where(low, 1.0, v))
    for i, s in enumerate(slots):
        gv_ref[0, :, i * LANES:(i + 1) * LANES] = s.astype(BF16)


def _seg_ones(width):
    head = np.arange(width) // HEAD_W
    m = (head[:, None] == head[None, :]).astype(np.float32)
    return jnp.asarray(np.concatenate([m, m], 0), BF16)


def _in_odd_call(x, g, shift, scale, w, q_gain, k_gain, cos, sin, tm, name):
    b, t, d = x.shape
    assert GQA_KV_W == LANES
    ones_q = _seg_ones(GQA_HEADS * GQA_DIM)
    ones_k = _seg_ones(GQA_KV_W)
    const = lambda a: pl.BlockSpec(a.shape, lambda i, j: (0,) * a.ndim)
    per_b = pl.BlockSpec((1, 1, d), lambda i, j: (i, 0, 0))
    tab = pl.BlockSpec((tm, LANES), lambda i, j: (j, 0))
    widths = (NA_W, NA_W, NA_W, GQA_HEADS * GQA_DIM, 2 * LANES, 4 * LANES)
    g2 = g.reshape(1, d).astype(F32)
    return pl.pallas_call(
        _in_odd_kernel,
        grid=(b, t // tm),
        in_specs=[pl.BlockSpec((1, tm, d), lambda i, j: (i, j, 0)), const(g2), per_b, per_b, const(w),
                  const(q_gain), const(k_gain), const(ones_q), const(ones_k), tab, tab],
        out_specs=[pl.BlockSpec((1, tm, n), lambda i, j: (i, j, 0)) for n in widths],
        out_shape=[jax.ShapeDtypeStruct((b, t, n), BF16) for n in widths],
        compiler_params=_params(2),
        name=name,
    )(x, g2, shift.reshape(b, 1, d), scale.reshape(b, 1, d), w, q_gain, k_gain, ones_q, ones_k, cos, sin)


def _attn_kernel(*refs, dc, chunks, n_src, masked_q):
    q_ref, kv_refs, o_ref = refs[0], refs[1:1 + 2 * n_src], refs[-1]
    tq = q_ref.shape[1]
    lane = lax.broadcasted_iota(jnp.int32, (tq, LANES), 1)
    accs = []
    for hh in range(2):
        if masked_q:
            qp = q_ref[0]
            q = jnp.where((lane < HEAD_W) if hh == 0 else (lane >= HEAD_W), qp, jnp.zeros_like(qp))
        else:
            q = q_ref[0, :, hh * dc:(hh + 1) * dc]
        m = acc = None
        for src, s0, n in chunks:
            kk = kv_refs[2 * src][0, s0:s0 + n, :]
            vv = kv_refs[2 * src + 1][0, s0:s0 + n, hh * LANES:(hh + 1) * LANES]
            s = lax.dot_general(q, kk, NT_DIMS, preferred_element_type=F32)
            mc = jnp.max(s, axis=-1, keepdims=True)
            if m is None:
                m_new = mc
                acc = jnp.dot(jnp.exp2(s - m_new).astype(BF16), vv, preferred_element_type=F32)
            else:
                m_new = jnp.maximum(m, mc)
                acc = jnp.exp2(m - m_new) * acc + jnp.dot(jnp.exp2(s - m_new).astype(BF16), vv,
                                                          preferred_element_type=F32)
            m = m_new
        accs.append(acc)
    low = lane < HEAD_W
    num = jnp.where(low, accs[0], accs[1])
    den = pltpu.roll(jnp.where(low, accs[1], accs[0]), HEAD_W, 1)
    o_ref[0] = (num / den).astype(o_ref.dtype)


def _attn_call(q, kvs, dc, n_pairs, n_groups, chunks, tq, masked_q, name):
    b, t_q, _ = q.shape
    per = n_pairs // n_groups
    q_w = LANES if masked_q else 2 * dc
    in_specs = [pl.BlockSpec((1, tq, q_w), lambda i, p, j: (i, j, p))]
    args = [q]
    for k, v in kvs:
        in_specs += [pl.BlockSpec((1, k.shape[1], dc), lambda i, p, j: (i, 0, p // per)),
                     pl.BlockSpec((1, v.shape[1], 2 * LANES), lambda i, p, j: (i, 0, p // per))]
        args += [k, v]
    return pl.pallas_call(
        functools.partial(_attn_kernel, dc=dc, chunks=tuple(chunks), n_src=len(kvs), masked_q=masked_q),
        grid=(b, n_pairs, t_q // tq),
        in_specs=in_specs,
        out_specs=pl.BlockSpec((1, tq, LANES), lambda i, p, j: (i, j, p)),
        out_shape=jax.ShapeDtypeStruct((b, t_q, n_pairs * LANES), BF16),
        compiler_params=_params(3),
        name=name,
    )(*args)


NA_QROWS = 4
NA_KROWS = 12


def _na_kernel(q_ref, k_ref, v_ref, kc_ref, vc_ref, bias_ref, o_ref):
    rb = pl.program_id(1)
    n_rows = k_ref.shape[1] // GRID_W
    base = jnp.clip(rb * NA_QROWS - NA_WIN_H // 2, 0, n_rows - NA_KROWS) * GRID_W
    base = pl.multiple_of(base, GRID_W)
    nq = NA_QROWS * GRID_W
    nk = NA_KROWS * GRID_W
    lane = lax.broadcasted_iota(jnp.int32, (nq, LANES), 1)
    for p in range(NA_HEADS // 2):
        cols = slice(p * LANES, (p + 1) * LANES)
        qp = q_ref[0, :, cols]
        kw = k_ref[0, pl.ds(base, nk), cols]
        vw = v_ref[0, pl.ds(base, nk), cols]
        kc = kc_ref[0, :, cols]
        vc = vc_ref[0, :, cols]
        outs = []
        for hh in range(2):
            sel = (lane < HEAD_W) if hh == 0 else (lane >= HEAD_W)
            qm = jnp.where(sel, qp, jnp.zeros_like(qp))
            s_loc = lax.dot_general(qm, kw, NT_DIMS, preferred_element_type=F32) + bias_ref[0, 2 * p + hh]
            s_ctx = lax.dot_general(qm, kc, NT_DIMS, preferred_element_type=F32)
            m = jnp.maximum(jnp.max(s_loc, axis=-1, keepdims=True), jnp.max(s_ctx, axis=-1, keepdims=True))
            p_loc = jnp.exp(s_loc - m)
            p_ctx = jnp.exp(s_ctx - m)
            l = jnp.sum(p_loc, axis=-1, keepdims=True) + jnp.sum(p_ctx, axis=-1, keepdims=True)
            acc = (jnp.dot(p_loc.astype(BF16), vw, preferred_element_type=F32)
                   + jnp.dot(p_ctx.astype(BF16), vc, preferred_element_type=F32))
            outs.append(acc / l)
        o_ref[0, :, cols] = jnp.where(lane < HEAD_W, outs[0], outs[1]).astype(o_ref.dtype)


def _na_bias_table(rpb, rows):
    n_blocks = rows // NA_QROWS
    n_h = rpb.shape[0]
    c = np.arange(GRID_W)[:, None]
    kc = np.arange(GRID_W)[None, :]
    cs = np.clip(c - NA_WIN_W // 2, 0, GRID_W - NA_WIN_W)
    col_ok = (kc >= cs) & (kc < cs + NA_WIN_W)
    dc = np.clip(kc - c + (NA_WIN_W - 1), 0, 2 * NA_WIN_W - 2)
    pick = (np.arange(2 * NA_WIN_W - 1)[:, None] == dc.reshape(-1)[None, :]).astype(np.float32)
    variants = []
    for r0 in (0, NA_QROWS * (n_blocks // 2), rows - NA_QROWS):
        base = int(np.clip(r0 - NA_WIN_H // 2, 0, rows - NA_KROWS))
        r = r0 + np.arange(NA_QROWS)[:, None]
        kr = base + np.arange(NA_KROWS)[None, :]
        rs = np.clip(r - NA_WIN_H // 2, 0, rows - NA_WIN_H)
        row_ok = (kr >= rs) & (kr < rs + NA_WIN_H)
        dr = np.clip(kr - r + (NA_WIN_H - 1), 0, 2 * NA_WIN_H - 2)
        by_row = rpb[:, dr.reshape(-1), :].astype(F32)
        full = jnp.einsum("hxd,dm->hxm", by_row, jnp.asarray(pick), precision=lax.Precision.HIGHEST)
        full = full.reshape(n_h, NA_QROWS, NA_KROWS, GRID_W, GRID_W).transpose(0, 1, 3, 2, 4)
        full = full.reshape(n_h, NA_QROWS * GRID_W, NA_KROWS * GRID_W)
        valid = (row_ok[:, None, :, None] & col_ok[None, :, None, :]).reshape(full.shape[1:])
        variants.append(jnp.where(valid[None], full, MASK_VALUE))
    return jnp.stack(variants, 0)


def _na_call(q, k, v, kc, vc, bias):
    b, t, w = q.shape
    n_blocks = t // (NA_QROWS * GRID_W)
    nq = NA_QROWS * GRID_W
    tc = kc.shape[1]

    def bias_map(i, r):
        return (jnp.where(r == 0, 0, jnp.where(r == n_blocks - 1, 2, 1)), 0, 0, 0)

    return pl.pallas_call(
        _na_kernel,
        grid=(b, n_blocks),
        in_specs=[pl.BlockSpec((1, nq, w), lambda i, r: (i, r, 0)),
                  pl.BlockSpec((1, t, w), lambda i, r: (i, 0, 0)),
                  pl.BlockSpec((1, t, w), lambda i, r: (i, 0, 0)),
                  pl.BlockSpec((1, tc, w), lambda i, r: (i, 0, 0)),
                  pl.BlockSpec((1, tc, w), lambda i, r: (i, 0, 0)),
                  pl.BlockSpec((1, NA_HEADS, nq, NA_KROWS * GRID_W), bias_map)],
        out_specs=pl.BlockSpec((1, nq, w), lambda i, r: (i, r, 0)),
        out_shape=jax.ShapeDtypeStruct((b, t, w), BF16),
        compiler_params=_params(2),
        name="na_attn",
    )(q, k, v, kc, vc, bias)


GDN_STEP_CHUNKS = 4
GDN_BLOCK = GDN_STEP_CHUNKS * GDN_CHUNK
GDN_HALO = 16
GDN_PAIRS = GDN_HEADS // 2
TN_DIMS = (((0,), (0,)), ((), ()))


def _split3(x):
    hi = x.astype(BF16)
    r = x - hi.astype(F32)
    mid = r.astype(BF16)
    lo = (r - mid.astype(F32)).astype(BF16)
    return hi, mid, lo


def _bd(x):
    lane = lax.broadcasted_iota(jnp.int32, x.shape, 1)
    z = jnp.zeros_like(x)
    return jnp.concatenate([jnp.where(lane < HEAD_W, x, z), jnp.where(lane >= HEAD_W, x, z)], axis=0)


def _dotb(a, b):
    return jnp.dot(a.astype(BF16), b.astype(BF16), preferred_element_type=F32)


def _mm_split(x, y):
    xh = x.astype(BF16)
    xl = (x - xh.astype(F32)).astype(BF16)
    yb = _bd(y)
    yh = yb.astype(BF16)
    yl = (yb - yh.astype(F32)).astype(BF16)
    return jnp.dot(jnp.concatenate([xh, xl, xh], axis=1), jnp.concatenate([yh, yh, yl], axis=0),
                   preferred_element_type=F32)


def _softplus(x):
    return jnp.maximum(x, 0.0) + jnp.log(1.0 + jnp.exp(-jnp.abs(x)))


def _gdn_kernel(cur_ref, prev_ref, next_ref, misc_ref, cw_ref, alog_ref, dtb_ref, eg_ref, eb_ref, ones_ref,
                tri3_ref, s0_ref, o_ref, sfin_ref, xs_ref, s_ref, *, rev, n_blk, n_tap):
    step = pl.program_id(1)
    blk = (n_blk - 1 - step) if rev else step
    c_len = GDN_CHUNK
    hw = GDN_HEADS * HEAD_W

    @pl.when(step == 0)
    def _():
        s_ref[...] = s0_ref[0]

    has_prev = (blk > 0).astype(F32)
    has_next = (blk < n_blk - 1).astype(F32)
    xs_ref[0:GDN_HALO, :] = prev_ref[0].astype(F32) * has_prev
    xs_ref[GDN_HALO:GDN_HALO + GDN_BLOCK, :] = cur_ref[0].astype(F32)
    xs_ref[GDN_HALO + GDN_BLOCK:, :] = next_ref[0].astype(F32) * has_next
    acc = None
    for j in range(n_tap):
        start = GDN_HALO - n_tap // 2 + j
        term = xs_ref[start:start + GDN_BLOCK, :] * cw_ref[j:j + 1, :]
        acc = term if acc is None else acc + term
    y = acc * jax.nn.sigmoid(acc)
    q, k, v = y[:, 0:hw], y[:, hw:2 * hw], y[:, 2 * hw:3 * hw]

    def l2n(x):
        x2 = x * x
        hi = x2.astype(BF16)
        mid = (x2 - hi.astype(F32)).astype(BF16)
        ss = jnp.dot(jnp.concatenate([hi, mid], axis=1), ones_ref[...], preferred_element_type=F32)
        return x * lax.rsqrt(ss + NORM_EPS)

    qn = l2n(q) * (GDN_DK ** -0.5)
    kn = l2n(k)

    misc = misc_ref[0]
    g_all = -jnp.exp(alog_ref[...]) * _softplus(misc + dtb_ref[...])
    b_all = jax.nn.sigmoid(misc)
    gx = jnp.dot(jnp.concatenate(_split3(g_all), axis=1), eg_ref[...], preferred_element_type=F32)
    bx = jnp.dot(jnp.concatenate(_split3(b_all), axis=1), eb_ref[...], preferred_element_type=F32)

    row = lax.broadcasted_iota(jnp.int32, (c_len, LANES), 0)
    colp = lax.broadcasted_iota(jnp.int32, (c_len, LANES), 1) & (HEAD_W - 1)
    tri = (row <= colp) if rev else (row >= colp)
    strict = (row < colp) if rev else (row > colp)
    eye_f = (row == colp).astype(F32)
    row_w = lax.broadcasted_iota(jnp.int32, (c_len, hw), 0)
    col_w = lax.broadcasted_iota(jnp.int32, (c_len, hw), 1) & (HEAD_W - 1)
    eye_w = row_w == col_w
    bd_row = lax.broadcasted_iota(jnp.int32, (LANES, LANES), 0)
    bd_col = lax.broadcasted_iota(jnp.int32, (LANES, LANES), 1)
    bd_mask = (bd_row < HEAD_W) == (bd_col < HEAD_W)
    ones8 = jnp.ones((8, 3 * c_len), BF16)

    keys = [(c, p) for c in range(GDN_STEP_CHUNKS) for p in range(GDN_PAIRS)]
    sl = [slice(p * LANES, (p + 1) * LANES) for p in range(GDN_PAIRS)]
    gtot, vb, kbg, qd, kd = {}, {}, {}, {}, {}
    ms, qks = {}, {}
    for c in range(GDN_STEP_CHUNKS):
        r = slice(c * c_len, (c + 1) * c_len)
        gc = jnp.dot(tri3_ref[...], jnp.concatenate(_split3(gx[r]), axis=0), preferred_element_type=F32)
        gtot[c] = gc[0:1] if rev else gc[c_len - 1:c_len]
        egc = jnp.exp(gc)
        kc = kn[r]
        kb = kc * bx[r]
        vb[c] = v[r] * bx[r]
        kbg[c] = kb * egc
        qd[c] = qn[r] * egc
        kd[c] = kc * jnp.exp(gtot[c] - gc)
        dg = jnp.where(eye_w, gc, 0.0)
        gr = jnp.dot(ones8, jnp.concatenate(_split3(dg), axis=0), preferred_element_type=F32)[0:1]
        for p in range(GDN_PAIRS):
            diff = gc[:, sl[p]] - gr[:, sl[p]]
            decay = jnp.where(tri, jnp.exp(jnp.where(tri, diff, 0.0)), 0.0)
            lhs = jnp.concatenate([kb[:, sl[p]], qn[r][:, sl[p]]], axis=0)
            aq = lax.dot_general(lhs.astype(BF16), _bd(kc[:, sl[p]]).astype(BF16), NT_DIMS,
                                 preferred_element_type=F32)
            ms[c, p] = jnp.where(strict, aq[0:c_len] * decay, 0.0)
            qks[c, p] = aq[c_len:2 * c_len] * decay

    rblk, cblk = row, colp
    xs = {}
    for lvl in range(6):
        sib = (cblk == rblk + 1) if rev else (cblk == rblk - 1)
        odd = (rblk & 1) == (0 if rev else 1)
        join = sib & odd
        for key in keys:
            c_s = jnp.where(join, ms[key], 0.0)
            if lvl == 0:
                xs[key] = eye_f - c_s
            else:
                xs[key] = xs[key] - _mm_split(xs[key], _mm_split(c_s, xs[key]))
        rblk, cblk = rblk >> 1, cblk >> 1
    uws = {}
    for c, p in keys:
        rhs = jnp.concatenate([_bd(vb[c][:, sl[p]]), _bd(kbg[c][:, sl[p]])], axis=1)
        uws[c, p] = _dotb(xs[c, p], rhs)

    order = range(GDN_STEP_CHUNKS - 1, -1, -1) if rev else range(GDN_STEP_CHUNKS)
    state = [s_ref[p] for p in range(GDN_PAIRS)]
    for c in order:
        r = slice(c * c_len, (c + 1) * c_len)
        for p in range(GDN_PAIRS):
            u, w = uws[c, p][:, 0:LANES], uws[c, p][:, LANES:2 * LANES]
            wq = _dotb(jnp.concatenate([w, qd[c][:, sl[p]]], axis=0), state[p])
            v_new = u - wq[0:c_len]
            o_ref[0, r, sl[p]] = wq[c_len:2 * c_len] + _dotb(qks[c, p], _bd(v_new))
            upd = lax.dot_general(kd[c][:, sl[p]].astype(BF16), v_new.astype(BF16), TN_DIMS,
                                  preferred_element_type=F32)
            state[p] = state[p] * jnp.exp(gtot[c][:, sl[p]]) + jnp.where(bd_mask, upd, 0.0)
    for p in range(GDN_PAIRS):
        s_ref[p] = state[p]

    @pl.when(step == n_blk - 1)
    def _():
        for p in range(GDN_PAIRS):
            sfin_ref[0, p] = state[p]


def _gdn_consts(conv_w, a_log, dt_bias, d, rev):
    n_tap, w = conv_w.shape
    cw = jnp.concatenate([conv_w.astype(F32), jnp.zeros((8 - n_tap, w), F32)], 0)
    pad = jnp.zeros((LANES - 2 * GDN_HEADS,), F32)
    alog = jnp.concatenate([a_log.reshape(-1).astype(F32), pad]).reshape(1, LANES)
    dtb = jnp.concatenate([dt_bias.reshape(-1).astype(F32), pad]).reshape(1, LANES)
    hw = GDN_HEADS * HEAD_W
    head_of_lane = np.arange(hw) // HEAD_W
    src = np.arange(LANES)[:, None]
    eg = (src == d * GDN_HEADS + head_of_lane[None, :]).astype(np.float32)
    eb = (src == 2 * GDN_HEADS + d * GDN_HEADS + head_of_lane[None, :]).astype(np.float32)
    ones_bd = (head_of_lane[:, None] == head_of_lane[None, :]).astype(np.float32)
    i = np.arange(GDN_CHUNK)
    tri = (i[:, None] <= i[None, :]) if rev else (i[:, None] >= i[None, :])
    as_bf = lambda a, reps, ax: jnp.asarray(np.concatenate([a] * reps, axis=ax), BF16)
    return cw, alog, dtb, as_bf(eg, 3, 0), as_bf(eb, 3, 0), as_bf(ones_bd, 2, 0), as_bf(tri.astype(np.float32), 3, 1)


def _gdn_call(qkv, misc, consts, s0, rev, name):
    b, t, w = qkv.shape
    n_blk = t // GDN_BLOCK
    per = GDN_BLOCK // GDN_HALO
    n_halo = t // GDN_HALO
    cw, alog, dtb, eg, eb, ones_bd, tri3 = consts
    n_tap = 5
    hw = GDN_HEADS * HEAD_W
    blk_of = (lambda s: n_blk - 1 - s) if rev else (lambda s: s)
    const = lambda a: pl.BlockSpec(a.shape, lambda i, s: (0,) * a.ndim)
    s_spec = pl.BlockSpec((1, GDN_PAIRS, LANES, LANES), lambda i, s: (i, 0, 0, 0))
    return pl.pallas_call(
        functools.partial(_gdn_kernel, rev=rev, n_blk=n_blk, n_tap=n_tap),
        grid=(b, n_blk),
        in_specs=[pl.BlockSpec((1, GDN_BLOCK, w), lambda i, s: (i, blk_of(s), 0)),
                  pl.BlockSpec((1, GDN_HALO, w), lambda i, s: (i, jnp.maximum(blk_of(s) * per - 1, 0), 0)),
                  pl.BlockSpec((1, GDN_HALO, w), lambda i, s: (i, jnp.minimum((blk_of(s) + 1) * per, n_halo - 1), 0)),
                  pl.BlockSpec((1, GDN_BLOCK, LANES), lambda i, s: (i, blk_of(s), 0)),
                  const(cw), const(alog), const(dtb), const(eg), const(eb), const(ones_bd), const(tri3), s_spec],
        out_specs=[pl.BlockSpec((1, GDN_BLOCK, hw), lambda i, s: (i, blk_of(s), 0)), s_spec],
        out_shape=[jax.ShapeDtypeStruct((b, t, hw), F32),
                   jax.ShapeDtypeStruct((b, GDN_PAIRS, LANES, LANES), F32)],
        scratch_shapes=[pltpu.VMEM((GDN_BLOCK + 2 * GDN_HALO, w), F32),
                        pltpu.VMEM((GDN_PAIRS, LANES, LANES), F32)],
        compiler_params=_params(2),
        name=name,
    )(qkv, qkv, qkv, misc, cw, alog, dtb, eg, eb, ones_bd, tri3, s0)


def _mix_ffn_kernel(*refs, ff_chunks, gdn_inputs):
    if gdn_inputs:
        (h_ref, of_ref, ob_ref, gate_ref, att_ref, on_ref, ones_ref, gtm_ref, shf_ref, scf_ref, gtf_ref, gpm_ref,
         gpf_ref, gqf_ref, wo_ref, wg_ref, wu_ref, wd_ref, o_ref) = refs
        o = of_ref[0] + ob_ref[0]
        gate = gate_ref[0].astype(F32)
        a = o * lax.rsqrt(_seg_mean_sq(o, ones_ref) + NORM_EPS) * on_ref[...] * (gate * jax.nn.sigmoid(gate))
        mix = jnp.concatenate([a.astype(BF16), att_ref[0]], axis=1)
    else:
        (h_ref, a_ref, b_ref, gtm_ref, shf_ref, scf_ref, gtf_ref, gpm_ref, gpf_ref, gqf_ref, wo_ref, wg_ref,
         wu_ref, wd_ref, o_ref) = refs
        mix = jnp.concatenate([a_ref[0], b_ref[0]], axis=1)
    y = jnp.dot(mix, wo_ref[...], preferred_element_type=F32)
    h1 = h_ref[0] + gtm_ref[0] * (_rms(y) * gpm_ref[...])
    u = (_rms(h1) * gpf_ref[...] * (1.0 + scf_ref[0]) + shf_ref[0]).astype(BF16)
    f = None
    for o, n in ff_chunks:
        gg = jnp.dot(u, wg_ref[:, o:o + n], preferred_element_type=F32)
        uu = jnp.dot(u, wu_ref[:, o:o + n], preferred_element_type=F32)
        a = (gg * jax.nn.sigmoid(gg) * uu).astype(BF16)
        part = jnp.dot(a, wd_ref[o:o + n, :], preferred_element_type=F32)
        f = part if f is None else f + part
    o_ref[0] = h1 + gtf_ref[0] * (_rms(f) * gqf_ref[...])


def _mix_ffn_call(h, mix_inputs, out_norm, gt_m, sh_f, sc_f, gt_f, g_post_mix, g_pre_ffn, g_post_ffn, wo, wg, wu,
                  wd, tm, name):
    b, t, d = h.shape
    dm = wo.shape[0]
    ff = wg.shape[1]
    gdn_inputs = out_norm is not None
    half = ff // 2
    assert half % LANES == 0
    ff_chunks = ((0, half), (half, ff - half))
    tok = lambda n: pl.BlockSpec((1, tm, n), lambda i, j: (i, j, 0))
    per_b = pl.BlockSpec((1, 1, d), lambda i, j: (i, 0, 0))
    vec = pl.BlockSpec((1, d), lambda i, j: (0, 0))
    const = lambda shape: pl.BlockSpec(shape, lambda i, j: (0, 0), pipeline_mode=pl.Buffered(1))
    r3 = lambda a: a.reshape(b, 1, d)
    r2 = lambda a: a.reshape(1, d).astype(F32)
    mix_specs = [tok(m.shape[-1]) for m in mix_inputs]
    mix_args = list(mix_inputs)
    if gdn_inputs:
        hw = GDN_HEADS * GDN_DV
        on = jnp.tile(out_norm.astype(F32), GDN_HEADS).reshape(1, hw)
        ones = _seg_ones(hw)
        mix_specs += [pl.BlockSpec(on.shape, lambda i, j: (0, 0)), pl.BlockSpec(ones.shape, lambda i, j: (0, 0))]
        mix_args += [on, ones]
    return pl.pallas_call(
        functools.partial(_mix_ffn_kernel, ff_chunks=ff_chunks, gdn_inputs=gdn_inputs),
        grid=(b, t // tm),
        in_specs=[tok(d)] + mix_specs + [per_b, per_b, per_b, per_b, vec, vec, vec,
                                         const((dm, d)), const((d, ff)), const((d, ff)), const((ff, d))],
        out_specs=tok(d),
        out_shape=jax.ShapeDtypeStruct((b, t, d), F32),
        compiler_params=_params(2),
        name=name,
    )(h, *mix_args, r3(gt_m), r3(sh_f), r3(sc_f), r3(gt_f), r2(g_post_mix), r2(g_pre_ffn), r2(g_post_ffn),
      wo, wg, wu, wd)


def _rope_tables(n_tokens, rot_dim):
    t = jnp.arange(n_tokens, dtype=jnp.int32)
    row = (t // GRID_W).astype(F32)
    col = (t % GRID_W).astype(F32)
    n_freq = rot_dim // 4
    inv_freq = ROPE_THETA ** (-jnp.arange(n_freq, dtype=F32) / n_freq)
    ang = jnp.concatenate([row[:, None] * inv_freq, col[:, None] * inv_freq], -1)
    cos = jnp.concatenate([jnp.cos(ang), jnp.cos(ang)], -1)
    sin = jnp.concatenate([-jnp.sin(ang), jnp.sin(ang)], -1)
    if rot_dim == HEAD_W:
        return jnp.tile(cos, (1, LANES // rot_dim)), jnp.tile(sin, (1, LANES // rot_dim))
    pad = LANES - rot_dim
    return (jnp.concatenate([cos, jnp.ones((n_tokens, pad), F32)], -1),
            jnp.concatenate([sin, jnp.zeros((n_tokens, pad), F32)], -1))


def _no_rope_tables(n_tokens):
    return jnp.ones((n_tokens, LANES), F32), jnp.zeros((n_tokens, LANES), F32)


def _gdn_bidirectional(lat_qkv, lat_misc, ctx_qkv, ctx_misc, conv_w, a_log, dt_bias):
    bsz = lat_qkv.shape[0]
    s_zero = jnp.zeros((bsz, GDN_PAIRS, LANES, LANES), F32)
    lat, ctx = [], []
    for d, rev in ((0, False), (1, True)):
        consts = _gdn_consts(conv_w, a_log, dt_bias, d, rev)
        o_c, s_c = _gdn_call(ctx_qkv, ctx_misc, consts, s_zero, rev, "gdn_ctx_%d" % d)
        o_l, _ = _gdn_call(lat_qkv, lat_misc, consts, s_c, rev, "gdn_lat_%d" % d)
        lat.append(o_l)
        ctx.append(o_c)
    return lat, ctx


LOG2E = 1.4426950408889634
ATTN_TQ = 1024


def _attn_chunks(l_len, s_len, size=512):
    return [(0, 0, l_len)] + [(1, o, size) for o in range(0, s_len, size)]


def _even_mixer(h, hc, mods, mods_c, g_pre, w_in, conv_w, a_log, dt_bias, q_norm, w_q_up, kv_norm, w_kv_up):
    s_len, l_len = h.shape[1], hc.shape[1]
    weights = _even_weights(w_in, q_norm, w_q_up, kv_norm, w_kv_up)
    qkv_l, gate_l, misc_l, q_l, k_l, v_l = _in_even_call(h, g_pre, *mods, weights, *_rope_tables(s_len, MLA_ROPE),
                                                         512, "in_even_lat")
    qkv_c, gate_c, misc_c, q_c, k_c, v_c = _in_even_call(hc, g_pre, *mods_c, weights, *_no_rope_tables(l_len),
                                                         l_len, "in_even_ctx")
    o_lat, o_ctx = _gdn_bidirectional(qkv_l, misc_l, qkv_c, misc_c, conv_w, a_log, dt_bias)
    b_lat = _attn_call(q_l, [(k_c, v_c), (k_l, v_l)], MLA_QC, MLA_PAIRS, MLA_PAIRS, _attn_chunks(l_len, s_len),
                       ATTN_TQ, False, "mla_attn_lat")
    b_ctx = _attn_call(q_c, [(k_c, v_c)], MLA_QC, MLA_PAIRS, MLA_PAIRS, [(0, 0, l_len)], l_len, False,
                       "mla_attn_ctx")
    return (o_lat[0], o_lat[1], gate_l, b_lat), (o_ctx[0], o_ctx[1], gate_c, b_ctx)


def _odd_mixer_last(h, hc, mods, mods_c, g_pre, w_in, rpb, q_norm, k_norm):
    s_len, l_len = h.shape[1], hc.shape[1]
    w = w_in.astype(BF16)
    q_gain = (jnp.tile(q_norm.astype(F32), GQA_HEADS) * (GQA_DIM ** -0.5 * LOG2E)).reshape(1, -1)
    k_gain = jnp.tile(k_norm.astype(F32), GQA_KV_HEADS).reshape(1, -1)
    naq_l, nak_l, nav_l, gq_l, gk_l, gv_l = _in_odd_call(h, g_pre, *mods, w, q_gain, k_gain,
                                                         *_rope_tables(s_len, GQA_DIM), 512, "in_odd_lat")
    _, nak_c, nav_c, _, gk_c, gv_c = _in_odd_call(hc, g_pre, *mods_c, w, q_gain, k_gain, *_no_rope_tables(l_len),
                                                  l_len, "in_odd_ctx")
    c_lat = _na_call(naq_l, nak_l, nav_l, nak_c, nav_c, _na_bias_table(rpb, s_len // GRID_W))
    d_lat = _attn_call(gq_l, [(gk_c, gv_c), (gk_l, gv_l)], LANES, GQA_HEADS // 2, GQA_KV_HEADS,
                       _attn_chunks(l_len, s_len), ATTN_TQ, True, "gqa_attn")
    return c_lat, d_lat


def kernel(x, c, ctx, c_ctx, w_mod, b_mod, g_pre_mix, g_post_mix, g_pre_ffn, g_post_ffn, w_ffn_gate, w_ffn_up,
           w_ffn_down, w_in_even, w_out_even, gdn_conv, gdn_a_log, gdn_dt_bias, gdn_out_norm, mla_q_norm,
           mla_w_q_up, mla_kv_norm, mla_w_kv_up, w_in_odd, w_out_odd, na_rpb, gqa_q_norm, gqa_k_norm):
    bsz, s_len, d = x.shape
    depth = w_mod.shape[0]
    assert depth == 2, "layer 0 = even mixer with context update, layer 1 = odd mixer (last)"
    c_rows = jnp.concatenate([c, c_ctx[None, :], jnp.zeros((16 - bsz - 1, d), F32)], 0)
    h, hc = x, ctx
    for i in range(depth):
        mod_all = _mod_call(c_rows, w_mod[i].astype(BF16), b_mod[i])
        mod = jnp.split(mod_all[:bsz], 6, axis=-1)
        mod_c = [jnp.broadcast_to(m, (bsz, d)) for m in jnp.split(mod_all[bsz:bsz + 1], 6, axis=-1)]
        sh_m, sc_m, gt_m, sh_f, sc_f, gt_f = mod
        csh_m, csc_m, cgt_m, csh_f, csc_f, cgt_f = mod_c
        wg, wu, wd = w_ffn_gate[i].astype(BF16), w_ffn_up[i].astype(BF16), w_ffn_down[i].astype(BF16)
        if i == 0:
            mix, mix_c = _even_mixer(h, hc, (sh_m, sc_m), (csh_m, csc_m), g_pre_mix[i], w_in_even[0], gdn_conv[0],
                                     gdn_a_log[0], gdn_dt_bias[0], mla_q_norm[0], mla_w_q_up[0], mla_kv_norm[0],
                                     mla_w_kv_up[0])
            wo, out_norm = w_out_even[0].astype(BF16), gdn_out_norm[0]
            hc = _mix_ffn_call(hc, mix_c, out_norm, cgt_m, csh_f, csc_f, cgt_f, g_post_mix[i], g_pre_ffn[i],
                               g_post_ffn[i], wo, wg, wu, wd, 256, "mix_ffn_ctx")
        else:
            mix = _odd_mixer_last(h, hc, (sh_m, sc_m), (csh_m, csc_m), g_pre_mix[i], w_in_odd[0], na_rpb[0],
                                  gqa_q_norm[0], gqa_k_norm[0])
            wo, out_norm = w_out_odd[0].astype(BF16), None
        h = _mix_ffn_call(h, mix, out_norm, gt_m, sh_f, sc_f, gt_f, g_post_mix[i], g_pre_ffn[i], g_post_ffn[i],
                          wo, wg, wu, wd, 256, "mix_ffn_lat%d" % i)
    return h
```

```python
import functools

import numpy as np
import jax
import jax.numpy as jnp
from jax import lax
from jax.experimental import pallas as pl
from jax.experimental.pallas import tpu as pltpu

F32 = jnp.float32
BF16 = jnp.bfloat16

GRID_W = 64
NORM_EPS = 1e-6
ROPE_THETA = 10000.0

GDN_HEADS = 8
GDN_DK = 64
GDN_DV = 64
GDN_CHUNK = 64
GDN_QK_W = GDN_HEADS * GDN_DK
GDN_V_W = GDN_HEADS * GDN_DV

MLA_HEADS = 8
MLA_NOPE = 64
MLA_ROPE = 32
MLA_V = 64
MLA_Q_LORA = 256
MLA_KV_LORA = 128

NA_HEADS = 8
NA_DIM = 64
NA_WIN_H = 8
NA_WIN_W = 16
NA_W = NA_HEADS * NA_DIM

GQA_HEADS = 8
GQA_KV_HEADS = 2
GQA_DIM = 64

LANES = 128
HEAD_W = 64
MASK_VALUE = -1e30
VMEM_LIMIT = 56 << 20

NT_DIMS = (((1,), (1,)), ((), ()))


def _params(n_grid, vmem=VMEM_LIMIT):
    return pltpu.CompilerParams(dimension_semantics=("arbitrary",) * n_grid, vmem_limit_bytes=vmem)


def _rms(x):
    return x * lax.rsqrt(jnp.mean(x * x, axis=-1, keepdims=True) + NORM_EPS)


def _mod_kernel(c_ref, w_ref, b_ref, o_ref):
    c = c_ref[...]
    a = (c * jax.nn.sigmoid(c)).astype(BF16)
    o_ref[...] = jnp.dot(a, w_ref[...], preferred_element_type=F32) + b_ref[...]


def _mod_call(c_rows, w, b):
    r, d = c_rows.shape
    n = w.shape[1]
    tn = 1024
    return pl.pallas_call(
        _mod_kernel,
        grid=(n // tn,),
        in_specs=[pl.BlockSpec((r, d), lambda j: (0, 0)),
                  pl.BlockSpec((d, tn), lambda j: (0, j)),
                  pl.BlockSpec((1, tn), lambda j: (0, j))],
        out_specs=pl.BlockSpec((r, tn), lambda j: (0, j)),
        out_shape=jax.ShapeDtypeStruct((r, n), F32),
        compiler_params=_params(1),
        name="mod",
    )(c_rows, w, b.reshape(1, n))


def _norm_mod(x_ref, g_ref, sh_ref, sc_ref):
    return (_rms(x_ref[0]) * g_ref[...] * (1.0 + sc_ref[0]) + sh_ref[0]).astype(BF16)


def _dotf(a, b):
    return jnp.dot(a, b, preferred_element_type=F32)


def _seg_mean_sq(x, ones_ref):
    x2 = x * x
    hi = x2.astype(BF16)
    mid = (x2 - hi.astype(F32)).astype(BF16)
    return _dotf(jnp.concatenate([hi, mid], axis=1), ones_ref[...]) * (1.0 / HEAD_W)


def _rotate(t, cos, sin_signed, half):
    lane = lax.broadcasted_iota(jnp.int32, t.shape, 1)
    n = t.shape[1]
    partner = jnp.where((lane & (2 * half - 1)) < half, pltpu.roll(t, n - half, 1), pltpu.roll(t, half, 1))
    return t * cos + partner * sin_signed


EVEN_QKV_W = 2 * GDN_QK_W + GDN_V_W
EVEN_GROUPS = (EVEN_QKV_W, GDN_V_W, MLA_Q_LORA, MLA_KV_LORA, LANES, LANES)
MLA_QC = 2 * LANES
MLA_PAIRS = MLA_HEADS // 2


def _in_even_kernel(x_ref, g_ref, sh_ref, sc_ref, w_ref, qg_ref, wq_ref, kvg_ref, wk_ref, wv_ref, vone_ref,
                    cos_ref, sin_ref, qkv_ref, gate_ref, misc_ref, q_ref, k_ref, v_ref):
    u = _norm_mod(x_ref, g_ref, sh_ref, sc_ref)
    offs = np.cumsum((0,) + EVEN_GROUPS)
    grp = lambda i: _dotf(u, w_ref[:, int(offs[i]):int(offs[i + 1])])
    qkv_ref[0] = grp(0).astype(BF16)
    gate_ref[0] = grp(1).astype(BF16)
    misc_ref[0] = grp(4)
    cos, sin = cos_ref[...], sin_ref[...]
    qn = (_rms(grp(2)) * qg_ref[...]).astype(BF16)
    for h in range(MLA_HEADS):
        blk = _dotf(qn, wq_ref[:, h * MLA_QC:(h + 1) * MLA_QC])
        q_ref[0, :, h * MLA_QC:h * MLA_QC + LANES] = blk[:, 0:LANES].astype(BF16)
        q_ref[0, :, h * MLA_QC + LANES:(h + 1) * MLA_QC] = _rotate(blk[:, LANES:], cos, sin,
                                                                    MLA_ROPE // 2).astype(BF16)
    kvn = (_rms(grp(3)) * kvg_ref[...]).astype(BF16)
    pe = _rotate(grp(5), cos, sin, MLA_ROPE // 2).astype(BF16)
    k_nope = _dotf(kvn, wk_ref[...])
    for p in range(MLA_PAIRS):
        k_ref[0, :, p * MLA_QC:p * MLA_QC + LANES] = k_nope[:, p * LANES:(p + 1) * LANES].astype(BF16)
        k_ref[0, :, p * MLA_QC + LANES:(p + 1) * MLA_QC] = pe
    v_ref[0] = (_dotf(kvn, wv_ref[...]) + vone_ref[...]).astype(BF16)


def _even_weights(w_in, q_norm, w_q_up, kv_norm, w_kv_up):
    d = w_in.shape[0]
    o0 = EVEN_QKV_W + GDN_V_W
    n_ab = 4 * GDN_HEADS
    o1 = o0 + n_ab
    o2 = o1 + MLA_Q_LORA
    o3 = o2 + MLA_KV_LORA
    zeros = lambda n, rows=d: jnp.zeros((rows, n), F32)
    w_cat = jnp.concatenate([w_in[:, :o0], w_in[:, o1:o2], w_in[:, o2:o3],
                             w_in[:, o0:o1], zeros(LANES - n_ab),
                             w_in[:, o3:], zeros(LANES - MLA_ROPE)], 1).astype(BF16)
    qh = MLA_NOPE + MLA_ROPE
    cols = []
    for h in range(MLA_HEADS):
        nope = w_q_up[:, h * qh:h * qh + MLA_NOPE]
        rp = w_q_up[:, h * qh + MLA_NOPE:(h + 1) * qh]
        z = zeros(HEAD_W, MLA_Q_LORA)
        cols += ([nope, z] if h % 2 == 0 else [z, nope]) + [rp, zeros(LANES - MLA_ROPE, MLA_Q_LORA)]
    wq = jnp.concatenate(cols, 1).astype(BF16)
    kvh = MLA_NOPE + MLA_V
    wk = jnp.concatenate([w_kv_up[:, h * kvh:h * kvh + MLA_NOPE] for h in range(MLA_HEADS)], 1).astype(BF16)
    vcols, ones = [], []
    for h in range(MLA_HEADS):
        vh = w_kv_up[:, h * kvh + MLA_NOPE:(h + 1) * kvh]
        z = zeros(HEAD_W, MLA_KV_LORA)
        vcols += [vh, z] if h % 2 == 0 else [z, vh]
        ones += [0.0, 1.0] if h % 2 == 0 else [1.0, 0.0]
    wv = jnp.concatenate(vcols, 1).astype(BF16)
    vone = jnp.asarray(np.repeat(np.asarray(ones, np.float32), HEAD_W)[None, :])
    q_gain = (q_norm * ((MLA_NOPE + MLA_ROPE) ** -0.5 * LOG2E)).reshape(1, -1).astype(F32)
    return w_cat, q_gain, wq, kv_norm.reshape(1, -1).astype(F32), wk, wv, vone


def _in_even_call(x, g, shift, scale, weights, cos, sin, tm, name):
    b, t, d = x.shape
    w_cat, q_gain, wq, kv_gain, wk, wv, vone = weights
    const = lambda a: pl.BlockSpec(a.shape, lambda i, j: (0,) * a.ndim)
    per_b = pl.BlockSpec((1, 1, d), lambda i, j: (i, 0, 0))
    tab = pl.BlockSpec((tm, LANES), lambda i, j: (j, 0))
    widths = (EVEN_QKV_W, GDN_V_W, LANES, MLA_HEADS * MLA_QC, MLA_PAIRS * MLA_QC, MLA_HEADS * LANES)
    dts = (BF16, BF16, F32, BF16, BF16, BF16)
    g2 = g.reshape(1, d).astype(F32)
    return pl.pallas_call(
        _in_even_kernel,
        grid=(b, t // tm),
        in_specs=[pl.BlockSpec((1, tm, d), lambda i, j: (i, j, 0)), const(g2), per_b, per_b, const(w_cat),
                  const(q_gain), const(wq), const(kv_gain), const(wk), const(wv), const(vone), tab, tab],
        out_specs=[pl.BlockSpec((1, tm, n), lambda i, j: (i, j, 0)) for n in widths],
        out_shape=[jax.ShapeDtypeStruct((b, t, n), dt) for n, dt in zip(widths, dts)],
        compiler_params=_params(2),
        name=name,
    )(x, g2, shift.reshape(b, 1, d), scale.reshape(b, 1, d), w_cat, q_gain, wq, kv_gain, wk, wv, vone, cos, sin)


GQA_KV_W = GQA_KV_HEADS * GQA_DIM


def _in_odd_kernel(x_ref, g_ref, sh_ref, sc_ref, w_ref, qg_ref, kg_ref, ones_q_ref, ones_k_ref, cos_ref, sin_ref,
                   naq_ref, nak_ref, nav_ref, gq_ref, gk_ref, gv_ref):
    u = _norm_mod(x_ref, g_ref, sh_ref, sc_ref)
    grp = lambda lo, n: _dotf(u, w_ref[:, lo:lo + n])
    naq_ref[0] = (grp(0, NA_W) * (NA_DIM ** -0.5)).astype(BF16)
    nak_ref[0] = grp(NA_W, NA_W).astype(BF16)
    nav_ref[0] = grp(2 * NA_W, NA_W).astype(BF16)
    cos, sin = cos_ref[...], sin_ref[...]
    o = 3 * NA_W
    q = grp(o, GQA_HEADS * GQA_DIM)
    qn = q * lax.rsqrt(_seg_mean_sq(q, ones_q_ref) + NORM_EPS) * qg_ref[...]
    for p in range(GQA_HEADS // 2):
        cols = slice(p * LANES, (p + 1) * LANES)
        gq_ref[0, :, cols] = _rotate(qn[:, cols], cos, sin, GQA_DIM // 2).astype(BF16)
    o += GQA_HEADS * GQA_DIM
    k = grp(o, GQA_KV_W)
    kr = _rotate(k * lax.rsqrt(_seg_mean_sq(k, ones_k_ref) + NORM_EPS) * kg_ref[...], cos, sin, GQA_DIM // 2)
    low = lax.broadcasted_iota(jnp.int32, k.shape, 1) < HEAD_W
    k_sw = pltpu.roll(kr, HEAD_W, 1)
    gk_ref[0, :, 0:LANES] = jnp.where(low, kr, k_sw).astype(BF16)
    gk_ref[0, :, LANES:2 * LANES] = jnp.where(low, k_sw, kr).astype(BF16)
    v = grp(o + GQA_KV_W, GQA_KV_W)
    v_sw = pltpu.roll(v, HEAD_W, 1)
    slots = (jnp.where(low, v, 1.0), jnp.where(low, 1.0, v_sw),
             jnp.where(low, v_sw, 1.0), jnp.where(low, 1.0, v))
    for i, s in enumerate(slots):
        gv_ref[0, :, i * LANES:(i + 1) * LANES] = s.astype(BF16)


def _seg_ones(width):
    head = np.arange(width) // HEAD_W
    m = (head[:, None] == head[None, :]).astype(np.float32)
    return jnp.asarray(np.concatenate([m, m], 0), BF16)


def _in_odd_call(x, g, shift, scale, w, q_gain, k_gain, cos, sin, tm, name):
    b, t, d = x.shape
    assert GQA_KV_W == LANES
    ones_q = _seg_ones(GQA_HEADS * GQA_DIM)
    ones_k = _seg_ones(GQA_KV_W)
    const = lambda a: pl.BlockSpec(a.shape, lambda i, j: (0,) * a.ndim)
    per_b = pl.BlockSpec((1, 1, d), lambda i, j: (i, 0, 0))
    tab = pl.BlockSpec((tm, LANES), lambda i, j: (j, 0))
    widths = (NA_W, NA_W, NA_W, GQA_HEADS * GQA_DIM, 2 * LANES, 4 * LANES)
    g2 = g.reshape(1, d).astype(F32)
    return pl.pallas_call(
        _in_odd_kernel,
        grid=(b, t // tm),
        in_specs=[pl.BlockSpec((1, tm, d), lambda i, j: (i, j, 0)), const(g2), per_b, per_b, const(w),
                  const(q_gain), const(k_gain), const(ones_q), const(ones_k), tab, tab],
        out_specs=[pl.BlockSpec((1, tm, n), lambda i, j: (i, j, 0)) for n in widths],
        out_shape=[jax.ShapeDtypeStruct((b, t, n), BF16) for n in widths],
        compiler_params=_params(2),
        name=name,
    )(x, g2, shift.reshape(b, 1, d), scale.reshape(b, 1, d), w, q_gain, k_gain, ones_q, ones_k, cos, sin)


def _attn_kernel(*refs, dc, chunks, n_src, masked_q):
    q_ref, kv_refs, o_ref = refs[0], refs[1:1 + 2 * n_src], refs[-1]
    tq = q_ref.shape[1]
    lane = lax.broadcasted_iota(jnp.int32, (tq, LANES), 1)
    qs = []
    for hh in range(2):
        if masked_q:
            qp = q_ref[0]
            qs.append(jnp.where((lane < HEAD_W) if hh == 0 else (lane >= HEAD_W), qp, jnp.zeros_like(qp)))
        else:
            qs.append(q_ref[0, :, hh * dc:(hh + 1) * dc])
    ms, accs = [None, None], [None, None]

    def scores(hh, src, s0, n):
        return lax.dot_general(qs[hh], kv_refs[2 * src][0, s0:s0 + n, :], NT_DIMS, preferred_element_type=F32)

    def consume(s, hh, src, s0, n):
        vv = kv_refs[2 * src + 1][0, s0:s0 + n, hh * LANES:(hh + 1) * LANES]
        mc = jnp.max(s, axis=-1, keepdims=True)
        if ms[hh] is None:
            m_new = mc
            accs[hh] = jnp.dot(jnp.exp2(s - m_new).astype(BF16), vv, preferred_element_type=F32)
        else:
            m_new = jnp.maximum(ms[hh], mc)
            accs[hh] = jnp.exp2(ms[hh] - m_new) * accs[hh] + jnp.dot(jnp.exp2(s - m_new).astype(BF16), vv,
                                                                     preferred_element_type=F32)
        ms[hh] = m_new

    items = [(hh,) + tuple(c) for c in chunks for hh in range(2)]
    pending = []
    for it in items:
        pending.append((scores(*it), it))
        if len(pending) > ATTN_LOOKAHEAD:
            s, it0 = pending.pop(0)
            consume(s, *it0)
    for s, it0 in pending:
        consume(s, *it0)
    low = lane < HEAD_W
    num = jnp.where(low, accs[0], accs[1])
    den = pltpu.roll(jnp.where(low, accs[1], accs[0]), HEAD_W, 1)
    o_ref[0] = (num / den).astype(o_ref.dtype)


def _attn_call(q, kvs, dc, n_pairs, n_groups, chunks, tq, masked_q, name):
    b, t_q, _ = q.shape
    per = n_pairs // n_groups
    q_w = LANES if masked_q else 2 * dc
    in_specs = [pl.BlockSpec((1, tq, q_w), lambda i, p, j: (i, j, p))]
    args = [q]
    for k, v in kvs:
        in_specs += [pl.BlockSpec((1, k.shape[1], dc), lambda i, p, j: (i, 0, p // per)),
                     pl.BlockSpec((1, v.shape[1], 2 * LANES), lambda i, p, j: (i, 0, p // per))]
        args += [k, v]
    return pl.pallas_call(
        functools.partial(_attn_kernel, dc=dc, chunks=tuple(chunks), n_src=len(kvs), masked_q=masked_q),
        grid=(b, n_pairs, t_q // tq),
        in_specs=in_specs,
        out_specs=pl.BlockSpec((1, tq, LANES), lambda i, p, j: (i, j, p)),
        out_shape=jax.ShapeDtypeStruct((b, t_q, n_pairs * LANES), BF16),
        compiler_params=_params(3),
        name=name,
    )(*args)


NA_QROWS = 4
NA_KROWS = 12


def _na_kernel(q_ref, k_ref, v_ref, kc_ref, vc_ref, bias_ref, o_ref):
    rb = pl.program_id(1)
    n_rows = k_ref.shape[1] // GRID_W
    base = jnp.clip(rb * NA_QROWS - NA_WIN_H // 2, 0, n_rows - NA_KROWS) * GRID_W
    base = pl.multiple_of(base, GRID_W)
    nq = NA_QROWS * GRID_W
    nk = NA_KROWS * GRID_W
    lane = lax.broadcasted_iota(jnp.int32, (nq, LANES), 1)
    for p in range(NA_HEADS // 2):
        cols = slice(p * LANES, (p + 1) * LANES)
        qp = q_ref[0, :, cols]
        kw = k_ref[0, pl.ds(base, nk), cols]
        vw = v_ref[0, pl.ds(base, nk), cols]
        kc = kc_ref[0, :, cols]
        vc = vc_ref[0, :, cols]
        outs = []
        for hh in range(2):
            sel = (lane < HEAD_W) if hh == 0 else (lane >= HEAD_W)
            qm = jnp.where(sel, qp, jnp.zeros_like(qp))
            s_loc = lax.dot_general(qm, kw, NT_DIMS, preferred_element_type=F32) + bias_ref[0, 2 * p + hh]
            s_ctx = lax.dot_general(qm, kc, NT_DIMS, preferred_element_type=F32)
            m = jnp.maximum(jnp.max(s_loc, axis=-1, keepdims=True), jnp.max(s_ctx, axis=-1, keepdims=True))
            p_loc = jnp.exp(s_loc - m)
            p_ctx = jnp.exp(s_ctx - m)
            l = jnp.sum(p_loc, axis=-1, keepdims=True) + jnp.sum(p_ctx, axis=-1, keepdims=True)
            acc = (jnp.dot(p_loc.astype(BF16), vw, preferred_element_type=F32)
                   + jnp.dot(p_ctx.astype(BF16), vc, preferred_element_type=F32))
            outs.append(acc / l)
        o_ref[0, :, cols] = jnp.where(lane < HEAD_W, outs[0], outs[1]).astype(o_ref.dtype)


def _na_bias_table(rpb, rows):
    n_blocks = rows // NA_QROWS
    n_h = rpb.shape[0]
    c = np.arange(GRID_W)[:, None]
    kc = np.arange(GRID_W)[None, :]
    cs = np.clip(c - NA_WIN_W // 2, 0, GRID_W - NA_WIN_W)
    col_ok = (kc >= cs) & (kc < cs + NA_WIN_W)
    dc = np.clip(kc - c + (NA_WIN_W - 1), 0, 2 * NA_WIN_W - 2)
    pick = (np.arange(2 * NA_WIN_W - 1)[:, None] == dc.reshape(-1)[None, :]).astype(np.float32)
    variants = []
    for r0 in (0, NA_QROWS * (n_blocks // 2), rows - NA_QROWS):
        base = int(np.clip(r0 - NA_WIN_H // 2, 0, rows - NA_KROWS))
        r = r0 + np.arange(NA_QROWS)[:, None]
        kr = base + np.arange(NA_KROWS)[None, :]
        rs = np.clip(r - NA_WIN_H // 2, 0, rows - NA_WIN_H)
        row_ok = (kr >= rs) & (kr < rs + NA_WIN_H)
        dr = np.clip(kr - r + (NA_WIN_H - 1), 0, 2 * NA_WIN_H - 2)
        by_row = rpb[:, dr.reshape(-1), :].astype(F32)
        full = jnp.einsum("hxd,dm->hxm", by_row, jnp.asarray(pick), precision=lax.Precision.HIGHEST)
        full = full.reshape(n_h, NA_QROWS, NA_KROWS, GRID_W, GRID_W).transpose(0, 1, 3, 2, 4)
        full = full.reshape(n_h, NA_QROWS * GRID_W, NA_KROWS * GRID_W)
        valid = (row_ok[:, None, :, None] & col_ok[None, :, None, :]).reshape(full.shape[1:])
        variants.append(jnp.where(valid[None], full, MASK_VALUE))
    return jnp.stack(variants, 0)


def _na_call(q, k, v, kc, vc, bias):
    b, t, w = q.shape
    n_blocks = t // (NA_QROWS * GRID_W)
    nq = NA_QROWS * GRID_W
    tc = kc.shape[1]

    def bias_map(i, r):
        return (jnp.where(r == 0, 0, jnp.where(r == n_blocks - 1, 2, 1)), 0, 0, 0)

    return pl.pallas_call(
        _na_kernel,
        grid=(b, n_blocks),
        in_specs=[pl.BlockSpec((1, nq, w), lambda i, r: (i, r, 0)),
                  pl.BlockSpec((1, t, w), lambda i, r: (i, 0, 0)),
                  pl.BlockSpec((1, t, w), lambda i, r: (i, 0, 0)),
                  pl.BlockSpec((1, tc, w), lambda i, r: (i, 0, 0)),
                  pl.BlockSpec((1, tc, w), lambda i, r: (i, 0, 0)),
                  pl.BlockSpec((1, NA_HEADS, nq, NA_KROWS * GRID_W), bias_map)],
        out_specs=pl.BlockSpec((1, nq, w), lambda i, r: (i, r, 0)),
        out_shape=jax.ShapeDtypeStruct((b, t, w), BF16),
        compiler_params=_params(2),
        name="na_attn",
    )(q, k, v, kc, vc, bias)


GDN_STEP_CHUNKS = 4
GDN_BLOCK = GDN_STEP_CHUNKS * GDN_CHUNK
GDN_HALO = 16
GDN_PAIRS = GDN_HEADS // 2
TN_DIMS = (((0,), (0,)), ((), ()))


def _split3(x):
    hi = x.astype(BF16)
    r = x - hi.astype(F32)
    mid = r.astype(BF16)
    lo = (r - mid.astype(F32)).astype(BF16)
    return hi, mid, lo


def _bd(x):
    lane = lax.broadcasted_iota(jnp.int32, x.shape, 1)
    z = jnp.zeros_like(x)
    return jnp.concatenate([jnp.where(lane < HEAD_W, x, z), jnp.where(lane >= HEAD_W, x, z)], axis=0)


def _dotb(a, b):
    return jnp.dot(a.astype(BF16), b.astype(BF16), preferred_element_type=F32)


def _mm_pair(x, y):
    return _dotb(x, _bd(y))


def _softplus(x):
    return jnp.maximum(x, 0.0) + jnp.log(1.0 + jnp.exp(-jnp.abs(x)))


def _gdn_kernel(cur_ref, prev_ref, next_ref, misc_ref, cw_ref, alog_ref, dtb_ref, eg_ref, eb_ref, ones_ref,
                tri3_ref, s0_ref, o_ref, sfin_ref, xs_ref, s_ref, *, rev, n_blk, n_tap):
    step = pl.program_id(1)
    blk = (n_blk - 1 - step) if rev else step
    c_len = GDN_CHUNK
    hw = GDN_HEADS * HEAD_W

    @pl.when(step == 0)
    def _():
        s_ref[...] = s0_ref[0]

    has_prev = (blk > 0).astype(F32)
    has_next = (blk < n_blk - 1).astype(F32)
    xs_ref[0:GDN_HALO, :] = prev_ref[0].astype(F32) * has_prev
    xs_ref[GDN_HALO:GDN_HALO + GDN_BLOCK, :] = cur_ref[0].astype(F32)
    xs_ref[GDN_HALO + GDN_BLOCK:, :] = next_ref[0].astype(F32) * has_next
    acc = None
    for j in range(n_tap):
        start = GDN_HALO - n_tap // 2 + j
        term = xs_ref[start:start + GDN_BLOCK, :] * cw_ref[j:j + 1, :]
        acc = term if acc is None else acc + term
    y = acc * jax.nn.sigmoid(acc)
    q, k, v = y[:, 0:hw], y[:, hw:2 * hw], y[:, 2 * hw:3 * hw]

    def l2n(x):
        x2 = x * x
        hi = x2.astype(BF16)
        mid = (x2 - hi.astype(F32)).astype(BF16)
        ss = jnp.dot(jnp.concatenate([hi, mid], axis=1), ones_ref[...], preferred_element_type=F32)
        return x * lax.rsqrt(ss + NORM_EPS)

    qn = l2n(q) * (GDN_DK ** -0.5)
    kn = l2n(k)

    misc = misc_ref[0]
    g_all = -jnp.exp(alog_ref[...]) * _softplus(misc + dtb_ref[...])
    b_all = jax.nn.sigmoid(misc)
    gx = jnp.dot(jnp.concatenate(_split3(g_all), axis=1), eg_ref[...], preferred_element_type=F32)
    bx = jnp.dot(jnp.concatenate(_split3(b_all), axis=1), eb_ref[...], preferred_element_type=F32)

    row = lax.broadcasted_iota(jnp.int32, (c_len, LANES), 0)
    colp = lax.broadcasted_iota(jnp.int32, (c_len, LANES), 1) & (HEAD_W - 1)
    tri = (row <= colp) if rev else (row >= colp)
    strict = (row < colp) if rev else (row > colp)
    eye_f = (row == colp).astype(F32)
    row_w = lax.broadcasted_iota(jnp.int32, (c_len, hw), 0)
    col_w = lax.broadcasted_iota(jnp.int32, (c_len, hw), 1) & (HEAD_W - 1)
    eye_w = row_w == col_w
    bd_row = lax.broadcasted_iota(jnp.int32, (LANES, LANES), 0)
    bd_col = lax.broadcasted_iota(jnp.int32, (LANES, LANES), 1)
    bd_mask = (bd_row < HEAD_W) == (bd_col < HEAD_W)
    ones8 = jnp.ones((8, 3 * c_len), BF16)

    keys = [(c, p) for c in range(GDN_STEP_CHUNKS) for p in range(GDN_PAIRS)]
    sl = [slice(p * LANES, (p + 1) * LANES) for p in range(GDN_PAIRS)]
    gtot, vb, kbg, qd, kd = {}, {}, {}, {}, {}
    ms, qks = {}, {}
    for c in range(GDN_STEP_CHUNKS):
        r = slice(c * c_len, (c + 1) * c_len)
        gc = jnp.dot(tri3_ref[...], jnp.concatenate(_split3(gx[r]), axis=0), preferred_element_type=F32)
        gtot[c] = gc[0:1] if rev else gc[c_len - 1:c_len]
        egc = jnp.exp(gc)
        kc = kn[r]
        kb = kc * bx[r]
        vb[c] = v[r] * bx[r]
        kbg[c] = kb * egc
        qd[c] = qn[r] * egc
        kd[c] = kc * jnp.exp(gtot[c] - gc)
        dg = jnp.where(eye_w, gc, 0.0)
        gr = jnp.dot(ones8, jnp.concatenate(_split3(dg), axis=0), preferred_element_type=F32)[0:1]
        for p in range(GDN_PAIRS):
            diff = gc[:, sl[p]] - gr[:, sl[p]]
            decay = jnp.where(tri, jnp.exp(jnp.where(tri, diff, 0.0)), 0.0)
            lhs = jnp.concatenate([kb[:, sl[p]], qn[r][:, sl[p]]], axis=0)
            aq = lax.dot_general(lhs.astype(BF16), _bd(kc[:, sl[p]]).astype(BF16), NT_DIMS,
                                 preferred_element_type=F32)
            ms[c, p] = jnp.where(strict, aq[0:c_len] * decay, 0.0)
            qks[c, p] = aq[c_len:2 * c_len] * decay

    rblk, cblk = row, colp
    xs = {}
    for lvl in range(6):
        sib = (cblk == rblk + 1) if rev else (cblk == rblk - 1)
        odd = (rblk & 1) == (0 if rev else 1)
        join = sib & odd
        if lvl == 0:
            xs = {key: eye_f - jnp.where(join, ms[key], 0.0) for key in keys}
        else:
            inner = {key: _mm_pair(jnp.where(join, ms[key], 0.0), xs[key]) for key in keys}
            xs = {key: xs[key] - _mm_pair(xs[key], inner[key]) for key in keys}
        rblk, cblk = rblk >> 1, cblk >> 1
    uws = {}
    for c, p in keys:
        rhs = jnp.concatenate([_bd(vb[c][:, sl[p]]), _bd(kbg[c][:, sl[p]])], axis=1)
        uws[c, p] = _dotb(xs[c, p], rhs)

    order = range(GDN_STEP_CHUNKS - 1, -1, -1) if rev else range(GDN_STEP_CHUNKS)
    state = [s_ref[p] for p in range(GDN_PAIRS)]
    pairs = range(GDN_PAIRS)
    for c in order:
        r = slice(c * c_len, (c + 1) * c_len)
        wq = [_dotb(jnp.concatenate([uws[c, p][:, LANES:2 * LANES], qd[c][:, sl[p]]], axis=0), state[p])
              for p in pairs]
        v_new = [uws[c, p][:, 0:LANES] - wq[p][0:c_len] for p in pairs]
        upd = [lax.dot_general(kd[c][:, sl[p]].astype(BF16), v_new[p].astype(BF16), TN_DIMS,
                               preferred_element_type=F32) for p in pairs]
        for p in pairs:
            o_ref[0, r, sl[p]] = wq[p][c_len:2 * c_len] + _dotb(qks[c, p], _bd(v_new[p]))
            state[p] = state[p] * jnp.exp(gtot[c][:, sl[p]]) + jnp.where(bd_mask, upd[p], 0.0)
    for p in range(GDN_PAIRS):
        s_ref[p] = state[p]

    @pl.when(step == n_blk - 1)
    def _():
        for p in range(GDN_PAIRS):
            sfin_ref[0, p] = state[p]


def _gdn_consts(conv_w, a_log, dt_bias, d, rev):
    n_tap, w = conv_w.shape
    cw = jnp.concatenate([conv_w.astype(F32), jnp.zeros((8 - n_tap, w), F32)], 0)
    pad = jnp.zeros((LANES - 2 * GDN_HEADS,), F32)
    alog = jnp.concatenate([a_log.reshape(-1).astype(F32), pad]).reshape(1, LANES)
    dtb = jnp.concatenate([dt_bias.reshape(-1).astype(F32), pad]).reshape(1, LANES)
    hw = GDN_HEADS * HEAD_W
    head_of_lane = np.arange(hw) // HEAD_W
    src = np.arange(LANES)[:, None]
    eg = (src == d * GDN_HEADS + head_of_lane[None, :]).astype(np.float32)
    eb = (src == 2 * GDN_HEADS + d * GDN_HEADS + head_of_lane[None, :]).astype(np.float32)
    ones_bd = (head_of_lane[:, None] == head_of_lane[None, :]).astype(np.float32)
    i = np.arange(GDN_CHUNK)
    tri = (i[:, None] <= i[None, :]) if rev else (i[:, None] >= i[None, :])
    as_bf = lambda a, reps, ax: jnp.asarray(np.concatenate([a] * reps, axis=ax), BF16)
    return cw, alog, dtb, as_bf(eg, 3, 0), as_bf(eb, 3, 0), as_bf(ones_bd, 2, 0), as_bf(tri.astype(np.float32), 3, 1)


def _gdn_call(qkv, misc, consts, s0, rev, name):
    b, t, w = qkv.shape
    n_blk = t // GDN_BLOCK
    per = GDN_BLOCK // GDN_HALO
    n_halo = t // GDN_HALO
    cw, alog, dtb, eg, eb, ones_bd, tri3 = consts
    n_tap = 5
    hw = GDN_HEADS * HEAD_W
    blk_of = (lambda s: n_blk - 1 - s) if rev else (lambda s: s)
    const = lambda a: pl.BlockSpec(a.shape, lambda i, s: (0,) * a.ndim)
    s_spec = pl.BlockSpec((1, GDN_PAIRS, LANES, LANES), lambda i, s: (i, 0, 0, 0))
    return pl.pallas_call(
        functools.partial(_gdn_kernel, rev=rev, n_blk=n_blk, n_tap=n_tap),
        grid=(b, n_blk),
        in_specs=[pl.BlockSpec((1, GDN_BLOCK, w), lambda i, s: (i, blk_of(s), 0)),
                  pl.BlockSpec((1, GDN_HALO, w), lambda i, s: (i, jnp.maximum(blk_of(s) * per - 1, 0), 0)),
                  pl.BlockSpec((1, GDN_HALO, w), lambda i, s: (i, jnp.minimum((blk_of(s) + 1) * per, n_halo - 1), 0)),
                  pl.BlockSpec((1, GDN_BLOCK, LANES), lambda i, s: (i, blk_of(s), 0)),
                  const(cw), const(alog), const(dtb), const(eg), const(eb), const(ones_bd), const(tri3), s_spec],
        out_specs=[pl.BlockSpec((1, GDN_BLOCK, hw), lambda i, s: (i, blk_of(s), 0)), s_spec],
        out_shape=[jax.ShapeDtypeStruct((b, t, hw), F32),
                   jax.ShapeDtypeStruct((b, GDN_PAIRS, LANES, LANES), F32)],
        scratch_shapes=[pltpu.VMEM((GDN_BLOCK + 2 * GDN_HALO, w), F32),
                        pltpu.VMEM((GDN_PAIRS, LANES, LANES), F32)],
        compiler_params=_params(2),
        name=name,
    )(qkv, qkv, qkv, misc, cw, alog, dtb, eg, eb, ones_bd, tri3, s0)


def _mix_ffn_kernel(*refs, ff_chunks, gdn_inputs):
    if gdn_inputs:
        (h_ref, of_ref, ob_ref, gate_ref, att_ref, on_ref, ones_ref, gtm_ref, shf_ref, scf_ref, gtf_ref, gpm_ref,
         gpf_ref, gqf_ref, wo_ref, wg_ref, wu_ref, wd_ref, o_ref) = refs
        o = of_ref[0] + ob_ref[0]
        gate = gate_ref[0].astype(F32)
        a = o * lax.rsqrt(_seg_mean_sq(o, ones_ref) + NORM_EPS) * on_ref[...] * (gate * jax.nn.sigmoid(gate))
        mix = jnp.concatenate([a.astype(BF16), att_ref[0]], axis=1)
    else:
        (h_ref, a_ref, b_ref, gtm_ref, shf_ref, scf_ref, gtf_ref, gpm_ref, gpf_ref, gqf_ref, wo_ref, wg_ref,
         wu_ref, wd_ref, o_ref) = refs
        mix = jnp.concatenate([a_ref[0], b_ref[0]], axis=1)
    y = jnp.dot(mix, wo_ref[...], preferred_element_type=F32)
    h1 = h_ref[0] + gtm_ref[0] * (_rms(y) * gpm_ref[...])
    u = (_rms(h1) * gpf_ref[...] * (1.0 + scf_ref[0]) + shf_ref[0]).astype(BF16)
    f = None
    for o, n in ff_chunks:
        gg = jnp.dot(u, wg_ref[:, o:o + n], preferred_element_type=F32)
        uu = jnp.dot(u, wu_ref[:, o:o + n], preferred_element_type=F32)
        a = (gg * jax.nn.sigmoid(gg) * uu).astype(BF16)
        part = jnp.dot(a, wd_ref[o:o + n, :], preferred_element_type=F32)
        f = part if f is None else f + part
    o_ref[0] = h1 + gtf_ref[0] * (_rms(f) * gqf_ref[...])


def _mix_ffn_call(h, mix_inputs, out_norm, gt_m, sh_f, sc_f, gt_f, g_post_mix, g_pre_ffn, g_post_ffn, wo, wg, wu,
                  wd, tm, name):
    b, t, d = h.shape
    dm = wo.shape[0]
    ff = wg.shape[1]
    gdn_inputs = out_norm is not None
    half = ff // 2
    assert half % LANES == 0
    ff_chunks = ((0, half), (half, ff - half))
    tok = lambda n: pl.BlockSpec((1, tm, n), lambda i, j: (i, j, 0))
    per_b = pl.BlockSpec((1, 1, d), lambda i, j: (i, 0, 0))
    vec = pl.BlockSpec((1, d), lambda i, j: (0, 0))
    const = lambda shape: pl.BlockSpec(shape, lambda i, j: (0, 0), pipeline_mode=pl.Buffered(1))
    r3 = lambda a: a.reshape(b, 1, d)
    r2 = lambda a: a.reshape(1, d).astype(F32)
    mix_specs = [tok(m.shape[-1]) for m in mix_inputs]
    mix_args = list(mix_inputs)
    if gdn_inputs:
        hw = GDN_HEADS * GDN_DV
        on = jnp.tile(out_norm.astype(F32), GDN_HEADS).reshape(1, hw)
        ones = _seg_ones(hw)
        mix_specs += [pl.BlockSpec(on.shape, lambda i, j: (0, 0)), pl.BlockSpec(ones.shape, lambda i, j: (0, 0))]
        mix_args += [on, ones]
    return pl.pallas_call(
        functools.partial(_mix_ffn_kernel, ff_chunks=ff_chunks, gdn_inputs=gdn_inputs),
        grid=(b, t // tm),
        in_specs=[tok(d)] + mix_specs + [per_b, per_b, per_b, per_b, vec, vec, vec,
                                         const((dm, d)), const((d, ff)), const((d, ff)), const((ff, d))],
        out_specs=tok(d),
        out_shape=jax.ShapeDtypeStruct((b, t, d), F32),
        compiler_params=_params(2),
        name=name,
    )(h, *mix_args, r3(gt_m), r3(sh_f), r3(sc_f), r3(gt_f), r2(g_post_mix), r2(g_pre_ffn), r2(g_post_ffn),
      wo, wg, wu, wd)


def _rope_tables(n_tokens, rot_dim):
    t = jnp.arange(n_tokens, dtype=jnp.int32)
    row = (t // GRID_W).astype(F32)
    col = (t % GRID_W).astype(F32)
    n_freq = rot_dim // 4
    inv_freq = ROPE_THETA ** (-jnp.arange(n_freq, dtype=F32) / n_freq)
    ang = jnp.concatenate([row[:, None] * inv_freq, col[:, None] * inv_freq], -1)
    cos = jnp.concatenate([jnp.cos(ang), jnp.cos(ang)], -1)
    sin = jnp.concatenate([-jnp.sin(ang), jnp.sin(ang)], -1)
    if rot_dim == HEAD_W:
        return jnp.tile(cos, (1, LANES // rot_dim)), jnp.tile(sin, (1, LANES // rot_dim))
    pad = LANES - rot_dim
    return (jnp.concatenate([cos, jnp.ones((n_tokens, pad), F32)], -1),
            jnp.concatenate([sin, jnp.zeros((n_tokens, pad), F32)], -1))


def _no_rope_tables(n_tokens):
    return jnp.ones((n_tokens, LANES), F32), jnp.zeros((n_tokens, LANES), F32)


def _gdn_bidirectional(lat_qkv, lat_misc, ctx_qkv, ctx_misc, conv_w, a_log, dt_bias):
    bsz = lat_qkv.shape[0]
    s_zero = jnp.zeros((bsz, GDN_PAIRS, LANES, LANES), F32)
    lat, ctx = [], []
    for d, rev in ((0, False), (1, True)):
        consts = _gdn_consts(conv_w, a_log, dt_bias, d, rev)
        o_c, s_c = _gdn_call(ctx_qkv, ctx_misc, consts, s_zero, rev, "gdn_ctx_%d" % d)
        o_l, _ = _gdn_call(lat_qkv, lat_misc, consts, s_c, rev, "gdn_lat_%d" % d)
        lat.append(o_l)
        ctx.append(o_c)
    return lat, ctx


LOG2E = 1.4426950408889634
ATTN_TQ = 1024
ATTN_LOOKAHEAD = 2


def _attn_chunks(l_len, s_len, size=512):
    return [(0, 0, l_len)] + [(1, o, size) for o in range(0, s_len, size)]


def _even_mixer(h, hc, mods, mods_c, g_pre, w_in, conv_w, a_log, dt_bias, q_norm, w_q_up, kv_norm, w_kv_up):
    s_len, l_len = h.shape[1], hc.shape[1]
    weights = _even_weights(w_in, q_norm, w_q_up, kv_norm, w_kv_up)
    qkv_l, gate_l, misc_l, q_l, k_l, v_l = _in_even_call(h, g_pre, *mods, weights, *_rope_tables(s_len, MLA_ROPE),
                                                         512, "in_even_lat")
    qkv_c, gate_c, misc_c, q_c, k_c, v_c = _in_even_call(hc, g_pre, *mods_c, weights, *_no_rope_tables(l_len),
                                                         l_len, "in_even_ctx")
    o_lat, o_ctx = _gdn_bidirectional(qkv_l, misc_l, qkv_c, misc_c, conv_w, a_log, dt_bias)
    b_lat = _attn_call(q_l, [(k_c, v_c), (k_l, v_l)], MLA_QC, MLA_PAIRS, MLA_PAIRS, _attn_chunks(l_len, s_len),
                       ATTN_TQ, False, "mla_attn_lat")
    b_ctx = _attn_call(q_c, [(k_c, v_c)], MLA_QC, MLA_PAIRS, MLA_PAIRS, [(0, 0, l_len)], l_len, False,
                       "mla_attn_ctx")
    return (o_lat[0], o_lat[1], gate_l, b_lat), (o_ctx[0], o_ctx[1], gate_c, b_ctx)


def _odd_mixer_last(h, hc, mods, mods_c, g_pre, w_in, rpb, q_norm, k_norm):
    s_len, l_len = h.shape[1], hc.shape[1]
    w = w_in.astype(BF16)
    q_gain = (jnp.tile(q_norm.astype(F32), GQA_HEADS) * (GQA_DIM ** -0.5 * LOG2E)).reshape(1, -1)
    k_gain = jnp.tile(k_norm.astype(F32), GQA_KV_HEADS).reshape(1, -1)
    naq_l, nak_l, nav_l, gq_l, gk_l, gv_l = _in_odd_call(h, g_pre, *mods, w, q_gain, k_gain,
                                                         *_rope_tables(s_len, GQA_DIM), 512, "in_odd_lat")
    _, nak_c, nav_c, _, gk_c, gv_c = _in_odd_call(hc, g_pre, *mods_c, w, q_gain, k_gain, *_no_rope_tables(l_len),
                                                  l_len, "in_odd_ctx")
    c_lat = _na_call(naq_l, nak_l, nav_l, nak_c, nav_c, _na_bias_table(rpb, s_len // GRID_W))
    d_lat = _attn_call(gq_l, [(gk_c, gv_c), (gk_l, gv_l)], LANES, GQA_HEADS // 2, GQA_KV_HEADS,
                       _attn_chunks(l_len, s_len), ATTN_TQ, True, "gqa_attn")
    return c_lat, d_lat


def kernel(x, c, ctx, c_ctx, w_mod, b_mod, g_pre_mix, g_post_mix, g_pre_ffn, g_post_ffn, w_ffn_gate, w_ffn_up,
           w_ffn_down, w_in_even, w_out_even, gdn_conv, gdn_a_log, gdn_dt_bias, gdn_out_norm, mla_q_norm,
           mla_w_q_up, mla_kv_norm, mla_w_kv_up, w_in_odd, w_out_odd, na_rpb, gqa_q_norm, gqa_k_norm):
    bsz, s_len, d = x.shape
    depth = w_mod.shape[0]
    assert depth == 2, "layer 0 = even mixer with context update, layer 1 = odd mixer (last)"
    c_rows = jnp.concatenate([c, c_ctx[None, :], jnp.zeros((16 - bsz - 1, d), F32)], 0)
    h, hc = x, ctx
    for i in range(depth):
        mod_all = _mod_call(c_rows, w_mod[i].astype(BF16), b_mod[i])
        mod = jnp.split(mod_all[:bsz], 6, axis=-1)
        mod_c = [jnp.broadcast_to(m, (bsz, d)) for m in jnp.split(mod_all[bsz:bsz + 1], 6, axis=-1)]
        sh_m, sc_m, gt_m, sh_f, sc_f, gt_f = mod
        csh_m, csc_m, cgt_m, csh_f, csc_f, cgt_f = mod_c
        wg, wu, wd = w_ffn_gate[i].astype(BF16), w_ffn_up[i].astype(BF16), w_ffn_down[i].astype(BF16)
        if i == 0:
            mix, mix_c = _even_mixer(h, hc, (sh_m, sc_m), (csh_m, csc_m), g_pre_mix[i], w_in_even[0], gdn_conv[0],
                                     gdn_a_log[0], gdn_dt_bias[0], mla_q_norm[0], mla_w_q_up[0], mla_kv_norm[0],
                                     mla_w_kv_up[0])
            wo, out_norm = w_out_even[0].astype(BF16), gdn_out_norm[0]
            hc = _mix_ffn_call(hc, mix_c, out_norm, cgt_m, csh_f, csc_f, cgt_f, g_post_mix[i], g_pre_ffn[i],
                               g_post_ffn[i], wo, wg, wu, wd, 256, "mix_ffn_ctx")
        else:
            mix = _odd_mixer_last(h, hc, (sh_m, sc_m), (csh_m, csc_m), g_pre_mix[i], w_in_odd[0], na_rpb[0],
                                  gqa_q_norm[0], gqa_k_norm[0])
            wo, out_norm = w_out_odd[0].astype(BF16), None
        h = _mix_ffn_call(h, mix, out_norm, gt_m, sh_f, sc_f, gt_f, g_post_mix[i], g_pre_ffn[i], g_post_ffn[i],
                          wo, wg, wu, wd, 256, "mix_ffn_lat%d" % i)
    return h
```

```python
import functools

import numpy as np
import jax
import jax.numpy as jnp
from jax import lax
from jax.experimental import pallas as pl
from jax.experimental.pallas import tpu as pltpu

F32 = jnp.float32
BF16 = jnp.bfloat16

GRID_W = 64
NORM_EPS = 1e-6
ROPE_THETA = 10000.0

GDN_HEADS = 8
GDN_DK = 64
GDN_DV = 64
GDN_CHUNK = 64
GDN_QK_W = GDN_HEADS * GDN_DK
GDN_V_W = GDN_HEADS * GDN_DV

MLA_HEADS = 8
MLA_NOPE = 64
MLA_ROPE = 32
MLA_V = 64
MLA_Q_LORA = 256
MLA_KV_LORA = 128

NA_HEADS = 8
NA_DIM = 64
NA_WIN_H = 8
NA_WIN_W = 16
NA_W = NA_HEADS * NA_DIM

GQA_HEADS = 8
GQA_KV_HEADS = 2
GQA_DIM = 64

LANES = 128
HEAD_W = 64
MASK_VALUE = -1e30
VMEM_LIMIT = 56 << 20

NT_DIMS = (((1,), (1,)), ((), ()))


def _params(n_grid, vmem=VMEM_LIMIT):
    return pltpu.CompilerParams(dimension_semantics=("arbitrary",) * n_grid, vmem_limit_bytes=vmem)


def _rms(x):
    return x * lax.rsqrt(jnp.mean(x * x, axis=-1, keepdims=True) + NORM_EPS)


def _mod_kernel(c_ref, w_ref, b_ref, o_ref):
    c = c_ref[...]
    a = (c * jax.nn.sigmoid(c)).astype(BF16)
    o_ref[...] = jnp.dot(a, w_ref[...], preferred_element_type=F32) + b_ref[...]


def _mod_call(c_rows, w, b):
    r, d = c_rows.shape
    n = w.shape[1]
    tn = 1024
    return pl.pallas_call(
        _mod_kernel,
        grid=(n // tn,),
        in_specs=[pl.BlockSpec((r, d), lambda j: (0, 0)),
                  pl.BlockSpec((d, tn), lambda j: (0, j)),
                  pl.BlockSpec((1, tn), lambda j: (0, j))],
        out_specs=pl.BlockSpec((r, tn), lambda j: (0, j)),
        out_shape=jax.ShapeDtypeStruct((r, n), F32),
        compiler_params=_params(1),
        name="mod",
    )(c_rows, w, b.reshape(1, n))


def _norm_mod(x_ref, g_ref, sh_ref, sc_ref):
    return (_rms(x_ref[0]) * g_ref[...] * (1.0 + sc_ref[0]) + sh_ref[0]).astype(BF16)


def _dotf(a, b):
    return jnp.dot(a, b, preferred_element_type=F32)


def _seg_mean_sq(x, ones_ref):
    x2 = x * x
    hi = x2.astype(BF16)
    mid = (x2 - hi.astype(F32)).astype(BF16)
    return _dotf(jnp.concatenate([hi, mid], axis=1), ones_ref[...]) * (1.0 / HEAD_W)


def _rotate(t, cos, sin_signed, half):
    lane = lax.broadcasted_iota(jnp.int32, t.shape, 1)
    n = t.shape[1]
    partner = jnp.where((lane & (2 * half - 1)) < half, pltpu.roll(t, n - half, 1), pltpu.roll(t, half, 1))
    return t * cos + partner * sin_signed


EVEN_QKV_W = 2 * GDN_QK_W + GDN_V_W
EVEN_GROUPS = (EVEN_QKV_W, GDN_V_W, MLA_Q_LORA, MLA_KV_LORA, LANES, LANES)
MLA_QC = 2 * LANES
MLA_PAIRS = MLA_HEADS // 2


def _in_even_kernel(x_ref, g_ref, sh_ref, sc_ref, w_ref, qg_ref, wq_ref, kvg_ref, wk_ref, wv_ref, vone_ref,
                    cos_ref, sin_ref, qkv_ref, gate_ref, misc_ref, q_ref, k_ref, v_ref):
    u = _norm_mod(x_ref, g_ref, sh_ref, sc_ref)
    offs = np.cumsum((0,) + EVEN_GROUPS)
    grp = lambda i: _dotf(u, w_ref[:, int(offs[i]):int(offs[i + 1])])
    q_down, kv_down, pe_raw = grp(2), grp(3), grp(5)
    qkv_ref[0] = grp(0).astype(BF16)
    gate_ref[0] = grp(1).astype(BF16)
    misc_ref[0] = grp(4)
    cos, sin = cos_ref[...], sin_ref[...]
    qn = (_rms(q_down) * qg_ref[...]).astype(BF16)
    kvn = (_rms(kv_down) * kvg_ref[...]).astype(BF16)
    for h in range(MLA_HEADS):
        blk = _dotf(qn, wq_ref[:, h * MLA_QC:(h + 1) * MLA_QC])
        q_ref[0, :, h * MLA_QC:h * MLA_QC + LANES] = blk[:, 0:LANES].astype(BF16)
        q_ref[0, :, h * MLA_QC + LANES:(h + 1) * MLA_QC] = _rotate(blk[:, LANES:], cos, sin,
                                                                    MLA_ROPE // 2).astype(BF16)
    pe = _rotate(pe_raw, cos, sin, MLA_ROPE // 2).astype(BF16)
    k_nope = _dotf(kvn, wk_ref[...])
    for p in range(MLA_PAIRS):
        k_ref[0, :, p * MLA_QC:p * MLA_QC + LANES] = k_nope[:, p * LANES:(p + 1) * LANES].astype(BF16)
        k_ref[0, :, p * MLA_QC + LANES:(p + 1) * MLA_QC] = pe
    v_ref[0] = (_dotf(kvn, wv_ref[...]) + vone_ref[...]).astype(BF16)


def _even_weights(w_in, q_norm, w_q_up, kv_norm, w_kv_up):
    d = w_in.shape[0]
    o0 = EVEN_QKV_W + GDN_V_W
    n_ab = 4 * GDN_HEADS
    o1 = o0 + n_ab
    o2 = o1 + MLA_Q_LORA
    o3 = o2 + MLA_KV_LORA
    zeros = lambda n, rows=d: jnp.zeros((rows, n), F32)
    w_cat = jnp.concatenate([w_in[:, :o0], w_in[:, o1:o2], w_in[:, o2:o3],
                             w_in[:, o0:o1], zeros(LANES - n_ab),
                             w_in[:, o3:], zeros(LANES - MLA_ROPE)], 1).astype(BF16)
    qh = MLA_NOPE + MLA_ROPE
    cols = []
    for h in range(MLA_HEADS):
        nope = w_q_up[:, h * qh:h * qh + MLA_NOPE]
        rp = w_q_up[:, h * qh + MLA_NOPE:(h + 1) * qh]
        z = zeros(HEAD_W, MLA_Q_LORA)
        cols += ([nope, z] if h % 2 == 0 else [z, nope]) + [rp, zeros(LANES - MLA_ROPE, MLA_Q_LORA)]
    wq = jnp.concatenate(cols, 1).astype(BF16)
    kvh = MLA_NOPE + MLA_V
    wk = jnp.concatenate([w_kv_up[:, h * kvh:h * kvh + MLA_NOPE] for h in range(MLA_HEADS)], 1).astype(BF16)
    vcols, ones = [], []
    for h in range(MLA_HEADS):
        vh = w_kv_up[:, h * kvh + MLA_NOPE:(h + 1) * kvh]
        z = zeros(HEAD_W, MLA_KV_LORA)
        vcols += [vh, z] if h % 2 == 0 else [z, vh]
        ones += [0.0, 1.0] if h % 2 == 0 else [1.0, 0.0]
    wv = jnp.concatenate(vcols, 1).astype(BF16)
    vone = jnp.asarray(np.repeat(np.asarray(ones, np.float32), HEAD_W)[None, :])
    q_gain = (q_norm * ((MLA_NOPE + MLA_ROPE) ** -0.5 * LOG2E)).reshape(1, -1).astype(F32)
    return w_cat, q_gain, wq, kv_norm.reshape(1, -1).astype(F32), wk, wv, vone


def _in_even_call(x, g, shift, scale, weights, cos, sin, tm, name):
    b, t, d = x.shape
    w_cat, q_gain, wq, kv_gain, wk, wv, vone = weights
    const = lambda a: pl.BlockSpec(a.shape, lambda i, j: (0,) * a.ndim)
    per_b = pl.BlockSpec((1, 1, d), lambda i, j: (i, 0, 0))
    tab = pl.BlockSpec((tm, LANES), lambda i, j: (j, 0))
    widths = (EVEN_QKV_W, GDN_V_W, LANES, MLA_HEADS * MLA_QC, MLA_PAIRS * MLA_QC, MLA_HEADS * LANES)
    dts = (BF16, BF16, F32, BF16, BF16, BF16)
    g2 = g.reshape(1, d).astype(F32)
    return pl.pallas_call(
        _in_even_kernel,
        grid=(b, t // tm),
        in_specs=[pl.BlockSpec((1, tm, d), lambda i, j: (i, j, 0)), const(g2), per_b, per_b, const(w_cat),
                  const(q_gain), const(wq), const(kv_gain), const(wk), const(wv), const(vone), tab, tab],
        out_specs=[pl.BlockSpec((1, tm, n), lambda i, j: (i, j, 0)) for n in widths],
        out_shape=[jax.ShapeDtypeStruct((b, t, n), dt) for n, dt in zip(widths, dts)],
        compiler_params=_params(2),
        name=name,
    )(x, g2, shift.reshape(b, 1, d), scale.reshape(b, 1, d), w_cat, q_gain, wq, kv_gain, wk, wv, vone, cos, sin)


GQA_KV_W = GQA_KV_HEADS * GQA_DIM


def _in_odd_kernel(x_ref, g_ref, sh_ref, sc_ref, w_ref, qg_ref, kg_ref, ones_q_ref, ones_k_ref, cos_ref, sin_ref,
                   naq_ref, nak_ref, nav_ref, gq_ref, gk_ref, gv_ref):
    u = _norm_mod(x_ref, g_ref, sh_ref, sc_ref)
    grp = lambda lo, n: _dotf(u, w_ref[:, lo:lo + n])
    o = 3 * NA_W
    q = grp(o, GQA_HEADS * GQA_DIM)
    o += GQA_HEADS * GQA_DIM
    k = grp(o, GQA_KV_W)
    v = grp(o + GQA_KV_W, GQA_KV_W)
    naq_ref[0] = (grp(0, NA_W) * (NA_DIM ** -0.5 * LOG2E)).astype(BF16)
    nak_ref[0] = grp(NA_W, NA_W).astype(BF16)
    na_v = grp(2 * NA_W, NA_W)
    low = lax.broadcasted_iota(jnp.int32, (na_v.shape[0], LANES), 1) < HEAD_W
    for p in range(NA_HEADS // 2):
        tile = na_v[:, p * LANES:(p + 1) * LANES]
        nav_ref[0, :, 2 * p * LANES:(2 * p + 1) * LANES] = tile.astype(BF16)
        nav_ref[0, :, (2 * p + 1) * LANES:(2 * p + 2) * LANES] = jnp.ones(tile.shape, BF16)
    q_ms = _seg_mean_sq(q, ones_q_ref)
    k_ms = _seg_mean_sq(k, ones_k_ref)
    cos, sin = cos_ref[...], sin_ref[...]
    qn = q * lax.rsqrt(q_ms + NORM_EPS) * qg_ref[...]
    for p in range(GQA_HEADS // 2):
        cols = slice(p * LANES, (p + 1) * LANES)
        gq_ref[0, :, cols] = _rotate(qn[:, cols], cos, sin, GQA_DIM // 2).astype(BF16)
    kr = _rotate(k * lax.rsqrt(k_ms + NORM_EPS) * kg_ref[...], cos, sin, GQA_DIM // 2)
    k_sw = pltpu.roll(kr, HEAD_W, 1)
    gk_ref[0, :, 0:LANES] = jnp.where(low, kr, k_sw).astype(BF16)
    gk_ref[0, :, LANES:2 * LANES] = jnp.where(low, k_sw, kr).astype(BF16)
    v_sw = pltpu.roll(v, HEAD_W, 1)
    slots = (jnp.where(low, v, 1.0), jnp.where(low, 1.0, v_sw),
             jnp.where(low, v_sw, 1.0), jnp.where(low, 1.0, v))
    for i, s in enumerate(slots):
        gv_ref[0, :, i * LANES:(i + 1) * LANES] = s.astype(BF16)


def _seg_ones(width):
    head = np.arange(width) // HEAD_W
    m = (head[:, None] == head[None, :]).astype(np.float32)
    return jnp.asarray(np.concatenate([m, m], 0), BF16)


def _in_odd_call(x, g, shift, scale, w, q_gain, k_gain, cos, sin, tm, name):
    b, t, d = x.shape
    assert GQA_KV_W == LANES
    ones_q = _seg_ones(GQA_HEADS * GQA_DIM)
    ones_k = _seg_ones(GQA_KV_W)
    const = lambda a: pl.BlockSpec(a.shape, lambda i, j: (0,) * a.ndim)
    per_b = pl.BlockSpec((1, 1, d), lambda i, j: (i, 0, 0))
    tab = pl.BlockSpec((tm, LANES), lambda i, j: (j, 0))
    widths = (NA_W, NA_W, 2 * NA_W, GQA_HEADS * GQA_DIM, 2 * LANES, 4 * LANES)
    g2 = g.reshape(1, d).astype(F32)
    return pl.pallas_call(
        _in_odd_kernel,
        grid=(b, t // tm),
        in_specs=[pl.BlockSpec((1, tm, d), lambda i, j: (i, j, 0)), const(g2), per_b, per_b, const(w),
                  const(q_gain), const(k_gain), const(ones_q), const(ones_k), tab, tab],
        out_specs=[pl.BlockSpec((1, tm, n), lambda i, j: (i, j, 0)) for n in widths],
        out_shape=[jax.ShapeDtypeStruct((b, t, n), BF16) for n in widths],
        compiler_params=_params(2),
        name=name,
    )(x, g2, shift.reshape(b, 1, d), scale.reshape(b, 1, d), w, q_gain, k_gain, ones_q, ones_k, cos, sin)


def _attn_kernel(*refs, dc, chunks, n_src, masked_q):
    q_ref, kv_refs, o_ref = refs[0], refs[1:1 + 2 * n_src], refs[-1]
    tq = q_ref.shape[1]
    lane = lax.broadcasted_iota(jnp.int32, (tq, LANES), 1)
    qs = []
    for hh in range(2):
        if masked_q:
            qp = q_ref[0]
            qs.append(jnp.where((lane < HEAD_W) if hh == 0 else (lane >= HEAD_W), qp, jnp.zeros_like(qp)))
        else:
            qs.append(q_ref[0, :, hh * dc:(hh + 1) * dc])
    ms, accs = [None, None], [None, None]

    def scores(hh, src, s0, n):
        return lax.dot_general(qs[hh], kv_refs[2 * src][0, s0:s0 + n, :], NT_DIMS, preferred_element_type=F32)

    def consume(s, hh, src, s0, n):
        vv = kv_refs[2 * src + 1][0, s0:s0 + n, hh * LANES:(hh + 1) * LANES]
        mc = jnp.max(s, axis=-1, keepdims=True)
        if ms[hh] is None:
            m_new = mc
            accs[hh] = jnp.dot(jnp.exp2(s - m_new).astype(BF16), vv, preferred_element_type=F32)
        else:
            m_new = jnp.maximum(ms[hh], mc)
            accs[hh] = jnp.exp2(ms[hh] - m_new) * accs[hh] + jnp.dot(jnp.exp2(s - m_new).astype(BF16), vv,
                                                                     preferred_element_type=F32)
        ms[hh] = m_new

    items = [(hh,) + tuple(c) for c in chunks for hh in range(2)]
    pending = []
    for it in items:
        pending.append((scores(*it), it))
        if len(pending) > ATTN_LOOKAHEAD:
            s, it0 = pending.pop(0)
            consume(s, *it0)
    for s, it0 in pending:
        consume(s, *it0)
    low = lane < HEAD_W
    num = jnp.where(low, accs[0], accs[1])
    den = pltpu.roll(jnp.where(low, accs[1], accs[0]), HEAD_W, 1)
    o_ref[0] = (num / den).astype(o_ref.dtype)


def _attn_call(q, kvs, dc, n_pairs, n_groups, chunks, tq, masked_q, name):
    b, t_q, _ = q.shape
    per = n_pairs // n_groups
    q_w = LANES if masked_q else 2 * dc
    in_specs = [pl.BlockSpec((1, tq, q_w), lambda i, p, j: (i, j, p))]
    args = [q]
    for k, v in kvs:
        in_specs += [pl.BlockSpec((1, k.shape[1], dc), lambda i, p, j: (i, 0, p // per)),
                     pl.BlockSpec((1, v.shape[1], 2 * LANES), lambda i, p, j: (i, 0, p // per))]
        args += [k, v]
    return pl.pallas_call(
        functools.partial(_attn_kernel, dc=dc, chunks=tuple(chunks), n_src=len(kvs), masked_q=masked_q),
        grid=(b, n_pairs, t_q // tq),
        in_specs=in_specs,
        out_specs=pl.BlockSpec((1, tq, LANES), lambda i, p, j: (i, j, p)),
        out_shape=jax.ShapeDtypeStruct((b, t_q, n_pairs * LANES), BF16),
        compiler_params=_params(3),
        name=name,
    )(*args)


NA_QROWS = 4
NA_KROWS = 12
NA_LOOKAHEAD = 1


def _na_kernel(q_ref, k_ref, v_ref, kc_ref, vc_ref, bias_ref, o_ref):
    rb = pl.program_id(1)
    n_rows = k_ref.shape[1] // GRID_W
    base = jnp.clip(rb * NA_QROWS - NA_WIN_H // 2, 0, n_rows - NA_KROWS) * GRID_W
    base = pl.multiple_of(base, GRID_W)
    nq = NA_QROWS * GRID_W
    nk = NA_KROWS * GRID_W
    lane = lax.broadcasted_iota(jnp.int32, (nq, LANES), 1)
    low = lane < HEAD_W

    def scores(p):
        cols = slice(p * LANES, (p + 1) * LANES)
        qp = q_ref[0, :, cols]
        zero = jnp.zeros_like(qp)
        q2 = jnp.concatenate([jnp.where(low, qp, zero), jnp.where(low, zero, qp)], axis=0)
        s_loc = lax.dot_general(q2, k_ref[0, pl.ds(base, nk), cols], NT_DIMS, preferred_element_type=F32)
        s_ctx = lax.dot_general(q2, kc_ref[0, :, cols], NT_DIMS, preferred_element_type=F32)
        return s_loc, s_ctx

    def consume(p, s_loc, s_ctx):
        vcols = slice(2 * p * LANES, (2 * p + 2) * LANES)
        s_loc = s_loc + bias_ref[0, p]
        m = jnp.maximum(jnp.max(s_loc, axis=-1, keepdims=True), jnp.max(s_ctx, axis=-1, keepdims=True))
        acc = (_dotf(jnp.exp2(s_loc - m).astype(BF16), v_ref[0, pl.ds(base, nk), vcols])
               + _dotf(jnp.exp2(s_ctx - m).astype(BF16), vc_ref[0, :, vcols]))
        even, odd = acc[0:nq], acc[nq:2 * nq]
        num = jnp.where(low, even[:, 0:LANES], odd[:, 0:LANES])
        den = jnp.where(low, even[:, LANES:2 * LANES], odd[:, LANES:2 * LANES])
        o_ref[0, :, p * LANES:(p + 1) * LANES] = (num / den).astype(o_ref.dtype)

    pending = []
    for p in range(NA_HEADS // 2):
        pending.append((p,) + scores(p))
        if len(pending) > NA_LOOKAHEAD:
            consume(*pending.pop(0))
    for item in pending:
        consume(*item)


def _na_bias_table(rpb, rows):
    n_blocks = rows // NA_QROWS
    n_h = rpb.shape[0]
    c = np.arange(GRID_W)[:, None]
    kc = np.arange(GRID_W)[None, :]
    cs = np.clip(c - NA_WIN_W // 2, 0, GRID_W - NA_WIN_W)
    col_ok = (kc >= cs) & (kc < cs + NA_WIN_W)
    dc = np.clip(kc - c + (NA_WIN_W - 1), 0, 2 * NA_WIN_W - 2)
    pick = (np.arange(2 * NA_WIN_W - 1)[:, None] == dc.reshape(-1)[None, :]).astype(np.float32)
    variants = []
    for r0 in (0, NA_QROWS * (n_blocks // 2), rows - NA_QROWS):
        base = int(np.clip(r0 - NA_WIN_H // 2, 0, rows - NA_KROWS))
        r = r0 + np.arange(NA_QROWS)[:, None]
        kr = base + np.arange(NA_KROWS)[None, :]
        rs = np.clip(r - NA_WIN_H // 2, 0, rows - NA_WIN_H)
        row_ok = (kr >= rs) & (kr < rs + NA_WIN_H)
        dr = np.clip(kr - r + (NA_WIN_H - 1), 0, 2 * NA_WIN_H - 2)
        by_row = rpb[:, dr.reshape(-1), :].astype(F32)
        full = jnp.einsum("hxd,dm->hxm", by_row, jnp.asarray(pick), precision=lax.Precision.HIGHEST)
        full = full.reshape(n_h, NA_QROWS, NA_KROWS, GRID_W, GRID_W).transpose(0, 1, 3, 2, 4)
        full = full.reshape(n_h, NA_QROWS * GRID_W, NA_KROWS * GRID_W)
        valid = (row_ok[:, None, :, None] & col_ok[None, :, None, :]).reshape(full.shape[1:])
        variants.append(jnp.where(valid[None], full * LOG2E, MASK_VALUE))
    return jnp.stack(variants, 0).reshape(3, n_h // 2, 2 * NA_QROWS * GRID_W, NA_KROWS * GRID_W)


def _na_call(q, k, v, kc, vc, bias):
    b, t, w = q.shape
    n_blocks = t // (NA_QROWS * GRID_W)
    nq = NA_QROWS * GRID_W
    tc = kc.shape[1]

    def bias_map(i, r):
        return (jnp.where(r == 0, 0, jnp.where(r == n_blocks - 1, 2, 1)), 0, 0, 0)

    return pl.pallas_call(
        _na_kernel,
        grid=(b, n_blocks),
        in_specs=[pl.BlockSpec((1, nq, w), lambda i, r: (i, r, 0)),
                  pl.BlockSpec((1, t, w), lambda i, r: (i, 0, 0)),
                  pl.BlockSpec((1, t, 2 * w), lambda i, r: (i, 0, 0)),
                  pl.BlockSpec((1, tc, w), lambda i, r: (i, 0, 0)),
                  pl.BlockSpec((1, tc, 2 * w), lambda i, r: (i, 0, 0)),
                  pl.BlockSpec((1, NA_HEADS // 2, 2 * nq, NA_KROWS * GRID_W), bias_map)],
        out_specs=pl.BlockSpec((1, nq, w), lambda i, r: (i, r, 0)),
        out_shape=jax.ShapeDtypeStruct((b, t, w), BF16),
        compiler_params=_params(2),
        name="na_attn",
    )(q, k, v, kc, vc, bias)


GDN_STEP_CHUNKS = 4
GDN_BLOCK = GDN_STEP_CHUNKS * GDN_CHUNK
GDN_HALO = 16
GDN_PAIRS = GDN_HEADS // 2
TN_DIMS = (((0,), (0,)), ((), ()))


def _split3(x):
    hi = x.astype(BF16)
    r = x - hi.astype(F32)
    mid = r.astype(BF16)
    lo = (r - mid.astype(F32)).astype(BF16)
    return hi, mid, lo


def _bd(x):
    lane = lax.broadcasted_iota(jnp.int32, x.shape, 1)
    z = jnp.zeros_like(x)
    return jnp.concatenate([jnp.where(lane < HEAD_W, x, z), jnp.where(lane >= HEAD_W, x, z)], axis=0)


def _dotb(a, b):
    return jnp.dot(a.astype(BF16), b.astype(BF16), preferred_element_type=F32)


def _mm_pair(x, y):
    return _dotb(x, _bd(y))


def _softplus(x):
    return jnp.maximum(x, 0.0) + jnp.log(1.0 + jnp.exp(-jnp.abs(x)))


def _gdn_kernel(cur_ref, prev_ref, next_ref, misc_ref, cw_ref, alog_ref, dtb_ref, eg_ref, eb_ref, ones_ref,
                tri3_ref, s0_ref, o_ref, sfin_ref, xs_ref, s_ref, *, rev, n_blk, n_tap):
    step = pl.program_id(1)
    blk = (n_blk - 1 - step) if rev else step
    c_len = GDN_CHUNK
    hw = GDN_HEADS * HEAD_W

    @pl.when(step == 0)
    def _():
        s_ref[...] = s0_ref[0]

    has_prev = (blk > 0).astype(F32)
    has_next = (blk < n_blk - 1).astype(F32)
    xs_ref[0:GDN_HALO, :] = prev_ref[0].astype(F32) * has_prev
    xs_ref[GDN_HALO:GDN_HALO + GDN_BLOCK, :] = cur_ref[0].astype(F32)
    xs_ref[GDN_HALO + GDN_BLOCK:, :] = next_ref[0].astype(F32) * has_next
    acc = None
    for j in range(n_tap):
        start = GDN_HALO - n_tap // 2 + j
        term = xs_ref[start:start + GDN_BLOCK, :] * cw_ref[j:j + 1, :]
        acc = term if acc is None else acc + term
    y = acc * jax.nn.sigmoid(acc)
    q, k, v = y[:, 0:hw], y[:, hw:2 * hw], y[:, 2 * hw:3 * hw]

    def l2n(x):
        x2 = x * x
        hi = x2.astype(BF16)
        mid = (x2 - hi.astype(F32)).astype(BF16)
        ss = jnp.dot(jnp.concatenate([hi, mid], axis=1), ones_ref[...], preferred_element_type=F32)
        return x * lax.rsqrt(ss + NORM_EPS)

    qn = l2n(q) * (GDN_DK ** -0.5)
    kn = l2n(k)

    misc = misc_ref[0]
    g_all = -jnp.exp(alog_ref[...]) * _softplus(misc + dtb_ref[...])
    b_all = jax.nn.sigmoid(misc)
    gx = jnp.dot(jnp.concatenate(_split3(g_all), axis=1), eg_ref[...], preferred_element_type=F32)
    bx = jnp.dot(jnp.concatenate(_split3(b_all), axis=1), eb_ref[...], preferred_element_type=F32)

    row = lax.broadcasted_iota(jnp.int32, (c_len, LANES), 0)
    colp = lax.broadcasted_iota(jnp.int32, (c_len, LANES), 1) & (HEAD_W - 1)
    tri = (row <= colp) if rev else (row >= colp)
    strict = (row < colp) if rev else (row > colp)
    eye_f = (row == colp).astype(F32)
    row_w = lax.broadcasted_iota(jnp.int32, (c_len, hw), 0)
    col_w = lax.broadcasted_iota(jnp.int32, (c_len, hw), 1) & (HEAD_W - 1)
    eye_w = row_w == col_w
    bd_row = lax.broadcasted_iota(jnp.int32, (LANES, LANES), 0)
    bd_col = lax.broadcasted_iota(jnp.int32, (LANES, LANES), 1)
    bd_mask = (bd_row < HEAD_W) == (bd_col < HEAD_W)
    ones8 = jnp.ones((8, 3 * c_len), BF16)

    keys = [(c, p) for c in range(GDN_STEP_CHUNKS) for p in range(GDN_PAIRS)]
    sl = [slice(p * LANES, (p + 1) * LANES) for p in range(GDN_PAIRS)]
    gtot, vb, kbg, qd, kd = {}, {}, {}, {}, {}
    ms, qks = {}, {}
    chunk_rows = [slice(c * c_len, (c + 1) * c_len) for c in range(GDN_STEP_CHUNKS)]
    gcs = [jnp.dot(tri3_ref[...], jnp.concatenate(_split3(gx[r]), axis=0), preferred_element_type=F32)
           for r in chunk_rows]
    kbs = [kn[r] * bx[r] for r in chunk_rows]
    aqs = {}
    for c, p in keys:
        r = chunk_rows[c]
        lhs = jnp.concatenate([kbs[c][:, sl[p]], qn[r][:, sl[p]]], axis=0)
        aqs[c, p] = lax.dot_general(lhs.astype(BF16), _bd(kn[r][:, sl[p]]).astype(BF16), NT_DIMS,
                                    preferred_element_type=F32)
    grs = [jnp.dot(ones8, jnp.concatenate(_split3(jnp.where(eye_w, gc, 0.0)), axis=0),
                   preferred_element_type=F32)[0:1] for gc in gcs]
    for c, r in enumerate(chunk_rows):
        gc = gcs[c]
        gtot[c] = gc[0:1] if rev else gc[c_len - 1:c_len]
        egc = jnp.exp(gc)
        vb[c] = v[r] * bx[r]
        kbg[c] = kbs[c] * egc
        qd[c] = qn[r] * egc
        kd[c] = kn[r] * jnp.exp(gtot[c] - gc)
        for p in range(GDN_PAIRS):
            diff = gc[:, sl[p]] - grs[c][:, sl[p]]
            decay = jnp.where(tri, jnp.exp(jnp.where(tri, diff, 0.0)), 0.0)
            ms[c, p] = jnp.where(strict, aqs[c, p][0:c_len] * decay, 0.0)
            qks[c, p] = aqs[c, p][c_len:2 * c_len] * decay

    rblk, cblk = row, colp
    xs = {}
    for lvl in range(6):
        sib = (cblk == rblk + 1) if rev else (cblk == rblk - 1)
        odd = (rblk & 1) == (0 if rev else 1)
        join = sib & odd
        if lvl == 0:
            xs = {key: eye_f - jnp.where(join, ms[key], 0.0) for key in keys}
        else:
            inner = {key: _mm_pair(jnp.where(join, ms[key], 0.0), xs[key]) for key in keys}
            xs = {key: xs[key] - _mm_pair(xs[key], inner[key]) for key in keys}
        rblk, cblk = rblk >> 1, cblk >> 1
    uws = {}
    for c, p in keys:
        rhs = jnp.concatenate([_bd(vb[c][:, sl[p]]), _bd(kbg[c][:, sl[p]])], axis=1)
        uws[c, p] = _dotb(xs[c, p], rhs)

    order = range(GDN_STEP_CHUNKS - 1, -1, -1) if rev else range(GDN_STEP_CHUNKS)
    state = [s_ref[p] for p in range(GDN_PAIRS)]
    pairs = range(GDN_PAIRS)
    for c in order:
        r = slice(c * c_len, (c + 1) * c_len)
        wq = [_dotb(jnp.concatenate([uws[c, p][:, LANES:2 * LANES], qd[c][:, sl[p]]], axis=0), state[p])
              for p in pairs]
        v_new = [uws[c, p][:, 0:LANES] - wq[p][0:c_len] for p in pairs]
        upd = [lax.dot_general(kd[c][:, sl[p]].astype(BF16), v_new[p].astype(BF16), TN_DIMS,
                               preferred_element_type=F32) for p in pairs]
        for p in pairs:
            o_ref[0, r, sl[p]] = wq[p][c_len:2 * c_len] + _dotb(qks[c, p], _bd(v_new[p]))
            state[p] = state[p] * jnp.exp(gtot[c][:, sl[p]]) + jnp.where(bd_mask, upd[p], 0.0)
    for p in range(GDN_PAIRS):
        s_ref[p] = state[p]

    @pl.when(step == n_blk - 1)
    def _():
        for p in range(GDN_PAIRS):
            sfin_ref[0, p] = state[p]


def _gdn_consts(conv_w, a_log, dt_bias, d, rev):
    n_tap, w = conv_w.shape
    cw = jnp.concatenate([conv_w.astype(F32), jnp.zeros((8 - n_tap, w), F32)], 0)
    pad = jnp.zeros((LANES - 2 * GDN_HEADS,), F32)
    alog = jnp.concatenate([a_log.reshape(-1).astype(F32), pad]).reshape(1, LANES)
    dtb = jnp.concatenate([dt_bias.reshape(-1).astype(F32), pad]).reshape(1, LANES)
    hw = GDN_HEADS * HEAD_W
    head_of_lane = np.arange(hw) // HEAD_W
    src = np.arange(LANES)[:, None]
    eg = (src == d * GDN_HEADS + head_of_lane[None, :]).astype(np.float32)
    eb = (src == 2 * GDN_HEADS + d * GDN_HEADS + head_of_lane[None, :]).astype(np.float32)
    ones_bd = (head_of_lane[:, None] == head_of_lane[None, :]).astype(np.float32)
    i = np.arange(GDN_CHUNK)
    tri = (i[:, None] <= i[None, :]) if rev else (i[:, None] >= i[None, :])
    as_bf = lambda a, reps, ax: jnp.asarray(np.concatenate([a] * reps, axis=ax), BF16)
    return cw, alog, dtb, as_bf(eg, 3, 0), as_bf(eb, 3, 0), as_bf(ones_bd, 2, 0), as_bf(tri.astype(np.float32), 3, 1)


def _gdn_call(qkv, misc, consts, s0, rev, name):
    b, t, w = qkv.shape
    n_blk = t // GDN_BLOCK
    per = GDN_BLOCK // GDN_HALO
    n_halo = t // GDN_HALO
    cw, alog, dtb, eg, eb, ones_bd, tri3 = consts
    n_tap = 5
    hw = GDN_HEADS * HEAD_W
    blk_of = (lambda s: n_blk - 1 - s) if rev else (lambda s: s)
    const = lambda a: pl.BlockSpec(a.shape, lambda i, s: (0,) * a.ndim)
    s_spec = pl.BlockSpec((1, GDN_PAIRS, LANES, LANES), lambda i, s: (i, 0, 0, 0))
    return pl.pallas_call(
        functools.partial(_gdn_kernel, rev=rev, n_blk=n_blk, n_tap=n_tap),
        grid=(b, n_blk),
        in_specs=[pl.BlockSpec((1, GDN_BLOCK, w), lambda i, s: (i, blk_of(s), 0)),
                  pl.BlockSpec((1, GDN_HALO, w), lambda i, s: (i, jnp.maximum(blk_of(s) * per - 1, 0), 0)),
                  pl.BlockSpec((1, GDN_HALO, w), lambda i, s: (i, jnp.minimum((blk_of(s) + 1) * per, n_halo - 1), 0)),
                  pl.BlockSpec((1, GDN_BLOCK, LANES), lambda i, s: (i, blk_of(s), 0)),
                  const(cw), const(alog), const(dtb), const(eg), const(eb), const(ones_bd), const(tri3), s_spec],
        out_specs=[pl.BlockSpec((1, GDN_BLOCK, hw), lambda i, s: (i, blk_of(s), 0)), s_spec],
        out_shape=[jax.ShapeDtypeStruct((b, t, hw), F32),
                   jax.ShapeDtypeStruct((b, GDN_PAIRS, LANES, LANES), F32)],
        scratch_shapes=[pltpu.VMEM((GDN_BLOCK + 2 * GDN_HALO, w), F32),
                        pltpu.VMEM((GDN_PAIRS, LANES, LANES), F32)],
        compiler_params=_params(2),
        name=name,
    )(qkv, qkv, qkv, misc, cw, alog, dtb, eg, eb, ones_bd, tri3, s0)


MIX_SUB_ROWS = 256


def _mix_ffn_kernel(*refs, ff_chunks, gdn_inputs, n_sub):
    tm = refs[0].shape[1]
    rows = [slice(i * (tm // n_sub), (i + 1) * (tm // n_sub)) for i in range(n_sub)]
    if gdn_inputs:
        (h_ref, of_ref, ob_ref, gate_ref, att_ref, on_ref, ones_ref, gtm_ref, shf_ref, scf_ref, gtf_ref, gpm_ref,
         gpf_ref, gqf_ref, wo_ref, wg_ref, wu_ref, wd_ref, o_ref) = refs
        mixes = []
        for r in rows:
            o = of_ref[0, r, :] + ob_ref[0, r, :]
            gate = gate_ref[0, r, :].astype(F32)
            a = o * lax.rsqrt(_seg_mean_sq(o, ones_ref) + NORM_EPS) * on_ref[...] * (gate * jax.nn.sigmoid(gate))
            mixes.append(jnp.concatenate([a.astype(BF16), att_ref[0, r, :]], axis=1))
    else:
        (h_ref, a_ref, b_ref, gtm_ref, shf_ref, scf_ref, gtf_ref, gpm_ref, gpf_ref, gqf_ref, wo_ref, wg_ref,
         wu_ref, wd_ref, o_ref) = refs
        mixes = [jnp.concatenate([a_ref[0, r, :], b_ref[0, r, :]], axis=1) for r in rows]
    ys = [_dotf(mix, wo_ref[...]) for mix in mixes]
    h1s = [h_ref[0, r, :] + gtm_ref[0] * (_rms(y) * gpm_ref[...]) for r, y in zip(rows, ys)]
    us = [(_rms(h1) * gpf_ref[...] * (1.0 + scf_ref[0]) + shf_ref[0]).astype(BF16) for h1 in h1s]
    acts = []
    for o, n in ff_chunks:
        for u in us:
            gg = _dotf(u, wg_ref[:, o:o + n])
            uu = _dotf(u, wu_ref[:, o:o + n])
            acts.append((gg * jax.nn.sigmoid(gg) * uu).astype(BF16))
    fs = [None] * n_sub
    for ci, (o, n) in enumerate(ff_chunks):
        for i in range(n_sub):
            part = _dotf(acts[ci * n_sub + i], wd_ref[o:o + n, :])
            fs[i] = part if fs[i] is None else fs[i] + part
    for r, h1, f in zip(rows, h1s, fs):
        o_ref[0, r, :] = h1 + gtf_ref[0] * (_rms(f) * gqf_ref[...])


def _mix_ffn_call(h, mix_inputs, out_norm, gt_m, sh_f, sc_f, gt_f, g_post_mix, g_pre_ffn, g_post_ffn, wo, wg, wu,
                  wd, tm, name):
    b, t, d = h.shape
    dm = wo.shape[0]
    ff = wg.shape[1]
    gdn_inputs = out_norm is not None
    half = ff // 2
    assert half % LANES == 0
    ff_chunks = ((0, half), (half, ff - half))
    tok = lambda n: pl.BlockSpec((1, tm, n), lambda i, j: (i, j, 0))
    per_b = pl.BlockSpec((1, 1, d), lambda i, j: (i, 0, 0))
    vec = pl.BlockSpec((1, d), lambda i, j: (0, 0))
    const = lambda shape: pl.BlockSpec(shape, lambda i, j: (0, 0), pipeline_mode=pl.Buffered(1))
    r3 = lambda a: a.reshape(b, 1, d)
    r2 = lambda a: a.reshape(1, d).astype(F32)
    mix_specs = [tok(m.shape[-1]) for m in mix_inputs]
    mix_args = list(mix_inputs)
    if gdn_inputs:
        hw = GDN_HEADS * GDN_DV
        on = jnp.tile(out_norm.astype(F32), GDN_HEADS).reshape(1, hw)
        ones = _seg_ones(hw)
        mix_specs += [pl.BlockSpec(on.shape, lambda i, j: (0, 0)), pl.BlockSpec(ones.shape, lambda i, j: (0, 0))]
        mix_args += [on, ones]
    return pl.pallas_call(
        functools.partial(_mix_ffn_kernel, ff_chunks=ff_chunks, gdn_inputs=gdn_inputs,
                          n_sub=tm // MIX_SUB_ROWS),
        grid=(b, t // tm),
        in_specs=[tok(d)] + mix_specs + [per_b, per_b, per_b, per_b, vec, vec, vec,
                                         const((dm, d)), const((d, ff)), const((d, ff)), const((ff, d))],
        out_specs=tok(d),
        out_shape=jax.ShapeDtypeStruct((b, t, d), F32),
        compiler_params=_params(2),
        name=name,
    )(h, *mix_args, r3(gt_m), r3(sh_f), r3(sc_f), r3(gt_f), r2(g_post_mix), r2(g_pre_ffn), r2(g_post_ffn),
      wo, wg, wu, wd)


def _rope_tables(n_tokens, rot_dim):
    t = jnp.arange(n_tokens, dtype=jnp.int32)
    row = (t // GRID_W).astype(F32)
    col = (t % GRID_W).astype(F32)
    n_freq = rot_dim // 4
    inv_freq = ROPE_THETA ** (-jnp.arange(n_freq, dtype=F32) / n_freq)
    ang = jnp.concatenate([row[:, None] * inv_freq, col[:, None] * inv_freq], -1)
    cos = jnp.concatenate([jnp.cos(ang), jnp.cos(ang)], -1)
    sin = jnp.concatenate([-jnp.sin(ang), jnp.sin(ang)], -1)
    if rot_dim == HEAD_W:
        return jnp.tile(cos, (1, LANES // rot_dim)), jnp.tile(sin, (1, LANES // rot_dim))
    pad = LANES - rot_dim
    return (jnp.concatenate([cos, jnp.ones((n_tokens, pad), F32)], -1),
            jnp.concatenate([sin, jnp.zeros((n_tokens, pad), F32)], -1))


def _no_rope_tables(n_tokens):
    return jnp.ones((n_tokens, LANES), F32), jnp.zeros((n_tokens, LANES), F32)


def _gdn_bidirectional(lat_qkv, lat_misc, ctx_qkv, ctx_misc, conv_w, a_log, dt_bias):
    bsz = lat_qkv.shape[0]
    s_zero = jnp.zeros((bsz, GDN_PAIRS, LANES, LANES), F32)
    lat, ctx = [], []
    for d, rev in ((0, False), (1, True)):
        consts = _gdn_consts(conv_w, a_log, dt_bias, d, rev)
        o_c, s_c = _gdn_call(ctx_qkv, ctx_misc, consts, s_zero, rev, "gdn_ctx_%d" % d)
        o_l, _ = _gdn_call(lat_qkv, lat_misc, consts, s_c, rev, "gdn_lat_%d" % d)
        lat.append(o_l)
        ctx.append(o_c)
    return lat, ctx


LOG2E = 1.4426950408889634
ATTN_TQ = 1024
ATTN_LOOKAHEAD = 2


def _attn_chunks(l_len, s_len, size=512):
    return [(0, 0, l_len)] + [(1, o, size) for o in range(0, s_len, size)]


def _even_mixer(h, hc, mods, mods_c, g_pre, w_in, conv_w, a_log, dt_bias, q_norm, w_q_up, kv_norm, w_kv_up):
    s_len, l_len = h.shape[1], hc.shape[1]
    weights = _even_weights(w_in, q_norm, w_q_up, kv_norm, w_kv_up)
    qkv_l, gate_l, misc_l, q_l, k_l, v_l = _in_even_call(h, g_pre, *mods, weights, *_rope_tables(s_len, MLA_ROPE),
                                                         512, "in_even_lat")
    qkv_c, gate_c, misc_c, q_c, k_c, v_c = _in_even_call(hc, g_pre, *mods_c, weights, *_no_rope_tables(l_len),
                                                         l_len, "in_even_ctx")
    o_lat, o_ctx = _gdn_bidirectional(qkv_l, misc_l, qkv_c, misc_c, conv_w, a_log, dt_bias)
    b_lat = _attn_call(q_l, [(k_c, v_c), (k_l, v_l)], MLA_QC, MLA_PAIRS, MLA_PAIRS, _attn_chunks(l_len, s_len),
                       ATTN_TQ, False, "mla_attn_lat")
    b_ctx = _attn_call(q_c, [(k_c, v_c)], MLA_QC, MLA_PAIRS, MLA_PAIRS, [(0, 0, l_len)], l_len, False,
                       "mla_attn_ctx")
    return (o_lat[0], o_lat[1], gate_l, b_lat), (o_ctx[0], o_ctx[1], gate_c, b_ctx)


def _odd_mixer_last(h, hc, mods, mods_c, g_pre, w_in, rpb, q_norm, k_norm):
    s_len, l_len = h.shape[1], hc.shape[1]
    w = w_in.astype(BF16)
    q_gain = (jnp.tile(q_norm.astype(F32), GQA_HEADS) * (GQA_DIM ** -0.5 * LOG2E)).reshape(1, -1)
    k_gain = jnp.tile(k_norm.astype(F32), GQA_KV_HEADS).reshape(1, -1)
    naq_l, nak_l, nav_l, gq_l, gk_l, gv_l = _in_odd_call(h, g_pre, *mods, w, q_gain, k_gain,
                                                         *_rope_tables(s_len, GQA_DIM), 512, "in_odd_lat")
    _, nak_c, nav_c, _, gk_c, gv_c = _in_odd_call(hc, g_pre, *mods_c, w, q_gain, k_gain, *_no_rope_tables(l_len),
                                                  l_len, "in_odd_ctx")
    c_lat = _na_call(naq_l, nak_l, nav_l, nak_c, nav_c, _na_bias_table(rpb, s_len // GRID_W))
    d_lat = _attn_call(gq_l, [(gk_c, gv_c), (gk_l, gv_l)], LANES, GQA_HEADS // 2, GQA_KV_HEADS,
                       _attn_chunks(l_len, s_len), ATTN_TQ, True, "gqa_attn")
    return c_lat, d_lat


def kernel(x, c, ctx, c_ctx, w_mod, b_mod, g_pre_mix, g_post_mix, g_pre_ffn, g_post_ffn, w_ffn_gate, w_ffn_up,
           w_ffn_down, w_in_even, w_out_even, gdn_conv, gdn_a_log, gdn_dt_bias, gdn_out_norm, mla_q_norm,
           mla_w_q_up, mla_kv_norm, mla_w_kv_up, w_in_odd, w_out_odd, na_rpb, gqa_q_norm, gqa_k_norm):
    bsz, s_len, d = x.shape
    depth = w_mod.shape[0]
    assert depth == 2, "layer 0 = even mixer with context update, layer 1 = odd mixer (last)"
    c_rows = jnp.concatenate([c, c_ctx[None, :], jnp.zeros((16 - bsz - 1, d), F32)], 0)
    h, hc = x, ctx
    for i in range(depth):
        mod_all = _mod_call(c_rows, w_mod[i].astype(BF16), b_mod[i])
        mod = jnp.split(mod_all[:bsz], 6, axis=-1)
        mod_c = [jnp.broadcast_to(m, (bsz, d)) for m in jnp.split(mod_all[bsz:bsz + 1], 6, axis=-1)]
        sh_m, sc_m, gt_m, sh_f, sc_f, gt_f = mod
        csh_m, csc_m, cgt_m, csh_f, csc_f, cgt_f = mod_c
        wg, wu, wd = w_ffn_gate[i].astype(BF16), w_ffn_up[i].astype(BF16), w_ffn_down[i].astype(BF16)
        if i == 0:
            mix, mix_c = _even_mixer(h, hc, (sh_m, sc_m), (csh_m, csc_m), g_pre_mix[i], w_in_even[0], gdn_conv[0],
                                     gdn_a_log[0], gdn_dt_bias[0], mla_q_norm[0], mla_w_q_up[0], mla_kv_norm[0],
                                     mla_w_kv_up[0])
            wo, out_norm = w_out_even[0].astype(BF16), gdn_out_norm[0]
            hc = _mix_ffn_call(hc, mix_c, out_norm, cgt_m, csh_f, csc_f, cgt_f, g_post_mix[i], g_pre_ffn[i],
                               g_post_ffn[i], wo, wg, wu, wd, 256, "mix_ffn_ctx")
        else:
            mix = _odd_mixer_last(h, hc, (sh_m, sc_m), (csh_m, csc_m), g_pre_mix[i], w_in_odd[0], na_rpb[0],
                                  gqa_q_norm[0], gqa_k_norm[0])
            wo, out_norm = w_out_odd[0].astype(BF16), None
        h = _mix_ffn_call(h, mix, out_norm, gt_m, sh_f, sc_f, gt_f, g_post_mix[i], g_pre_ffn[i], g_post_ffn[i],
                          wo, wg, wu, wd, 2 * MIX_SUB_ROWS, "mix_ffn_lat%d" % i)
    return h
```

```python
import functools

import numpy as np
import jax
import jax.numpy as jnp
from jax import lax
from jax.experimental import pallas as pl
from jax.experimental.pallas import tpu as pltpu

F32 = jnp.float32
BF16 = jnp.bfloat16

GRID_W = 64
NORM_EPS = 1e-6
ROPE_THETA = 10000.0

GDN_HEADS = 8
GDN_DK = 64
GDN_DV = 64
GDN_CHUNK = 64
GDN_QK_W = GDN_HEADS * GDN_DK
GDN_V_W = GDN_HEADS * GDN_DV

MLA_HEADS = 8
MLA_NOPE = 64
MLA_ROPE = 32
MLA_V = 64
MLA_Q_LORA = 256
MLA_KV_LORA = 128

NA_HEADS = 8
NA_DIM = 64
NA_WIN_H = 8
NA_WIN_W = 16
NA_W = NA_HEADS * NA_DIM

GQA_HEADS = 8
GQA_KV_HEADS = 2
GQA_DIM = 64

LANES = 128
HEAD_W = 64
MASK_VALUE = -1e30
VMEM_LIMIT = 56 << 20

NT_DIMS = (((1,), (1,)), ((), ()))


def _params(n_grid, vmem=VMEM_LIMIT):
    return pltpu.CompilerParams(dimension_semantics=("arbitrary",) * n_grid, vmem_limit_bytes=vmem)


def _rms(x):
    return x * lax.rsqrt(jnp.mean(x * x, axis=-1, keepdims=True) + NORM_EPS)


def _mod_kernel(c_ref, w_ref, b_ref, o_ref):
    c = c_ref[...]
    a = (c * jax.nn.sigmoid(c)).astype(BF16)
    o_ref[...] = jnp.dot(a, w_ref[...], preferred_element_type=F32) + b_ref[...]


def _mod_call(c_rows, w, b):
    r, d = c_rows.shape
    n = w.shape[1]
    tn = 1024
    return pl.pallas_call(
        _mod_kernel,
        grid=(n // tn,),
        in_specs=[pl.BlockSpec((r, d), lambda j: (0, 0)),
                  pl.BlockSpec((d, tn), lambda j: (0, j)),
                  pl.BlockSpec((1, tn), lambda j: (0, j))],
        out_specs=pl.BlockSpec((r, tn), lambda j: (0, j)),
        out_shape=jax.ShapeDtypeStruct((r, n), F32),
        compiler_params=_params(1),
        name="mod",
    )(c_rows, w, b.reshape(1, n))


def _norm_mod(x_ref, g_ref, sh_ref, sc_ref):
    return (_rms(x_ref[0]) * g_ref[...] * (1.0 + sc_ref[0]) + sh_ref[0]).astype(BF16)


def _dotf(a, b):
    return jnp.dot(a, b, preferred_element_type=F32)


def _seg_mean_sq(x, ones_ref):
    x2 = x * x
    hi = x2.astype(BF16)
    mid = (x2 - hi.astype(F32)).astype(BF16)
    return _dotf(jnp.concatenate([hi, mid], axis=1), ones_ref[...]) * (1.0 / HEAD_W)


def _rotate(t, cos, sin_signed, half):
    lane = lax.broadcasted_iota(jnp.int32, t.shape, 1)
    n = t.shape[1]
    partner = jnp.where((lane & (2 * half - 1)) < half, pltpu.roll(t, n - half, 1), pltpu.roll(t, half, 1))
    return t * cos + partner * sin_signed


EVEN_QKV_W = 2 * GDN_QK_W + GDN_V_W
EVEN_GROUPS = (EVEN_QKV_W, GDN_V_W, MLA_Q_LORA, MLA_KV_LORA, LANES, LANES)
MLA_QC = 2 * LANES
MLA_PAIRS = MLA_HEADS // 2


CONV_HALO = 16


def _in_even_kernel(x_ref, xp_ref, xn_ref, g_ref, sh_ref, sc_ref, w_ref, qg_ref, wq_ref, kvg_ref, wk_ref, wv_ref,
                    vone_ref, cos_ref, sin_ref, cw_ref, ones_ref, qkv_ref, gate_ref, misc_ref, q_ref, k_ref, v_ref,
                    zs_ref, *, n_tap):
    j = pl.program_id(1)
    tm = x_ref.shape[1]
    u = _norm_mod(x_ref, g_ref, sh_ref, sc_ref)
    offs = np.cumsum((0,) + EVEN_GROUPS)
    grp = lambda i: _dotf(u, w_ref[:, int(offs[i]):int(offs[i + 1])])
    u_ext = jnp.concatenate([_norm_mod(xp_ref, g_ref, sh_ref, sc_ref), u, _norm_mod(xn_ref, g_ref, sh_ref, sc_ref)],
                            axis=0)
    z = _dotf(u_ext, w_ref[:, 0:EVEN_QKV_W])
    zs_ref[0:CONV_HALO, :] = z[0:CONV_HALO] * (j > 0).astype(F32)
    zs_ref[CONV_HALO:CONV_HALO + tm, :] = z[CONV_HALO:CONV_HALO + tm]
    zs_ref[CONV_HALO + tm:, :] = z[CONV_HALO + tm:] * (j < pl.num_programs(1) - 1).astype(F32)
    q_down, kv_down, pe_raw = grp(2), grp(3), grp(5)
    gate_ref[0] = grp(1).astype(BF16)
    misc_ref[0] = grp(4)
    cos, sin = cos_ref[...], sin_ref[...]
    qn = (_rms(q_down) * qg_ref[...]).astype(BF16)
    kvn = (_rms(kv_down) * kvg_ref[...]).astype(BF16)
    for h in range(MLA_HEADS):
        blk = _dotf(qn, wq_ref[:, h * MLA_QC:(h + 1) * MLA_QC])
        q_ref[0, :, h * MLA_QC:h * MLA_QC + LANES] = blk[:, 0:LANES].astype(BF16)
        q_ref[0, :, h * MLA_QC + LANES:(h + 1) * MLA_QC] = _rotate(blk[:, LANES:], cos, sin,
                                                                    MLA_ROPE // 2).astype(BF16)
    pe = _rotate(pe_raw, cos, sin, MLA_ROPE // 2).astype(BF16)
    k_nope = _dotf(kvn, wk_ref[...])
    for p in range(MLA_PAIRS):
        k_ref[0, :, p * MLA_QC:p * MLA_QC + LANES] = k_nope[:, p * LANES:(p + 1) * LANES].astype(BF16)
        k_ref[0, :, p * MLA_QC + LANES:(p + 1) * MLA_QC] = pe
    v_ref[0] = (_dotf(kvn, wv_ref[...]) + vone_ref[...]).astype(BF16)
    acc = None
    for t in range(n_tap):
        start = CONV_HALO - n_tap // 2 + t
        term = zs_ref[start:start + tm, :] * cw_ref[t:t + 1, :]
        acc = term if acc is None else acc + term
    y = acc * jax.nn.sigmoid(acc)
    hw = GDN_QK_W
    qc, kc = y[:, 0:hw], y[:, hw:2 * hw]
    q_ss = _seg_mean_sq(qc, ones_ref) * HEAD_W
    k_ss = _seg_mean_sq(kc, ones_ref) * HEAD_W
    qkv_ref[0, :, 0:hw] = (qc * lax.rsqrt(q_ss + NORM_EPS) * (GDN_DK ** -0.5)).astype(BF16)
    qkv_ref[0, :, hw:2 * hw] = (kc * lax.rsqrt(k_ss + NORM_EPS)).astype(BF16)
    qkv_ref[0, :, 2 * hw:] = y[:, 2 * hw:].astype(BF16)


def _even_weights(w_in, q_norm, w_q_up, kv_norm, w_kv_up):
    d = w_in.shape[0]
    o0 = EVEN_QKV_W + GDN_V_W
    n_ab = 4 * GDN_HEADS
    o1 = o0 + n_ab
    o2 = o1 + MLA_Q_LORA
    o3 = o2 + MLA_KV_LORA
    zeros = lambda n, rows=d: jnp.zeros((rows, n), F32)
    w_cat = jnp.concatenate([w_in[:, :o0], w_in[:, o1:o2], w_in[:, o2:o3],
                             w_in[:, o0:o1], zeros(LANES - n_ab),
                             w_in[:, o3:], zeros(LANES - MLA_ROPE)], 1).astype(BF16)
    qh = MLA_NOPE + MLA_ROPE
    cols = []
    for h in range(MLA_HEADS):
        nope = w_q_up[:, h * qh:h * qh + MLA_NOPE]
        rp = w_q_up[:, h * qh + MLA_NOPE:(h + 1) * qh]
        z = zeros(HEAD_W, MLA_Q_LORA)
        cols += ([nope, z] if h % 2 == 0 else [z, nope]) + [rp, zeros(LANES - MLA_ROPE, MLA_Q_LORA)]
    wq = jnp.concatenate(cols, 1).astype(BF16)
    kvh = MLA_NOPE + MLA_V
    wk = jnp.concatenate([w_kv_up[:, h * kvh:h * kvh + MLA_NOPE] for h in range(MLA_HEADS)], 1).astype(BF16)
    vcols, ones = [], []
    for h in range(MLA_HEADS):
        vh = w_kv_up[:, h * kvh + MLA_NOPE:(h + 1) * kvh]
        z = zeros(HEAD_W, MLA_KV_LORA)
        vcols += [vh, z] if h % 2 == 0 else [z, vh]
        ones += [0.0, 1.0] if h % 2 == 0 else [1.0, 0.0]
    wv = jnp.concatenate(vcols, 1).astype(BF16)
    vone = jnp.asarray(np.repeat(np.asarray(ones, np.float32), HEAD_W)[None, :])
    q_gain = (q_norm * ((MLA_NOPE + MLA_ROPE) ** -0.5 * LOG2E)).reshape(1, -1).astype(F32)
    return w_cat, q_gain, wq, kv_norm.reshape(1, -1).astype(F32), wk, wv, vone


def _in_even_call(x, g, shift, scale, weights, conv_w, cos, sin, tm, name):
    b, t, d = x.shape
    w_cat, q_gain, wq, kv_gain, wk, wv, vone = weights
    n_tap = conv_w.shape[0]
    cw = jnp.concatenate([conv_w.astype(F32), jnp.zeros((8 - n_tap, conv_w.shape[1]), F32)], 0)
    ones = _seg_ones(GDN_QK_W)
    per = tm // CONV_HALO
    n_halo = t // CONV_HALO
    const = lambda a: pl.BlockSpec(a.shape, lambda i, j: (0,) * a.ndim)
    per_b = pl.BlockSpec((1, 1, d), lambda i, j: (i, 0, 0))
    tab = pl.BlockSpec((tm, LANES), lambda i, j: (j, 0))
    widths = (EVEN_QKV_W, GDN_V_W, LANES, MLA_HEADS * MLA_QC, MLA_PAIRS * MLA_QC, MLA_HEADS * LANES)
    dts = (BF16, BF16, F32, BF16, BF16, BF16)
    g2 = g.reshape(1, d).astype(F32)
    return pl.pallas_call(
        functools.partial(_in_even_kernel, n_tap=n_tap),
        grid=(b, t // tm),
        in_specs=[pl.BlockSpec((1, tm, d), lambda i, j: (i, j, 0)),
                  pl.BlockSpec((1, CONV_HALO, d), lambda i, j: (i, jnp.maximum(j * per - 1, 0), 0)),
                  pl.BlockSpec((1, CONV_HALO, d), lambda i, j: (i, jnp.minimum((j + 1) * per, n_halo - 1), 0)),
                  const(g2), per_b, per_b, const(w_cat), const(q_gain), const(wq), const(kv_gain), const(wk),
                  const(wv), const(vone), tab, tab, const(cw), const(ones)],
        out_specs=[pl.BlockSpec((1, tm, n), lambda i, j: (i, j, 0)) for n in widths],
        out_shape=[jax.ShapeDtypeStruct((b, t, n), dt) for n, dt in zip(widths, dts)],
        scratch_shapes=[pltpu.VMEM((tm + 2 * CONV_HALO, EVEN_QKV_W), F32)],
        compiler_params=_params(2),
        name=name,
    )(x, x, x, g2, shift.reshape(b, 1, d), scale.reshape(b, 1, d), w_cat, q_gain, wq, kv_gain, wk, wv, vone,
      cos, sin, cw, ones)


GQA_KV_W = GQA_KV_HEADS * GQA_DIM


def _in_odd_kernel(x_ref, g_ref, sh_ref, sc_ref, w_ref, qg_ref, kg_ref, ones_q_ref, ones_k_ref, cos_ref, sin_ref,
                   naq_ref, nak_ref, nav_ref, gq_ref, gk_ref, gv_ref):
    u = _norm_mod(x_ref, g_ref, sh_ref, sc_ref)
    grp = lambda lo, n: _dotf(u, w_ref[:, lo:lo + n])
    o = 3 * NA_W
    q = grp(o, GQA_HEADS * GQA_DIM)
    o += GQA_HEADS * GQA_DIM
    k = grp(o, GQA_KV_W)
    v = grp(o + GQA_KV_W, GQA_KV_W)
    naq_ref[0] = (grp(0, NA_W) * (NA_DIM ** -0.5 * LOG2E)).astype(BF16)
    nak_ref[0] = grp(NA_W, NA_W).astype(BF16)
    na_v = grp(2 * NA_W, NA_W)
    low = lax.broadcasted_iota(jnp.int32, (na_v.shape[0], LANES), 1) < HEAD_W
    for p in range(NA_HEADS // 2):
        tile = na_v[:, p * LANES:(p + 1) * LANES]
        nav_ref[0, :, 2 * p * LANES:(2 * p + 1) * LANES] = tile.astype(BF16)
        nav_ref[0, :, (2 * p + 1) * LANES:(2 * p + 2) * LANES] = jnp.ones(tile.shape, BF16)
    q_ms = _seg_mean_sq(q, ones_q_ref)
    k_ms = _seg_mean_sq(k, ones_k_ref)
    cos, sin = cos_ref[...], sin_ref[...]
    qn = q * lax.rsqrt(q_ms + NORM_EPS) * qg_ref[...]
    for p in range(GQA_HEADS // 2):
        cols = slice(p * LANES, (p + 1) * LANES)
        gq_ref[0, :, cols] = _rotate(qn[:, cols], cos, sin, GQA_DIM // 2).astype(BF16)
    kr = _rotate(k * lax.rsqrt(k_ms + NORM_EPS) * kg_ref[...], cos, sin, GQA_DIM // 2)
    k_sw = pltpu.roll(kr, HEAD_W, 1)
    gk_ref[0, :, 0:LANES] = jnp.where(low, kr, k_sw).astype(BF16)
    gk_ref[0, :, LANES:2 * LANES] = jnp.where(low, k_sw, kr).astype(BF16)
    v_sw = pltpu.roll(v, HEAD_W, 1)
    slots = (jnp.where(low, v, 1.0), jnp.where(low, 1.0, v_sw),
             jnp.where(low, v_sw, 1.0), jnp.where(low, 1.0, v))
    for i, s in enumerate(slots):
        gv_ref[0, :, i * LANES:(i + 1) * LANES] = s.astype(BF16)


def _seg_ones(width):
    head = np.arange(width) // HEAD_W
    m = (head[:, None] == head[None, :]).astype(np.float32)
    return jnp.asarray(np.concatenate([m, m], 0), BF16)


def _in_odd_call(x, g, shift, scale, w, q_gain, k_gain, cos, sin, tm, name):
    b, t, d = x.shape
    assert GQA_KV_W == LANES
    ones_q = _seg_ones(GQA_HEADS * GQA_DIM)
    ones_k = _seg_ones(GQA_KV_W)
    const = lambda a: pl.BlockSpec(a.shape, lambda i, j: (0,) * a.ndim)
    per_b = pl.BlockSpec((1, 1, d), lambda i, j: (i, 0, 0))
    tab = pl.BlockSpec((tm, LANES), lambda i, j: (j, 0))
    widths = (NA_W, NA_W, 2 * NA_W, GQA_HEADS * GQA_DIM, 2 * LANES, 4 * LANES)
    g2 = g.reshape(1, d).astype(F32)
    return pl.pallas_call(
        _in_odd_kernel,
        grid=(b, t // tm),
        in_specs=[pl.BlockSpec((1, tm, d), lambda i, j: (i, j, 0)), const(g2), per_b, per_b, const(w),
                  const(q_gain), const(k_gain), const(ones_q), const(ones_k), tab, tab],
        out_specs=[pl.BlockSpec((1, tm, n), lambda i, j: (i, j, 0)) for n in widths],
        out_shape=[jax.ShapeDtypeStruct((b, t, n), BF16) for n in widths],
        compiler_params=_params(2),
        name=name,
    )(x, g2, shift.reshape(b, 1, d), scale.reshape(b, 1, d), w, q_gain, k_gain, ones_q, ones_k, cos, sin)


def _attn_kernel(*refs, dc, chunks, n_src, masked_q):
    q_ref, kv_refs, o_ref = refs[0], refs[1:1 + 2 * n_src], refs[-1]
    tq = q_ref.shape[1]
    lane = lax.broadcasted_iota(jnp.int32, (tq, LANES), 1)
    qs = []
    for hh in range(2):
        if masked_q:
            qp = q_ref[0]
            qs.append(jnp.where((lane < HEAD_W) if hh == 0 else (lane >= HEAD_W), qp, jnp.zeros_like(qp)))
        else:
            qs.append(q_ref[0, :, hh * dc:(hh + 1) * dc])
    ms, accs = [None, None], [None, None]

    def scores(hh, src, s0, n):
        return lax.dot_general(qs[hh], kv_refs[2 * src][0, s0:s0 + n, :], NT_DIMS, preferred_element_type=F32)

    def consume(s, hh, src, s0, n):
        vv = kv_refs[2 * src + 1][0, s0:s0 + n, hh * LANES:(hh + 1) * LANES]
        mc = jnp.max(s, axis=-1, keepdims=True)
        if ms[hh] is None:
            m_new = mc
            accs[hh] = jnp.dot(jnp.exp2(s - m_new).astype(BF16), vv, preferred_element_type=F32)
        else:
            m_new = jnp.maximum(ms[hh], mc)
            accs[hh] = jnp.exp2(ms[hh] - m_new) * accs[hh] + jnp.dot(jnp.exp2(s - m_new).astype(BF16), vv,
                                                                     preferred_element_type=F32)
        ms[hh] = m_new

    items = [(hh,) + tuple(c) for c in chunks for hh in range(2)]
    pending = []
    for it in items:
        pending.append((scores(*it), it))
        if len(pending) > ATTN_LOOKAHEAD:
            s, it0 = pending.pop(0)
            consume(s, *it0)
    for s, it0 in pending:
        consume(s, *it0)
    low = lane < HEAD_W
    num = jnp.where(low, accs[0], accs[1])
    den = pltpu.roll(jnp.where(low, accs[1], accs[0]), HEAD_W, 1)
    o_ref[0] = (num / den).astype(o_ref.dtype)


def _attn_call(q, kvs, dc, n_pairs, n_groups, chunks, tq, masked_q, name):
    b, t_q, _ = q.shape
    per = n_pairs // n_groups
    q_w = LANES if masked_q else 2 * dc
    in_specs = [pl.BlockSpec((1, tq, q_w), lambda i, p, j: (i, j, p))]
    args = [q]
    for k, v in kvs:
        in_specs += [pl.BlockSpec((1, k.shape[1], dc), lambda i, p, j: (i, 0, p // per)),
                     pl.BlockSpec((1, v.shape[1], 2 * LANES), lambda i, p, j: (i, 0, p // per))]
        args += [k, v]
    return pl.pallas_call(
        functools.partial(_attn_kernel, dc=dc, chunks=tuple(chunks), n_src=len(kvs), masked_q=masked_q),
        grid=(b, n_pairs, t_q // tq),
        in_specs=in_specs,
        out_specs=pl.BlockSpec((1, tq, LANES), lambda i, p, j: (i, j, p)),
        out_shape=jax.ShapeDtypeStruct((b, t_q, n_pairs * LANES), BF16),
        compiler_params=_params(3),
        name=name,
    )(*args)


NA_QROWS = 4
NA_KROWS = 12
NA_LOOKAHEAD = 1


def _na_kernel(q_ref, k_ref, v_ref, kc_ref, vc_ref, bias_ref, o_ref):
    rb = pl.program_id(1)
    n_rows = k_ref.shape[1] // GRID_W
    base = jnp.clip(rb * NA_QROWS - NA_WIN_H // 2, 0, n_rows - NA_KROWS) * GRID_W
    base = pl.multiple_of(base, GRID_W)
    nq = NA_QROWS * GRID_W
    nk = NA_KROWS * GRID_W
    lane = lax.broadcasted_iota(jnp.int32, (nq, LANES), 1)
    low = lane < HEAD_W

    def scores(p):
        cols = slice(p * LANES, (p + 1) * LANES)
        qp = q_ref[0, :, cols]
        zero = jnp.zeros_like(qp)
        q2 = jnp.concatenate([jnp.where(low, qp, zero), jnp.where(low, zero, qp)], axis=0)
        s_loc = lax.dot_general(q2, k_ref[0, pl.ds(base, nk), cols], NT_DIMS, preferred_element_type=F32)
        s_ctx = lax.dot_general(q2, kc_ref[0, :, cols], NT_DIMS, preferred_element_type=F32)
        return s_loc, s_ctx

    def consume(p, s_loc, s_ctx):
        vcols = slice(2 * p * LANES, (2 * p + 2) * LANES)
        s_loc = s_loc + bias_ref[0, p]
        m = jnp.maximum(jnp.max(s_loc, axis=-1, keepdims=True), jnp.max(s_ctx, axis=-1, keepdims=True))
        acc = (_dotf(jnp.exp2(s_loc - m).astype(BF16), v_ref[0, pl.ds(base, nk), vcols])
               + _dotf(jnp.exp2(s_ctx - m).astype(BF16), vc_ref[0, :, vcols]))
        even, odd = acc[0:nq], acc[nq:2 * nq]
        num = jnp.where(low, even[:, 0:LANES], odd[:, 0:LANES])
        den = jnp.where(low, even[:, LANES:2 * LANES], odd[:, LANES:2 * LANES])
        o_ref[0, :, p * LANES:(p + 1) * LANES] = (num / den).astype(o_ref.dtype)

    pending = []
    for p in range(NA_HEADS // 2):
        pending.append((p,) + scores(p))
        if len(pending) > NA_LOOKAHEAD:
            consume(*pending.pop(0))
    for item in pending:
        consume(*item)


def _na_bias_table(rpb, rows):
    n_blocks = rows // NA_QROWS
    n_h = rpb.shape[0]
    c = np.arange(GRID_W)[:, None]
    kc = np.arange(GRID_W)[None, :]
    cs = np.clip(c - NA_WIN_W // 2, 0, GRID_W - NA_WIN_W)
    col_ok = (kc >= cs) & (kc < cs + NA_WIN_W)
    dc = np.clip(kc - c + (NA_WIN_W - 1), 0, 2 * NA_WIN_W - 2)
    pick = (np.arange(2 * NA_WIN_W - 1)[:, None] == dc.reshape(-1)[None, :]).astype(np.float32)
    variants = []
    for r0 in (0, NA_QROWS * (n_blocks // 2), rows - NA_QROWS):
        base = int(np.clip(r0 - NA_WIN_H // 2, 0, rows - NA_KROWS))
        r = r0 + np.arange(NA_QROWS)[:, None]
        kr = base + np.arange(NA_KROWS)[None, :]
        rs = np.clip(r - NA_WIN_H // 2, 0, rows - NA_WIN_H)
        row_ok = (kr >= rs) & (kr < rs + NA_WIN_H)
        dr = np.clip(kr - r + (NA_WIN_H - 1), 0, 2 * NA_WIN_H - 2)
        by_row = rpb[:, dr.reshape(-1), :].astype(F32)
        full = jnp.einsum("hxd,dm->hxm", by_row, jnp.asarray(pick), precision=lax.Precision.HIGHEST)
        full = full.reshape(n_h, NA_QROWS, NA_KROWS, GRID_W, GRID_W).transpose(0, 1, 3, 2, 4)
        full = full.reshape(n_h, NA_QROWS * GRID_W, NA_KROWS * GRID_W)
        valid = (row_ok[:, None, :, None] & col_ok[None, :, None, :]).reshape(full.shape[1:])
        variants.append(jnp.where(valid[None], full * LOG2E, MASK_VALUE))
    return jnp.stack(variants, 0).reshape(3, n_h // 2, 2 * NA_QROWS * GRID_W, NA_KROWS * GRID_W)


def _na_call(q, k, v, kc, vc, bias):
    b, t, w = q.shape
    n_blocks = t // (NA_QROWS * GRID_W)
    nq = NA_QROWS * GRID_W
    tc = kc.shape[1]

    def bias_map(i, r):
        return (jnp.where(r == 0, 0, jnp.where(r == n_blocks - 1, 2, 1)), 0, 0, 0)

    return pl.pallas_call(
        _na_kernel,
        grid=(b, n_blocks),
        in_specs=[pl.BlockSpec((1, nq, w), lambda i, r: (i, r, 0)),
                  pl.BlockSpec((1, t, w), lambda i, r: (i, 0, 0)),
                  pl.BlockSpec((1, t, 2 * w), lambda i, r: (i, 0, 0)),
                  pl.BlockSpec((1, tc, w), lambda i, r: (i, 0, 0)),
                  pl.BlockSpec((1, tc, 2 * w), lambda i, r: (i, 0, 0)),
                  pl.BlockSpec((1, NA_HEADS // 2, 2 * nq, NA_KROWS * GRID_W), bias_map)],
        out_specs=pl.BlockSpec((1, nq, w), lambda i, r: (i, r, 0)),
        out_shape=jax.ShapeDtypeStruct((b, t, w), BF16),
        compiler_params=_params(2),
        name="na_attn",
    )(q, k, v, kc, vc, bias)


GDN_STEP_CHUNKS = 4
GDN_BLOCK = GDN_STEP_CHUNKS * GDN_CHUNK
GDN_PAIRS = GDN_HEADS // 2
TN_DIMS = (((0,), (0,)), ((), ()))


def _split3(x):
    hi = x.astype(BF16)
    r = x - hi.astype(F32)
    mid = r.astype(BF16)
    lo = (r - mid.astype(F32)).astype(BF16)
    return hi, mid, lo


def _bd(x):
    lane = lax.broadcasted_iota(jnp.int32, x.shape, 1)
    z = jnp.zeros_like(x)
    return jnp.concatenate([jnp.where(lane < HEAD_W, x, z), jnp.where(lane >= HEAD_W, x, z)], axis=0)


def _dotb(a, b):
    return jnp.dot(a.astype(BF16), b.astype(BF16), preferred_element_type=F32)


def _mm_pair(x, y):
    return _dotb(x, _bd(y))


def _softplus(x):
    return jnp.maximum(x, 0.0) + jnp.log(1.0 + jnp.exp(-jnp.abs(x)))


def _gdn_kernel(qkv_ref, misc_ref, alog_ref, dtb_ref, eg_ref, eb_ref, tri3_ref, s0_ref, o_ref, sfin_ref, s_ref, *,
                rev, n_blk):
    step = pl.program_id(1)
    c_len = GDN_CHUNK
    hw = GDN_HEADS * HEAD_W

    @pl.when(step == 0)
    def _():
        s_ref[...] = s0_ref[0]

    qn = qkv_ref[0, :, 0:hw].astype(F32)
    kn = qkv_ref[0, :, hw:2 * hw].astype(F32)
    v = qkv_ref[0, :, 2 * hw:3 * hw].astype(F32)

    misc = misc_ref[0]
    g_all = -jnp.exp(alog_ref[...]) * _softplus(misc + dtb_ref[...])
    b_all = jax.nn.sigmoid(misc)
    gx = jnp.dot(jnp.concatenate(_split3(g_all), axis=1), eg_ref[...], preferred_element_type=F32)
    bx = jnp.dot(jnp.concatenate(_split3(b_all), axis=1), eb_ref[...], preferred_element_type=F32)

    row = lax.broadcasted_iota(jnp.int32, (c_len, LANES), 0)
    colp = lax.broadcasted_iota(jnp.int32, (c_len, LANES), 1) & (HEAD_W - 1)
    tri = (row <= colp) if rev else (row >= colp)
    strict = (row < colp) if rev else (row > colp)
    eye_f = (row == colp).astype(F32)
    row_w = lax.broadcasted_iota(jnp.int32, (c_len, hw), 0)
    col_w = lax.broadcasted_iota(jnp.int32, (c_len, hw), 1) & (HEAD_W - 1)
    eye_w = row_w == col_w
    bd_row = lax.broadcasted_iota(jnp.int32, (LANES, LANES), 0)
    bd_col = lax.broadcasted_iota(jnp.int32, (LANES, LANES), 1)
    bd_mask = (bd_row < HEAD_W) == (bd_col < HEAD_W)
    ones8 = jnp.ones((8, 3 * c_len), BF16)

    keys = [(c, p) for c in range(GDN_STEP_CHUNKS) for p in range(GDN_PAIRS)]
    sl = [slice(p * LANES, (p + 1) * LANES) for p in range(GDN_PAIRS)]
    gtot, vb, kbg, qd, kd = {}, {}, {}, {}, {}
    ms, qks = {}, {}
    chunk_rows = [slice(c * c_len, (c + 1) * c_len) for c in range(GDN_STEP_CHUNKS)]
    gcs = [jnp.dot(tri3_ref[...], jnp.concatenate(_split3(gx[r]), axis=0), preferred_element_type=F32)
           for r in chunk_rows]
    kbs = [kn[r] * bx[r] for r in chunk_rows]
    aqs = {}
    for c, p in keys:
        r = chunk_rows[c]
        lhs = jnp.concatenate([kbs[c][:, sl[p]], qn[r][:, sl[p]]], axis=0)
        aqs[c, p] = lax.dot_general(lhs.astype(BF16), _bd(kn[r][:, sl[p]]).astype(BF16), NT_DIMS,
                                    preferred_element_type=F32)
    grs = [jnp.dot(ones8, jnp.concatenate(_split3(jnp.where(eye_w, gc, 0.0)), axis=0),
                   preferred_element_type=F32)[0:1] for gc in gcs]
    for c, r in enumerate(chunk_rows):
        gc = gcs[c]
        gtot[c] = gc[0:1] if rev else gc[c_len - 1:c_len]
        egc = jnp.exp(gc)
        vb[c] = v[r] * bx[r]
        kbg[c] = kbs[c] * egc
        qd[c] = qn[r] * egc
        kd[c] = kn[r] * jnp.exp(gtot[c] - gc)
        for p in range(GDN_PAIRS):
            diff = gc[:, sl[p]] - grs[c][:, sl[p]]
            decay = jnp.where(tri, jnp.exp(jnp.where(tri, diff, 0.0)), 0.0)
            ms[c, p] = jnp.where(strict, aqs[c, p][0:c_len] * decay, 0.0)
            qks[c, p] = aqs[c, p][c_len:2 * c_len] * decay

    rblk, cblk = row, colp
    xs = {}
    for lvl in range(6):
        sib = (cblk == rblk + 1) if rev else (cblk == rblk - 1)
        odd = (rblk & 1) == (0 if rev else 1)
        join = sib & odd
        if lvl == 0:
            xs = {key: eye_f - jnp.where(join, ms[key], 0.0) for key in keys}
        else:
            inner = {key: _mm_pair(jnp.where(join, ms[key], 0.0), xs[key]) for key in keys}
            xs = {key: xs[key] - _mm_pair(xs[key], inner[key]) for key in keys}
        rblk, cblk = rblk >> 1, cblk >> 1
    uws = {}
    for c, p in keys:
        rhs = jnp.concatenate([_bd(vb[c][:, sl[p]]), _bd(kbg[c][:, sl[p]])], axis=1)
        uws[c, p] = _dotb(xs[c, p], rhs)

    order = range(GDN_STEP_CHUNKS - 1, -1, -1) if rev else range(GDN_STEP_CHUNKS)
    state = [s_ref[p] for p in range(GDN_PAIRS)]
    pairs = range(GDN_PAIRS)
    for c in order:
        r = slice(c * c_len, (c + 1) * c_len)
        wq = [_dotb(jnp.concatenate([uws[c, p][:, LANES:2 * LANES], qd[c][:, sl[p]]], axis=0), state[p])
              for p in pairs]
        v_new = [uws[c, p][:, 0:LANES] - wq[p][0:c_len] for p in pairs]
        upd = [lax.dot_general(kd[c][:, sl[p]].astype(BF16), v_new[p].astype(BF16), TN_DIMS,
                               preferred_element_type=F32) for p in pairs]
        for p in pairs:
            o_ref[0, r, sl[p]] = wq[p][c_len:2 * c_len] + _dotb(qks[c, p], _bd(v_new[p]))
            state[p] = state[p] * jnp.exp(gtot[c][:, sl[p]]) + jnp.where(bd_mask, upd[p], 0.0)
    for p in range(GDN_PAIRS):
        s_ref[p] = state[p]

    @pl.when(step == n_blk - 1)
    def _():
        for p in range(GDN_PAIRS):
            sfin_ref[0, p] = state[p]


def _gdn_consts(a_log, dt_bias, d, rev):
    pad = jnp.zeros((LANES - 2 * GDN_HEADS,), F32)
    alog = jnp.concatenate([a_log.reshape(-1).astype(F32), pad]).reshape(1, LANES)
    dtb = jnp.concatenate([dt_bias.reshape(-1).astype(F32), pad]).reshape(1, LANES)
    hw = GDN_HEADS * HEAD_W
    head_of_lane = np.arange(hw) // HEAD_W
    src = np.arange(LANES)[:, None]
    eg = (src == d * GDN_HEADS + head_of_lane[None, :]).astype(np.float32)
    eb = (src == 2 * GDN_HEADS + d * GDN_HEADS + head_of_lane[None, :]).astype(np.float32)
    i = np.arange(GDN_CHUNK)
    tri = (i[:, None] <= i[None, :]) if rev else (i[:, None] >= i[None, :])
    as_bf = lambda a, reps, ax: jnp.asarray(np.concatenate([a] * reps, axis=ax), BF16)
    return alog, dtb, as_bf(eg, 3, 0), as_bf(eb, 3, 0), as_bf(tri.astype(np.float32), 3, 1)


def _gdn_call(qkv, misc, consts, s0, rev, name):
    b, t, w = qkv.shape
    n_blk = t // GDN_BLOCK
    hw = GDN_HEADS * HEAD_W
    blk_of = (lambda s: n_blk - 1 - s) if rev else (lambda s: s)
    const = lambda a: pl.BlockSpec(a.shape, lambda i, s: (0,) * a.ndim)
    s_spec = pl.BlockSpec((1, GDN_PAIRS, LANES, LANES), lambda i, s: (i, 0, 0, 0))
    return pl.pallas_call(
        functools.partial(_gdn_kernel, rev=rev, n_blk=n_blk),
        grid=(b, n_blk),
        in_specs=[pl.BlockSpec((1, GDN_BLOCK, w), lambda i, s: (i, blk_of(s), 0)),
                  pl.BlockSpec((1, GDN_BLOCK, LANES), lambda i, s: (i, blk_of(s), 0))]
                 + [const(a) for a in consts] + [s_spec],
        out_specs=[pl.BlockSpec((1, GDN_BLOCK, hw), lambda i, s: (i, blk_of(s), 0)), s_spec],
        out_shape=[jax.ShapeDtypeStruct((b, t, hw), F32),
                   jax.ShapeDtypeStruct((b, GDN_PAIRS, LANES, LANES), F32)],
        scratch_shapes=[pltpu.VMEM((GDN_PAIRS, LANES, LANES), F32)],
        compiler_params=_params(2),
        name=name,
    )(qkv, misc, *consts, s0)


MIX_SUB_ROWS = 256


def _mix_ffn_kernel(*refs, ff_chunks, gdn_inputs, n_sub):
    tm = refs[0].shape[1]
    rows = [slice(i * (tm // n_sub), (i + 1) * (tm // n_sub)) for i in range(n_sub)]
    if gdn_inputs:
        (h_ref, of_ref, ob_ref, gate_ref, att_ref, on_ref, ones_ref, gtm_ref, shf_ref, scf_ref, gtf_ref, gpm_ref,
         gpf_ref, gqf_ref, wo_ref, wg_ref, wu_ref, wd_ref, o_ref) = refs
        mixes = []
        for r in rows:
            o = of_ref[0, r, :] + ob_ref[0, r, :]
            gate = gate_ref[0, r, :].astype(F32)
            a = o * lax.rsqrt(_seg_mean_sq(o, ones_ref) + NORM_EPS) * on_ref[...] * (gate * jax.nn.sigmoid(gate))
            mixes.append(jnp.concatenate([a.astype(BF16), att_ref[0, r, :]], axis=1))
    else:
        (h_ref, a_ref, b_ref, gtm_ref, shf_ref, scf_ref, gtf_ref, gpm_ref, gpf_ref, gqf_ref, wo_ref, wg_ref,
         wu_ref, wd_ref, o_ref) = refs
        mixes = [jnp.concatenate([a_ref[0, r, :], b_ref[0, r, :]], axis=1) for r in rows]
    ys = [_dotf(mix, wo_ref[...]) for mix in mixes]
    h1s = [h_ref[0, r, :] + gtm_ref[0] * (_rms(y) * gpm_ref[...]) for r, y in zip(rows, ys)]
    us = [(_rms(h1) * gpf_ref[...] * (1.0 + scf_ref[0]) + shf_ref[0]).astype(BF16) for h1 in h1s]
    acts = []
    for o, n in ff_chunks:
        for u in us:
            gg = _dotf(u, wg_ref[:, o:o + n])
            uu = _dotf(u, wu_ref[:, o:o + n])
            acts.append((gg * jax.nn.sigmoid(gg) * uu).astype(BF16))
    fs = [None] * n_sub
    for ci, (o, n) in enumerate(ff_chunks):
        for i in range(n_sub):
            part = _dotf(acts[ci * n_sub + i], wd_ref[o:o + n, :])
            fs[i] = part if fs[i] is None else fs[i] + part
    for r, h1, f in zip(rows, h1s, fs):
        o_ref[0, r, :] = h1 + gtf_ref[0] * (_rms(f) * gqf_ref[...])


def _mix_ffn_call(h, mix_inputs, out_norm, gt_m, sh_f, sc_f, gt_f, g_post_mix, g_pre_ffn, g_post_ffn, wo, wg, wu,
                  wd, tm, name):
    b, t, d = h.shape
    dm = wo.shape[0]
    ff = wg.shape[1]
    gdn_inputs = out_norm is not None
    half = ff // 2
    assert half % LANES == 0
    ff_chunks = ((0, half), (half, ff - half))
    tok = lambda n: pl.BlockSpec((1, tm, n), lambda i, j: (i, j, 0))
    per_b = pl.BlockSpec((1, 1, d), lambda i, j: (i, 0, 0))
    vec = pl.BlockSpec((1, d), lambda i, j: (0, 0))
    const = lambda shape: pl.BlockSpec(shape, lambda i, j: (0, 0), pipeline_mode=pl.Buffered(1))
    r3 = lambda a: a.reshape(b, 1, d)
    r2 = lambda a: a.reshape(1, d).astype(F32)
    mix_specs = [tok(m.shape[-1]) for m in mix_inputs]
    mix_args = list(mix_inputs)
    if gdn_inputs:
        hw = GDN_HEADS * GDN_DV
        on = jnp.tile(out_norm.astype(F32), GDN_HEADS).reshape(1, hw)
        ones = _seg_ones(hw)
        mix_specs += [pl.BlockSpec(on.shape, lambda i, j: (0, 0)), pl.BlockSpec(ones.shape, lambda i, j: (0, 0))]
        mix_args += [on, ones]
    return pl.pallas_call(
        functools.partial(_mix_ffn_kernel, ff_chunks=ff_chunks, gdn_inputs=gdn_inputs,
                          n_sub=tm // MIX_SUB_ROWS),
        grid=(b, t // tm),
        in_specs=[tok(d)] + mix_specs + [per_b, per_b, per_b, per_b, vec, vec, vec,
                                         const((dm, d)), const((d, ff)), const((d, ff)), const((ff, d))],
        out_specs=tok(d),
        out_shape=jax.ShapeDtypeStruct((b, t, d), F32),
        compiler_params=_params(2),
        name=name,
    )(h, *mix_args, r3(gt_m), r3(sh_f), r3(sc_f), r3(gt_f), r2(g_post_mix), r2(g_pre_ffn), r2(g_post_ffn),
      wo, wg, wu, wd)


def _rope_tables(n_tokens, rot_dim):
    t = jnp.arange(n_tokens, dtype=jnp.int32)
    row = (t // GRID_W).astype(F32)
    col = (t % GRID_W).astype(F32)
    n_freq = rot_dim // 4
    inv_freq = ROPE_THETA ** (-jnp.arange(n_freq, dtype=F32) / n_freq)
    ang = jnp.concatenate([row[:, None] * inv_freq, col[:, None] * inv_freq], -1)
    cos = jnp.concatenate([jnp.cos(ang), jnp.cos(ang)], -1)
    sin = jnp.concatenate([-jnp.sin(ang), jnp.sin(ang)], -1)
    if rot_dim == HEAD_W:
        return jnp.tile(cos, (1, LANES // rot_dim)), jnp.tile(sin, (1, LANES // rot_dim))
    pad = LANES - rot_dim
    return (jnp.concatenate([cos, jnp.ones((n_tokens, pad), F32)], -1),
            jnp.concatenate([sin, jnp.zeros((n_tokens, pad), F32)], -1))


def _no_rope_tables(n_tokens):
    return jnp.ones((n_tokens, LANES), F32), jnp.zeros((n_tokens, LANES), F32)


def _gdn_bidirectional(lat_qkv, lat_misc, ctx_qkv, ctx_misc, a_log, dt_bias):
    bsz = lat_qkv.shape[0]
    s_zero = jnp.zeros((bsz, GDN_PAIRS, LANES, LANES), F32)
    lat, ctx = [], []
    for d, rev in ((0, False), (1, True)):
        consts = _gdn_consts(a_log, dt_bias, d, rev)
        o_c, s_c = _gdn_call(ctx_qkv, ctx_misc, consts, s_zero, rev, "gdn_ctx_%d" % d)
        o_l, _ = _gdn_call(lat_qkv, lat_misc, consts, s_c, rev, "gdn_lat_%d" % d)
        lat.append(o_l)
        ctx.append(o_c)
    return lat, ctx


LOG2E = 1.4426950408889634
ATTN_TQ = 1024
ATTN_LOOKAHEAD = 2


def _attn_chunks(l_len, s_len, size=512):
    return [(0, 0, l_len)] + [(1, o, size) for o in range(0, s_len, size)]


def _even_mixer(h, hc, mods, mods_c, g_pre, w_in, conv_w, a_log, dt_bias, q_norm, w_q_up, kv_norm, w_kv_up):
    s_len, l_len = h.shape[1], hc.shape[1]
    weights = _even_weights(w_in, q_norm, w_q_up, kv_norm, w_kv_up)
    qkv_l, gate_l, misc_l, q_l, k_l, v_l = _in_even_call(h, g_pre, *mods, weights, conv_w,
                                                         *_rope_tables(s_len, MLA_ROPE), 512, "in_even_lat")
    qkv_c, gate_c, misc_c, q_c, k_c, v_c = _in_even_call(hc, g_pre, *mods_c, weights, conv_w,
                                                         *_no_rope_tables(l_len), l_len, "in_even_ctx")
    o_lat, o_ctx = _gdn_bidirectional(qkv_l, misc_l, qkv_c, misc_c, a_log, dt_bias)
    b_lat = _attn_call(q_l, [(k_c, v_c), (k_l, v_l)], MLA_QC, MLA_PAIRS, MLA_PAIRS, _attn_chunks(l_len, s_len),
                       ATTN_TQ, False, "mla_attn_lat")
    b_ctx = _attn_call(q_c, [(k_c, v_c)], MLA_QC, MLA_PAIRS, MLA_PAIRS, [(0, 0, l_len)], l_len, False,
                       "mla_attn_ctx")
    return (o_lat[0], o_lat[1], gate_l, b_lat), (o_ctx[0], o_ctx[1], gate_c, b_ctx)


def _odd_mixer_last(h, hc, mods, mods_c, g_pre, w_in, rpb, q_norm, k_norm):
    s_len, l_len = h.shape[1], hc.shape[1]
    w = w_in.astype(BF16)
    q_gain = (jnp.tile(q_norm.astype(F32), GQA_HEADS) * (GQA_DIM ** -0.5 * LOG2E)).reshape(1, -1)
    k_gain = jnp.tile(k_norm.astype(F32), GQA_KV_HEADS).reshape(1, -1)
    naq_l, nak_l, nav_l, gq_l, gk_l, gv_l = _in_odd_call(h, g_pre, *mods, w, q_gain, k_gain,
                                                         *_rope_tables(s_len, GQA_DIM), 512, "in_odd_lat")
    _, nak_c, nav_c, _, gk_c, gv_c = _in_odd_call(hc, g_pre, *mods_c, w, q_gain, k_gain, *_no_rope_tables(l_len),
                                                  l_len, "in_odd_ctx")
    c_lat = _na_call(naq_l, nak_l, nav_l, nak_c, nav_c, _na_bias_table(rpb, s_len // GRID_W))
    d_lat = _attn_call(gq_l, [(gk_c, gv_c), (gk_l, gv_l)], LANES, GQA_HEADS // 2, GQA_KV_HEADS,
                       _attn_chunks(l_len, s_len), ATTN_TQ, True, "gqa_attn")
    return c_lat, d_lat


def kernel(x, c, ctx, c_ctx, w_mod, b_mod, g_pre_mix, g_post_mix, g_pre_ffn, g_post_ffn, w_ffn_gate, w_ffn_up,
           w_ffn_down, w_in_even, w_out_even, gdn_conv, gdn_a_log, gdn_dt_bias, gdn_out_norm, mla_q_norm,
           mla_w_q_up, mla_kv_norm, mla_w_kv_up, w_in_odd, w_out_odd, na_rpb, gqa_q_norm, gqa_k_norm):
    bsz, s_len, d = x.shape
    depth = w_mod.shape[0]
    assert depth == 2, "layer 0 = even mixer with context update, layer 1 = odd mixer (last)"
    c_rows = jnp.concatenate([c, c_ctx[None, :], jnp.zeros((16 - bsz - 1, d), F32)], 0)
    h, hc = x, ctx
    for i in range(depth):
        mod_all = _mod_call(c_rows, w_mod[i].astype(BF16), b_mod[i])
        mod = jnp.split(mod_all[:bsz], 6, axis=-1)
        mod_c = [jnp.broadcast_to(m, (bsz, d)) for m in jnp.split(mod_all[bsz:bsz + 1], 6, axis=-1)]
        sh_m, sc_m, gt_m, sh_f, sc_f, gt_f = mod
        csh_m, csc_m, cgt_m, csh_f, csc_f, cgt_f = mod_c
        wg, wu, wd = w_ffn_gate[i].astype(BF16), w_ffn_up[i].astype(BF16), w_ffn_down[i].astype(BF16)
        if i == 0:
            mix, mix_c = _even_mixer(h, hc, (sh_m, sc_m), (csh_m, csc_m), g_pre_mix[i], w_in_even[0], gdn_conv[0],
                                     gdn_a_log[0], gdn_dt_bias[0], mla_q_norm[0], mla_w_q_up[0], mla_kv_norm[0],
                                     mla_w_kv_up[0])
            wo, out_norm = w_out_even[0].astype(BF16), gdn_out_norm[0]
            hc = _mix_ffn_call(hc, mix_c, out_norm, cgt_m, csh_f, csc_f, cgt_f, g_post_mix[i], g_pre_ffn[i],
                               g_post_ffn[i], wo, wg, wu, wd, 256, "mix_ffn_ctx")
        else:
            mix = _odd_mixer_last(h, hc, (sh_m, sc_m), (csh_m, csc_m), g_pre_mix[i], w_in_odd[0], na_rpb[0],
                                  gqa_q_norm[0], gqa_k_norm[0])
            wo, out_norm = w_out_odd[0].astype(BF16), None
        h = _mix_ffn_call(h, mix, out_norm, gt_m, sh_f, sc_f, gt_f, g_post_mix[i], g_pre_ffn[i], g_post_ffn[i],
                          wo, wg, wu, wd, 2 * MIX_SUB_ROWS, "mix_ffn_lat%d" % i)
    return h
```

```python
import functools

import numpy as np
import jax
import jax.numpy as jnp
from jax import lax
from jax.experimental import pallas as pl
from jax.experimental.pallas import tpu as pltpu

F32 = jnp.float32
BF16 = jnp.bfloat16

GRID_W = 64
NORM_EPS = 1e-6
ROPE_THETA = 10000.0

GDN_HEADS = 8
GDN_DK = 64
GDN_DV = 64
GDN_CHUNK = 64
GDN_QK_W = GDN_HEADS * GDN_DK
GDN_V_W = GDN_HEADS * GDN_DV

MLA_HEADS = 8
MLA_NOPE = 64
MLA_ROPE = 32
MLA_V = 64
MLA_Q_LORA = 256
MLA_KV_LORA = 128

NA_HEADS = 8
NA_DIM = 64
NA_WIN_H = 8
NA_WIN_W = 16
NA_W = NA_HEADS * NA_DIM

GQA_HEADS = 8
GQA_KV_HEADS = 2
GQA_DIM = 64

LANES = 128
MXU_TILE = 256
HEAD_W = 64
MASK_VALUE = -1e30
VMEM_LIMIT = 56 << 20

NT_DIMS = (((1,), (1,)), ((), ()))


def _params(n_grid, vmem=VMEM_LIMIT):
    return pltpu.CompilerParams(dimension_semantics=("arbitrary",) * n_grid, vmem_limit_bytes=vmem)


def _rms(x):
    return x * lax.rsqrt(jnp.mean(x * x, axis=-1, keepdims=True) + NORM_EPS)


def _mod_kernel(c_ref, w_ref, b_ref, o_ref):
    c = c_ref[...]
    a = (c * jax.nn.sigmoid(c)).astype(BF16)
    o_ref[...] = jnp.dot(a, w_ref[...], preferred_element_type=F32) + b_ref[...]


def _mod_call(c_rows, w, b):
    r, d = c_rows.shape
    n = w.shape[1]
    tn = 1024
    return pl.pallas_call(
        _mod_kernel,
        grid=(n // tn,),
        in_specs=[pl.BlockSpec((r, d), lambda j: (0, 0)),
                  pl.BlockSpec((d, tn), lambda j: (0, j)),
                  pl.BlockSpec((1, tn), lambda j: (0, j))],
        out_specs=pl.BlockSpec((r, tn), lambda j: (0, j)),
        out_shape=jax.ShapeDtypeStruct((r, n), F32),
        compiler_params=_params(1),
        name="mod",
    )(c_rows, w, b.reshape(1, n))


def _norm_mod(x_ref, g_ref, sh_ref, sc_ref):
    return (_rms(x_ref[0]) * g_ref[...] * (1.0 + sc_ref[0]) + sh_ref[0]).astype(BF16)


def _dotf(a, b):
    return jnp.dot(a, b, preferred_element_type=F32)


def _seg_mean_sq(x, ones_ref):
    x2 = x * x
    hi = x2.astype(BF16)
    mid = (x2 - hi.astype(F32)).astype(BF16)
    return _dotf(jnp.concatenate([hi, mid], axis=1), ones_ref[...]) * (1.0 / HEAD_W)


def _rotate(t, cos, sin_signed, half):
    lane = lax.broadcasted_iota(jnp.int32, t.shape, 1)
    n = t.shape[1]
    partner = jnp.where((lane & (2 * half - 1)) < half, pltpu.roll(t, n - half, 1), pltpu.roll(t, half, 1))
    return t * cos + partner * sin_signed


EVEN_QKV_W = 2 * GDN_QK_W + GDN_V_W
EVEN_GROUPS = (EVEN_QKV_W, GDN_V_W, MLA_Q_LORA, MLA_KV_LORA, LANES, LANES)
MLA_QC = 2 * LANES
MLA_PAIRS = MLA_HEADS // 2


CONV_HALO = 16


def _in_even_kernel(x_ref, xp_ref, xn_ref, g_ref, sh_ref, sc_ref, w_ref, qg_ref, wq_ref, kvg_ref, wk_ref, wv_ref,
                    vone_ref, cos_ref, sin_ref, cw_ref, ones_ref, qkv_ref, gate_ref, misc_ref, q_ref, k_ref, v_ref,
                    zs_ref, *, n_tap):
    j = pl.program_id(1)
    tm = x_ref.shape[1]
    u = _norm_mod(x_ref, g_ref, sh_ref, sc_ref)
    offs = np.cumsum((0,) + EVEN_GROUPS)
    grp = lambda i: _dotf(u, w_ref[:, int(offs[i]):int(offs[i + 1])])
    u_ext = jnp.concatenate([_norm_mod(xp_ref, g_ref, sh_ref, sc_ref), u, _norm_mod(xn_ref, g_ref, sh_ref, sc_ref)],
                            axis=0)
    z = _dotf(u_ext, w_ref[:, 0:EVEN_QKV_W])
    zs_ref[0:CONV_HALO, :] = z[0:CONV_HALO] * (j > 0).astype(F32)
    zs_ref[CONV_HALO:CONV_HALO + tm, :] = z[CONV_HALO:CONV_HALO + tm]
    zs_ref[CONV_HALO + tm:, :] = z[CONV_HALO + tm:] * (j < pl.num_programs(1) - 1).astype(F32)
    q_down, kv_down, pe_raw = grp(2), grp(3), grp(5)
    gate_ref[0] = grp(1).astype(BF16)
    misc_ref[0] = grp(4)
    cos, sin = cos_ref[...], sin_ref[...]
    qn = (_rms(q_down) * qg_ref[...]).astype(BF16)
    kvn = (_rms(kv_down) * kvg_ref[...]).astype(BF16)
    for h in range(MLA_HEADS):
        blk = _dotf(qn, wq_ref[:, h * MLA_QC:(h + 1) * MLA_QC])
        q_ref[0, :, h * MLA_QC:h * MLA_QC + LANES] = blk[:, 0:LANES].astype(BF16)
        q_ref[0, :, h * MLA_QC + LANES:(h + 1) * MLA_QC] = _rotate(blk[:, LANES:], cos, sin,
                                                                    MLA_ROPE // 2).astype(BF16)
    pe = _rotate(pe_raw, cos, sin, MLA_ROPE // 2).astype(BF16)
    k_nope = _dotf(kvn, wk_ref[...])
    for p in range(MLA_PAIRS):
        k_ref[0, :, p * MLA_QC:p * MLA_QC + LANES] = k_nope[:, p * LANES:(p + 1) * LANES].astype(BF16)
        k_ref[0, :, p * MLA_QC + LANES:(p + 1) * MLA_QC] = pe
    v_ref[0] = (_dotf(kvn, wv_ref[...]) + vone_ref[...]).astype(BF16)
    acc = None
    for t in range(n_tap):
        start = CONV_HALO - n_tap // 2 + t
        term = zs_ref[start:start + tm, :] * cw_ref[t:t + 1, :]
        acc = term if acc is None else acc + term
    y = acc * jax.nn.sigmoid(acc)
    hw = GDN_QK_W
    qc, kc = y[:, 0:hw], y[:, hw:2 * hw]
    q_ss = _seg_mean_sq(qc, ones_ref) * HEAD_W
    k_ss = _seg_mean_sq(kc, ones_ref) * HEAD_W
    qkv_ref[0, :, 0:hw] = (qc * lax.rsqrt(q_ss + NORM_EPS) * (GDN_DK ** -0.5)).astype(BF16)
    qkv_ref[0, :, hw:2 * hw] = (kc * lax.rsqrt(k_ss + NORM_EPS)).astype(BF16)
    qkv_ref[0, :, 2 * hw:] = y[:, 2 * hw:].astype(BF16)


def _even_weights(w_in, q_norm, w_q_up, kv_norm, w_kv_up):
    d = w_in.shape[0]
    o0 = EVEN_QKV_W + GDN_V_W
    n_ab = 4 * GDN_HEADS
    o1 = o0 + n_ab
    o2 = o1 + MLA_Q_LORA
    o3 = o2 + MLA_KV_LORA
    zeros = lambda n, rows=d: jnp.zeros((rows, n), F32)
    w_cat = jnp.concatenate([w_in[:, :o0], w_in[:, o1:o2], w_in[:, o2:o3],
                             w_in[:, o0:o1], zeros(LANES - n_ab),
                             w_in[:, o3:], zeros(LANES - MLA_ROPE)], 1).astype(BF16)
    qh = MLA_NOPE + MLA_ROPE
    cols = []
    for h in range(MLA_HEADS):
        nope = w_q_up[:, h * qh:h * qh + MLA_NOPE]
        rp = w_q_up[:, h * qh + MLA_NOPE:(h + 1) * qh]
        z = zeros(HEAD_W, MLA_Q_LORA)
        cols += ([nope, z] if h % 2 == 0 else [z, nope]) + [rp, zeros(LANES - MLA_ROPE, MLA_Q_LORA)]
    wq = jnp.concatenate(cols, 1).astype(BF16)
    kvh = MLA_NOPE + MLA_V
    wk = jnp.concatenate([w_kv_up[:, h * kvh:h * kvh + MLA_NOPE] for h in range(MLA_HEADS)], 1).astype(BF16)
    vcols, ones = [], []
    for h in range(MLA_HEADS):
        vh = w_kv_up[:, h * kvh + MLA_NOPE:(h + 1) * kvh]
        z = zeros(HEAD_W, MLA_KV_LORA)
        vcols += [vh, z] if h % 2 == 0 else [z, vh]
        ones += [0.0, 1.0] if h % 2 == 0 else [1.0, 0.0]
    wv = jnp.concatenate(vcols, 1).astype(BF16)
    vone = jnp.asarray(np.repeat(np.asarray(ones, np.float32), HEAD_W)[None, :])
    q_gain = (q_norm * ((MLA_NOPE + MLA_ROPE) ** -0.5 * LOG2E)).reshape(1, -1).astype(F32)
    return w_cat, q_gain, wq, kv_norm.reshape(1, -1).astype(F32), wk, wv, vone


def _in_even_call(x, g, shift, scale, weights, conv_w, cos, sin, tm, name):
    b, t, d = x.shape
    w_cat, q_gain, wq, kv_gain, wk, wv, vone = weights
    n_tap = conv_w.shape[0]
    cw = jnp.concatenate([conv_w.astype(F32), jnp.zeros((8 - n_tap, conv_w.shape[1]), F32)], 0)
    ones = _seg_ones(GDN_QK_W)
    per = tm // CONV_HALO
    n_halo = t // CONV_HALO
    const = lambda a: pl.BlockSpec(a.shape, lambda i, j: (0,) * a.ndim)
    per_b = pl.BlockSpec((1, 1, d), lambda i, j: (i, 0, 0))
    tab = pl.BlockSpec((tm, LANES), lambda i, j: (j, 0))
    widths = (EVEN_QKV_W, GDN_V_W, LANES, MLA_HEADS * MLA_QC, MLA_PAIRS * MLA_QC, MLA_HEADS * LANES)
    dts = (BF16, BF16, F32, BF16, BF16, BF16)
    g2 = g.reshape(1, d).astype(F32)
    return pl.pallas_call(
        functools.partial(_in_even_kernel, n_tap=n_tap),
        grid=(b, t // tm),
        in_specs=[pl.BlockSpec((1, tm, d), lambda i, j: (i, j, 0)),
                  pl.BlockSpec((1, CONV_HALO, d), lambda i, j: (i, jnp.maximum(j * per - 1, 0), 0)),
                  pl.BlockSpec((1, CONV_HALO, d), lambda i, j: (i, jnp.minimum((j + 1) * per, n_halo - 1), 0)),
                  const(g2), per_b, per_b, const(w_cat), const(q_gain), const(wq), const(kv_gain), const(wk),
                  const(wv), const(vone), tab, tab, const(cw), const(ones)],
        out_specs=[pl.BlockSpec((1, tm, n), lambda i, j: (i, j, 0)) for n in widths],
        out_shape=[jax.ShapeDtypeStruct((b, t, n), dt) for n, dt in zip(widths, dts)],
        scratch_shapes=[pltpu.VMEM((tm + 2 * CONV_HALO, EVEN_QKV_W), F32)],
        compiler_params=_params(2),
        name=name,
    )(x, x, x, g2, shift.reshape(b, 1, d), scale.reshape(b, 1, d), w_cat, q_gain, wq, kv_gain, wk, wv, vone,
      cos, sin, cw, ones)


GQA_KV_W = GQA_KV_HEADS * GQA_DIM


def _in_odd_kernel(x_ref, g_ref, sh_ref, sc_ref, w_ref, qg_ref, kg_ref, ones_q_ref, ones_k_ref, cos_ref, sin_ref,
                   naq_ref, nak_ref, nav_ref, gq_ref, gk_ref, gv_ref):
    u = _norm_mod(x_ref, g_ref, sh_ref, sc_ref)
    grp = lambda lo, n: _dotf(u, w_ref[:, lo:lo + n])
    o = 3 * NA_W
    q = grp(o, GQA_HEADS * GQA_DIM)
    o += GQA_HEADS * GQA_DIM
    k = grp(o, GQA_KV_W)
    v = grp(o + GQA_KV_W, GQA_KV_W)
    naq_ref[0] = (grp(0, NA_W) * (NA_DIM ** -0.5 * LOG2E)).astype(BF16)
    nak_ref[0] = grp(NA_W, NA_W).astype(BF16)
    na_v = grp(2 * NA_W, NA_W)
    low = lax.broadcasted_iota(jnp.int32, (na_v.shape[0], LANES), 1) < HEAD_W
    for p in range(NA_HEADS // 2):
        tile = na_v[:, p * LANES:(p + 1) * LANES]
        nav_ref[0, :, 2 * p * LANES:(2 * p + 1) * LANES] = tile.astype(BF16)
        nav_ref[0, :, (2 * p + 1) * LANES:(2 * p + 2) * LANES] = jnp.ones(tile.shape, BF16)
    q_ms = _seg_mean_sq(q, ones_q_ref)
    k_ms = _seg_mean_sq(k, ones_k_ref)
    cos, sin = cos_ref[...], sin_ref[...]
    qn = q * lax.rsqrt(q_ms + NORM_EPS) * qg_ref[...]
    for p in range(GQA_HEADS // 2):
        cols = slice(p * LANES, (p + 1) * LANES)
        gq_ref[0, :, cols] = _rotate(qn[:, cols], cos, sin, GQA_DIM // 2).astype(BF16)
    kr = _rotate(k * lax.rsqrt(k_ms + NORM_EPS) * kg_ref[...], cos, sin, GQA_DIM // 2)
    k_sw = pltpu.roll(kr, HEAD_W, 1)
    gk_ref[0, :, 0:LANES] = jnp.where(low, kr, k_sw).astype(BF16)
    gk_ref[0, :, LANES:2 * LANES] = jnp.where(low, k_sw, kr).astype(BF16)
    v_sw = pltpu.roll(v, HEAD_W, 1)
    slots = (jnp.where(low, v, 1.0), jnp.where(low, 1.0, v_sw),
             jnp.where(low, v_sw, 1.0), jnp.where(low, 1.0, v))
    for i, s in enumerate(slots):
        gv_ref[0, :, i * LANES:(i + 1) * LANES] = s.astype(BF16)


def _seg_ones(width):
    head = np.arange(width) // HEAD_W
    m = (head[:, None] == head[None, :]).astype(np.float32)
    return jnp.asarray(np.concatenate([m, m], 0), BF16)


def _in_odd_call(x, g, shift, scale, w, q_gain, k_gain, cos, sin, tm, name):
    b, t, d = x.shape
    assert GQA_KV_W == LANES
    ones_q = _seg_ones(GQA_HEADS * GQA_DIM)
    ones_k = _seg_ones(GQA_KV_W)
    const = lambda a: pl.BlockSpec(a.shape, lambda i, j: (0,) * a.ndim)
    per_b = pl.BlockSpec((1, 1, d), lambda i, j: (i, 0, 0))
    tab = pl.BlockSpec((tm, LANES), lambda i, j: (j, 0))
    widths = (NA_W, NA_W, 2 * NA_W, GQA_HEADS * GQA_DIM, 2 * LANES, 4 * LANES)
    g2 = g.reshape(1, d).astype(F32)
    return pl.pallas_call(
        _in_odd_kernel,
        grid=(b, t // tm),
        in_specs=[pl.BlockSpec((1, tm, d), lambda i, j: (i, j, 0)), const(g2), per_b, per_b, const(w),
                  const(q_gain), const(k_gain), const(ones_q), const(ones_k), tab, tab],
        out_specs=[pl.BlockSpec((1, tm, n), lambda i, j: (i, j, 0)) for n in widths],
        out_shape=[jax.ShapeDtypeStruct((b, t, n), BF16) for n in widths],
        compiler_params=_params(2),
        name=name,
    )(x, g2, shift.reshape(b, 1, d), scale.reshape(b, 1, d), w, q_gain, k_gain, ones_q, ones_k, cos, sin)


def _attn_kernel(*refs, dc, chunks, n_src, masked_q):
    q_ref, kv_refs, o_ref = refs[0], refs[1:1 + 2 * n_src], refs[-1]
    tq = q_ref.shape[1]
    lane = lax.broadcasted_iota(jnp.int32, (tq, LANES), 1)
    qs = []
    for hh in range(2):
        if masked_q:
            qp = q_ref[0]
            qs.append(jnp.where((lane < HEAD_W) if hh == 0 else (lane >= HEAD_W), qp, jnp.zeros_like(qp)))
        else:
            qs.append(q_ref[0, :, hh * dc:(hh + 1) * dc])
    ms, accs = [None, None], [None, None]

    def scores(hh, src, s0, n):
        return lax.dot_general(qs[hh], kv_refs[2 * src][0, s0:s0 + n, :], NT_DIMS, preferred_element_type=F32)

    def consume(s, hh, src, s0, n):
        vv = kv_refs[2 * src + 1][0, s0:s0 + n, hh * LANES:(hh + 1) * LANES]
        mc = jnp.max(s, axis=-1, keepdims=True)
        if ms[hh] is None:
            m_new = mc
            accs[hh] = jnp.dot(jnp.exp2(s - m_new).astype(BF16), vv, preferred_element_type=F32)
        else:
            m_new = jnp.maximum(ms[hh], mc)
            accs[hh] = jnp.exp2(ms[hh] - m_new) * accs[hh] + jnp.dot(jnp.exp2(s - m_new).astype(BF16), vv,
                                                                     preferred_element_type=F32)
        ms[hh] = m_new

    items = [(hh,) + tuple(c) for c in chunks for hh in range(2)]
    pending = []
    for it in items:
        pending.append((scores(*it), it))
        if len(pending) > ATTN_LOOKAHEAD:
            s, it0 = pending.pop(0)
            consume(s, *it0)
    for s, it0 in pending:
        consume(s, *it0)
    low = lane < HEAD_W
    num = jnp.where(low, accs[0], accs[1])
    den = pltpu.roll(jnp.where(low, accs[1], accs[0]), HEAD_W, 1)
    o_ref[0] = (num / den).astype(o_ref.dtype)


def _attn_call(q, kvs, dc, n_pairs, n_groups, chunks, tq, masked_q, name):
    b, t_q, _ = q.shape
    per = n_pairs // n_groups
    q_w = LANES if masked_q else 2 * dc
    in_specs = [pl.BlockSpec((1, tq, q_w), lambda i, p, j: (i, j, p))]
    args = [q]
    for k, v in kvs:
        in_specs += [pl.BlockSpec((1, k.shape[1], dc), lambda i, p, j: (i, 0, p // per)),
                     pl.BlockSpec((1, v.shape[1], 2 * LANES), lambda i, p, j: (i, 0, p // per))]
        args += [k, v]
    return pl.pallas_call(
        functools.partial(_attn_kernel, dc=dc, chunks=tuple(chunks), n_src=len(kvs), masked_q=masked_q),
        grid=(b, n_pairs, t_q // tq),
        in_specs=in_specs,
        out_specs=pl.BlockSpec((1, tq, LANES), lambda i, p, j: (i, j, p)),
        out_shape=jax.ShapeDtypeStruct((b, t_q, n_pairs * LANES), BF16),
        compiler_params=_params(3),
        name=name,
    )(*args)


NA_QROWS = 4
NA_KROWS = 12
NA_LOOKAHEAD = 1


def _na_kernel(q_ref, k_ref, v_ref, kc_ref, vc_ref, bias_ref, o_ref):
    rb = pl.program_id(1)
    n_rows = k_ref.shape[1] // GRID_W
    base = jnp.clip(rb * NA_QROWS - NA_WIN_H // 2, 0, n_rows - NA_KROWS) * GRID_W
    base = pl.multiple_of(base, GRID_W)
    nq = NA_QROWS * GRID_W
    nk = NA_KROWS * GRID_W
    lane = lax.broadcasted_iota(jnp.int32, (nq, LANES), 1)
    low = lane < HEAD_W

    def scores(p):
        cols = slice(p * LANES, (p + 1) * LANES)
        qp = q_ref[0, :, cols]
        zero = jnp.zeros_like(qp)
        q2 = jnp.concatenate([jnp.where(low, qp, zero), jnp.where(low, zero, qp)], axis=0)
        s_loc = lax.dot_general(q2, k_ref[0, pl.ds(base, nk), cols], NT_DIMS, preferred_element_type=F32)
        s_ctx = lax.dot_general(q2, kc_ref[0, :, cols], NT_DIMS, preferred_element_type=F32)
        return s_loc, s_ctx

    def consume(p, s_loc, s_ctx):
        vcols = slice(2 * p * LANES, (2 * p + 2) * LANES)
        s_loc = s_loc + bias_ref[0, p]
        m = jnp.maximum(jnp.max(s_loc, axis=-1, keepdims=True), jnp.max(s_ctx, axis=-1, keepdims=True))
        acc = (_dotf(jnp.exp2(s_loc - m).astype(BF16), v_ref[0, pl.ds(base, nk), vcols])
               + _dotf(jnp.exp2(s_ctx - m).astype(BF16), vc_ref[0, :, vcols]))
        even, odd = acc[0:nq], acc[nq:2 * nq]
        num = jnp.where(low, even[:, 0:LANES], odd[:, 0:LANES])
        den = jnp.where(low, even[:, LANES:2 * LANES], odd[:, LANES:2 * LANES])
        o_ref[0, :, p * LANES:(p + 1) * LANES] = (num / den).astype(o_ref.dtype)

    pending = []
    for p in range(NA_HEADS // 2):
        pending.append((p,) + scores(p))
        if len(pending) > NA_LOOKAHEAD:
            consume(*pending.pop(0))
    for item in pending:
        consume(*item)


def _na_bias_table(rpb, rows):
    n_blocks = rows // NA_QROWS
    n_h = rpb.shape[0]
    c = np.arange(GRID_W)[:, None]
    kc = np.arange(GRID_W)[None, :]
    cs = np.clip(c - NA_WIN_W // 2, 0, GRID_W - NA_WIN_W)
    col_ok = (kc >= cs) & (kc < cs + NA_WIN_W)
    dc = np.clip(kc - c + (NA_WIN_W - 1), 0, 2 * NA_WIN_W - 2)
    pick = (np.arange(2 * NA_WIN_W - 1)[:, None] == dc.reshape(-1)[None, :]).astype(np.float32)
    variants = []
    for r0 in (0, NA_QROWS * (n_blocks // 2), rows - NA_QROWS):
        base = int(np.clip(r0 - NA_WIN_H // 2, 0, rows - NA_KROWS))
        r = r0 + np.arange(NA_QROWS)[:, None]
        kr = base + np.arange(NA_KROWS)[None, :]
        rs = np.clip(r - NA_WIN_H // 2, 0, rows - NA_WIN_H)
        row_ok = (kr >= rs) & (kr < rs + NA_WIN_H)
        dr = np.clip(kr - r + (NA_WIN_H - 1), 0, 2 * NA_WIN_H - 2)
        by_row = rpb[:, dr.reshape(-1), :].astype(F32)
        full = jnp.einsum("hxd,dm->hxm", by_row, jnp.asarray(pick), precision=lax.Precision.HIGHEST)
        full = full.reshape(n_h, NA_QROWS, NA_KROWS, GRID_W, GRID_W).transpose(0, 1, 3, 2, 4)
        full = full.reshape(n_h, NA_QROWS * GRID_W, NA_KROWS * GRID_W)
        valid = (row_ok[:, None, :, None] & col_ok[None, :, None, :]).reshape(full.shape[1:])
        variants.append(jnp.where(valid[None], full * LOG2E, MASK_VALUE))
    return jnp.stack(variants, 0).reshape(3, n_h // 2, 2 * NA_QROWS * GRID_W, NA_KROWS * GRID_W)


def _na_call(q, k, v, kc, vc, bias):
    b, t, w = q.shape
    n_blocks = t // (NA_QROWS * GRID_W)
    nq = NA_QROWS * GRID_W
    tc = kc.shape[1]

    def bias_map(i, r):
        return (jnp.where(r == 0, 0, jnp.where(r == n_blocks - 1, 2, 1)), 0, 0, 0)

    return pl.pallas_call(
        _na_kernel,
        grid=(b, n_blocks),
        in_specs=[pl.BlockSpec((1, nq, w), lambda i, r: (i, r, 0)),
                  pl.BlockSpec((1, t, w), lambda i, r: (i, 0, 0)),
                  pl.BlockSpec((1, t, 2 * w), lambda i, r: (i, 0, 0)),
                  pl.BlockSpec((1, tc, w), lambda i, r: (i, 0, 0)),
                  pl.BlockSpec((1, tc, 2 * w), lambda i, r: (i, 0, 0)),
                  pl.BlockSpec((1, NA_HEADS // 2, 2 * nq, NA_KROWS * GRID_W), bias_map)],
        out_specs=pl.BlockSpec((1, nq, w), lambda i, r: (i, r, 0)),
        out_shape=jax.ShapeDtypeStruct((b, t, w), BF16),
        compiler_params=_params(2),
        name="na_attn",
    )(q, k, v, kc, vc, bias)


GDN_STEP_CHUNKS = 4
GDN_BLOCK = GDN_STEP_CHUNKS * GDN_CHUNK
GDN_PAIRS = GDN_HEADS // 2
TN_DIMS = (((0,), (0,)), ((), ()))


def _split3(x):
    hi = x.astype(BF16)
    r = x - hi.astype(F32)
    mid = r.astype(BF16)
    lo = (r - mid.astype(F32)).astype(BF16)
    return hi, mid, lo


def _bd(x):
    lane = lax.broadcasted_iota(jnp.int32, x.shape, 1)
    z = jnp.zeros_like(x)
    return jnp.concatenate([jnp.where(lane < HEAD_W, x, z), jnp.where(lane >= HEAD_W, x, z)], axis=0)


def _dotb(a, b):
    return jnp.dot(a.astype(BF16), b.astype(BF16), preferred_element_type=F32)


def _mm_pair(x, y):
    return _dotb(x, _bd(y))


def _softplus(x):
    return jnp.maximum(x, 0.0) + jnp.log(1.0 + jnp.exp(-jnp.abs(x)))


def _gdn_kernel(qkv_ref, misc_ref, alog_ref, dtb_ref, eg_ref, eb_ref, tri3_ref, s0_ref, o_ref, sfin_ref, s_ref, *,
                rev, n_blk):
    step = pl.program_id(1)
    c_len = GDN_CHUNK
    hw = GDN_HEADS * HEAD_W

    @pl.when(step == 0)
    def _():
        s_ref[...] = s0_ref[0]

    qn = qkv_ref[0, :, 0:hw].astype(F32)
    kn = qkv_ref[0, :, hw:2 * hw].astype(F32)
    v = qkv_ref[0, :, 2 * hw:3 * hw].astype(F32)

    misc = misc_ref[0]
    g_all = -jnp.exp(alog_ref[...]) * _softplus(misc + dtb_ref[...])
    b_all = jax.nn.sigmoid(misc)
    gx = jnp.dot(jnp.concatenate(_split3(g_all), axis=1), eg_ref[...], preferred_element_type=F32)
    bx = jnp.dot(jnp.concatenate(_split3(b_all), axis=1), eb_ref[...], preferred_element_type=F32)

    row = lax.broadcasted_iota(jnp.int32, (c_len, LANES), 0)
    colp = lax.broadcasted_iota(jnp.int32, (c_len, LANES), 1) & (HEAD_W - 1)
    tri = (row <= colp) if rev else (row >= colp)
    strict = (row < colp) if rev else (row > colp)
    eye_f = (row == colp).astype(F32)
    row_w = lax.broadcasted_iota(jnp.int32, (c_len, hw), 0)
    col_w = lax.broadcasted_iota(jnp.int32, (c_len, hw), 1) & (HEAD_W - 1)
    eye_w = row_w == col_w
    bd_row = lax.broadcasted_iota(jnp.int32, (LANES, LANES), 0)
    bd_col = lax.broadcasted_iota(jnp.int32, (LANES, LANES), 1)
    bd_mask = (bd_row < HEAD_W) == (bd_col < HEAD_W)
    ones8 = jnp.ones((8, 3 * c_len), BF16)

    keys = [(c, p) for c in range(GDN_STEP_CHUNKS) for p in range(GDN_PAIRS)]
    sl = [slice(p * LANES, (p + 1) * LANES) for p in range(GDN_PAIRS)]
    gtot, vb, kbg, qd, kd = {}, {}, {}, {}, {}
    ms, qks = {}, {}
    chunk_rows = [slice(c * c_len, (c + 1) * c_len) for c in range(GDN_STEP_CHUNKS)]
    gcs = [jnp.dot(tri3_ref[...], jnp.concatenate(_split3(gx[r]), axis=0), preferred_element_type=F32)
           for r in chunk_rows]
    kbs = [kn[r] * bx[r] for r in chunk_rows]
    aqs = {}
    for c, p in keys:
        r = chunk_rows[c]
        lhs = jnp.concatenate([kbs[c][:, sl[p]], qn[r][:, sl[p]]], axis=0)
        aqs[c, p] = lax.dot_general(lhs.astype(BF16), _bd(kn[r][:, sl[p]]).astype(BF16), NT_DIMS,
                                    preferred_element_type=F32)
    grs = [jnp.dot(ones8, jnp.concatenate(_split3(jnp.where(eye_w, gc, 0.0)), axis=0),
                   preferred_element_type=F32)[0:1] for gc in gcs]
    for c, r in enumerate(chunk_rows):
        gc = gcs[c]
        gtot[c] = gc[0:1] if rev else gc[c_len - 1:c_len]
        egc = jnp.exp(gc)
        vb[c] = v[r] * bx[r]
        kbg[c] = kbs[c] * egc
        qd[c] = qn[r] * egc
        kd[c] = kn[r] * jnp.exp(gtot[c] - gc)
        for p in range(GDN_PAIRS):
            diff = gc[:, sl[p]] - grs[c][:, sl[p]]
            decay = jnp.where(tri, jnp.exp(jnp.where(tri, diff, 0.0)), 0.0)
            ms[c, p] = jnp.where(strict, aqs[c, p][0:c_len] * decay, 0.0)
            qks[c, p] = aqs[c, p][c_len:2 * c_len] * decay

    rblk, cblk = row, colp
    xs = {}
    for lvl in range(6):
        sib = (cblk == rblk + 1) if rev else (cblk == rblk - 1)
        odd = (rblk & 1) == (0 if rev else 1)
        join = sib & odd
        if lvl == 0:
            xs = {key: eye_f - jnp.where(join, ms[key], 0.0) for key in keys}
        else:
            inner = {key: _mm_pair(jnp.where(join, ms[key], 0.0), xs[key]) for key in keys}
            xs = {key: xs[key] - _mm_pair(xs[key], inner[key]) for key in keys}
        rblk, cblk = rblk >> 1, cblk >> 1
    uws = {}
    for c, p in keys:
        rhs = jnp.concatenate([_bd(vb[c][:, sl[p]]), _bd(kbg[c][:, sl[p]])], axis=1)
        uws[c, p] = _dotb(xs[c, p], rhs)

    order = range(GDN_STEP_CHUNKS - 1, -1, -1) if rev else range(GDN_STEP_CHUNKS)
    state = [s_ref[p] for p in range(GDN_PAIRS)]
    pairs = range(GDN_PAIRS)
    for c in order:
        r = slice(c * c_len, (c + 1) * c_len)
        wq = [_dotb(jnp.concatenate([uws[c, p][:, LANES:2 * LANES], qd[c][:, sl[p]]], axis=0), state[p])
              for p in pairs]
        v_new = [uws[c, p][:, 0:LANES] - wq[p][0:c_len] for p in pairs]
        upd = [lax.dot_general(kd[c][:, sl[p]].astype(BF16), v_new[p].astype(BF16), TN_DIMS,
                               preferred_element_type=F32) for p in pairs]
        for p in pairs:
            o_ref[0, r, sl[p]] = wq[p][c_len:2 * c_len] + _dotb(qks[c, p], _bd(v_new[p]))
            state[p] = state[p] * jnp.exp(gtot[c][:, sl[p]]) + jnp.where(bd_mask, upd[p], 0.0)
    for p in range(GDN_PAIRS):
        s_ref[p] = state[p]

    @pl.when(step == n_blk - 1)
    def _():
        for p in range(GDN_PAIRS):
            sfin_ref[0, p] = state[p]


def _gdn_consts(a_log, dt_bias, d, rev):
    pad = jnp.zeros((LANES - 2 * GDN_HEADS,), F32)
    alog = jnp.concatenate([a_log.reshape(-1).astype(F32), pad]).reshape(1, LANES)
    dtb = jnp.concatenate([dt_bias.reshape(-1).astype(F32), pad]).reshape(1, LANES)
    hw = GDN_HEADS * HEAD_W
    head_of_lane = np.arange(hw) // HEAD_W
    src = np.arange(LANES)[:, None]
    eg = (src == d * GDN_HEADS + head_of_lane[None, :]).astype(np.float32)
    eb = (src == 2 * GDN_HEADS + d * GDN_HEADS + head_of_lane[None, :]).astype(np.float32)
    i = np.arange(GDN_CHUNK)
    tri = (i[:, None] <= i[None, :]) if rev else (i[:, None] >= i[None, :])
    as_bf = lambda a, reps, ax: jnp.asarray(np.concatenate([a] * reps, axis=ax), BF16)
    return alog, dtb, as_bf(eg, 3, 0), as_bf(eb, 3, 0), as_bf(tri.astype(np.float32), 3, 1)


def _gdn_call(qkv, misc, consts, s0, rev, name):
    b, t, w = qkv.shape
    n_blk = t // GDN_BLOCK
    hw = GDN_HEADS * HEAD_W
    blk_of = (lambda s: n_blk - 1 - s) if rev else (lambda s: s)
    const = lambda a: pl.BlockSpec(a.shape, lambda i, s: (0,) * a.ndim)
    s_spec = pl.BlockSpec((1, GDN_PAIRS, LANES, LANES), lambda i, s: (i, 0, 0, 0))
    return pl.pallas_call(
        functools.partial(_gdn_kernel, rev=rev, n_blk=n_blk),
        grid=(b, n_blk),
        in_specs=[pl.BlockSpec((1, GDN_BLOCK, w), lambda i, s: (i, blk_of(s), 0)),
                  pl.BlockSpec((1, GDN_BLOCK, LANES), lambda i, s: (i, blk_of(s), 0))]
                 + [const(a) for a in consts] + [s_spec],
        out_specs=[pl.BlockSpec((1, GDN_BLOCK, hw), lambda i, s: (i, blk_of(s), 0)), s_spec],
        out_shape=[jax.ShapeDtypeStruct((b, t, hw), F32),
                   jax.ShapeDtypeStruct((b, GDN_PAIRS, LANES, LANES), F32)],
        scratch_shapes=[pltpu.VMEM((GDN_PAIRS, LANES, LANES), F32)],
        compiler_params=_params(2),
        name=name,
    )(qkv, misc, *consts, s0)


MIX_SUB_ROWS = 256


def _mix_ffn_kernel(*refs, ff_chunks, gdn_inputs, n_sub):
    tm = refs[0].shape[1]
    rows = [slice(i * (tm // n_sub), (i + 1) * (tm // n_sub)) for i in range(n_sub)]
    if gdn_inputs:
        (h_ref, of_ref, ob_ref, gate_ref, att_ref, on_ref, ones_ref, gtm_ref, shf_ref, scf_ref, gtf_ref, gpm_ref,
         gpf_ref, gqf_ref, wo_ref, wg_ref, wu_ref, wd_ref, o_ref) = refs
        mixes = []
        for r in rows:
            o = of_ref[0, r, :] + ob_ref[0, r, :]
            gate = gate_ref[0, r, :].astype(F32)
            a = o * lax.rsqrt(_seg_mean_sq(o, ones_ref) + NORM_EPS) * on_ref[...] * (gate * jax.nn.sigmoid(gate))
            mixes.append(jnp.concatenate([a.astype(BF16), att_ref[0, r, :]], axis=1))
    else:
        (h_ref, a_ref, b_ref, gtm_ref, shf_ref, scf_ref, gtf_ref, gpm_ref, gpf_ref, gqf_ref, wo_ref, wg_ref,
         wu_ref, wd_ref, o_ref) = refs
        mixes = [jnp.concatenate([a_ref[0, r, :], b_ref[0, r, :]], axis=1) for r in rows]
    ys = [_dotf(mix, wo_ref[...]) for mix in mixes]
    h1s = [h_ref[0, r, :] + gtm_ref[0] * (_rms(y) * gpm_ref[...]) for r, y in zip(rows, ys)]
    us = [(_rms(h1) * gpf_ref[...] * (1.0 + scf_ref[0]) + shf_ref[0]).astype(BF16) for h1 in h1s]
    acts = []
    for o, n in ff_chunks:
        for u in us:
            gg = _dotf(u, wg_ref[:, o:o + n])
            uu = _dotf(u, wu_ref[:, o:o + n])
            acts.append((gg * jax.nn.sigmoid(gg) * uu).astype(BF16))
    fs = [None] * n_sub
    for ci, (o, n) in enumerate(ff_chunks):
        for i in range(n_sub):
            part = _dotf(acts[ci * n_sub + i], wd_ref[o:o + n, :])
            fs[i] = part if fs[i] is None else fs[i] + part
    for r, h1, f in zip(rows, h1s, fs):
        o_ref[0, r, :] = h1 + gtf_ref[0] * (_rms(f) * gqf_ref[...])


def _mix_ffn_call(h, mix_inputs, out_norm, gt_m, sh_f, sc_f, gt_f, g_post_mix, g_pre_ffn, g_post_ffn, wo, wg, wu,
                  wd, tm, name):
    b, t, d = h.shape
    dm = wo.shape[0]
    ff = wg.shape[1]
    gdn_inputs = out_norm is not None
    half = pl.cdiv(ff // 2, MXU_TILE) * MXU_TILE
    assert 0 < half < ff and ff % LANES == 0
    ff_chunks = ((0, half), (half, ff - half))
    tok = lambda n: pl.BlockSpec((1, tm, n), lambda i, j: (i, j, 0))
    per_b = pl.BlockSpec((1, 1, d), lambda i, j: (i, 0, 0))
    vec = pl.BlockSpec((1, d), lambda i, j: (0, 0))
    const = lambda shape: pl.BlockSpec(shape, lambda i, j: (0, 0), pipeline_mode=pl.Buffered(1))
    r3 = lambda a: a.reshape(b, 1, d)
    r2 = lambda a: a.reshape(1, d).astype(F32)
    mix_specs = [tok(m.shape[-1]) for m in mix_inputs]
    mix_args = list(mix_inputs)
    if gdn_inputs:
        hw = GDN_HEADS * GDN_DV
        on = jnp.tile(out_norm.astype(F32), GDN_HEADS).reshape(1, hw)
        ones = _seg_ones(hw)
        mix_specs += [pl.BlockSpec(on.shape, lambda i, j: (0, 0)), pl.BlockSpec(ones.shape, lambda i, j: (0, 0))]
        mix_args += [on, ones]
    return pl.pallas_call(
        functools.partial(_mix_ffn_kernel, ff_chunks=ff_chunks, gdn_inputs=gdn_inputs,
                          n_sub=tm // MIX_SUB_ROWS),
        grid=(b, t // tm),
        in_specs=[tok(d)] + mix_specs + [per_b, per_b, per_b, per_b, vec, vec, vec,
                                         const((dm, d)), const((d, ff)), const((d, ff)), const((ff, d))],
        out_specs=tok(d),
        out_shape=jax.ShapeDtypeStruct((b, t, d), F32),
        compiler_params=_params(2),
        name=name,
    )(h, *mix_args, r3(gt_m), r3(sh_f), r3(sc_f), r3(gt_f), r2(g_post_mix), r2(g_pre_ffn), r2(g_post_ffn),
      wo, wg, wu, wd)


def _rope_tables(n_tokens, rot_dim):
    t = jnp.arange(n_tokens, dtype=jnp.int32)
    row = (t // GRID_W).astype(F32)
    col = (t % GRID_W).astype(F32)
    n_freq = rot_dim // 4
    inv_freq = ROPE_THETA ** (-jnp.arange(n_freq, dtype=F32) / n_freq)
    ang = jnp.concatenate([row[:, None] * inv_freq, col[:, None] * inv_freq], -1)
    cos = jnp.concatenate([jnp.cos(ang), jnp.cos(ang)], -1)
    sin = jnp.concatenate([-jnp.sin(ang), jnp.sin(ang)], -1)
    if rot_dim == HEAD_W:
        return jnp.tile(cos, (1, LANES // rot_dim)), jnp.tile(sin, (1, LANES // rot_dim))
    pad = LANES - rot_dim
    return (jnp.concatenate([cos, jnp.ones((n_tokens, pad), F32)], -1),
            jnp.concatenate([sin, jnp.zeros((n_tokens, pad), F32)], -1))


def _no_rope_tables(n_tokens):
    return jnp.ones((n_tokens, LANES), F32), jnp.zeros((n_tokens, LANES), F32)


def _gdn_bidirectional(lat_qkv, lat_misc, ctx_qkv, ctx_misc, a_log, dt_bias):
    bsz = lat_qkv.shape[0]
    s_zero = jnp.zeros((bsz, GDN_PAIRS, LANES, LANES), F32)
    lat, ctx = [], []
    for d, rev in ((0, False), (1, True)):
        consts = _gdn_consts(a_log, dt_bias, d, rev)
        o_c, s_c = _gdn_call(ctx_qkv, ctx_misc, consts, s_zero, rev, "gdn_ctx_%d" % d)
        o_l, _ = _gdn_call(lat_qkv, lat_misc, consts, s_c, rev, "gdn_lat_%d" % d)
        lat.append(o_l)
        ctx.append(o_c)
    return lat, ctx


LOG2E = 1.4426950408889634
ATTN_TQ = 1024
ATTN_LOOKAHEAD = 2


def _attn_chunks(l_len, s_len, size):
    return [(0, 0, l_len)] + [(1, o, size) for o in range(0, s_len, size)]


def _even_mixer(h, hc, mods, mods_c, g_pre, w_in, conv_w, a_log, dt_bias, q_norm, w_q_up, kv_norm, w_kv_up):
    s_len, l_len = h.shape[1], hc.shape[1]
    weights = _even_weights(w_in, q_norm, w_q_up, kv_norm, w_kv_up)
    qkv_l, gate_l, misc_l, q_l, k_l, v_l = _in_even_call(h, g_pre, *mods, weights, conv_w,
                                                         *_rope_tables(s_len, MLA_ROPE), 512, "in_even_lat")
    qkv_c, gate_c, misc_c, q_c, k_c, v_c = _in_even_call(hc, g_pre, *mods_c, weights, conv_w,
                                                         *_no_rope_tables(l_len), l_len, "in_even_ctx")
    o_lat, o_ctx = _gdn_bidirectional(qkv_l, misc_l, qkv_c, misc_c, a_log, dt_bias)
    b_lat = _attn_call(q_l, [(k_c, v_c), (k_l, v_l)], MLA_QC, MLA_PAIRS, MLA_PAIRS, _attn_chunks(l_len, s_len, 512),
                       ATTN_TQ, False, "mla_attn_lat")
    b_ctx = _attn_call(q_c, [(k_c, v_c)], MLA_QC, MLA_PAIRS, MLA_PAIRS, [(0, 0, l_len)], l_len, False,
                       "mla_attn_ctx")
    return (o_lat[0], o_lat[1], gate_l, b_lat), (o_ctx[0], o_ctx[1], gate_c, b_ctx)


def _odd_mixer_last(h, hc, mods, mods_c, g_pre, w_in, rpb, q_norm, k_norm):
    s_len, l_len = h.shape[1], hc.shape[1]
    w = w_in.astype(BF16)
    q_gain = (jnp.tile(q_norm.astype(F32), GQA_HEADS) * (GQA_DIM ** -0.5 * LOG2E)).reshape(1, -1)
    k_gain = jnp.tile(k_norm.astype(F32), GQA_KV_HEADS).reshape(1, -1)
    naq_l, nak_l, nav_l, gq_l, gk_l, gv_l = _in_odd_call(h, g_pre, *mods, w, q_gain, k_gain,
                                                         *_rope_tables(s_len, GQA_DIM), 512, "in_odd_lat")
    _, nak_c, nav_c, _, gk_c, gv_c = _in_odd_call(hc, g_pre, *mods_c, w, q_gain, k_gain, *_no_rope_tables(l_len),
                                                  l_len, "in_odd_ctx")
    c_lat = _na_call(naq_l, nak_l, nav_l, nak_c, nav_c, _na_bias_table(rpb, s_len // GRID_W))
    d_lat = _attn_call(gq_l, [(gk_c, gv_c), (gk_l, gv_l)], LANES, GQA_HEADS // 2, GQA_KV_HEADS,
                       _attn_chunks(l_len, s_len, 256), ATTN_TQ, True, "gqa_attn")
    return c_lat, d_lat


def kernel(x, c, ctx, c_ctx, w_mod, b_mod, g_pre_mix, g_post_mix, g_pre_ffn, g_post_ffn, w_ffn_gate, w_ffn_up,
           w_ffn_down, w_in_even, w_out_even, gdn_conv, gdn_a_log, gdn_dt_bias, gdn_out_norm, mla_q_norm,
           mla_w_q_up, mla_kv_norm, mla_w_kv_up, w_in_odd, w_out_odd, na_rpb, gqa_q_norm, gqa_k_norm):
    bsz, s_len, d = x.shape
    depth = w_mod.shape[0]
    assert depth == 2, "layer 0 = even mixer with context update, layer 1 = odd mixer (last)"
    c_rows = jnp.concatenate([c, c_ctx[None, :], jnp.zeros((16 - bsz - 1, d), F32)], 0)
    h, hc = x, ctx
    for i in range(depth):
        mod_all = _mod_call(c_rows, w_mod[i].astype(BF16), b_mod[i])
        mod = jnp.split(mod_all[:bsz], 6, axis=-1)
        mod_c = [jnp.broadcast_to(m, (bsz, d)) for m in jnp.split(mod_all[bsz:bsz + 1], 6, axis=-1)]
        sh_m, sc_m, gt_m, sh_f, sc_f, gt_f = mod
        csh_m, csc_m, cgt_m, csh_f, csc_f, cgt_f = mod_c
        wg, wu, wd = w_ffn_gate[i].astype(BF16), w_ffn_up[i].astype(BF16), w_ffn_down[i].astype(BF16)
        if i == 0:
            mix, mix_c = _even_mixer(h, hc, (sh_m, sc_m), (csh_m, csc_m), g_pre_mix[i], w_in_even[0], gdn_conv[0],
                                     gdn_a_log[0], gdn_dt_bias[0], mla_q_norm[0], mla_w_q_up[0], mla_kv_norm[0],
                                     mla_w_kv_up[0])
            wo, out_norm = w_out_even[0].astype(BF16), gdn_out_norm[0]
            hc = _mix_ffn_call(hc, mix_c, out_norm, cgt_m, csh_f, csc_f, cgt_f, g_post_mix[i], g_pre_ffn[i],
                               g_post_ffn[i], wo, wg, wu, wd, 256, "mix_ffn_ctx")
        else:
            mix = _odd_mixer_last(h, hc, (sh_m, sc_m), (csh_m, csc_m), g_pre_mix[i], w_in_odd[0], na_rpb[0],
                                  gqa_q_norm[0], gqa_k_norm[0])
            wo, out_norm = w_out_odd[0].astype(BF16), None
        h = _mix_ffn_call(h, mix, out_norm, gt_m, sh_f, sc_f, gt_f, g_post_mix[i], g_pre_ffn[i], g_post_ffn[i],
                          wo, wg, wu, wd, 2 * MIX_SUB_ROWS, "mix_ffn_lat%d" % i)
    return h
```

```python
import functools

import numpy as np
import jax
import jax.numpy as jnp
from jax import lax
from jax.experimental import pallas as pl
from jax.experimental.pallas import tpu as pltpu

F32 = jnp.float32
BF16 = jnp.bfloat16

GRID_W = 64
NORM_EPS = 1e-6
ROPE_THETA = 10000.0

GDN_HEADS = 8
GDN_DK = 64
GDN_DV = 64
GDN_CHUNK = 64
GDN_QK_W = GDN_HEADS * GDN_DK
GDN_V_W = GDN_HEADS * GDN_DV

MLA_HEADS = 8
MLA_NOPE = 64
MLA_ROPE = 32
MLA_V = 64
MLA_Q_LORA = 256
MLA_KV_LORA = 128

NA_HEADS = 8
NA_DIM = 64
NA_WIN_H = 8
NA_WIN_W = 16
NA_W = NA_HEADS * NA_DIM

GQA_HEADS = 8
GQA_KV_HEADS = 2
GQA_DIM = 64

LANES = 128
MXU_TILE = 256
HEAD_W = 64
MASK_VALUE = -1e30
VMEM_LIMIT = 56 << 20

NT_DIMS = (((1,), (1,)), ((), ()))


def _params(n_grid, vmem=VMEM_LIMIT):
    return pltpu.CompilerParams(dimension_semantics=("arbitrary",) * n_grid, vmem_limit_bytes=vmem)


def _rms(x):
    return x * lax.rsqrt(jnp.mean(x * x, axis=-1, keepdims=True) + NORM_EPS)


def _mod_kernel(c_ref, w_ref, b_ref, o_ref):
    c = c_ref[...]
    a = (c * jax.nn.sigmoid(c)).astype(BF16)
    o_ref[...] = jnp.dot(a, w_ref[...], preferred_element_type=F32) + b_ref[...]


def _mod_call(c_rows, w, b):
    r, d = c_rows.shape
    n = w.shape[1]
    tn = 1024
    return pl.pallas_call(
        _mod_kernel,
        grid=(n // tn,),
        in_specs=[pl.BlockSpec((r, d), lambda j: (0, 0)),
                  pl.BlockSpec((d, tn), lambda j: (0, j)),
                  pl.BlockSpec((1, tn), lambda j: (0, j))],
        out_specs=pl.BlockSpec((r, tn), lambda j: (0, j)),
        out_shape=jax.ShapeDtypeStruct((r, n), F32),
        compiler_params=_params(1),
        name="mod",
    )(c_rows, w, b.reshape(1, n))


def _norm_mod(x_ref, g_ref, sh_ref, sc_ref):
    return (_rms(x_ref[0]) * g_ref[...] * (1.0 + sc_ref[0]) + sh_ref[0]).astype(BF16)


def _dotf(a, b):
    return jnp.dot(a, b, preferred_element_type=F32)


def _seg_mean_sq(x, ones_ref):
    x2 = x * x
    hi = x2.astype(BF16)
    mid = (x2 - hi.astype(F32)).astype(BF16)
    return _dotf(jnp.concatenate([hi, mid], axis=1), ones_ref[...]) * (1.0 / HEAD_W)


def _rotate(t, cos, sin_signed, half):
    lane = lax.broadcasted_iota(jnp.int32, t.shape, 1)
    n = t.shape[1]
    partner = jnp.where((lane & (2 * half - 1)) < half, pltpu.roll(t, n - half, 1), pltpu.roll(t, half, 1))
    return t * cos + partner * sin_signed


EVEN_QKV_W = 2 * GDN_QK_W + GDN_V_W
EVEN_GROUPS = (EVEN_QKV_W, GDN_V_W, MLA_Q_LORA, MLA_KV_LORA, LANES, LANES)
MLA_QC = 2 * LANES
MLA_PAIRS = MLA_HEADS // 2


CONV_HALO = 16


def _in_even_kernel(x_ref, xp_ref, xn_ref, g_ref, sh_ref, sc_ref, w_ref, qg_ref, wq_ref, kvg_ref, wk_ref, wv_ref,
                    vone_ref, cos_ref, sin_ref, cw_ref, ones_ref, qkv_ref, gate_ref, misc_ref, q_ref, k_ref, v_ref,
                    zs_ref, *, n_tap):
    j = pl.program_id(1)
    tm = x_ref.shape[1]
    u = _norm_mod(x_ref, g_ref, sh_ref, sc_ref)
    offs = np.cumsum((0,) + EVEN_GROUPS)
    grp = lambda i: _dotf(u, w_ref[:, int(offs[i]):int(offs[i + 1])])
    u_ext = jnp.concatenate([_norm_mod(xp_ref, g_ref, sh_ref, sc_ref), u, _norm_mod(xn_ref, g_ref, sh_ref, sc_ref)],
                            axis=0)
    z = _dotf(u_ext, w_ref[:, 0:EVEN_QKV_W])
    zs_ref[0:CONV_HALO, :] = z[0:CONV_HALO] * (j > 0).astype(F32)
    zs_ref[CONV_HALO:CONV_HALO + tm, :] = z[CONV_HALO:CONV_HALO + tm]
    zs_ref[CONV_HALO + tm:, :] = z[CONV_HALO + tm:] * (j < pl.num_programs(1) - 1).astype(F32)
    q_down, kv_down, pe_raw = grp(2), grp(3), grp(5)
    gate_ref[0] = grp(1).astype(BF16)
    misc_ref[0] = grp(4)
    cos, sin = cos_ref[...], sin_ref[...]
    qn = (_rms(q_down) * qg_ref[...]).astype(BF16)
    kvn = (_rms(kv_down) * kvg_ref[...]).astype(BF16)
    for h in range(MLA_HEADS):
        blk = _dotf(qn, wq_ref[:, h * MLA_QC:(h + 1) * MLA_QC])
        q_ref[0, :, h * MLA_QC:h * MLA_QC + LANES] = blk[:, 0:LANES].astype(BF16)
        q_ref[0, :, h * MLA_QC + LANES:(h + 1) * MLA_QC] = _rotate(blk[:, LANES:], cos, sin,
                                                                    MLA_ROPE // 2).astype(BF16)
    pe = _rotate(pe_raw, cos, sin, MLA_ROPE // 2).astype(BF16)
    k_nope = _dotf(kvn, wk_ref[...])
    for p in range(MLA_PAIRS):
        k_ref[0, :, p * MLA_QC:p * MLA_QC + LANES] = k_nope[:, p * LANES:(p + 1) * LANES].astype(BF16)
        k_ref[0, :, p * MLA_QC + LANES:(p + 1) * MLA_QC] = pe
    v_ref[0] = (_dotf(kvn, wv_ref[...]) + vone_ref[...]).astype(BF16)
    acc = None
    for t in range(n_tap):
        start = CONV_HALO - n_tap // 2 + t
        term = zs_ref[start:start + tm, :] * cw_ref[t:t + 1, :]
        acc = term if acc is None else acc + term
    y = acc * jax.nn.sigmoid(acc)
    hw = GDN_QK_W
    qc, kc = y[:, 0:hw], y[:, hw:2 * hw]
    q_ss = _seg_mean_sq(qc, ones_ref) * HEAD_W
    k_ss = _seg_mean_sq(kc, ones_ref) * HEAD_W
    qkv_ref[0, :, 0:hw] = (qc * lax.rsqrt(q_ss + NORM_EPS) * (GDN_DK ** -0.5)).astype(BF16)
    qkv_ref[0, :, hw:2 * hw] = (kc * lax.rsqrt(k_ss + NORM_EPS)).astype(BF16)
    qkv_ref[0, :, 2 * hw:] = y[:, 2 * hw:].astype(BF16)


def _even_weights(w_in, q_norm, w_q_up, kv_norm, w_kv_up):
    d = w_in.shape[0]
    o0 = EVEN_QKV_W + GDN_V_W
    n_ab = 4 * GDN_HEADS
    o1 = o0 + n_ab
    o2 = o1 + MLA_Q_LORA
    o3 = o2 + MLA_KV_LORA
    zeros = lambda n, rows=d: jnp.zeros((rows, n), F32)
    w_cat = jnp.concatenate([w_in[:, :o0], w_in[:, o1:o2], w_in[:, o2:o3],
                             w_in[:, o0:o1], zeros(LANES - n_ab),
                             w_in[:, o3:], zeros(LANES - MLA_ROPE)], 1).astype(BF16)
    qh = MLA_NOPE + MLA_ROPE
    cols = []
    for h in range(MLA_HEADS):
        nope = w_q_up[:, h * qh:h * qh + MLA_NOPE]
        rp = w_q_up[:, h * qh + MLA_NOPE:(h + 1) * qh]
        z = zeros(HEAD_W, MLA_Q_LORA)
        cols += ([nope, z] if h % 2 == 0 else [z, nope]) + [rp, zeros(LANES - MLA_ROPE, MLA_Q_LORA)]
    wq = jnp.concatenate(cols, 1).astype(BF16)
    kvh = MLA_NOPE + MLA_V
    wk = jnp.concatenate([w_kv_up[:, h * kvh:h * kvh + MLA_NOPE] for h in range(MLA_HEADS)], 1).astype(BF16)
    vcols, ones = [], []
    for h in range(MLA_HEADS):
        vh = w_kv_up[:, h * kvh + MLA_NOPE:(h + 1) * kvh]
        z = zeros(HEAD_W, MLA_KV_LORA)
        vcols += [vh, z] if h % 2 == 0 else [z, vh]
        ones += [0.0, 1.0] if h % 2 == 0 else [1.0, 0.0]
    wv = jnp.concatenate(vcols, 1).astype(BF16)
    vone = jnp.asarray(np.repeat(np.asarray(ones, np.float32), HEAD_W)[None, :])
    q_gain = (q_norm * ((MLA_NOPE + MLA_ROPE) ** -0.5 * LOG2E)).reshape(1, -1).astype(F32)
    return w_cat, q_gain, wq, kv_norm.reshape(1, -1).astype(F32), wk, wv, vone


def _in_even_call(x, g, shift, scale, weights, conv_w, cos, sin, tm, name):
    b, t, d = x.shape
    w_cat, q_gain, wq, kv_gain, wk, wv, vone = weights
    n_tap = conv_w.shape[0]
    cw = jnp.concatenate([conv_w.astype(F32), jnp.zeros((8 - n_tap, conv_w.shape[1]), F32)], 0)
    ones = _seg_ones(GDN_QK_W)
    per = tm // CONV_HALO
    n_halo = t // CONV_HALO
    const = lambda a: pl.BlockSpec(a.shape, lambda i, j: (0,) * a.ndim)
    per_b = pl.BlockSpec((1, 1, d), lambda i, j: (i, 0, 0))
    tab = pl.BlockSpec((tm, LANES), lambda i, j: (j, 0))
    widths = (EVEN_QKV_W, GDN_V_W, LANES, MLA_HEADS * MLA_QC, MLA_PAIRS * MLA_QC, MLA_HEADS * LANES)
    dts = (BF16, BF16, F32, BF16, BF16, BF16)
    g2 = g.reshape(1, d).astype(F32)
    return pl.pallas_call(
        functools.partial(_in_even_kernel, n_tap=n_tap),
        grid=(b, t // tm),
        in_specs=[pl.BlockSpec((1, tm, d), lambda i, j: (i, j, 0)),
                  pl.BlockSpec((1, CONV_HALO, d), lambda i, j: (i, jnp.maximum(j * per - 1, 0), 0)),
                  pl.BlockSpec((1, CONV_HALO, d), lambda i, j: (i, jnp.minimum((j + 1) * per, n_halo - 1), 0)),
                  const(g2), per_b, per_b, const(w_cat), const(q_gain), const(wq), const(kv_gain), const(wk),
                  const(wv), const(vone), tab, tab, const(cw), const(ones)],
        out_specs=[pl.BlockSpec((1, tm, n), lambda i, j: (i, j, 0)) for n in widths],
        out_shape=[jax.ShapeDtypeStruct((b, t, n), dt) for n, dt in zip(widths, dts)],
        scratch_shapes=[pltpu.VMEM((tm + 2 * CONV_HALO, EVEN_QKV_W), F32)],
        compiler_params=_params(2),
        name=name,
    )(x, x, x, g2, shift.reshape(b, 1, d), scale.reshape(b, 1, d), w_cat, q_gain, wq, kv_gain, wk, wv, vone,
      cos, sin, cw, ones)


GQA_KV_W = GQA_KV_HEADS * GQA_DIM


def _in_odd_kernel(x_ref, g_ref, sh_ref, sc_ref, w_ref, qg_ref, kg_ref, ones_q_ref, ones_k_ref, cos_ref, sin_ref,
                   naq_ref, nak_ref, nav_ref, gq_ref, gk_ref, gv_ref):
    u = _norm_mod(x_ref, g_ref, sh_ref, sc_ref)
    grp = lambda lo, n: _dotf(u, w_ref[:, lo:lo + n])
    o = 3 * NA_W
    q = grp(o, GQA_HEADS * GQA_DIM)
    o += GQA_HEADS * GQA_DIM
    k = grp(o, GQA_KV_W)
    v = grp(o + GQA_KV_W, GQA_KV_W)
    naq_ref[0] = (grp(0, NA_W) * (NA_DIM ** -0.5 * LOG2E)).astype(BF16)
    nak_ref[0] = grp(NA_W, NA_W).astype(BF16)
    na_v = grp(2 * NA_W, NA_W)
    low = lax.broadcasted_iota(jnp.int32, (na_v.shape[0], LANES), 1) < HEAD_W
    for p in range(NA_HEADS // 2):
        tile = na_v[:, p * LANES:(p + 1) * LANES]
        nav_ref[0, :, 2 * p * LANES:(2 * p + 1) * LANES] = tile.astype(BF16)
        nav_ref[0, :, (2 * p + 1) * LANES:(2 * p + 2) * LANES] = jnp.ones(tile.shape, BF16)
    q_ms = _seg_mean_sq(q, ones_q_ref)
    k_ms = _seg_mean_sq(k, ones_k_ref)
    cos, sin = cos_ref[...], sin_ref[...]
    qn = q * lax.rsqrt(q_ms + NORM_EPS) * qg_ref[...]
    for p in range(GQA_HEADS // 2):
        cols = slice(p * LANES, (p + 1) * LANES)
        gq_ref[0, :, cols] = _rotate(qn[:, cols], cos, sin, GQA_DIM // 2).astype(BF16)
    kr = _rotate(k * lax.rsqrt(k_ms + NORM_EPS) * kg_ref[...], cos, sin, GQA_DIM // 2)
    k_sw = pltpu.roll(kr, HEAD_W, 1)
    gk_ref[0, :, 0:LANES] = jnp.where(low, kr, k_sw).astype(BF16)
    gk_ref[0, :, LANES:2 * LANES] = jnp.where(low, k_sw, kr).astype(BF16)
    v_sw = pltpu.roll(v, HEAD_W, 1)
    slots = (jnp.where(low, v, 1.0), jnp.where(low, 1.0, v_sw),
             jnp.where(low, v_sw, 1.0), jnp.where(low, 1.0, v))
    for i, s in enumerate(slots):
        gv_ref[0, :, i * LANES:(i + 1) * LANES] = s.astype(BF16)


def _seg_ones(width):
    head = np.arange(width) // HEAD_W
    m = (head[:, None] == head[None, :]).astype(np.float32)
    return jnp.asarray(np.concatenate([m, m], 0), BF16)


def _in_odd_call(x, g, shift, scale, w, q_gain, k_gain, cos, sin, tm, name):
    b, t, d = x.shape
    assert GQA_KV_W == LANES
    ones_q = _seg_ones(GQA_HEADS * GQA_DIM)
    ones_k = _seg_ones(GQA_KV_W)
    const = lambda a: pl.BlockSpec(a.shape, lambda i, j: (0,) * a.ndim)
    per_b = pl.BlockSpec((1, 1, d), lambda i, j: (i, 0, 0))
    tab = pl.BlockSpec((tm, LANES), lambda i, j: (j, 0))
    widths = (NA_W, NA_W, 2 * NA_W, GQA_HEADS * GQA_DIM, 2 * LANES, 4 * LANES)
    g2 = g.reshape(1, d).astype(F32)
    return pl.pallas_call(
        _in_odd_kernel,
        grid=(b, t // tm),
        in_specs=[pl.BlockSpec((1, tm, d), lambda i, j: (i, j, 0)), const(g2), per_b, per_b, const(w),
                  const(q_gain), const(k_gain), const(ones_q), const(ones_k), tab, tab],
        out_specs=[pl.BlockSpec((1, tm, n), lambda i, j: (i, j, 0)) for n in widths],
        out_shape=[jax.ShapeDtypeStruct((b, t, n), BF16) for n in widths],
        compiler_params=_params(2),
        name=name,
    )(x, g2, shift.reshape(b, 1, d), scale.reshape(b, 1, d), w, q_gain, k_gain, ones_q, ones_k, cos, sin)


def _attn_kernel(*refs, dc, chunks, n_src, masked_q):
    q_ref, kv_refs, o_ref = refs[0], refs[1:1 + 2 * n_src], refs[-1]
    tq = q_ref.shape[1]
    lane = lax.broadcasted_iota(jnp.int32, (tq, LANES), 1)
    qs = []
    for hh in range(2):
        if masked_q:
            qp = q_ref[0]
            qs.append(jnp.where((lane < HEAD_W) if hh == 0 else (lane >= HEAD_W), qp, jnp.zeros_like(qp)))
        else:
            qs.append(q_ref[0, :, hh * dc:(hh + 1) * dc])
    ms, accs = [None, None], [None, None]

    def scores(hh, src, s0, n):
        return lax.dot_general(qs[hh], kv_refs[2 * src][0, s0:s0 + n, :], NT_DIMS, preferred_element_type=F32)

    def consume(s, hh, src, s0, n):
        vv = kv_refs[2 * src + 1][0, s0:s0 + n, hh * LANES:(hh + 1) * LANES]
        mc = jnp.max(s, axis=-1, keepdims=True)
        m_new = mc if ms[hh] is None else jnp.maximum(ms[hh], mc)
        pv = _dotf(jnp.exp2((s - m_new).astype(BF16)), vv)
        accs[hh] = pv if ms[hh] is None else jnp.exp2(ms[hh] - m_new) * accs[hh] + pv
        ms[hh] = m_new

    items = [(hh,) + tuple(c) for c in chunks for hh in range(2)]
    pending = []
    for it in items:
        pending.append((scores(*it), it))
        if len(pending) > ATTN_LOOKAHEAD:
            s, it0 = pending.pop(0)
            consume(s, *it0)
    for s, it0 in pending:
        consume(s, *it0)
    low = lane < HEAD_W
    num = jnp.where(low, accs[0], accs[1])
    den = pltpu.roll(jnp.where(low, accs[1], accs[0]), HEAD_W, 1)
    o_ref[0] = (num / den).astype(o_ref.dtype)


def _attn_call(q, kvs, dc, n_pairs, n_groups, chunks, tq, masked_q, name):
    b, t_q, _ = q.shape
    per = n_pairs // n_groups
    q_w = LANES if masked_q else 2 * dc
    in_specs = [pl.BlockSpec((1, tq, q_w), lambda i, p, j: (i, j, p))]
    args = [q]
    for k, v in kvs:
        in_specs += [pl.BlockSpec((1, k.shape[1], dc), lambda i, p, j: (i, 0, p // per)),
                     pl.BlockSpec((1, v.shape[1], 2 * LANES), lambda i, p, j: (i, 0, p // per))]
        args += [k, v]
    return pl.pallas_call(
        functools.partial(_attn_kernel, dc=dc, chunks=tuple(chunks), n_src=len(kvs), masked_q=masked_q),
        grid=(b, n_pairs, t_q // tq),
        in_specs=in_specs,
        out_specs=pl.BlockSpec((1, tq, LANES), lambda i, p, j: (i, j, p)),
        out_shape=jax.ShapeDtypeStruct((b, t_q, n_pairs * LANES), BF16),
        compiler_params=_params(3),
        name=name,
    )(*args)


NA_QROWS = 4
NA_KROWS = 12
NA_LOOKAHEAD = 1


def _na_kernel(q_ref, k_ref, v_ref, kc_ref, vc_ref, bias_ref, o_ref):
    rb = pl.program_id(1)
    n_rows = k_ref.shape[1] // GRID_W
    base = jnp.clip(rb * NA_QROWS - NA_WIN_H // 2, 0, n_rows - NA_KROWS) * GRID_W
    base = pl.multiple_of(base, GRID_W)
    nq = NA_QROWS * GRID_W
    nk = NA_KROWS * GRID_W
    lane = lax.broadcasted_iota(jnp.int32, (nq, LANES), 1)
    low = lane < HEAD_W

    def scores(p):
        cols = slice(p * LANES, (p + 1) * LANES)
        qp = q_ref[0, :, cols]
        zero = jnp.zeros_like(qp)
        q2 = jnp.concatenate([jnp.where(low, qp, zero), jnp.where(low, zero, qp)], axis=0)
        s_loc = lax.dot_general(q2, k_ref[0, pl.ds(base, nk), cols], NT_DIMS, preferred_element_type=F32)
        s_ctx = lax.dot_general(q2, kc_ref[0, :, cols], NT_DIMS, preferred_element_type=F32)
        return s_loc, s_ctx

    def consume(p, s_loc, s_ctx):
        vcols = slice(2 * p * LANES, (2 * p + 2) * LANES)
        s_loc = s_loc + bias_ref[0, p]
        m = jnp.maximum(jnp.max(s_loc, axis=-1, keepdims=True), jnp.max(s_ctx, axis=-1, keepdims=True))
        acc = (_dotf(jnp.exp2((s_loc - m).astype(BF16)), v_ref[0, pl.ds(base, nk), vcols])
               + _dotf(jnp.exp2((s_ctx - m).astype(BF16)), vc_ref[0, :, vcols]))
        even, odd = acc[0:nq], acc[nq:2 * nq]
        num = jnp.where(low, even[:, 0:LANES], odd[:, 0:LANES])
        den = jnp.where(low, even[:, LANES:2 * LANES], odd[:, LANES:2 * LANES])
        o_ref[0, :, p * LANES:(p + 1) * LANES] = (num / den).astype(o_ref.dtype)

    pending = []
    for p in range(NA_HEADS // 2):
        pending.append((p,) + scores(p))
        if len(pending) > NA_LOOKAHEAD:
            consume(*pending.pop(0))
    for item in pending:
        consume(*item)


def _na_bias_table(rpb, rows):
    n_blocks = rows // NA_QROWS
    n_h = rpb.shape[0]
    c = np.arange(GRID_W)[:, None]
    kc = np.arange(GRID_W)[None, :]
    cs = np.clip(c - NA_WIN_W // 2, 0, GRID_W - NA_WIN_W)
    col_ok = (kc >= cs) & (kc < cs + NA_WIN_W)
    dc = np.clip(kc - c + (NA_WIN_W - 1), 0, 2 * NA_WIN_W - 2)
    pick = (np.arange(2 * NA_WIN_W - 1)[:, None] == dc.reshape(-1)[None, :]).astype(np.float32)
    variants = []
    for r0 in (0, NA_QROWS * (n_blocks // 2), rows - NA_QROWS):
        base = int(np.clip(r0 - NA_WIN_H // 2, 0, rows - NA_KROWS))
        r = r0 + np.arange(NA_QROWS)[:, None]
        kr = base + np.arange(NA_KROWS)[None, :]
        rs = np.clip(r - NA_WIN_H // 2, 0, rows - NA_WIN_H)
        row_ok = (kr >= rs) & (kr < rs + NA_WIN_H)
        dr = np.clip(kr - r + (NA_WIN_H - 1), 0, 2 * NA_WIN_H - 2)
        by_row = rpb[:, dr.reshape(-1), :].astype(F32)
        full = jnp.einsum("hxd,dm->hxm", by_row, jnp.asarray(pick), precision=lax.Precision.HIGHEST)
        full = full.reshape(n_h, NA_QROWS, NA_KROWS, GRID_W, GRID_W).transpose(0, 1, 3, 2, 4)
        full = full.reshape(n_h, NA_QROWS * GRID_W, NA_KROWS * GRID_W)
        valid = (row_ok[:, None, :, None] & col_ok[None, :, None, :]).reshape(full.shape[1:])
        variants.append(jnp.where(valid[None], full * LOG2E, MASK_VALUE))
    return jnp.stack(variants, 0).reshape(3, n_h // 2, 2 * NA_QROWS * GRID_W, NA_KROWS * GRID_W)


def _na_call(q, k, v, kc, vc, bias):
    b, t, w = q.shape
    n_blocks = t // (NA_QROWS * GRID_W)
    nq = NA_QROWS * GRID_W
    tc = kc.shape[1]

    def bias_map(i, r):
        return (jnp.where(r == 0, 0, jnp.where(r == n_blocks - 1, 2, 1)), 0, 0, 0)

    return pl.pallas_call(
        _na_kernel,
        grid=(b, n_blocks),
        in_specs=[pl.BlockSpec((1, nq, w), lambda i, r: (i, r, 0)),
                  pl.BlockSpec((1, t, w), lambda i, r: (i, 0, 0)),
                  pl.BlockSpec((1, t, 2 * w), lambda i, r: (i, 0, 0)),
                  pl.BlockSpec((1, tc, w), lambda i, r: (i, 0, 0)),
                  pl.BlockSpec((1, tc, 2 * w), lambda i, r: (i, 0, 0)),
                  pl.BlockSpec((1, NA_HEADS // 2, 2 * nq, NA_KROWS * GRID_W), bias_map)],
        out_specs=pl.BlockSpec((1, nq, w), lambda i, r: (i, r, 0)),
        out_shape=jax.ShapeDtypeStruct((b, t, w), BF16),
        compiler_params=_params(2),
        name="na_attn",
    )(q, k, v, kc, vc, bias)


GDN_MAX_BLOCK = 8 * GDN_CHUNK
GDN_PAIRS = GDN_HEADS // 2
TN_DIMS = (((0,), (0,)), ((), ()))


def _split3(x):
    hi = x.astype(BF16)
    r = x - hi.astype(F32)
    mid = r.astype(BF16)
    lo = (r - mid.astype(F32)).astype(BF16)
    return hi, mid, lo


def _bd(x):
    lane = lax.broadcasted_iota(jnp.int32, x.shape, 1)
    z = jnp.zeros_like(x)
    return jnp.concatenate([jnp.where(lane < HEAD_W, x, z), jnp.where(lane >= HEAD_W, x, z)], axis=0)


def _dotb(a, b):
    return jnp.dot(a.astype(BF16), b.astype(BF16), preferred_element_type=F32)


def _mm_pair(x, y):
    return _dotb(x, _bd(y))


def _softplus(x):
    return jnp.maximum(x, 0.0) + jnp.log(1.0 + jnp.exp(-jnp.abs(x)))


def _gdn_kernel(qkv_ref, misc_ref, alog_ref, dtb_ref, eg_ref, eb_ref, tri3_ref, s0_ref, o_ref, sfin_ref, s_ref, *,
                rev, n_blk):
    step = pl.program_id(1)
    c_len = GDN_CHUNK
    hw = GDN_HEADS * HEAD_W

    @pl.when(step == 0)
    def _():
        s_ref[...] = s0_ref[0]

    qn = qkv_ref[0, :, 0:hw].astype(F32)
    kn = qkv_ref[0, :, hw:2 * hw].astype(F32)
    v = qkv_ref[0, :, 2 * hw:3 * hw].astype(F32)

    misc = misc_ref[0]
    g_all = -jnp.exp(alog_ref[...]) * _softplus(misc + dtb_ref[...])
    b_all = jax.nn.sigmoid(misc)
    gx = jnp.dot(jnp.concatenate(_split3(g_all), axis=1), eg_ref[...], preferred_element_type=F32)
    bx = jnp.dot(jnp.concatenate(_split3(b_all), axis=1), eb_ref[...], preferred_element_type=F32)

    row = lax.broadcasted_iota(jnp.int32, (c_len, LANES), 0)
    colp = lax.broadcasted_iota(jnp.int32, (c_len, LANES), 1) & (HEAD_W - 1)
    tri = (row <= colp) if rev else (row >= colp)
    strict = (row < colp) if rev else (row > colp)
    eye_f = (row == colp).astype(F32)
    row_w = lax.broadcasted_iota(jnp.int32, (c_len, hw), 0)
    col_w = lax.broadcasted_iota(jnp.int32, (c_len, hw), 1) & (HEAD_W - 1)
    eye_w = row_w == col_w
    bd_row = lax.broadcasted_iota(jnp.int32, (LANES, LANES), 0)
    bd_col = lax.broadcasted_iota(jnp.int32, (LANES, LANES), 1)
    bd_mask = (bd_row < HEAD_W) == (bd_col < HEAD_W)
    ones8 = jnp.ones((8, 3 * c_len), BF16)

    n_chunks = qkv_ref.shape[1] // c_len
    keys = [(c, p) for c in range(n_chunks) for p in range(GDN_PAIRS)]
    sl = [slice(p * LANES, (p + 1) * LANES) for p in range(GDN_PAIRS)]
    gtot, vb, kbg, qd, kd = {}, {}, {}, {}, {}
    ms, qks = {}, {}
    chunk_rows = [slice(c * c_len, (c + 1) * c_len) for c in range(n_chunks)]
    gcs = [jnp.dot(tri3_ref[...], jnp.concatenate(_split3(gx[r]), axis=0), preferred_element_type=F32)
           for r in chunk_rows]
    kbs = [kn[r] * bx[r] for r in chunk_rows]
    aqs = {}
    for c, p in keys:
        r = chunk_rows[c]
        lhs = jnp.concatenate([kbs[c][:, sl[p]], qn[r][:, sl[p]]], axis=0)
        aqs[c, p] = lax.dot_general(lhs.astype(BF16), _bd(kn[r][:, sl[p]]).astype(BF16), NT_DIMS,
                                    preferred_element_type=F32)
    grs = [jnp.dot(ones8, jnp.concatenate(_split3(jnp.where(eye_w, gc, 0.0)), axis=0),
                   preferred_element_type=F32)[0:1] for gc in gcs]
    for c, r in enumerate(chunk_rows):
        gc = gcs[c]
        gtot[c] = gc[0:1] if rev else gc[c_len - 1:c_len]
        egc = jnp.exp(gc)
        vb[c] = v[r] * bx[r]
        kbg[c] = kbs[c] * egc
        qd[c] = qn[r] * egc
        kd[c] = kn[r] * jnp.exp(gtot[c] - gc)
        for p in range(GDN_PAIRS):
            diff = gc[:, sl[p]] - grs[c][:, sl[p]]
            decay = jnp.where(tri, jnp.exp(jnp.where(tri, diff, 0.0)), 0.0)
            ms[c, p] = jnp.where(strict, aqs[c, p][0:c_len] * decay, 0.0)
            qks[c, p] = aqs[c, p][c_len:2 * c_len] * decay

    rblk, cblk = row, colp
    xs = {}
    for lvl in range(6):
        sib = (cblk == rblk + 1) if rev else (cblk == rblk - 1)
        odd = (rblk & 1) == (0 if rev else 1)
        join = sib & odd
        if lvl == 0:
            xs = {key: eye_f - jnp.where(join, ms[key], 0.0) for key in keys}
        else:
            inner = {key: _mm_pair(jnp.where(join, ms[key], 0.0), xs[key]) for key in keys}
            xs = {key: xs[key] - _mm_pair(xs[key], inner[key]) for key in keys}
        rblk, cblk = rblk >> 1, cblk >> 1
    uws = {}
    for c, p in keys:
        rhs = jnp.concatenate([_bd(vb[c][:, sl[p]]), _bd(kbg[c][:, sl[p]])], axis=1)
        uws[c, p] = _dotb(xs[c, p], rhs)

    order = range(n_chunks - 1, -1, -1) if rev else range(n_chunks)
    state = [s_ref[p] for p in range(GDN_PAIRS)]
    pairs = range(GDN_PAIRS)
    for c in order:
        r = slice(c * c_len, (c + 1) * c_len)
        wq = [_dotb(jnp.concatenate([uws[c, p][:, LANES:2 * LANES], qd[c][:, sl[p]]], axis=0), state[p])
              for p in pairs]
        v_new = [uws[c, p][:, 0:LANES] - wq[p][0:c_len] for p in pairs]
        upd = [lax.dot_general(kd[c][:, sl[p]].astype(BF16), v_new[p].astype(BF16), TN_DIMS,
                               preferred_element_type=F32) for p in pairs]
        for p in pairs:
            o_ref[0, r, sl[p]] = wq[p][c_len:2 * c_len] + _dotb(qks[c, p], _bd(v_new[p]))
            state[p] = state[p] * jnp.exp(gtot[c][:, sl[p]]) + jnp.where(bd_mask, upd[p], 0.0)
    for p in range(GDN_PAIRS):
        s_ref[p] = state[p]

    @pl.when(step == n_blk - 1)
    def _():
        for p in range(GDN_PAIRS):
            sfin_ref[0, p] = state[p]


def _gdn_consts(a_log, dt_bias, d, rev):
    pad = jnp.zeros((LANES - 2 * GDN_HEADS,), F32)
    alog = jnp.concatenate([a_log.reshape(-1).astype(F32), pad]).reshape(1, LANES)
    dtb = jnp.concatenate([dt_bias.reshape(-1).astype(F32), pad]).reshape(1, LANES)
    hw = GDN_HEADS * HEAD_W
    head_of_lane = np.arange(hw) // HEAD_W
    src = np.arange(LANES)[:, None]
    eg = (src == d * GDN_HEADS + head_of_lane[None, :]).astype(np.float32)
    eb = (src == 2 * GDN_HEADS + d * GDN_HEADS + head_of_lane[None, :]).astype(np.float32)
    i = np.arange(GDN_CHUNK)
    tri = (i[:, None] <= i[None, :]) if rev else (i[:, None] >= i[None, :])
    as_bf = lambda a, reps, ax: jnp.asarray(np.concatenate([a] * reps, axis=ax), BF16)
    return alog, dtb, as_bf(eg, 3, 0), as_bf(eb, 3, 0), as_bf(tri.astype(np.float32), 3, 1)


def _gdn_call(qkv, misc, consts, s0, rev, name):
    b, t, w = qkv.shape
    blk = min(t, GDN_MAX_BLOCK)
    assert t % blk == 0 and blk % GDN_CHUNK == 0
    n_blk = t // blk
    hw = GDN_HEADS * HEAD_W
    blk_of = (lambda s: n_blk - 1 - s) if rev else (lambda s: s)
    const = lambda a: pl.BlockSpec(a.shape, lambda i, s: (0,) * a.ndim)
    s_spec = pl.BlockSpec((1, GDN_PAIRS, LANES, LANES), lambda i, s: (i, 0, 0, 0))
    return pl.pallas_call(
        functools.partial(_gdn_kernel, rev=rev, n_blk=n_blk),
        grid=(b, n_blk),
        in_specs=[pl.BlockSpec((1, blk, w), lambda i, s: (i, blk_of(s), 0)),
                  pl.BlockSpec((1, blk, LANES), lambda i, s: (i, blk_of(s), 0))]
                 + [const(a) for a in consts] + [s_spec],
        out_specs=[pl.BlockSpec((1, blk, hw), lambda i, s: (i, blk_of(s), 0)), s_spec],
        out_shape=[jax.ShapeDtypeStruct((b, t, hw), F32),
                   jax.ShapeDtypeStruct((b, GDN_PAIRS, LANES, LANES), F32)],
        scratch_shapes=[pltpu.VMEM((GDN_PAIRS, LANES, LANES), F32)],
        compiler_params=_params(2),
        name=name,
    )(qkv, misc, *consts, s0)


MIX_SUB_ROWS = 256


def _mix_ffn_kernel(*refs, ff_chunks, gdn_inputs, n_sub):
    tm = refs[0].shape[1]
    rows = [slice(i * (tm // n_sub), (i + 1) * (tm // n_sub)) for i in range(n_sub)]
    if gdn_inputs:
        (h_ref, of_ref, ob_ref, gate_ref, att_ref, on_ref, ones_ref, gtm_ref, shf_ref, scf_ref, gtf_ref, gpm_ref,
         gpf_ref, gqf_ref, wo_ref, wg_ref, wu_ref, wd_ref, o_ref) = refs
        mixes = []
        for r in rows:
            o = of_ref[0, r, :] + ob_ref[0, r, :]
            gate = gate_ref[0, r, :].astype(F32)
            a = o * lax.rsqrt(_seg_mean_sq(o, ones_ref) + NORM_EPS) * on_ref[...] * (gate * jax.nn.sigmoid(gate))
            mixes.append(jnp.concatenate([a.astype(BF16), att_ref[0, r, :]], axis=1))
    else:
        (h_ref, a_ref, b_ref, gtm_ref, shf_ref, scf_ref, gtf_ref, gpm_ref, gpf_ref, gqf_ref, wo_ref, wg_ref,
         wu_ref, wd_ref, o_ref) = refs
        mixes = [jnp.concatenate([a_ref[0, r, :], b_ref[0, r, :]], axis=1) for r in rows]
    ys = [_dotf(mix, wo_ref[...]) for mix in mixes]
    h1s = [h_ref[0, r, :] + gtm_ref[0] * (_rms(y) * gpm_ref[...]) for r, y in zip(rows, ys)]
    us = [(_rms(h1) * gpf_ref[...] * (1.0 + scf_ref[0]) + shf_ref[0]).astype(BF16) for h1 in h1s]
    acts = []
    for o, n in ff_chunks:
        for u in us:
            gg = _dotf(u, wg_ref[:, o:o + n])
            uu = _dotf(u, wu_ref[:, o:o + n])
            acts.append((gg * jax.nn.sigmoid(gg) * uu).astype(BF16))
    fs = [None] * n_sub
    for ci, (o, n) in enumerate(ff_chunks):
        for i in range(n_sub):
            part = _dotf(acts[ci * n_sub + i], wd_ref[o:o + n, :])
            fs[i] = part if fs[i] is None else fs[i] + part
    for r, h1, f in zip(rows, h1s, fs):
        o_ref[0, r, :] = h1 + gtf_ref[0] * (_rms(f) * gqf_ref[...])


def _mix_ffn_call(h, mix_inputs, out_norm, gt_m, sh_f, sc_f, gt_f, g_post_mix, g_pre_ffn, g_post_ffn, wo, wg, wu,
                  wd, tm, name):
    b, t, d = h.shape
    dm = wo.shape[0]
    ff = wg.shape[1]
    gdn_inputs = out_norm is not None
    half = pl.cdiv(ff // 2, MXU_TILE) * MXU_TILE
    assert 0 < half < ff and ff % LANES == 0
    ff_chunks = ((0, half), (half, ff - half))
    tok = lambda n: pl.BlockSpec((1, tm, n), lambda i, j: (i, j, 0))
    per_b = pl.BlockSpec((1, 1, d), lambda i, j: (i, 0, 0))
    vec = pl.BlockSpec((1, d), lambda i, j: (0, 0))
    const = lambda shape: pl.BlockSpec(shape, lambda i, j: (0, 0), pipeline_mode=pl.Buffered(1))
    r3 = lambda a: a.reshape(b, 1, d)
    r2 = lambda a: a.reshape(1, d).astype(F32)
    mix_specs = [tok(m.shape[-1]) for m in mix_inputs]
    mix_args = list(mix_inputs)
    if gdn_inputs:
        hw = GDN_HEADS * GDN_DV
        on = jnp.tile(out_norm.astype(F32), GDN_HEADS).reshape(1, hw)
        ones = _seg_ones(hw)
        mix_specs += [pl.BlockSpec(on.shape, lambda i, j: (0, 0)), pl.BlockSpec(ones.shape, lambda i, j: (0, 0))]
        mix_args += [on, ones]
    return pl.pallas_call(
        functools.partial(_mix_ffn_kernel, ff_chunks=ff_chunks, gdn_inputs=gdn_inputs,
                          n_sub=tm // MIX_SUB_ROWS),
        grid=(b, t // tm),
        in_specs=[tok(d)] + mix_specs + [per_b, per_b, per_b, per_b, vec, vec, vec,
                                         const((dm, d)), const((d, ff)), const((d, ff)), const((ff, d))],
        out_specs=tok(d),
        out_shape=jax.ShapeDtypeStruct((b, t, d), F32),
        compiler_params=_params(2),
        name=name,
    )(h, *mix_args, r3(gt_m), r3(sh_f), r3(sc_f), r3(gt_f), r2(g_post_mix), r2(g_pre_ffn), r2(g_post_ffn),
      wo, wg, wu, wd)


def _rope_tables(n_tokens, rot_dim):
    t = jnp.arange(n_tokens, dtype=jnp.int32)
    row = (t // GRID_W).astype(F32)
    col = (t % GRID_W).astype(F32)
    n_freq = rot_dim // 4
    inv_freq = ROPE_THETA ** (-jnp.arange(n_freq, dtype=F32) / n_freq)
    ang = jnp.concatenate([row[:, None] * inv_freq, col[:, None] * inv_freq], -1)
    cos = jnp.concatenate([jnp.cos(ang), jnp.cos(ang)], -1)
    sin = jnp.concatenate([-jnp.sin(ang), jnp.sin(ang)], -1)
    if rot_dim == HEAD_W:
        return jnp.tile(cos, (1, LANES // rot_dim)), jnp.tile(sin, (1, LANES // rot_dim))
    pad = LANES - rot_dim
    return (jnp.concatenate([cos, jnp.ones((n_tokens, pad), F32)], -1),
            jnp.concatenate([sin, jnp.zeros((n_tokens, pad), F32)], -1))


def _no_rope_tables(n_tokens):
    return jnp.ones((n_tokens, LANES), F32), jnp.zeros((n_tokens, LANES), F32)


def _gdn_bidirectional(lat_qkv, lat_misc, ctx_qkv, ctx_misc, a_log, dt_bias):
    bsz = lat_qkv.shape[0]
    s_zero = jnp.zeros((bsz, GDN_PAIRS, LANES, LANES), F32)
    lat, ctx = [], []
    for d, rev in ((0, False), (1, True)):
        consts = _gdn_consts(a_log, dt_bias, d, rev)
        o_c, s_c = _gdn_call(ctx_qkv, ctx_misc, consts, s_zero, rev, "gdn_ctx_%d" % d)
        o_l, _ = _gdn_call(lat_qkv, lat_misc, consts, s_c, rev, "gdn_lat_%d" % d)
        lat.append(o_l)
        ctx.append(o_c)
    return lat, ctx


LOG2E = 1.4426950408889634
ATTN_TQ = 1024
ATTN_LOOKAHEAD = 2


def _attn_chunks(l_len, s_len, size):
    return [(0, 0, l_len)] + [(1, o, size) for o in range(0, s_len, size)]


def _even_mixer(h, hc, mods, mods_c, g_pre, w_in, conv_w, a_log, dt_bias, q_norm, w_q_up, kv_norm, w_kv_up):
    s_len, l_len = h.shape[1], hc.shape[1]
    weights = _even_weights(w_in, q_norm, w_q_up, kv_norm, w_kv_up)
    qkv_l, gate_l, misc_l, q_l, k_l, v_l = _in_even_call(h, g_pre, *mods, weights, conv_w,
                                                         *_rope_tables(s_len, MLA_ROPE), 512, "in_even_lat")
    qkv_c, gate_c, misc_c, q_c, k_c, v_c = _in_even_call(hc, g_pre, *mods_c, weights, conv_w,
                                                         *_no_rope_tables(l_len), l_len, "in_even_ctx")
    o_lat, o_ctx = _gdn_bidirectional(qkv_l, misc_l, qkv_c, misc_c, a_log, dt_bias)
    b_lat = _attn_call(q_l, [(k_c, v_c), (k_l, v_l)], MLA_QC, MLA_PAIRS, MLA_PAIRS, _attn_chunks(l_len, s_len, 512),
                       ATTN_TQ, False, "mla_attn_lat")
    b_ctx = _attn_call(q_c, [(k_c, v_c)], MLA_QC, MLA_PAIRS, MLA_PAIRS, [(0, 0, l_len)], l_len, False,
                       "mla_attn_ctx")
    return (o_lat[0], o_lat[1], gate_l, b_lat), (o_ctx[0], o_ctx[1], gate_c, b_ctx)


def _odd_mixer_last(h, hc, mods, mods_c, g_pre, w_in, rpb, q_norm, k_norm):
    s_len, l_len = h.shape[1], hc.shape[1]
    w = w_in.astype(BF16)
    q_gain = (jnp.tile(q_norm.astype(F32), GQA_HEADS) * (GQA_DIM ** -0.5 * LOG2E)).reshape(1, -1)
    k_gain = jnp.tile(k_norm.astype(F32), GQA_KV_HEADS).reshape(1, -1)
    naq_l, nak_l, nav_l, gq_l, gk_l, gv_l = _in_odd_call(h, g_pre, *mods, w, q_gain, k_gain,
                                                         *_rope_tables(s_len, GQA_DIM), 512, "in_odd_lat")
    _, nak_c, nav_c, _, gk_c, gv_c = _in_odd_call(hc, g_pre, *mods_c, w, q_gain, k_gain, *_no_rope_tables(l_len),
                                                  l_len, "in_odd_ctx")
    c_lat = _na_call(naq_l, nak_l, nav_l, nak_c, nav_c, _na_bias_table(rpb, s_len // GRID_W))
    d_lat = _attn_call(gq_l, [(gk_c, gv_c), (gk_l, gv_l)], LANES, GQA_HEADS // 2, GQA_KV_HEADS,
                       _attn_chunks(l_len, s_len, 256), ATTN_TQ, True, "gqa_attn")
    return c_lat, d_lat


def kernel(x, c, ctx, c_ctx, w_mod, b_mod, g_pre_mix, g_post_mix, g_pre_ffn, g_post_ffn, w_ffn_gate, w_ffn_up,
           w_ffn_down, w_in_even, w_out_even, gdn_conv, gdn_a_log, gdn_dt_bias, gdn_out_norm, mla_q_norm,
           mla_w_q_up, mla_kv_norm, mla_w_kv_up, w_in_odd, w_out_odd, na_rpb, gqa_q_norm, gqa_k_norm):
    bsz, s_len, d = x.shape
    depth = w_mod.shape[0]
    assert depth == 2, "layer 0 = even mixer with context update, layer 1 = odd mixer (last)"
    c_rows = jnp.concatenate([c, c_ctx[None, :], jnp.zeros((16 - bsz - 1, d), F32)], 0)
    h, hc = x, ctx
    for i in range(depth):
        mod_all = _mod_call(c_rows, w_mod[i].astype(BF16), b_mod[i])
        mod = jnp.split(mod_all[:bsz], 6, axis=-1)
        mod_c = [jnp.broadcast_to(m, (bsz, d)) for m in jnp.split(mod_all[bsz:bsz + 1], 6, axis=-1)]
        sh_m, sc_m, gt_m, sh_f, sc_f, gt_f = mod
        csh_m, csc_m, cgt_m, csh_f, csc_f, cgt_f = mod_c
        wg, wu, wd = w_ffn_gate[i].astype(BF16), w_ffn_up[i].astype(BF16), w_ffn_down[i].astype(BF16)
        if i == 0:
            mix, mix_c = _even_mixer(h, hc, (sh_m, sc_m), (csh_m, csc_m), g_pre_mix[i], w_in_even[0], gdn_conv[0],
                                     gdn_a_log[0], gdn_dt_bias[0], mla_q_norm[0], mla_w_q_up[0], mla_kv_norm[0],
                                     mla_w_kv_up[0])
            wo, out_norm = w_out_even[0].astype(BF16), gdn_out_norm[0]
            hc = _mix_ffn_call(hc, mix_c, out_norm, cgt_m, csh_f, csc_f, cgt_f, g_post_mix[i], g_pre_ffn[i],
                               g_post_ffn[i], wo, wg, wu, wd, 256, "mix_ffn_ctx")
        else:
            mix = _odd_mixer_last(h, hc, (sh_m, sc_m), (csh_m, csc_m), g_pre_mix[i], w_in_odd[0], na_rpb[0],
                                  gqa_q_norm[0], gqa_k_norm[0])
            wo, out_norm = w_out_odd[0].astype(BF16), None
        h = _mix_ffn_call(h, mix, out_norm, gt_m, sh_f, sc_f, gt_f, g_post_mix[i], g_pre_ffn[i], g_post_ffn[i],
                          wo, wg, wu, wd, 2 * MIX_SUB_ROWS, "mix_ffn_lat%d" % i)
    return h
```

```python
import functools

import numpy as np
import jax
import jax.numpy as jnp
from jax import lax
from jax.experimental import pallas as pl
from jax.experimental.pallas import tpu as pltpu

F32 = jnp.float32
BF16 = jnp.bfloat16

GRID_W = 64
NORM_EPS = 1e-6
ROPE_THETA = 10000.0

GDN_HEADS = 8
GDN_DK = 64
GDN_DV = 64
GDN_CHUNK = 64
GDN_QK_W = GDN_HEADS * GDN_DK
GDN_V_W = GDN_HEADS * GDN_DV

MLA_HEADS = 8
MLA_NOPE = 64
MLA_ROPE = 32
MLA_V = 64
MLA_Q_LORA = 256
MLA_KV_LORA = 128

NA_HEADS = 8
NA_DIM = 64
NA_WIN_H = 8
NA_WIN_W = 16
NA_W = NA_HEADS * NA_DIM

GQA_HEADS = 8
GQA_KV_HEADS = 2
GQA_DIM = 64

LANES = 128
MXU_TILE = 256
HEAD_W = 64
MASK_VALUE = -1e30
VMEM_LIMIT = 56 << 20

NT_DIMS = (((1,), (1,)), ((), ()))


def _params(n_grid, vmem=VMEM_LIMIT):
    return pltpu.CompilerParams(dimension_semantics=("arbitrary",) * n_grid, vmem_limit_bytes=vmem)


def _rms(x):
    return x * lax.rsqrt(jnp.mean(x * x, axis=-1, keepdims=True) + NORM_EPS)


def _mod_kernel(c_ref, w_ref, b_ref, o_ref):
    c = c_ref[...]
    a = (c * jax.nn.sigmoid(c)).astype(BF16)
    o_ref[...] = jnp.dot(a, w_ref[...], preferred_element_type=F32) + b_ref[...]


def _mod_call(c_rows, w, b):
    r, d = c_rows.shape
    n = w.shape[1]
    tn = 1024
    return pl.pallas_call(
        _mod_kernel,
        grid=(n // tn,),
        in_specs=[pl.BlockSpec((r, d), lambda j: (0, 0)),
                  pl.BlockSpec((d, tn), lambda j: (0, j)),
                  pl.BlockSpec((1, tn), lambda j: (0, j))],
        out_specs=pl.BlockSpec((r, tn), lambda j: (0, j)),
        out_shape=jax.ShapeDtypeStruct((r, n), F32),
        compiler_params=_params(1),
        name="mod",
    )(c_rows, w, b.reshape(1, n))


def _norm_mod(x_ref, g_ref, sh_ref, sc_ref):
    return (_rms(x_ref[0]) * g_ref[...] * (1.0 + sc_ref[0]) + sh_ref[0]).astype(BF16)


def _dotf(a, b):
    return jnp.dot(a, b, preferred_element_type=F32)


def _seg_mean_sq(x, ones_ref):
    x2 = x * x
    hi = x2.astype(BF16)
    mid = (x2 - hi.astype(F32)).astype(BF16)
    return _dotf(jnp.concatenate([hi, mid], axis=1), ones_ref[...]) * (1.0 / HEAD_W)


def _rotate(t, cos, sin_signed, half):
    lane = lax.broadcasted_iota(jnp.int32, t.shape, 1)
    n = t.shape[1]
    partner = jnp.where((lane & (2 * half - 1)) < half, pltpu.roll(t, n - half, 1), pltpu.roll(t, half, 1))
    return t * cos + partner * sin_signed


EVEN_QKV_W = 2 * GDN_QK_W + GDN_V_W
EVEN_GROUPS = (EVEN_QKV_W, GDN_V_W, MLA_Q_LORA, MLA_KV_LORA, LANES, LANES)
MLA_QC = 2 * LANES
MLA_PAIRS = MLA_HEADS // 2


CONV_HALO = 16


def _in_even_kernel(x_ref, xp_ref, xn_ref, g_ref, sh_ref, sc_ref, w_ref, qg_ref, wq_ref, kvg_ref, wk_ref, wv_ref,
                    vone_ref, cos_ref, sin_ref, cw_ref, ones_ref, qkv_ref, gate_ref, misc_ref, q_ref, k_ref, v_ref,
                    zs_ref, *, n_tap):
    j = pl.program_id(1)
    tm = x_ref.shape[1]
    u = _norm_mod(x_ref, g_ref, sh_ref, sc_ref)
    offs = np.cumsum((0,) + EVEN_GROUPS)
    grp = lambda i: _dotf(u, w_ref[:, int(offs[i]):int(offs[i + 1])])
    u_ext = jnp.concatenate([_norm_mod(xp_ref, g_ref, sh_ref, sc_ref), u, _norm_mod(xn_ref, g_ref, sh_ref, sc_ref)],
                            axis=0)
    z = _dotf(u_ext, w_ref[:, 0:EVEN_QKV_W])
    zs_ref[0:CONV_HALO, :] = z[0:CONV_HALO] * (j > 0).astype(F32)
    zs_ref[CONV_HALO:CONV_HALO + tm, :] = z[CONV_HALO:CONV_HALO + tm]
    zs_ref[CONV_HALO + tm:, :] = z[CONV_HALO + tm:] * (j < pl.num_programs(1) - 1).astype(F32)
    q_down, kv_down, pe_raw = grp(2), grp(3), grp(5)
    gate_ref[0] = grp(1).astype(BF16)
    misc_ref[0] = grp(4)
    cos, sin = cos_ref[...], sin_ref[...]
    qn = (_rms(q_down) * qg_ref[...]).astype(BF16)
    kvn = (_rms(kv_down) * kvg_ref[...]).astype(BF16)
    for h in range(MLA_HEADS):
        blk = _dotf(qn, wq_ref[:, h * MLA_QC:(h + 1) * MLA_QC])
        q_ref[0, :, h * MLA_QC:h * MLA_QC + LANES] = blk[:, 0:LANES].astype(BF16)
        q_ref[0, :, h * MLA_QC + LANES:(h + 1) * MLA_QC] = _rotate(blk[:, LANES:], cos, sin,
                                                                    MLA_ROPE // 2).astype(BF16)
    pe = _rotate(pe_raw, cos, sin, MLA_ROPE // 2).astype(BF16)
    k_nope = _dotf(kvn, wk_ref[...])
    for p in range(MLA_PAIRS):
        k_ref[0, :, p * MLA_QC:p * MLA_QC + LANES] = k_nope[:, p * LANES:(p + 1) * LANES].astype(BF16)
        k_ref[0, :, p * MLA_QC + LANES:(p + 1) * MLA_QC] = pe
    v_ref[0] = (_dotf(kvn, wv_ref[...]) + vone_ref[...]).astype(BF16)
    acc = None
    for t in range(n_tap):
        start = CONV_HALO - n_tap // 2 + t
        term = zs_ref[start:start + tm, :] * cw_ref[t:t + 1, :]
        acc = term if acc is None else acc + term
    y = acc * jax.nn.sigmoid(acc)
    hw = GDN_QK_W
    qc, kc = y[:, 0:hw], y[:, hw:2 * hw]
    q_ss = _seg_mean_sq(qc, ones_ref) * HEAD_W
    k_ss = _seg_mean_sq(kc, ones_ref) * HEAD_W
    qkv_ref[0, :, 0:hw] = (qc * lax.rsqrt(q_ss + NORM_EPS) * (GDN_DK ** -0.5)).astype(BF16)
    qkv_ref[0, :, hw:2 * hw] = (kc * lax.rsqrt(k_ss + NORM_EPS)).astype(BF16)
    qkv_ref[0, :, 2 * hw:] = y[:, 2 * hw:].astype(BF16)


def _even_weights(w_in, q_norm, w_q_up, kv_norm, w_kv_up):
    d = w_in.shape[0]
    o0 = EVEN_QKV_W + GDN_V_W
    n_ab = 4 * GDN_HEADS
    o1 = o0 + n_ab
    o2 = o1 + MLA_Q_LORA
    o3 = o2 + MLA_KV_LORA
    zeros = lambda n, rows=d: jnp.zeros((rows, n), F32)
    w_cat = jnp.concatenate([w_in[:, :o0], w_in[:, o1:o2], w_in[:, o2:o3],
                             w_in[:, o0:o1], zeros(LANES - n_ab),
                             w_in[:, o3:], zeros(LANES - MLA_ROPE)], 1).astype(BF16)
    qh = MLA_NOPE + MLA_ROPE
    cols = []
    for h in range(MLA_HEADS):
        nope = w_q_up[:, h * qh:h * qh + MLA_NOPE]
        rp = w_q_up[:, h * qh + MLA_NOPE:(h + 1) * qh]
        z = zeros(HEAD_W, MLA_Q_LORA)
        cols += ([nope, z] if h % 2 == 0 else [z, nope]) + [rp, zeros(LANES - MLA_ROPE, MLA_Q_LORA)]
    wq = jnp.concatenate(cols, 1).astype(BF16)
    kvh = MLA_NOPE + MLA_V
    wk = jnp.concatenate([w_kv_up[:, h * kvh:h * kvh + MLA_NOPE] for h in range(MLA_HEADS)], 1).astype(BF16)
    vcols, ones = [], []
    for h in range(MLA_HEADS):
        vh = w_kv_up[:, h * kvh + MLA_NOPE:(h + 1) * kvh]
        z = zeros(HEAD_W, MLA_KV_LORA)
        vcols += [vh, z] if h % 2 == 0 else [z, vh]
        ones += [0.0, 1.0] if h % 2 == 0 else [1.0, 0.0]
    wv = jnp.concatenate(vcols, 1).astype(BF16)
    vone = jnp.asarray(np.repeat(np.asarray(ones, np.float32), HEAD_W)[None, :])
    q_gain = (q_norm * ((MLA_NOPE + MLA_ROPE) ** -0.5 * LOG2E)).reshape(1, -1).astype(F32)
    return w_cat, q_gain, wq, kv_norm.reshape(1, -1).astype(F32), wk, wv, vone


def _in_even_call(x, g, shift, scale, weights, conv_w, cos, sin, tm, name):
    b, t, d = x.shape
    w_cat, q_gain, wq, kv_gain, wk, wv, vone = weights
    n_tap = conv_w.shape[0]
    cw = jnp.concatenate([conv_w.astype(F32), jnp.zeros((8 - n_tap, conv_w.shape[1]), F32)], 0)
    ones = _seg_ones(GDN_QK_W)
    per = tm // CONV_HALO
    n_halo = t // CONV_HALO
    const = lambda a: pl.BlockSpec(a.shape, lambda i, j: (0,) * a.ndim)
    per_b = pl.BlockSpec((1, 1, d), lambda i, j: (i, 0, 0))
    tab = pl.BlockSpec((tm, LANES), lambda i, j: (j, 0))
    widths = (EVEN_QKV_W, GDN_V_W, LANES, MLA_HEADS * MLA_QC, MLA_PAIRS * MLA_QC, MLA_HEADS * LANES)
    dts = (BF16, BF16, F32, BF16, BF16, BF16)
    g2 = g.reshape(1, d).astype(F32)
    return pl.pallas_call(
        functools.partial(_in_even_kernel, n_tap=n_tap),
        grid=(b, t // tm),
        in_specs=[pl.BlockSpec((1, tm, d), lambda i, j: (i, j, 0)),
                  pl.BlockSpec((1, CONV_HALO, d), lambda i, j: (i, jnp.maximum(j * per - 1, 0), 0)),
                  pl.BlockSpec((1, CONV_HALO, d), lambda i, j: (i, jnp.minimum((j + 1) * per, n_halo - 1), 0)),
                  const(g2), per_b, per_b, const(w_cat), const(q_gain), const(wq), const(kv_gain), const(wk),
                  const(wv), const(vone), tab, tab, const(cw), const(ones)],
        out_specs=[pl.BlockSpec((1, tm, n), lambda i, j: (i, j, 0)) for n in widths],
        out_shape=[jax.ShapeDtypeStruct((b, t, n), dt) for n, dt in zip(widths, dts)],
        scratch_shapes=[pltpu.VMEM((tm + 2 * CONV_HALO, EVEN_QKV_W), F32)],
        compiler_params=_params(2),
        name=name,
    )(x, x, x, g2, shift.reshape(b, 1, d), scale.reshape(b, 1, d), w_cat, q_gain, wq, kv_gain, wk, wv, vone,
      cos, sin, cw, ones)


GQA_KV_W = GQA_KV_HEADS * GQA_DIM


def _in_odd_kernel(x_ref, g_ref, sh_ref, sc_ref, w_ref, qg_ref, kg_ref, ones_q_ref, ones_k_ref, cos_ref, sin_ref,
                   naq_ref, nak_ref, nav_ref, gq_ref, gk_ref, gv_ref):
    u = _norm_mod(x_ref, g_ref, sh_ref, sc_ref)
    grp = lambda lo, n: _dotf(u, w_ref[:, lo:lo + n])
    o = 3 * NA_W
    q = grp(o, GQA_HEADS * GQA_DIM)
    o += GQA_HEADS * GQA_DIM
    k = grp(o, GQA_KV_W)
    v = grp(o + GQA_KV_W, GQA_KV_W)
    naq_ref[0] = (grp(0, NA_W) * (NA_DIM ** -0.5 * LOG2E)).astype(BF16)
    nak_ref[0] = grp(NA_W, NA_W).astype(BF16)
    na_v = grp(2 * NA_W, NA_W)
    low = lax.broadcasted_iota(jnp.int32, (na_v.shape[0], LANES), 1) < HEAD_W
    for p in range(NA_HEADS // 2):
        tile = na_v[:, p * LANES:(p + 1) * LANES]
        nav_ref[0, :, 2 * p * LANES:(2 * p + 1) * LANES] = tile.astype(BF16)
        nav_ref[0, :, (2 * p + 1) * LANES:(2 * p + 2) * LANES] = jnp.ones(tile.shape, BF16)
    q_ms = _seg_mean_sq(q, ones_q_ref)
    k_ms = _seg_mean_sq(k, ones_k_ref)
    cos, sin = cos_ref[...], sin_ref[...]
    qn = q * lax.rsqrt(q_ms + NORM_EPS) * qg_ref[...]
    for p in range(GQA_HEADS // 2):
        cols = slice(p * LANES, (p + 1) * LANES)
        gq_ref[0, :, cols] = _rotate(qn[:, cols], cos, sin, GQA_DIM // 2).astype(BF16)
    kr = _rotate(k * lax.rsqrt(k_ms + NORM_EPS) * kg_ref[...], cos, sin, GQA_DIM // 2)
    k_sw = pltpu.roll(kr, HEAD_W, 1)
    gk_ref[0, :, 0:LANES] = jnp.where(low, kr, k_sw).astype(BF16)
    gk_ref[0, :, LANES:2 * LANES] = jnp.where(low, k_sw, kr).astype(BF16)
    v_sw = pltpu.roll(v, HEAD_W, 1)
    slots = (jnp.where(low, v, 1.0), jnp.where(low, 1.0, v_sw),
             jnp.where(low, v_sw, 1.0), jnp.where(low, 1.0, v))
    for i, s in enumerate(slots):
        gv_ref[0, :, i * LANES:(i + 1) * LANES] = s.astype(BF16)


def _seg_ones(width):
    head = np.arange(width) // HEAD_W
    m = (head[:, None] == head[None, :]).astype(np.float32)
    return jnp.asarray(np.concatenate([m, m], 0), BF16)


def _in_odd_call(x, g, shift, scale, w, q_gain, k_gain, cos, sin, tm, name):
    b, t, d = x.shape
    assert GQA_KV_W == LANES
    ones_q = _seg_ones(GQA_HEADS * GQA_DIM)
    ones_k = _seg_ones(GQA_KV_W)
    const = lambda a: pl.BlockSpec(a.shape, lambda i, j: (0,) * a.ndim)
    per_b = pl.BlockSpec((1, 1, d), lambda i, j: (i, 0, 0))
    tab = pl.BlockSpec((tm, LANES), lambda i, j: (j, 0))
    widths = (NA_W, NA_W, 2 * NA_W, GQA_HEADS * GQA_DIM, 2 * LANES, 4 * LANES)
    g2 = g.reshape(1, d).astype(F32)
    return pl.pallas_call(
        _in_odd_kernel,
        grid=(b, t // tm),
        in_specs=[pl.BlockSpec((1, tm, d), lambda i, j: (i, j, 0)), const(g2), per_b, per_b, const(w),
                  const(q_gain), const(k_gain), const(ones_q), const(ones_k), tab, tab],
        out_specs=[pl.BlockSpec((1, tm, n), lambda i, j: (i, j, 0)) for n in widths],
        out_shape=[jax.ShapeDtypeStruct((b, t, n), BF16) for n in widths],
        compiler_params=_params(2),
        name=name,
    )(x, g2, shift.reshape(b, 1, d), scale.reshape(b, 1, d), w, q_gain, k_gain, ones_q, ones_k, cos, sin)


def _attn_kernel(*refs, dc, chunks, n_src, masked_q):
    q_ref, kv_refs, o_ref = refs[0], refs[1:1 + 2 * n_src], refs[-1]
    tq = q_ref.shape[1]
    lane = lax.broadcasted_iota(jnp.int32, (tq, LANES), 1)
    qs = []
    for hh in range(2):
        if masked_q:
            qp = q_ref[0]
            qs.append(jnp.where((lane < HEAD_W) if hh == 0 else (lane >= HEAD_W), qp, jnp.zeros_like(qp)))
        else:
            qs.append(q_ref[0, :, hh * dc:(hh + 1) * dc])
    ms, accs = [None, None], [None, None]

    def scores(hh, src, s0, n):
        return lax.dot_general(qs[hh], kv_refs[2 * src][0, s0:s0 + n, :], NT_DIMS, preferred_element_type=F32)

    def consume(s, hh, src, s0, n):
        vv = kv_refs[2 * src + 1][0, s0:s0 + n, hh * LANES:(hh + 1) * LANES]
        mc = jnp.max(s, axis=-1, keepdims=True)
        m_new = mc if ms[hh] is None else jnp.maximum(ms[hh], mc)
        pv = _dotf(jnp.exp2(s - m_new).astype(BF16), vv)
        accs[hh] = pv if ms[hh] is None else jnp.exp2(ms[hh] - m_new) * accs[hh] + pv
        ms[hh] = m_new

    items = [(hh,) + tuple(c) for c in chunks for hh in range(2)]
    pending = []
    for it in items:
        pending.append((scores(*it), it))
        if len(pending) > ATTN_LOOKAHEAD:
            s, it0 = pending.pop(0)
            consume(s, *it0)
    for s, it0 in pending:
        consume(s, *it0)
    low = lane < HEAD_W
    num = jnp.where(low, accs[0], accs[1])
    den = pltpu.roll(jnp.where(low, accs[1], accs[0]), HEAD_W, 1)
    o_ref[0] = (num / den).astype(o_ref.dtype)


def _attn_call(q, kvs, dc, n_pairs, n_groups, chunks, tq, masked_q, name):
    b, t_q, _ = q.shape
    per = n_pairs // n_groups
    q_w = LANES if masked_q else 2 * dc
    in_specs = [pl.BlockSpec((1, tq, q_w), lambda i, p, j: (i, j, p))]
    args = [q]
    for k, v in kvs:
        in_specs += [pl.BlockSpec((1, k.shape[1], dc), lambda i, p, j: (i, 0, p // per)),
                     pl.BlockSpec((1, v.shape[1], 2 * LANES), lambda i, p, j: (i, 0, p // per))]
        args += [k, v]
    return pl.pallas_call(
        functools.partial(_attn_kernel, dc=dc, chunks=tuple(chunks), n_src=len(kvs), masked_q=masked_q),
        grid=(b, n_pairs, t_q // tq),
        in_specs=in_specs,
        out_specs=pl.BlockSpec((1, tq, LANES), lambda i, p, j: (i, j, p)),
        out_shape=jax.ShapeDtypeStruct((b, t_q, n_pairs * LANES), BF16),
        compiler_params=_params(3),
        name=name,
    )(*args)


NA_QROWS = 4
NA_KROWS = 12
NA_LOOKAHEAD = 1


def _na_kernel(q_ref, k_ref, v_ref, kc_ref, vc_ref, bias_ref, o_ref):
    rb = pl.program_id(1)
    n_rows = k_ref.shape[1] // GRID_W
    base = jnp.clip(rb * NA_QROWS - NA_WIN_H // 2, 0, n_rows - NA_KROWS) * GRID_W
    base = pl.multiple_of(base, GRID_W)
    nq = NA_QROWS * GRID_W
    nk = NA_KROWS * GRID_W
    lane = lax.broadcasted_iota(jnp.int32, (nq, LANES), 1)
    low = lane < HEAD_W

    def scores(p):
        cols = slice(p * LANES, (p + 1) * LANES)
        qp = q_ref[0, :, cols]
        zero = jnp.zeros_like(qp)
        q2 = jnp.concatenate([jnp.where(low, qp, zero), jnp.where(low, zero, qp)], axis=0)
        s_loc = lax.dot_general(q2, k_ref[0, pl.ds(base, nk), cols], NT_DIMS, preferred_element_type=F32)
        s_ctx = lax.dot_general(q2, kc_ref[0, :, cols], NT_DIMS, preferred_element_type=F32)
        return s_loc, s_ctx

    def consume(p, s_loc, s_ctx):
        vcols = slice(2 * p * LANES, (2 * p + 2) * LANES)
        s_loc = s_loc + bias_ref[0, p]
        m = jnp.maximum(jnp.max(s_loc, axis=-1, keepdims=True), jnp.max(s_ctx, axis=-1, keepdims=True))
        acc = (_dotf(jnp.exp2(s_loc - m).astype(BF16), v_ref[0, pl.ds(base, nk), vcols])
               + _dotf(jnp.exp2(s_ctx - m).astype(BF16), vc_ref[0, :, vcols]))
        even, odd = acc[0:nq], acc[nq:2 * nq]
        num = jnp.where(low, even[:, 0:LANES], odd[:, 0:LANES])
        den = jnp.where(low, even[:, LANES:2 * LANES], odd[:, LANES:2 * LANES])
        o_ref[0, :, p * LANES:(p + 1) * LANES] = (num / den).astype(o_ref.dtype)

    pending = []
    for p in range(NA_HEADS // 2):
        pending.append((p,) + scores(p))
        if len(pending) > NA_LOOKAHEAD:
            consume(*pending.pop(0))
    for item in pending:
        consume(*item)


def _na_bias_table(rpb, rows):
    n_blocks = rows // NA_QROWS
    n_h = rpb.shape[0]
    c = np.arange(GRID_W)[:, None]
    kc = np.arange(GRID_W)[None, :]
    cs = np.clip(c - NA_WIN_W // 2, 0, GRID_W - NA_WIN_W)
    col_ok = (kc >= cs) & (kc < cs + NA_WIN_W)
    dc = np.clip(kc - c + (NA_WIN_W - 1), 0, 2 * NA_WIN_W - 2)
    pick = (np.arange(2 * NA_WIN_W - 1)[:, None, None] == dc[None]).astype(np.float32)
    by_col = jnp.einsum("hdm,mck->hcdk", rpb.astype(F32) * LOG2E, jnp.asarray(pick),
                        precision=lax.Precision.HIGHEST)
    variants = []
    for r0 in (0, NA_QROWS * (n_blocks // 2), rows - NA_QROWS):
        base = int(np.clip(r0 - NA_WIN_H // 2, 0, rows - NA_KROWS))
        r = r0 + np.arange(NA_QROWS)[:, None]
        kr = base + np.arange(NA_KROWS)[None, :]
        rs = np.clip(r - NA_WIN_H // 2, 0, rows - NA_WIN_H)
        row_ok = (kr >= rs) & (kr < rs + NA_WIN_H)
        dr = np.clip(kr - r + (NA_WIN_H - 1), 0, 2 * NA_WIN_H - 2)
        full = jnp.stack([by_col[:, :, dr[i], :] for i in range(NA_QROWS)], 1)
        valid = row_ok[:, None, :, None] & col_ok[None, :, None, :]
        full = jnp.where(valid[None], full, MASK_VALUE)
        variants.append(full.reshape(n_h, NA_QROWS * GRID_W, NA_KROWS * GRID_W))
    return jnp.stack(variants, 0).reshape(3, n_h // 2, 2 * NA_QROWS * GRID_W, NA_KROWS * GRID_W)


def _na_call(q, k, v, kc, vc, bias):
    b, t, w = q.shape
    n_blocks = t // (NA_QROWS * GRID_W)
    nq = NA_QROWS * GRID_W
    tc = kc.shape[1]

    def bias_map(i, r):
        return (jnp.where(r == 0, 0, jnp.where(r == n_blocks - 1, 2, 1)), 0, 0, 0)

    return pl.pallas_call(
        _na_kernel,
        grid=(b, n_blocks),
        in_specs=[pl.BlockSpec((1, nq, w), lambda i, r: (i, r, 0)),
                  pl.BlockSpec((1, t, w), lambda i, r: (i, 0, 0)),
                  pl.BlockSpec((1, t, 2 * w), lambda i, r: (i, 0, 0)),
                  pl.BlockSpec((1, tc, w), lambda i, r: (i, 0, 0)),
                  pl.BlockSpec((1, tc, 2 * w), lambda i, r: (i, 0, 0)),
                  pl.BlockSpec((1, NA_HEADS // 2, 2 * nq, NA_KROWS * GRID_W), bias_map)],
        out_specs=pl.BlockSpec((1, nq, w), lambda i, r: (i, r, 0)),
        out_shape=jax.ShapeDtypeStruct((b, t, w), BF16),
        compiler_params=_params(2),
        name="na_attn",
    )(q, k, v, kc, vc, bias)


GDN_MAX_BLOCK = 16 * GDN_CHUNK
GDN_GROUP_CHUNKS = 4
GDN_PAIRS = GDN_HEADS // 2
TN_DIMS = (((0,), (0,)), ((), ()))


def _split3(x):
    hi = x.astype(BF16)
    r = x - hi.astype(F32)
    mid = r.astype(BF16)
    lo = (r - mid.astype(F32)).astype(BF16)
    return hi, mid, lo


def _bd(x):
    lane = lax.broadcasted_iota(jnp.int32, x.shape, 1)
    z = jnp.zeros_like(x)
    return jnp.concatenate([jnp.where(lane < HEAD_W, x, z), jnp.where(lane >= HEAD_W, x, z)], axis=0)


def _dotb(a, b):
    return jnp.dot(a.astype(BF16), b.astype(BF16), preferred_element_type=F32)


def _mm_pair(x, y):
    return _dotb(x, _bd(y))


def _softplus(x):
    return jnp.maximum(x, 0.0) + jnp.log(1.0 + jnp.exp(-jnp.abs(x)))


def _gdn_kernel(qkv_ref, misc_ref, alog_ref, dtb_ref, eg_ref, eb_ref, tri3_ref, s0_ref, o_ref, sfin_ref, s_ref, *,
                rev, n_blk):
    step = pl.program_id(1)
    c_len = GDN_CHUNK
    hw = GDN_HEADS * HEAD_W

    @pl.when(step == 0)
    def _():
        s_ref[...] = s0_ref[0]

    qn = qkv_ref[0, :, 0:hw].astype(F32)
    kn = qkv_ref[0, :, hw:2 * hw].astype(F32)
    v = qkv_ref[0, :, 2 * hw:3 * hw].astype(F32)

    misc = misc_ref[0]
    g_all = -jnp.exp(alog_ref[...]) * _softplus(misc + dtb_ref[...])
    b_all = jax.nn.sigmoid(misc)
    gx = jnp.dot(jnp.concatenate(_split3(g_all), axis=1), eg_ref[...], preferred_element_type=F32)
    bx = jnp.dot(jnp.concatenate(_split3(b_all), axis=1), eb_ref[...], preferred_element_type=F32)

    row = lax.broadcasted_iota(jnp.int32, (c_len, LANES), 0)
    colp = lax.broadcasted_iota(jnp.int32, (c_len, LANES), 1) & (HEAD_W - 1)
    tri = (row <= colp) if rev else (row >= colp)
    strict = (row < colp) if rev else (row > colp)
    eye_f = (row == colp).astype(F32)
    row_w = lax.broadcasted_iota(jnp.int32, (c_len, hw), 0)
    col_w = lax.broadcasted_iota(jnp.int32, (c_len, hw), 1) & (HEAD_W - 1)
    eye_w = row_w == col_w
    bd_row = lax.broadcasted_iota(jnp.int32, (LANES, LANES), 0)
    bd_col = lax.broadcasted_iota(jnp.int32, (LANES, LANES), 1)
    bd_mask = (bd_row < HEAD_W) == (bd_col < HEAD_W)
    ones8 = jnp.ones((8, 3 * c_len), BF16)

    n_chunks = qkv_ref.shape[1] // c_len
    pairs = range(GDN_PAIRS)
    sl = [slice(p * LANES, (p + 1) * LANES) for p in pairs]
    rows_of = lambda c: slice(c * c_len, (c + 1) * c_len)
    gcs, kbs, aqs, grs, gtot, vb, kbg, qd, kd, ms, qks, xs, inner, uws = ({} for _ in range(14))
    state = [s_ref[p] for p in pairs]

    def prep_stages(chunks):
        keys = [(c, p) for c in chunks for p in pairs]

        def decay_sums():
            for c in chunks:
                gcs[c] = jnp.dot(tri3_ref[...], jnp.concatenate(_split3(gx[rows_of(c)]), axis=0),
                                 preferred_element_type=F32)
                kbs[c] = kn[rows_of(c)] * bx[rows_of(c)]

        def gram():
            for c, p in keys:
                r = rows_of(c)
                lhs = jnp.concatenate([kbs[c][:, sl[p]], qn[r][:, sl[p]]], axis=0)
                aqs[c, p] = lax.dot_general(lhs.astype(BF16), _bd(kn[r][:, sl[p]]).astype(BF16), NT_DIMS,
                                            preferred_element_type=F32)

        def decay_rows():
            for c in chunks:
                grs[c] = jnp.dot(ones8, jnp.concatenate(_split3(jnp.where(eye_w, gcs[c], 0.0)), axis=0),
                                 preferred_element_type=F32)[0:1]

        def masks():
            for c in chunks:
                r, gc = rows_of(c), gcs[c]
                gtot[c] = gc[0:1] if rev else gc[c_len - 1:c_len]
                egc = jnp.exp(gc)
                vb[c] = v[r] * bx[r]
                kbg[c] = kbs[c] * egc
                qd[c] = qn[r] * egc
                kd[c] = kn[r] * jnp.exp(gtot[c] - gc)
                for p in pairs:
                    diff = gc[:, sl[p]] - grs[c][:, sl[p]]
                    decay = jnp.where(tri, jnp.exp(jnp.where(tri, diff, 0.0)), 0.0)
                    ms[c, p] = jnp.where(strict, aqs[c, p][0:c_len] * decay, 0.0)
                    qks[c, p] = aqs[c, p][c_len:2 * c_len] * decay

        def join_mask(lvl):
            rblk, cblk = row >> lvl, colp >> lvl
            sib = (cblk == rblk + 1) if rev else (cblk == rblk - 1)
            return sib & ((rblk & 1) == (0 if rev else 1))

        def level0():
            for key in keys:
                xs[key] = eye_f - jnp.where(join_mask(0), ms[key], 0.0)

        def first_product(lvl):
            def run():
                for key in keys:
                    inner[key] = _mm_pair(jnp.where(join_mask(lvl), ms[key], 0.0), xs[key])
            return run

        def second_product():
            for key in keys:
                xs[key] = xs[key] - _mm_pair(xs[key], inner[key])

        def apply_inverse():
            for c, p in keys:
                rhs = jnp.concatenate([_bd(vb[c][:, sl[p]]), _bd(kbg[c][:, sl[p]])], axis=1)
                uws[c, p] = _dotb(xs[c, p], rhs)

        stages = [decay_sums, gram, decay_rows, masks, level0]
        for lvl in range(1, 6):
            stages += [first_product(lvl), second_product]
        return stages + [apply_inverse]

    def scan_stages(chunks):
        v_new = {}
        wq = {}

        def read_state(c):
            def run():
                for p in pairs:
                    wq[p] = _dotb(jnp.concatenate([uws[c, p][:, LANES:2 * LANES], qd[c][:, sl[p]]], axis=0), state[p])
                    v_new[p] = uws[c, p][:, 0:LANES] - wq[p][0:c_len]
            return run

        def write_state(c):
            def run():
                upd = [lax.dot_general(kd[c][:, sl[p]].astype(BF16), v_new[p].astype(BF16), TN_DIMS,
                                       preferred_element_type=F32) for p in pairs]
                for p in pairs:
                    o_ref[0, rows_of(c), sl[p]] = wq[p][c_len:2 * c_len] + _dotb(qks[c, p], _bd(v_new[p]))
                    state[p] = state[p] * jnp.exp(gtot[c][:, sl[p]]) + jnp.where(bd_mask, upd[p], 0.0)
            return run

        stages = []
        for c in chunks:
            stages += [read_state(c), write_state(c)]
        return stages

    order = list(range(n_chunks - 1, -1, -1) if rev else range(n_chunks))
    halves = [order[i:i + GDN_GROUP_CHUNKS] for i in range(0, n_chunks, GDN_GROUP_CHUNKS)]
    for stage in prep_stages(halves[0]):
        stage()
    for half, nxt in zip(halves, halves[1:] + [None]):
        scan = scan_stages(half)
        prep = prep_stages(nxt) if nxt else []
        done = 0
        for i, stage in enumerate(prep):
            stage()
            while done < len(scan) and done * len(prep) < (i + 1) * len(scan):
                scan[done]()
                done += 1
        for stage in scan[done:]:
            stage()
    for p in range(GDN_PAIRS):
        s_ref[p] = state[p]

    @pl.when(step == n_blk - 1)
    def _():
        for p in range(GDN_PAIRS):
            sfin_ref[0, p] = state[p]


def _gdn_consts(a_log, dt_bias, d, rev):
    pad = jnp.zeros((LANES - 2 * GDN_HEADS,), F32)
    alog = jnp.concatenate([a_log.reshape(-1).astype(F32), pad]).reshape(1, LANES)
    dtb = jnp.concatenate([dt_bias.reshape(-1).astype(F32), pad]).reshape(1, LANES)
    hw = GDN_HEADS * HEAD_W
    head_of_lane = np.arange(hw) // HEAD_W
    src = np.arange(LANES)[:, None]
    eg = (src == d * GDN_HEADS + head_of_lane[None, :]).astype(np.float32)
    eb = (src == 2 * GDN_HEADS + d * GDN_HEADS + head_of_lane[None, :]).astype(np.float32)
    i = np.arange(GDN_CHUNK)
    tri = (i[:, None] <= i[None, :]) if rev else (i[:, None] >= i[None, :])
    as_bf = lambda a, reps, ax: jnp.asarray(np.concatenate([a] * reps, axis=ax), BF16)
    return alog, dtb, as_bf(eg, 3, 0), as_bf(eb, 3, 0), as_bf(tri.astype(np.float32), 3, 1)


def _gdn_call(qkv, misc, consts, s0, rev, name):
    b, t, w = qkv.shape
    blk = min(t, GDN_MAX_BLOCK)
    assert t % blk == 0 and blk % GDN_CHUNK == 0
    n_blk = t // blk
    hw = GDN_HEADS * HEAD_W
    blk_of = (lambda s: n_blk - 1 - s) if rev else (lambda s: s)
    const = lambda a: pl.BlockSpec(a.shape, lambda i, s: (0,) * a.ndim)
    s_spec = pl.BlockSpec((1, GDN_PAIRS, LANES, LANES), lambda i, s: (i, 0, 0, 0))
    return pl.pallas_call(
        functools.partial(_gdn_kernel, rev=rev, n_blk=n_blk),
        grid=(b, n_blk),
        in_specs=[pl.BlockSpec((1, blk, w), lambda i, s: (i, blk_of(s), 0)),
                  pl.BlockSpec((1, blk, LANES), lambda i, s: (i, blk_of(s), 0))]
                 + [const(a) for a in consts] + [s_spec],
        out_specs=[pl.BlockSpec((1, blk, hw), lambda i, s: (i, blk_of(s), 0)), s_spec],
        out_shape=[jax.ShapeDtypeStruct((b, t, hw), F32),
                   jax.ShapeDtypeStruct((b, GDN_PAIRS, LANES, LANES), F32)],
        scratch_shapes=[pltpu.VMEM((GDN_PAIRS, LANES, LANES), F32)],
        compiler_params=_params(2),
        name=name,
    )(qkv, misc, *consts, s0)


MIX_SUB_ROWS = 256


def _mix_ffn_kernel(*refs, ff_chunks, gdn_inputs, n_sub):
    tm = refs[0].shape[1]
    rows = [slice(i * (tm // n_sub), (i + 1) * (tm // n_sub)) for i in range(n_sub)]
    if gdn_inputs:
        (h_ref, of_ref, ob_ref, gate_ref, att_ref, on_ref, ones_ref, gtm_ref, shf_ref, scf_ref, gtf_ref, gpm_ref,
         gpf_ref, gqf_ref, wo_ref, wg_ref, wu_ref, wd_ref, o_ref) = refs
        mixes = []
        for r in rows:
            o = of_ref[0, r, :] + ob_ref[0, r, :]
            gate = gate_ref[0, r, :].astype(F32)
            a = o * lax.rsqrt(_seg_mean_sq(o, ones_ref) + NORM_EPS) * on_ref[...] * (gate * jax.nn.sigmoid(gate))
            mixes.append(jnp.concatenate([a.astype(BF16), att_ref[0, r, :]], axis=1))
    else:
        (h_ref, a_ref, b_ref, gtm_ref, shf_ref, scf_ref, gtf_ref, gpm_ref, gpf_ref, gqf_ref, wo_ref, wg_ref,
         wu_ref, wd_ref, o_ref) = refs
        mixes = [jnp.concatenate([a_ref[0, r, :], b_ref[0, r, :]], axis=1) for r in rows]
    ys = [_dotf(mix, wo_ref[...]) for mix in mixes]
    h1s = [h_ref[0, r, :] + gtm_ref[0] * (_rms(y) * gpm_ref[...]) for r, y in zip(rows, ys)]
    us = [(_rms(h1) * gpf_ref[...] * (1.0 + scf_ref[0]) + shf_ref[0]).astype(BF16) for h1 in h1s]
    acts = []
    for o, n in ff_chunks:
        for u in us:
            gg = _dotf(u, wg_ref[:, o:o + n])
            uu = _dotf(u, wu_ref[:, o:o + n])
            acts.append((gg * jax.nn.sigmoid(gg) * uu).astype(BF16))
    fs = [None] * n_sub
    for ci, (o, n) in enumerate(ff_chunks):
        for i in range(n_sub):
            part = _dotf(acts[ci * n_sub + i], wd_ref[o:o + n, :])
            fs[i] = part if fs[i] is None else fs[i] + part
    for r, h1, f in zip(rows, h1s, fs):
        o_ref[0, r, :] = h1 + gtf_ref[0] * (_rms(f) * gqf_ref[...])


def _mix_ffn_call(h, mix_inputs, out_norm, gt_m, sh_f, sc_f, gt_f, g_post_mix, g_pre_ffn, g_post_ffn, wo, wg, wu,
                  wd, tm, name):
    b, t, d = h.shape
    dm = wo.shape[0]
    ff = wg.shape[1]
    gdn_inputs = out_norm is not None
    half = pl.cdiv(ff // 2, MXU_TILE) * MXU_TILE
    assert 0 < half < ff and ff % LANES == 0
    ff_chunks = ((0, half), (half, ff - half))
    tok = lambda n: pl.BlockSpec((1, tm, n), lambda i, j: (i, j, 0))
    per_b = pl.BlockSpec((1, 1, d), lambda i, j: (i, 0, 0))
    vec = pl.BlockSpec((1, d), lambda i, j: (0, 0))
    const = lambda shape: pl.BlockSpec(shape, lambda i, j: (0, 0), pipeline_mode=pl.Buffered(1))
    r3 = lambda a: a.reshape(b, 1, d)
    r2 = lambda a: a.reshape(1, d).astype(F32)
    mix_specs = [tok(m.shape[-1]) for m in mix_inputs]
    mix_args = list(mix_inputs)
    if gdn_inputs:
        hw = GDN_HEADS * GDN_DV
        on = jnp.tile(out_norm.astype(F32), GDN_HEADS).reshape(1, hw)
        ones = _seg_ones(hw)
        mix_specs += [pl.BlockSpec(on.shape, lambda i, j: (0, 0)), pl.BlockSpec(ones.shape, lambda i, j: (0, 0))]
        mix_args += [on, ones]
    return pl.pallas_call(
        functools.partial(_mix_ffn_kernel, ff_chunks=ff_chunks, gdn_inputs=gdn_inputs,
                          n_sub=tm // MIX_SUB_ROWS),
        grid=(b, t // tm),
        in_specs=[tok(d)] + mix_specs + [per_b, per_b, per_b, per_b, vec, vec, vec,
                                         const((dm, d)), const((d, ff)), const((d, ff)), const((ff, d))],
        out_specs=tok(d),
        out_shape=jax.ShapeDtypeStruct((b, t, d), F32),
        compiler_params=_params(2),
        name=name,
    )(h, *mix_args, r3(gt_m), r3(sh_f), r3(sc_f), r3(gt_f), r2(g_post_mix), r2(g_pre_ffn), r2(g_post_ffn),
      wo, wg, wu, wd)


def _rope_tables(n_tokens, rot_dim):
    t = jnp.arange(n_tokens, dtype=jnp.int32)
    row = (t // GRID_W).astype(F32)
    col = (t % GRID_W).astype(F32)
    n_freq = rot_dim // 4
    inv_freq = ROPE_THETA ** (-jnp.arange(n_freq, dtype=F32) / n_freq)
    ang = jnp.concatenate([row[:, None] * inv_freq, col[:, None] * inv_freq], -1)
    cos = jnp.concatenate([jnp.cos(ang), jnp.cos(ang)], -1)
    sin = jnp.concatenate([-jnp.sin(ang), jnp.sin(ang)], -1)
    if rot_dim == HEAD_W:
        return jnp.tile(cos, (1, LANES // rot_dim)), jnp.tile(sin, (1, LANES // rot_dim))
    pad = LANES - rot_dim
    return (jnp.concatenate([cos, jnp.ones((n_tokens, pad), F32)], -1),
            jnp.concatenate([sin, jnp.zeros((n_tokens, pad), F32)], -1))


def _no_rope_tables(n_tokens):
    return jnp.ones((n_tokens, LANES), F32), jnp.zeros((n_tokens, LANES), F32)


def _gdn_bidirectional(lat_qkv, lat_misc, ctx_qkv, ctx_misc, a_log, dt_bias):
    bsz = lat_qkv.shape[0]
    s_zero = jnp.zeros((bsz, GDN_PAIRS, LANES, LANES), F32)
    lat, ctx = [], []
    for d, rev in ((0, False), (1, True)):
        consts = _gdn_consts(a_log, dt_bias, d, rev)
        o_c, s_c = _gdn_call(ctx_qkv, ctx_misc, consts, s_zero, rev, "gdn_ctx_%d" % d)
        o_l, _ = _gdn_call(lat_qkv, lat_misc, consts, s_c, rev, "gdn_lat_%d" % d)
        lat.append(o_l)
        ctx.append(o_c)
    return lat, ctx


LOG2E = 1.4426950408889634
ATTN_TQ = 1024
ATTN_LOOKAHEAD = 2


def _attn_chunks(l_len, s_len, size):
    return [(0, 0, l_len)] + [(1, o, size) for o in range(0, s_len, size)]


def _even_mixer(h, hc, mods, mods_c, g_pre, w_in, conv_w, a_log, dt_bias, q_norm, w_q_up, kv_norm, w_kv_up):
    s_len, l_len = h.shape[1], hc.shape[1]
    weights = _even_weights(w_in, q_norm, w_q_up, kv_norm, w_kv_up)
    qkv_l, gate_l, misc_l, q_l, k_l, v_l = _in_even_call(h, g_pre, *mods, weights, conv_w,
                                                         *_rope_tables(s_len, MLA_ROPE), 512, "in_even_lat")
    qkv_c, gate_c, misc_c, q_c, k_c, v_c = _in_even_call(hc, g_pre, *mods_c, weights, conv_w,
                                                         *_no_rope_tables(l_len), l_len, "in_even_ctx")
    o_lat, o_ctx = _gdn_bidirectional(qkv_l, misc_l, qkv_c, misc_c, a_log, dt_bias)
    b_lat = _attn_call(q_l, [(k_c, v_c), (k_l, v_l)], MLA_QC, MLA_PAIRS, MLA_PAIRS, _attn_chunks(l_len, s_len, 512),
                       ATTN_TQ, False, "mla_attn_lat")
    b_ctx = _attn_call(q_c, [(k_c, v_c)], MLA_QC, MLA_PAIRS, MLA_PAIRS, [(0, 0, l_len)], l_len, False,
                       "mla_attn_ctx")
    return (o_lat[0], o_lat[1], gate_l, b_lat), (o_ctx[0], o_ctx[1], gate_c, b_ctx)


def _odd_mixer_last(h, hc, mods, mods_c, g_pre, w_in, rpb, q_norm, k_norm):
    s_len, l_len = h.shape[1], hc.shape[1]
    w = w_in.astype(BF16)
    q_gain = (jnp.tile(q_norm.astype(F32), GQA_HEADS) * (GQA_DIM ** -0.5 * LOG2E)).reshape(1, -1)
    k_gain = jnp.tile(k_norm.astype(F32), GQA_KV_HEADS).reshape(1, -1)
    naq_l, nak_l, nav_l, gq_l, gk_l, gv_l = _in_odd_call(h, g_pre, *mods, w, q_gain, k_gain,
                                                         *_rope_tables(s_len, GQA_DIM), 512, "in_odd_lat")
    _, nak_c, nav_c, _, gk_c, gv_c = _in_odd_call(hc, g_pre, *mods_c, w, q_gain, k_gain, *_no_rope_tables(l_len),
                                                  l_len, "in_odd_ctx")
    c_lat = _na_call(naq_l, nak_l, nav_l, nak_c, nav_c, _na_bias_table(rpb, s_len // GRID_W))
    d_lat = _attn_call(gq_l, [(gk_c, gv_c), (gk_l, gv_l)], LANES, GQA_HEADS // 2, GQA_KV_HEADS,
                       _attn_chunks(l_len, s_len, 256), ATTN_TQ, True, "gqa_attn")
    return c_lat, d_lat


def kernel(x, c, ctx, c_ctx, w_mod, b_mod, g_pre_mix, g_post_mix, g_pre_ffn, g_post_ffn, w_ffn_gate, w_ffn_up,
           w_ffn_down, w_in_even, w_out_even, gdn_conv, gdn_a_log, gdn_dt_bias, gdn_out_norm, mla_q_norm,
           mla_w_q_up, mla_kv_norm, mla_w_kv_up, w_in_odd, w_out_odd, na_rpb, gqa_q_norm, gqa_k_norm):
    bsz, s_len, d = x.shape
    depth = w_mod.shape[0]
    assert depth == 2, "layer 0 = even mixer with context update, layer 1 = odd mixer (last)"
    c_rows = jnp.concatenate([c, c_ctx[None, :], jnp.zeros((16 - bsz - 1, d), F32)], 0)
    h, hc = x, ctx
    for i in range(depth):
        mod_all = _mod_call(c_rows, w_mod[i].astype(BF16), b_mod[i])
        mod = jnp.split(mod_all[:bsz], 6, axis=-1)
        mod_c = [jnp.broadcast_to(m, (bsz, d)) for m in jnp.split(mod_all[bsz:bsz + 1], 6, axis=-1)]
        sh_m, sc_m, gt_m, sh_f, sc_f, gt_f = mod
        csh_m, csc_m, cgt_m, csh_f, csc_f, cgt_f = mod_c
        wg, wu, wd = w_ffn_gate[i].astype(BF16), w_ffn_up[i].astype(BF16), w_ffn_down[i].astype(BF16)
        if i == 0:
            mix, mix_c = _even_mixer(h, hc, (sh_m, sc_m), (csh_m, csc_m), g_pre_mix[i], w_in_even[0], gdn_conv[0],
                                     gdn_a_log[0], gdn_dt_bias[0], mla_q_norm[0], mla_w_q_up[0], mla_kv_norm[0],
                                     mla_w_kv_up[0])
            wo, out_norm = w_out_even[0].astype(BF16), gdn_out_norm[0]
            hc = _mix_ffn_call(hc, mix_c, out_norm, cgt_m, csh_f, csc_f, cgt_f, g_post_mix[i], g_pre_ffn[i],
                               g_post_ffn[i], wo, wg, wu, wd, 256, "mix_ffn_ctx")
        else:
            mix = _odd_mixer_last(h, hc, (sh_m, sc_m), (csh_m, csc_m), g_pre_mix[i], w_in_odd[0], na_rpb[0],
                                  gqa_q_norm[0], gqa_k_norm[0])
            wo, out_norm = w_out_odd[0].astype(BF16), None
        h = _mix_ffn_call(h, mix, out_norm, gt_m, sh_f, sc_f, gt_f, g_post_mix[i], g_pre_ffn[i], g_post_ffn[i],
                          wo, wg, wu, wd, 2 * MIX_SUB_ROWS, "mix_ffn_lat%d" % i)
    return h
```

```python
import functools
from typing import NamedTuple, Optional

import numpy as np
import jax
import jax.numpy as jnp
from jax import lax
from jax.experimental import pallas as pl
from jax.experimental.pallas import tpu as pltpu

F32 = jnp.float32
BF16 = jnp.bfloat16

GRID_W = 64
NORM_EPS = 1e-6
ROPE_THETA = 10000.0

GDN_HEADS = 8
GDN_DK = 64
GDN_DV = 64
GDN_CHUNK = 64
GDN_QK_W = GDN_HEADS * GDN_DK
GDN_V_W = GDN_HEADS * GDN_DV

MLA_HEADS = 8
MLA_NOPE = 64
MLA_ROPE = 32
MLA_V = 64
MLA_Q_LORA = 256
MLA_KV_LORA = 128

NA_HEADS = 8
NA_DIM = 64
NA_WIN_H = 8
NA_WIN_W = 16
NA_W = NA_HEADS * NA_DIM

GQA_HEADS = 8
GQA_KV_HEADS = 2
GQA_DIM = 64

LANES = 128
MXU_TILE = 256
HEAD_W = 64
MASK_VALUE = -1e30
VMEM_LIMIT = 56 << 20

NT_DIMS = (((1,), (1,)), ((), ()))


def _params(n_grid, vmem=VMEM_LIMIT):
    return pltpu.CompilerParams(dimension_semantics=("arbitrary",) * n_grid, vmem_limit_bytes=vmem)


def _rms(x):
    return x * lax.rsqrt(jnp.mean(x * x, axis=-1, keepdims=True) + NORM_EPS)


def _mod_kernel(c_ref, w_ref, b_ref, o_ref):
    c = c_ref[...]
    a = (c * jax.nn.sigmoid(c)).astype(BF16)
    o_ref[...] = jnp.dot(a, w_ref[...], preferred_element_type=F32) + b_ref[...]


def _mod_call(c_rows, w, b):
    r, d = c_rows.shape
    n = w.shape[1]
    tn = 1024
    return pl.pallas_call(
        _mod_kernel,
        grid=(n // tn,),
        in_specs=[pl.BlockSpec((r, d), lambda j: (0, 0)),
                  pl.BlockSpec((d, tn), lambda j: (0, j)),
                  pl.BlockSpec((1, tn), lambda j: (0, j))],
        out_specs=pl.BlockSpec((r, tn), lambda j: (0, j)),
        out_shape=jax.ShapeDtypeStruct((r, n), F32),
        compiler_params=_params(1),
        name="mod",
    )(c_rows, w, b.reshape(1, n))


class _Mod(NamedTuple):
    table: jax.Array
    comp: int
    row: Optional[int]


N_MOD = 6


def _mod_operand(m, d):
    if m.row is None:
        return m.table, pl.BlockSpec((1, 1, d), lambda i, j: (i * N_MOD + m.comp, 0, 0))
    return m.table, pl.BlockSpec((1, 1, d), lambda i, j: (m.row * N_MOD + m.comp, 0, 0))


def _norm_mod(x_ref, g_ref, sh_ref, sc_ref):
    return (_rms(x_ref[0]) * g_ref[...] * (1.0 + sc_ref[0]) + sh_ref[0]).astype(BF16)


def _dotf(a, b):
    return jnp.dot(a, b, preferred_element_type=F32)


def _seg_mean_sq(x, ones_ref):
    x2 = x * x
    hi = x2.astype(BF16)
    mid = (x2 - hi.astype(F32)).astype(BF16)
    return _dotf(jnp.concatenate([hi, mid], axis=1), ones_ref[...]) * (1.0 / HEAD_W)


def _rotate(t, cos, sin_signed, half):
    lane = lax.broadcasted_iota(jnp.int32, t.shape, 1)
    n = t.shape[1]
    partner = jnp.where((lane & (2 * half - 1)) < half, pltpu.roll(t, n - half, 1), pltpu.roll(t, half, 1))
    return t * cos + partner * sin_signed


EVEN_QKV_W = 2 * GDN_QK_W + GDN_V_W
EVEN_GROUPS = (EVEN_QKV_W, GDN_V_W, MLA_Q_LORA, MLA_KV_LORA, LANES, LANES)
MLA_QC = 2 * LANES
MLA_PAIRS = MLA_HEADS // 2


CONV_HALO = 16


def _in_even_kernel(x_ref, xp_ref, xn_ref, g_ref, sh_ref, sc_ref, w_ref, qg_ref, wq_ref, kvg_ref, wk_ref, wv_ref,
                    vone_ref, cos_ref, sin_ref, cw_ref, ones_ref, qkv_ref, gate_ref, misc_ref, q_ref, k_ref, v_ref,
                    zs_ref, *, n_tap):
    j = pl.program_id(1)
    tm = x_ref.shape[1]
    u = _norm_mod(x_ref, g_ref, sh_ref, sc_ref)
    offs = np.cumsum((0,) + EVEN_GROUPS)
    grp = lambda i: _dotf(u, w_ref[:, int(offs[i]):int(offs[i + 1])])
    u_ext = jnp.concatenate([_norm_mod(xp_ref, g_ref, sh_ref, sc_ref), u, _norm_mod(xn_ref, g_ref, sh_ref, sc_ref)],
                            axis=0)
    z = _dotf(u_ext, w_ref[:, 0:EVEN_QKV_W])
    zs_ref[0:CONV_HALO, :] = z[0:CONV_HALO] * (j > 0).astype(F32)
    zs_ref[CONV_HALO:CONV_HALO + tm, :] = z[CONV_HALO:CONV_HALO + tm]
    zs_ref[CONV_HALO + tm:, :] = z[CONV_HALO + tm:] * (j < pl.num_programs(1) - 1).astype(F32)
    q_down, kv_down, pe_raw = grp(2), grp(3), grp(5)
    gate_ref[0] = grp(1).astype(BF16)
    misc_ref[0] = grp(4)
    cos, sin = cos_ref[...], sin_ref[...]
    qn = (_rms(q_down) * qg_ref[...]).astype(BF16)
    kvn = (_rms(kv_down) * kvg_ref[...]).astype(BF16)
    for h in range(MLA_HEADS):
        blk = _dotf(qn, wq_ref[:, h * MLA_QC:(h + 1) * MLA_QC])
        q_ref[0, :, h * MLA_QC:h * MLA_QC + LANES] = blk[:, 0:LANES].astype(BF16)
        q_ref[0, :, h * MLA_QC + LANES:(h + 1) * MLA_QC] = _rotate(blk[:, LANES:], cos, sin,
                                                                    MLA_ROPE // 2).astype(BF16)
    pe = _rotate(pe_raw, cos, sin, MLA_ROPE // 2).astype(BF16)
    k_nope = _dotf(kvn, wk_ref[...])
    for p in range(MLA_PAIRS):
        k_ref[0, :, p * MLA_QC:p * MLA_QC + LANES] = k_nope[:, p * LANES:(p + 1) * LANES].astype(BF16)
        k_ref[0, :, p * MLA_QC + LANES:(p + 1) * MLA_QC] = pe
    v_ref[0] = (_dotf(kvn, wv_ref[...]) + vone_ref[...]).astype(BF16)
    acc = None
    for t in range(n_tap):
        start = CONV_HALO - n_tap // 2 + t
        term = zs_ref[start:start + tm, :] * cw_ref[t:t + 1, :]
        acc = term if acc is None else acc + term
    y = acc * jax.nn.sigmoid(acc)
    hw = GDN_QK_W
    qc, kc = y[:, 0:hw], y[:, hw:2 * hw]
    q_ss = _seg_mean_sq(qc, ones_ref) * HEAD_W
    k_ss = _seg_mean_sq(kc, ones_ref) * HEAD_W
    qkv_ref[0, :, 0:hw] = (qc * lax.rsqrt(q_ss + NORM_EPS) * (GDN_DK ** -0.5)).astype(BF16)
    qkv_ref[0, :, hw:2 * hw] = (kc * lax.rsqrt(k_ss + NORM_EPS)).astype(BF16)
    qkv_ref[0, :, 2 * hw:] = y[:, 2 * hw:].astype(BF16)


def _even_weights(w_in, q_norm, w_q_up, kv_norm, w_kv_up):
    d = w_in.shape[0]
    o0 = EVEN_QKV_W + GDN_V_W
    n_ab = 4 * GDN_HEADS
    o1 = o0 + n_ab
    o2 = o1 + MLA_Q_LORA
    o3 = o2 + MLA_KV_LORA
    zeros = lambda n, rows=d: jnp.zeros((rows, n), F32)
    w_cat = jnp.concatenate([w_in[:, :o0], w_in[:, o1:o2], w_in[:, o2:o3],
                             w_in[:, o0:o1], zeros(LANES - n_ab),
                             w_in[:, o3:], zeros(LANES - MLA_ROPE)], 1).astype(BF16)
    qh = MLA_NOPE + MLA_ROPE
    cols = []
    for h in range(MLA_HEADS):
        nope = w_q_up[:, h * qh:h * qh + MLA_NOPE]
        rp = w_q_up[:, h * qh + MLA_NOPE:(h + 1) * qh]
        z = zeros(HEAD_W, MLA_Q_LORA)
        cols += ([nope, z] if h % 2 == 0 else [z, nope]) + [rp, zeros(LANES - MLA_ROPE, MLA_Q_LORA)]
    wq = jnp.concatenate(cols, 1).astype(BF16)
    kvh = MLA_NOPE + MLA_V
    wk = jnp.concatenate([w_kv_up[:, h * kvh:h * kvh + MLA_NOPE] for h in range(MLA_HEADS)], 1).astype(BF16)
    vcols, ones = [], []
    for h in range(MLA_HEADS):
        vh = w_kv_up[:, h * kvh + MLA_NOPE:(h + 1) * kvh]
        z = zeros(HEAD_W, MLA_KV_LORA)
        vcols += [vh, z] if h % 2 == 0 else [z, vh]
        ones += [0.0, 1.0] if h % 2 == 0 else [1.0, 0.0]
    wv = jnp.concatenate(vcols, 1).astype(BF16)
    vone = jnp.asarray(np.repeat(np.asarray(ones, np.float32), HEAD_W)[None, :])
    q_gain = (q_norm * ((MLA_NOPE + MLA_ROPE) ** -0.5 * LOG2E)).reshape(1, -1).astype(F32)
    return w_cat, q_gain, wq, kv_norm.reshape(1, -1).astype(F32), wk, wv, vone


def _in_even_call(x, g, shift, scale, weights, conv_w, cos, sin, tm, name):
    b, t, d = x.shape
    w_cat, q_gain, wq, kv_gain, wk, wv, vone = weights
    n_tap = conv_w.shape[0]
    cw = jnp.concatenate([conv_w.astype(F32), jnp.zeros((8 - n_tap, conv_w.shape[1]), F32)], 0)
    ones = _seg_ones(GDN_QK_W)
    per = tm // CONV_HALO
    n_halo = t // CONV_HALO
    const = lambda a: pl.BlockSpec(a.shape, lambda i, j: (0,) * a.ndim)
    (sh_arr, sh_spec), (sc_arr, sc_spec) = _mod_operand(shift, d), _mod_operand(scale, d)
    tab = pl.BlockSpec((tm, LANES), lambda i, j: (j, 0))
    widths = (EVEN_QKV_W, GDN_V_W, LANES, MLA_HEADS * MLA_QC, MLA_PAIRS * MLA_QC, MLA_HEADS * LANES)
    dts = (BF16, BF16, F32, BF16, BF16, BF16)
    g2 = g.reshape(1, d).astype(F32)
    return pl.pallas_call(
        functools.partial(_in_even_kernel, n_tap=n_tap),
        grid=(b, t // tm),
        in_specs=[pl.BlockSpec((1, tm, d), lambda i, j: (i, j, 0)),
                  pl.BlockSpec((1, CONV_HALO, d), lambda i, j: (i, jnp.maximum(j * per - 1, 0), 0)),
                  pl.BlockSpec((1, CONV_HALO, d), lambda i, j: (i, jnp.minimum((j + 1) * per, n_halo - 1), 0)),
                  const(g2), sh_spec, sc_spec, const(w_cat), const(q_gain), const(wq), const(kv_gain), const(wk),
                  const(wv), const(vone), tab, tab, const(cw), const(ones)],
        out_specs=[pl.BlockSpec((1, tm, n), lambda i, j: (i, j, 0)) for n in widths],
        out_shape=[jax.ShapeDtypeStruct((b, t, n), dt) for n, dt in zip(widths, dts)],
        scratch_shapes=[pltpu.VMEM((tm + 2 * CONV_HALO, EVEN_QKV_W), F32)],
        compiler_params=_params(2),
        name=name,
    )(x, x, x, g2, sh_arr, sc_arr, w_cat, q_gain, wq, kv_gain, wk, wv, vone,
      cos, sin, cw, ones)


GQA_KV_W = GQA_KV_HEADS * GQA_DIM
PROJ_SUB_ROWS = 256


def _in_odd_kernel(x_ref, g_ref, sh_ref, sc_ref, w_ref, qg_ref, kg_ref, ones_q_ref, ones_k_ref, cos_ref, sin_ref,
                   naq_ref, nak_ref, nav_ref, gq_ref, gk_ref, gv_ref):
    tm = x_ref.shape[1]
    n_sub = max(tm // PROJ_SUB_ROWS, 1)
    subs = [slice(i * (tm // n_sub), (i + 1) * (tm // n_sub)) for i in range(n_sub)]
    us = [(_rms(x_ref[0, r, :]) * g_ref[...] * (1.0 + sc_ref[0]) + sh_ref[0]).astype(BF16) for r in subs]
    o_q = 3 * NA_W
    o_k = o_q + GQA_HEADS * GQA_DIM
    qs = [_dotf(u, w_ref[:, o_q:o_k]) for u in us]
    ks = [_dotf(u, w_ref[:, o_k:o_k + GQA_KV_W]) for u in us]
    vs = [_dotf(u, w_ref[:, o_k + GQA_KV_W:o_k + 2 * GQA_KV_W]) for u in us]
    low = lax.broadcasted_iota(jnp.int32, (tm // n_sub, LANES), 1) < HEAD_W
    for r, u in zip(subs, us):
        naq_ref[0, r, :] = (_dotf(u, w_ref[:, 0:NA_W]) * (NA_DIM ** -0.5 * LOG2E)).astype(BF16)
        nak_ref[0, r, :] = _dotf(u, w_ref[:, NA_W:2 * NA_W]).astype(BF16)
        na_v = _dotf(u, w_ref[:, 2 * NA_W:3 * NA_W])
        for p in range(NA_HEADS // 2):
            tile = na_v[:, p * LANES:(p + 1) * LANES]
            nav_ref[0, r, 2 * p * LANES:(2 * p + 1) * LANES] = tile.astype(BF16)
            nav_ref[0, r, (2 * p + 1) * LANES:(2 * p + 2) * LANES] = jnp.ones(tile.shape, BF16)
    q_ms = [_seg_mean_sq(q, ones_q_ref) for q in qs]
    k_ms = [_seg_mean_sq(k, ones_k_ref) for k in ks]
    for i, r in enumerate(subs):
        cos, sin = cos_ref[r, :], sin_ref[r, :]
        qn = qs[i] * lax.rsqrt(q_ms[i] + NORM_EPS) * qg_ref[...]
        for p in range(GQA_HEADS // 2):
            cols = slice(p * LANES, (p + 1) * LANES)
            gq_ref[0, r, cols] = _rotate(qn[:, cols], cos, sin, GQA_DIM // 2).astype(BF16)
        kr = _rotate(ks[i] * lax.rsqrt(k_ms[i] + NORM_EPS) * kg_ref[...], cos, sin, GQA_DIM // 2)
        k_sw = pltpu.roll(kr, HEAD_W, 1)
        gk_ref[0, r, 0:LANES] = jnp.where(low, kr, k_sw).astype(BF16)
        gk_ref[0, r, LANES:2 * LANES] = jnp.where(low, k_sw, kr).astype(BF16)
        v = vs[i]
        v_sw = pltpu.roll(v, HEAD_W, 1)
        slots = (jnp.where(low, v, 1.0), jnp.where(low, 1.0, v_sw),
                 jnp.where(low, v_sw, 1.0), jnp.where(low, 1.0, v))
        for j, s in enumerate(slots):
            gv_ref[0, r, j * LANES:(j + 1) * LANES] = s.astype(BF16)


def _seg_ones(width):
    head = np.arange(width) // HEAD_W
    m = (head[:, None] == head[None, :]).astype(np.float32)
    return jnp.asarray(np.concatenate([m, m], 0), BF16)


def _in_odd_call(x, g, shift, scale, w, q_gain, k_gain, cos, sin, tm, name):
    b, t, d = x.shape
    assert GQA_KV_W == LANES
    ones_q = _seg_ones(GQA_HEADS * GQA_DIM)
    ones_k = _seg_ones(GQA_KV_W)
    const = lambda a: pl.BlockSpec(a.shape, lambda i, j: (0,) * a.ndim)
    (sh_arr, sh_spec), (sc_arr, sc_spec) = _mod_operand(shift, d), _mod_operand(scale, d)
    tab = pl.BlockSpec((tm, LANES), lambda i, j: (j, 0))
    widths = (NA_W, NA_W, 2 * NA_W, GQA_HEADS * GQA_DIM, 2 * LANES, 4 * LANES)
    g2 = g.reshape(1, d).astype(F32)
    return pl.pallas_call(
        _in_odd_kernel,
        grid=(b, t // tm),
        in_specs=[pl.BlockSpec((1, tm, d), lambda i, j: (i, j, 0)), const(g2), sh_spec, sc_spec, const(w),
                  const(q_gain), const(k_gain), const(ones_q), const(ones_k), tab, tab],
        out_specs=[pl.BlockSpec((1, tm, n), lambda i, j: (i, j, 0)) for n in widths],
        out_shape=[jax.ShapeDtypeStruct((b, t, n), BF16) for n in widths],
        compiler_params=_params(2),
        name=name,
    )(x, g2, sh_arr, sc_arr, w, q_gain, k_gain, ones_q, ones_k, cos, sin)


def _attn_kernel(*refs, dc, chunks, n_src, masked_q):
    q_ref, kv_refs, o_ref = refs[0], refs[1:1 + 2 * n_src], refs[-1]
    tq = q_ref.shape[1]
    lane = lax.broadcasted_iota(jnp.int32, (tq, LANES), 1)
    qs = []
    for hh in range(2):
        if masked_q:
            qp = q_ref[0]
            qs.append(jnp.where((lane < HEAD_W) if hh == 0 else (lane >= HEAD_W), qp, jnp.zeros_like(qp)))
        else:
            qs.append(q_ref[0, :, hh * dc:(hh + 1) * dc])
    ms, accs = [None, None], [None, None]

    def scores(hh, src, s0, n):
        return lax.dot_general(qs[hh], kv_refs[2 * src][0, s0:s0 + n, :], NT_DIMS, preferred_element_type=F32)

    def consume(s, hh, src, s0, n):
        vv = kv_refs[2 * src + 1][0, s0:s0 + n, hh * LANES:(hh + 1) * LANES]
        mc = jnp.max(s, axis=-1, keepdims=True)
        m_new = mc if ms[hh] is None else jnp.maximum(ms[hh], mc)
        pv = _dotf(jnp.exp2(s - m_new).astype(BF16), vv)
        accs[hh] = pv if ms[hh] is None else jnp.exp2(ms[hh] - m_new) * accs[hh] + pv
        ms[hh] = m_new

    items = [(hh,) + tuple(c) for c in chunks for hh in range(2)]
    pending = []
    for it in items:
        pending.append((scores(*it), it))
        if len(pending) > ATTN_LOOKAHEAD:
            s, it0 = pending.pop(0)
            consume(s, *it0)
    for s, it0 in pending:
        consume(s, *it0)
    low = lane < HEAD_W
    num = jnp.where(low, accs[0], accs[1])
    den = pltpu.roll(jnp.where(low, accs[1], accs[0]), HEAD_W, 1)
    o_ref[0] = (num / den).astype(o_ref.dtype)


def _attn_call(q, kvs, dc, n_pairs, n_groups, chunks, tq, masked_q, name):
    b, t_q, _ = q.shape
    per = n_pairs // n_groups
    q_w = LANES if masked_q else 2 * dc
    in_specs = [pl.BlockSpec((1, tq, q_w), lambda i, p, j: (i, j, p))]
    args = [q]
    for k, v in kvs:
        in_specs += [pl.BlockSpec((1, k.shape[1], dc), lambda i, p, j: (i, 0, p // per)),
                     pl.BlockSpec((1, v.shape[1], 2 * LANES), lambda i, p, j: (i, 0, p // per))]
        args += [k, v]
    return pl.pallas_call(
        functools.partial(_attn_kernel, dc=dc, chunks=tuple(chunks), n_src=len(kvs), masked_q=masked_q),
        grid=(b, n_pairs, t_q // tq),
        in_specs=in_specs,
        out_specs=pl.BlockSpec((1, tq, LANES), lambda i, p, j: (i, j, p)),
        out_shape=jax.ShapeDtypeStruct((b, t_q, n_pairs * LANES), BF16),
        compiler_params=_params(3),
        name=name,
    )(*args)


NA_QROWS = 4
NA_KROWS = 12
NA_LOOKAHEAD = 1


def _na_kernel(q_ref, k_ref, v_ref, kc_ref, vc_ref, bias_ref, o_ref):
    rb = pl.program_id(1)
    n_rows = k_ref.shape[1] // GRID_W
    base = jnp.clip(rb * NA_QROWS - NA_WIN_H // 2, 0, n_rows - NA_KROWS) * GRID_W
    base = pl.multiple_of(base, GRID_W)
    nq = NA_QROWS * GRID_W
    nk = NA_KROWS * GRID_W
    lane = lax.broadcasted_iota(jnp.int32, (nq, LANES), 1)
    low = lane < HEAD_W

    def scores(p):
        cols = slice(p * LANES, (p + 1) * LANES)
        qp = q_ref[0, :, cols]
        zero = jnp.zeros_like(qp)
        q2 = jnp.concatenate([jnp.where(low, qp, zero), jnp.where(low, zero, qp)], axis=0)
        s_loc = lax.dot_general(q2, k_ref[0, pl.ds(base, nk), cols], NT_DIMS, preferred_element_type=F32)
        s_ctx = lax.dot_general(q2, kc_ref[0, :, cols], NT_DIMS, preferred_element_type=F32)
        return s_loc, s_ctx

    def consume(p, s_loc, s_ctx):
        vcols = slice(2 * p * LANES, (2 * p + 2) * LANES)
        s_loc = s_loc + bias_ref[0, p]
        m = jnp.maximum(jnp.max(s_loc, axis=-1, keepdims=True), jnp.max(s_ctx, axis=-1, keepdims=True))
        acc = (_dotf(jnp.exp2(s_loc - m).astype(BF16), v_ref[0, pl.ds(base, nk), vcols])
               + _dotf(jnp.exp2(s_ctx - m).astype(BF16), vc_ref[0, :, vcols]))
        even, odd = acc[0:nq], acc[nq:2 * nq]
        num = jnp.where(low, even[:, 0:LANES], odd[:, 0:LANES])
        den = jnp.where(low, even[:, LANES:2 * LANES], odd[:, LANES:2 * LANES])
        o_ref[0, :, p * LANES:(p + 1) * LANES] = (num / den).astype(o_ref.dtype)

    pending = []
    for p in range(NA_HEADS // 2):
        pending.append((p,) + scores(p))
        if len(pending) > NA_LOOKAHEAD:
            consume(*pending.pop(0))
    for item in pending:
        consume(*item)


def _na_bias_table(rpb, rows):
    n_blocks = rows // NA_QROWS
    n_h = rpb.shape[0]
    c = np.arange(GRID_W)[:, None]
    kc = np.arange(GRID_W)[None, :]
    cs = np.clip(c - NA_WIN_W // 2, 0, GRID_W - NA_WIN_W)
    col_ok = (kc >= cs) & (kc < cs + NA_WIN_W)
    dc = np.clip(kc - c + (NA_WIN_W - 1), 0, 2 * NA_WIN_W - 2)
    pick = (np.arange(2 * NA_WIN_W - 1)[:, None, None] == dc[None]).astype(np.float32)
    by_col = jnp.einsum("hdm,mck->hcdk", rpb.astype(F32) * LOG2E, jnp.asarray(pick),
                        precision=lax.Precision.HIGHEST)
    variants = []
    for r0 in (0, NA_QROWS * (n_blocks // 2), rows - NA_QROWS):
        base = int(np.clip(r0 - NA_WIN_H // 2, 0, rows - NA_KROWS))
        r = r0 + np.arange(NA_QROWS)[:, None]
        kr = base + np.arange(NA_KROWS)[None, :]
        rs = np.clip(r - NA_WIN_H // 2, 0, rows - NA_WIN_H)
        row_ok = (kr >= rs) & (kr < rs + NA_WIN_H)
        dr = np.clip(kr - r + (NA_WIN_H - 1), 0, 2 * NA_WIN_H - 2)
        full = jnp.stack([by_col[:, :, dr[i], :] for i in range(NA_QROWS)], 1)
        valid = row_ok[:, None, :, None] & col_ok[None, :, None, :]
        full = jnp.where(valid[None], full, MASK_VALUE)
        variants.append(full.reshape(n_h, NA_QROWS * GRID_W, NA_KROWS * GRID_W))
    return jnp.stack(variants, 0).reshape(3, n_h // 2, 2 * NA_QROWS * GRID_W, NA_KROWS * GRID_W)


def _na_call(q, k, v, kc, vc, bias):
    b, t, w = q.shape
    n_blocks = t // (NA_QROWS * GRID_W)
    nq = NA_QROWS * GRID_W
    tc = kc.shape[1]

    def bias_map(i, r):
        return (jnp.where(r == 0, 0, jnp.where(r == n_blocks - 1, 2, 1)), 0, 0, 0)

    return pl.pallas_call(
        _na_kernel,
        grid=(b, n_blocks),
        in_specs=[pl.BlockSpec((1, nq, w), lambda i, r: (i, r, 0)),
                  pl.BlockSpec((1, t, w), lambda i, r: (i, 0, 0)),
                  pl.BlockSpec((1, t, 2 * w), lambda i, r: (i, 0, 0)),
                  pl.BlockSpec((1, tc, w), lambda i, r: (i, 0, 0)),
                  pl.BlockSpec((1, tc, 2 * w), lambda i, r: (i, 0, 0)),
                  pl.BlockSpec((1, NA_HEADS // 2, 2 * nq, NA_KROWS * GRID_W), bias_map)],
        out_specs=pl.BlockSpec((1, nq, w), lambda i, r: (i, r, 0)),
        out_shape=jax.ShapeDtypeStruct((b, t, w), BF16),
        compiler_params=_params(2),
        name="na_attn",
    )(q, k, v, kc, vc, bias)


GDN_MAX_BLOCK = 16 * GDN_CHUNK
GDN_GROUP_CHUNKS = 4
GDN_PAIRS = GDN_HEADS // 2
TN_DIMS = (((0,), (0,)), ((), ()))


def _split3(x):
    hi = x.astype(BF16)
    r = x - hi.astype(F32)
    mid = r.astype(BF16)
    lo = (r - mid.astype(F32)).astype(BF16)
    return hi, mid, lo


def _bd(x):
    lane = lax.broadcasted_iota(jnp.int32, x.shape, 1)
    z = jnp.zeros_like(x)
    return jnp.concatenate([jnp.where(lane < HEAD_W, x, z), jnp.where(lane >= HEAD_W, x, z)], axis=0)


def _dotb(a, b):
    return jnp.dot(a.astype(BF16), b.astype(BF16), preferred_element_type=F32)


def _mm_pair(x, y):
    return _dotb(x, _bd(y))


def _softplus(x):
    return jnp.maximum(x, 0.0) + jnp.log(1.0 + jnp.exp(-jnp.abs(x)))


def _gdn_kernel(qkv_ref, misc_ref, alog_ref, dtb_ref, eg_ref, eb_ref, tri3_ref, s0_ref, o_ref, sfin_ref, s_ref, *,
                rev, n_blk):
    step = pl.program_id(1)
    c_len = GDN_CHUNK
    hw = GDN_HEADS * HEAD_W

    @pl.when(step == 0)
    def _():
        s_ref[...] = s0_ref[0]

    qn = qkv_ref[0, :, 0:hw].astype(F32)
    kn = qkv_ref[0, :, hw:2 * hw].astype(F32)
    v = qkv_ref[0, :, 2 * hw:3 * hw].astype(F32)

    misc = misc_ref[0]
    g_all = -jnp.exp(alog_ref[...]) * _softplus(misc + dtb_ref[...])
    b_all = jax.nn.sigmoid(misc)
    gx = jnp.dot(jnp.concatenate(_split3(g_all), axis=1), eg_ref[...], preferred_element_type=F32)
    bx = jnp.dot(jnp.concatenate(_split3(b_all), axis=1), eb_ref[...], preferred_element_type=F32)

    row = lax.broadcasted_iota(jnp.int32, (c_len, LANES), 0)
    colp = lax.broadcasted_iota(jnp.int32, (c_len, LANES), 1) & (HEAD_W - 1)
    tri = (row <= colp) if rev else (row >= colp)
    strict = (row < colp) if rev else (row > colp)
    eye_f = (row == colp).astype(F32)
    row_w = lax.broadcasted_iota(jnp.int32, (c_len, hw), 0)
    col_w = lax.broadcasted_iota(jnp.int32, (c_len, hw), 1) & (HEAD_W - 1)
    eye_w = row_w == col_w
    bd_row = lax.broadcasted_iota(jnp.int32, (LANES, LANES), 0)
    bd_col = lax.broadcasted_iota(jnp.int32, (LANES, LANES), 1)
    bd_mask = (bd_row < HEAD_W) == (bd_col < HEAD_W)
    ones8 = jnp.ones((8, 3 * c_len), BF16)

    n_chunks = qkv_ref.shape[1] // c_len
    pairs = range(GDN_PAIRS)
    sl = [slice(p * LANES, (p + 1) * LANES) for p in pairs]
    rows_of = lambda c: slice(c * c_len, (c + 1) * c_len)
    gcs, kbs, aqs, grs, gtot, vb, kbg, qd, kd, ms, qks, xs, inner, uws = ({} for _ in range(14))
    state = [s_ref[p] for p in pairs]

    def prep_stages(chunks):
        keys = [(c, p) for c in chunks for p in pairs]

        def decay_sums():
            for c in chunks:
                gcs[c] = jnp.dot(tri3_ref[...], jnp.concatenate(_split3(gx[rows_of(c)]), axis=0),
                                 preferred_element_type=F32)
                kbs[c] = kn[rows_of(c)] * bx[rows_of(c)]

        def gram():
            for c, p in keys:
                r = rows_of(c)
                lhs = jnp.concatenate([kbs[c][:, sl[p]], qn[r][:, sl[p]]], axis=0)
                aqs[c, p] = lax.dot_general(lhs.astype(BF16), _bd(kn[r][:, sl[p]]).astype(BF16), NT_DIMS,
                                            preferred_element_type=F32)

        def decay_rows():
            for c in chunks:
                grs[c] = jnp.dot(ones8, jnp.concatenate(_split3(jnp.where(eye_w, gcs[c], 0.0)), axis=0),
                                 preferred_element_type=F32)[0:1]

        def masks():
            for c in chunks:
                r, gc = rows_of(c), gcs[c]
                gtot[c] = gc[0:1] if rev else gc[c_len - 1:c_len]
                egc = jnp.exp(gc)
                vb[c] = v[r] * bx[r]
                kbg[c] = kbs[c] * egc
                qd[c] = qn[r] * egc
                kd[c] = kn[r] * jnp.exp(gtot[c] - gc)
                for p in pairs:
                    diff = gc[:, sl[p]] - grs[c][:, sl[p]]
                    decay = jnp.where(tri, jnp.exp(jnp.where(tri, diff, 0.0)), 0.0)
                    ms[c, p] = jnp.where(strict, aqs[c, p][0:c_len] * decay, 0.0)
                    qks[c, p] = aqs[c, p][c_len:2 * c_len] * decay

        def join_mask(lvl):
            rblk, cblk = row >> lvl, colp >> lvl
            sib = (cblk == rblk + 1) if rev else (cblk == rblk - 1)
            return sib & ((rblk & 1) == (0 if rev else 1))

        def level0():
            for key in keys:
                xs[key] = eye_f - jnp.where(join_mask(0), ms[key], 0.0)

        def first_product(lvl):
            def run():
                for key in keys:
                    inner[key] = _mm_pair(jnp.where(join_mask(lvl), ms[key], 0.0), xs[key])
            return run

        def second_product():
            for key in keys:
                xs[key] = xs[key] - _mm_pair(xs[key], inner[key])

        def apply_inverse():
            for c, p in keys:
                rhs = jnp.concatenate([_bd(vb[c][:, sl[p]]), _bd(kbg[c][:, sl[p]])], axis=1)
                uws[c, p] = _dotb(xs[c, p], rhs)

        stages = [decay_sums, gram, decay_rows, masks, level0]
        for lvl in range(1, 6):
            stages += [first_product(lvl), second_product]
        return stages + [apply_inverse]

    def scan_stages(chunks):
        v_new = {}
        wq = {}

        def read_state(c):
            def run():
                for p in pairs:
                    wq[p] = _dotb(jnp.concatenate([uws[c, p][:, LANES:2 * LANES], qd[c][:, sl[p]]], axis=0), state[p])
                    v_new[p] = uws[c, p][:, 0:LANES] - wq[p][0:c_len]
            return run

        def write_state(c):
            def run():
                upd = [lax.dot_general(kd[c][:, sl[p]].astype(BF16), v_new[p].astype(BF16), TN_DIMS,
                                       preferred_element_type=F32) for p in pairs]
                for p in pairs:
                    o_ref[0, rows_of(c), sl[p]] = wq[p][c_len:2 * c_len] + _dotb(qks[c, p], _bd(v_new[p]))
                    state[p] = state[p] * jnp.exp(gtot[c][:, sl[p]]) + jnp.where(bd_mask, upd[p], 0.0)
            return run

        stages = []
        for c in chunks:
            stages += [read_state(c), write_state(c)]
        return stages

    order = list(range(n_chunks - 1, -1, -1) if rev else range(n_chunks))
    halves = [order[i:i + GDN_GROUP_CHUNKS] for i in range(0, n_chunks, GDN_GROUP_CHUNKS)]
    for stage in prep_stages(halves[0]):
        stage()
    for half, nxt in zip(halves, halves[1:] + [None]):
        scan = scan_stages(half)
        prep = prep_stages(nxt) if nxt else []
        done = 0
        for i, stage in enumerate(prep):
            stage()
            while done < len(scan) and done * len(prep) < (i + 1) * len(scan):
                scan[done]()
                done += 1
        for stage in scan[done:]:
            stage()
    for p in range(GDN_PAIRS):
        s_ref[p] = state[p]

    @pl.when(step == n_blk - 1)
    def _():
        for p in range(GDN_PAIRS):
            sfin_ref[0, p] = state[p]


def _gdn_consts(a_log, dt_bias, d, rev):
    pad = jnp.zeros((LANES - 2 * GDN_HEADS,), F32)
    alog = jnp.concatenate([a_log.reshape(-1).astype(F32), pad]).reshape(1, LANES)
    dtb = jnp.concatenate([dt_bias.reshape(-1).astype(F32), pad]).reshape(1, LANES)
    hw = GDN_HEADS * HEAD_W
    head_of_lane = np.arange(hw) // HEAD_W
    src = np.arange(LANES)[:, None]
    eg = (src == d * GDN_HEADS + head_of_lane[None, :]).astype(np.float32)
    eb = (src == 2 * GDN_HEADS + d * GDN_HEADS + head_of_lane[None, :]).astype(np.float32)
    i = np.arange(GDN_CHUNK)
    tri = (i[:, None] <= i[None, :]) if rev else (i[:, None] >= i[None, :])
    as_bf = lambda a, reps, ax: jnp.asarray(np.concatenate([a] * reps, axis=ax), BF16)
    return alog, dtb, as_bf(eg, 3, 0), as_bf(eb, 3, 0), as_bf(tri.astype(np.float32), 3, 1)


def _gdn_call(qkv, misc, consts, s0, rev, name):
    b, t, w = qkv.shape
    blk = min(t, GDN_MAX_BLOCK)
    assert t % blk == 0 and blk % GDN_CHUNK == 0
    n_blk = t // blk
    hw = GDN_HEADS * HEAD_W
    blk_of = (lambda s: n_blk - 1 - s) if rev else (lambda s: s)
    const = lambda a: pl.BlockSpec(a.shape, lambda i, s: (0,) * a.ndim)
    s_spec = pl.BlockSpec((1, GDN_PAIRS, LANES, LANES), lambda i, s: (i, 0, 0, 0))
    return pl.pallas_call(
        functools.partial(_gdn_kernel, rev=rev, n_blk=n_blk),
        grid=(b, n_blk),
        in_specs=[pl.BlockSpec((1, blk, w), lambda i, s: (i, blk_of(s), 0)),
                  pl.BlockSpec((1, blk, LANES), lambda i, s: (i, blk_of(s), 0))]
                 + [const(a) for a in consts] + [s_spec],
        out_specs=[pl.BlockSpec((1, blk, hw), lambda i, s: (i, blk_of(s), 0)), s_spec],
        out_shape=[jax.ShapeDtypeStruct((b, t, hw), F32),
                   jax.ShapeDtypeStruct((b, GDN_PAIRS, LANES, LANES), F32)],
        scratch_shapes=[pltpu.VMEM((GDN_PAIRS, LANES, LANES), F32)],
        compiler_params=_params(2),
        name=name,
    )(qkv, misc, *consts, s0)


MIX_SUB_ROWS = 256


def _mix_ffn_kernel(*refs, ff_chunks, gdn_inputs, n_sub):
    tm = refs[0].shape[1]
    rows = [slice(i * (tm // n_sub), (i + 1) * (tm // n_sub)) for i in range(n_sub)]
    if gdn_inputs:
        (h_ref, of_ref, ob_ref, gate_ref, att_ref, on_ref, ones_ref, gtm_ref, shf_ref, scf_ref, gtf_ref, gpm_ref,
         gpf_ref, gqf_ref, wo_ref, wg_ref, wu_ref, wd_ref, o_ref) = refs
        mixes = []
        for r in rows:
            o = of_ref[0, r, :] + ob_ref[0, r, :]
            gate = gate_ref[0, r, :].astype(F32)
            a = o * lax.rsqrt(_seg_mean_sq(o, ones_ref) + NORM_EPS) * on_ref[...] * (gate * jax.nn.sigmoid(gate))
            mixes.append(jnp.concatenate([a.astype(BF16), att_ref[0, r, :]], axis=1))
    else:
        (h_ref, a_ref, b_ref, gtm_ref, shf_ref, scf_ref, gtf_ref, gpm_ref, gpf_ref, gqf_ref, wo_ref, wg_ref,
         wu_ref, wd_ref, o_ref) = refs
        mixes = [jnp.concatenate([a_ref[0, r, :], b_ref[0, r, :]], axis=1) for r in rows]
    ys = [_dotf(mix, wo_ref[...]) for mix in mixes]
    h1s = [h_ref[0, r, :] + gtm_ref[0] * (_rms(y) * gpm_ref[...]) for r, y in zip(rows, ys)]
    us = [(_rms(h1) * gpf_ref[...] * (1.0 + scf_ref[0]) + shf_ref[0]).astype(BF16) for h1 in h1s]
    acts = []
    for o, n in ff_chunks:
        for u in us:
            gg = _dotf(u, wg_ref[:, o:o + n])
            uu = _dotf(u, wu_ref[:, o:o + n])
            acts.append((gg * jax.nn.sigmoid(gg) * uu).astype(BF16))
    fs = [None] * n_sub
    for ci, (o, n) in enumerate(ff_chunks):
        for i in range(n_sub):
            part = _dotf(acts[ci * n_sub + i], wd_ref[o:o + n, :])
            fs[i] = part if fs[i] is None else fs[i] + part
    for r, h1, f in zip(rows, h1s, fs):
        o_ref[0, r, :] = h1 + gtf_ref[0] * (_rms(f) * gqf_ref[...])


def _mix_ffn_call(h, mix_inputs, out_norm, gt_m, sh_f, sc_f, gt_f, g_post_mix, g_pre_ffn, g_post_ffn, wo, wg, wu,
                  wd, tm, name):
    b, t, d = h.shape
    dm = wo.shape[0]
    ff = wg.shape[1]
    gdn_inputs = out_norm is not None
    half = pl.cdiv(ff // 2, MXU_TILE) * MXU_TILE
    assert 0 < half < ff and ff % LANES == 0
    ff_chunks = ((0, half), (half, ff - half))
    tok = lambda n: pl.BlockSpec((1, tm, n), lambda i, j: (i, j, 0))
    mods = [_mod_operand(m, d) for m in (gt_m, sh_f, sc_f, gt_f)]
    vec = pl.BlockSpec((1, d), lambda i, j: (0, 0))
    const = lambda shape: pl.BlockSpec(shape, lambda i, j: (0, 0), pipeline_mode=pl.Buffered(1))
    r2 = lambda a: a.reshape(1, d).astype(F32)
    mix_specs = [tok(m.shape[-1]) for m in mix_inputs]
    mix_args = list(mix_inputs)
    if gdn_inputs:
        hw = GDN_HEADS * GDN_DV
        on = jnp.tile(out_norm.astype(F32), GDN_HEADS).reshape(1, hw)
        ones = _seg_ones(hw)
        mix_specs += [pl.BlockSpec(on.shape, lambda i, j: (0, 0)), pl.BlockSpec(ones.shape, lambda i, j: (0, 0))]
        mix_args += [on, ones]
    return pl.pallas_call(
        functools.partial(_mix_ffn_kernel, ff_chunks=ff_chunks, gdn_inputs=gdn_inputs,
                          n_sub=tm // MIX_SUB_ROWS),
        grid=(b, t // tm),
        in_specs=[tok(d)] + mix_specs + [spec for _, spec in mods]
                 + [vec, vec, vec, const((dm, d)), const((d, ff)), const((d, ff)), const((ff, d))],
        out_specs=tok(d),
        out_shape=jax.ShapeDtypeStruct((b, t, d), F32),
        compiler_params=_params(2),
        name=name,
    )(h, *mix_args, *[arr for arr, _ in mods], r2(g_post_mix), r2(g_pre_ffn), r2(g_post_ffn), wo, wg, wu, wd)


def _rope_tables(n_tokens, rot_dim):
    t = jnp.arange(n_tokens, dtype=jnp.int32)
    row = (t // GRID_W).astype(F32)
    col = (t % GRID_W).astype(F32)
    n_freq = rot_dim // 4
    inv_freq = ROPE_THETA ** (-jnp.arange(n_freq, dtype=F32) / n_freq)
    ang = jnp.concatenate([row[:, None] * inv_freq, col[:, None] * inv_freq], -1)
    cos = jnp.concatenate([jnp.cos(ang), jnp.cos(ang)], -1)
    sin = jnp.concatenate([-jnp.sin(ang), jnp.sin(ang)], -1)
    if rot_dim == HEAD_W:
        return jnp.tile(cos, (1, LANES // rot_dim)), jnp.tile(sin, (1, LANES // rot_dim))
    pad = LANES - rot_dim
    return (jnp.concatenate([cos, jnp.ones((n_tokens, pad), F32)], -1),
            jnp.concatenate([sin, jnp.zeros((n_tokens, pad), F32)], -1))


def _no_rope_tables(n_tokens):
    return jnp.ones((n_tokens, LANES), F32), jnp.zeros((n_tokens, LANES), F32)


def _gdn_bidirectional(lat_qkv, lat_misc, ctx_qkv, ctx_misc, a_log, dt_bias):
    bsz = lat_qkv.shape[0]
    s_zero = jnp.zeros((bsz, GDN_PAIRS, LANES, LANES), F32)
    lat, ctx = [], []
    for d, rev in ((0, False), (1, True)):
        consts = _gdn_consts(a_log, dt_bias, d, rev)
        o_c, s_c = _gdn_call(ctx_qkv, ctx_misc, consts, s_zero, rev, "gdn_ctx_%d" % d)
        o_l, _ = _gdn_call(lat_qkv, lat_misc, consts, s_c, rev, "gdn_lat_%d" % d)
        lat.append(o_l)
        ctx.append(o_c)
    return lat, ctx


LOG2E = 1.4426950408889634
ATTN_TQ = 1024
ATTN_LOOKAHEAD = 2


def _attn_chunks(l_len, s_len, size):
    return [(0, 0, l_len)] + [(1, o, size) for o in range(0, s_len, size)]


def _even_mixer(h, hc, mods, mods_c, g_pre, w_in, conv_w, a_log, dt_bias, q_norm, w_q_up, kv_norm, w_kv_up):
    s_len, l_len = h.shape[1], hc.shape[1]
    weights = _even_weights(w_in, q_norm, w_q_up, kv_norm, w_kv_up)
    qkv_l, gate_l, misc_l, q_l, k_l, v_l = _in_even_call(h, g_pre, *mods, weights, conv_w,
                                                         *_rope_tables(s_len, MLA_ROPE), 512, "in_even_lat")
    qkv_c, gate_c, misc_c, q_c, k_c, v_c = _in_even_call(hc, g_pre, *mods_c, weights, conv_w,
                                                         *_no_rope_tables(l_len), l_len, "in_even_ctx")
    o_lat, o_ctx = _gdn_bidirectional(qkv_l, misc_l, qkv_c, misc_c, a_log, dt_bias)
    b_lat = _attn_call(q_l, [(k_c, v_c), (k_l, v_l)], MLA_QC, MLA_PAIRS, MLA_PAIRS, _attn_chunks(l_len, s_len, 512),
                       ATTN_TQ, False, "mla_attn_lat")
    b_ctx = _attn_call(q_c, [(k_c, v_c)], MLA_QC, MLA_PAIRS, MLA_PAIRS, [(0, 0, l_len)], l_len, False,
                       "mla_attn_ctx")
    return (o_lat[0], o_lat[1], gate_l, b_lat), (o_ctx[0], o_ctx[1], gate_c, b_ctx)


def _odd_mixer_last(h, hc, mods, mods_c, g_pre, w_in, rpb, q_norm, k_norm):
    s_len, l_len = h.shape[1], hc.shape[1]
    w = w_in.astype(BF16)
    q_gain = (jnp.tile(q_norm.astype(F32), GQA_HEADS) * (GQA_DIM ** -0.5 * LOG2E)).reshape(1, -1)
    k_gain = jnp.tile(k_norm.astype(F32), GQA_KV_HEADS).reshape(1, -1)
    naq_l, nak_l, nav_l, gq_l, gk_l, gv_l = _in_odd_call(h, g_pre, *mods, w, q_gain, k_gain,
                                                         *_rope_tables(s_len, GQA_DIM), 512, "in_odd_lat")
    _, nak_c, nav_c, _, gk_c, gv_c = _in_odd_call(hc, g_pre, *mods_c, w, q_gain, k_gain, *_no_rope_tables(l_len),
                                                  l_len, "in_odd_ctx")
    c_lat = _na_call(naq_l, nak_l, nav_l, nak_c, nav_c, _na_bias_table(rpb, s_len // GRID_W))
    d_lat = _attn_call(gq_l, [(gk_c, gv_c), (gk_l, gv_l)], LANES, GQA_HEADS // 2, GQA_KV_HEADS,
                       _attn_chunks(l_len, s_len, 256), ATTN_TQ, True, "gqa_attn")
    return c_lat, d_lat


def kernel(x, c, ctx, c_ctx, w_mod, b_mod, g_pre_mix, g_post_mix, g_pre_ffn, g_post_ffn, w_ffn_gate, w_ffn_up,
           w_ffn_down, w_in_even, w_out_even, gdn_conv, gdn_a_log, gdn_dt_bias, gdn_out_norm, mla_q_norm,
           mla_w_q_up, mla_kv_norm, mla_w_kv_up, w_in_odd, w_out_odd, na_rpb, gqa_q_norm, gqa_k_norm):
    bsz, s_len, d = x.shape
    depth = w_mod.shape[0]
    assert depth == 2, "layer 0 = even mixer with context update, layer 1 = odd mixer (last)"
    c_rows = jnp.concatenate([c, c_ctx[None, :], jnp.zeros((16 - bsz - 1, d), F32)], 0)
    h, hc = x, ctx
    for i in range(depth):
        mod_tab = _mod_call(c_rows, w_mod[i].astype(BF16), b_mod[i]).reshape(c_rows.shape[0] * N_MOD, 1, d)
        sh_m, sc_m, gt_m, sh_f, sc_f, gt_f = (_Mod(mod_tab, k, None) for k in range(N_MOD))
        csh_m, csc_m, cgt_m, csh_f, csc_f, cgt_f = (_Mod(mod_tab, k, bsz) for k in range(N_MOD))
        wg, wu, wd = w_ffn_gate[i].astype(BF16), w_ffn_up[i].astype(BF16), w_ffn_down[i].astype(BF16)
        if i == 0:
            mix, mix_c = _even_mixer(h, hc, (sh_m, sc_m), (csh_m, csc_m), g_pre_mix[i], w_in_even[0], gdn_conv[0],
                                     gdn_a_log[0], gdn_dt_bias[0], mla_q_norm[0], mla_w_q_up[0], mla_kv_norm[0],
                                     mla_w_kv_up[0])
            wo, out_norm = w_out_even[0].astype(BF16), gdn_out_norm[0]
            hc = _mix_ffn_call(hc, mix_c, out_norm, cgt_m, csh_f, csc_f, cgt_f, g_post_mix[i], g_pre_ffn[i],
                               g_post_ffn[i], wo, wg, wu, wd, 256, "mix_ffn_ctx")
        else:
            mix = _odd_mixer_last(h, hc, (sh_m, sc_m), (csh_m, csc_m), g_pre_mix[i], w_in_odd[0], na_rpb[0],
                                  gqa_q_norm[0], gqa_k_norm[0])
            wo, out_norm = w_out_odd[0].astype(BF16), None
        h = _mix_ffn_call(h, mix, out_norm, gt_m, sh_f, sc_f, gt_f, g_post_mix[i], g_pre_ffn[i], g_post_ffn[i],
                          wo, wg, wu, wd, 2 * MIX_SUB_ROWS, "mix_ffn_lat%d" % i)
    return h
```

```python
import functools
from typing import NamedTuple, Optional

import numpy as np
import jax
import jax.numpy as jnp
from jax import lax
from jax.experimental import pallas as pl
from jax.experimental.pallas import tpu as pltpu

F32 = jnp.float32
BF16 = jnp.bfloat16

GRID_W = 64
NORM_EPS = 1e-6
ROPE_THETA = 10000.0

GDN_HEADS = 8
GDN_DK = 64
GDN_DV = 64
GDN_CHUNK = 64
GDN_QK_W = GDN_HEADS * GDN_DK
GDN_V_W = GDN_HEADS * GDN_DV

MLA_HEADS = 8
MLA_NOPE = 64
MLA_ROPE = 32
MLA_V = 64
MLA_Q_LORA = 256
MLA_KV_LORA = 128

NA_HEADS = 8
NA_DIM = 64
NA_WIN_H = 8
NA_WIN_W = 16
NA_W = NA_HEADS * NA_DIM

GQA_HEADS = 8
GQA_KV_HEADS = 2
GQA_DIM = 64

LANES = 128
MXU_TILE = 256
HEAD_W = 64
MASK_VALUE = -1e30
VMEM_LIMIT = 56 << 20

NT_DIMS = (((1,), (1,)), ((), ()))


def _params(n_grid, vmem=VMEM_LIMIT):
    return pltpu.CompilerParams(dimension_semantics=("arbitrary",) * n_grid, vmem_limit_bytes=vmem)


def _rms(x):
    return x * lax.rsqrt(jnp.mean(x * x, axis=-1, keepdims=True) + NORM_EPS)


def _mod_kernel(c_ref, w_ref, b_ref, o_ref):
    c = c_ref[...]
    a = (c * jax.nn.sigmoid(c)).astype(BF16)
    o_ref[...] = jnp.dot(a, w_ref[...], preferred_element_type=F32) + b_ref[...]


def _mod_call(c_rows, w, b):
    r, d = c_rows.shape
    n = w.shape[1]
    tn = 1024
    return pl.pallas_call(
        _mod_kernel,
        grid=(n // tn,),
        in_specs=[pl.BlockSpec((r, d), lambda j: (0, 0)),
                  pl.BlockSpec((d, tn), lambda j: (0, j)),
                  pl.BlockSpec((1, tn), lambda j: (0, j))],
        out_specs=pl.BlockSpec((r, tn), lambda j: (0, j)),
        out_shape=jax.ShapeDtypeStruct((r, n), F32),
        compiler_params=_params(1),
        name="mod",
    )(c_rows, w, b.reshape(1, n))


class _Mod(NamedTuple):
    table: jax.Array
    comp: int
    row: Optional[int]


N_MOD = 6


def _mod_operand(m, d):
    if m.row is None:
        return m.table, pl.BlockSpec((1, 1, d), lambda i, j: (i * N_MOD + m.comp, 0, 0))
    return m.table, pl.BlockSpec((1, 1, d), lambda i, j: (m.row * N_MOD + m.comp, 0, 0))


def _norm_mod(x_ref, g_ref, sh_ref, sc_ref):
    return (_rms(x_ref[0]) * g_ref[...] * (1.0 + sc_ref[0]) + sh_ref[0]).astype(BF16)


def _dotf(a, b):
    return jnp.dot(a, b, preferred_element_type=F32)


def _seg_mean_sq(x, ones_ref):
    x2 = x * x
    hi = x2.astype(BF16)
    mid = (x2 - hi.astype(F32)).astype(BF16)
    return _dotf(jnp.concatenate([hi, mid], axis=1), ones_ref[...]) * (1.0 / HEAD_W)


def _rotate(t, cos, sin_signed, half):
    lane = lax.broadcasted_iota(jnp.int32, t.shape, 1)
    n = t.shape[1]
    partner = jnp.where((lane & (2 * half - 1)) < half, pltpu.roll(t, n - half, 1), pltpu.roll(t, half, 1))
    return t * cos + partner * sin_signed


EVEN_QKV_W = 2 * GDN_QK_W + GDN_V_W
EVEN_GROUPS = (EVEN_QKV_W, GDN_V_W, MLA_Q_LORA, MLA_KV_LORA, LANES, LANES)
MLA_QC = 2 * LANES
MLA_PAIRS = MLA_HEADS // 2


CONV_HALO = 16


def _in_even_kernel(x_ref, xp_ref, xn_ref, g_ref, sh_ref, sc_ref, w_ref, qg_ref, wq_ref, kvg_ref, wk_ref, wv_ref,
                    vone_ref, cos_ref, sin_ref, cw_ref, ones_ref, qkv_ref, gate_ref, misc_ref, q_ref, k_ref, v_ref,
                    zs_ref, *, n_tap):
    j = pl.program_id(1)
    tm = x_ref.shape[1]
    u = _norm_mod(x_ref, g_ref, sh_ref, sc_ref)
    offs = np.cumsum((0,) + EVEN_GROUPS)
    grp = lambda i: _dotf(u, w_ref[:, int(offs[i]):int(offs[i + 1])])
    u_ext = jnp.concatenate([_norm_mod(xp_ref, g_ref, sh_ref, sc_ref), u, _norm_mod(xn_ref, g_ref, sh_ref, sc_ref)],
                            axis=0)
    z = _dotf(u_ext, w_ref[:, 0:EVEN_QKV_W])
    zs_ref[0:CONV_HALO, :] = z[0:CONV_HALO] * (j > 0).astype(F32)
    zs_ref[CONV_HALO:CONV_HALO + tm, :] = z[CONV_HALO:CONV_HALO + tm]
    zs_ref[CONV_HALO + tm:, :] = z[CONV_HALO + tm:] * (j < pl.num_programs(1) - 1).astype(F32)
    q_down, kv_down, pe_raw = grp(2), grp(3), grp(5)
    gate_ref[0] = grp(1).astype(BF16)
    misc_ref[0] = grp(4)
    cos, sin = cos_ref[...], sin_ref[...]
    qn = (_rms(q_down) * qg_ref[...]).astype(BF16)
    kvn = (_rms(kv_down) * kvg_ref[...]).astype(BF16)
    for h in range(MLA_HEADS):
        blk = _dotf(qn, wq_ref[:, h * MLA_QC:(h + 1) * MLA_QC])
        q_ref[0, :, h * MLA_QC:h * MLA_QC + LANES] = blk[:, 0:LANES].astype(BF16)
        q_ref[0, :, h * MLA_QC + LANES:(h + 1) * MLA_QC] = _rotate(blk[:, LANES:], cos, sin,
                                                                    MLA_ROPE // 2).astype(BF16)
    pe = _rotate(pe_raw, cos, sin, MLA_ROPE // 2).astype(BF16)
    k_nope = _dotf(kvn, wk_ref[...])
    for p in range(MLA_PAIRS):
        k_ref[0, :, p * MLA_QC:p * MLA_QC + LANES] = k_nope[:, p * LANES:(p + 1) * LANES].astype(BF16)
        k_ref[0, :, p * MLA_QC + LANES:(p + 1) * MLA_QC] = pe
    v_ref[0] = (_dotf(kvn, wv_ref[...]) + vone_ref[...]).astype(BF16)
    acc = None
    for t in range(n_tap):
        start = CONV_HALO - n_tap // 2 + t
        term = zs_ref[start:start + tm, :] * cw_ref[t:t + 1, :]
        acc = term if acc is None else acc + term
    y = acc * jax.nn.sigmoid(acc)
    hw = GDN_QK_W
    qc, kc = y[:, 0:hw], y[:, hw:2 * hw]
    q_ss = _seg_mean_sq(qc, ones_ref) * HEAD_W
    k_ss = _seg_mean_sq(kc, ones_ref) * HEAD_W
    qkv_ref[0, :, 0:hw] = (qc * lax.rsqrt(q_ss + NORM_EPS) * (GDN_DK ** -0.5)).astype(BF16)
    qkv_ref[0, :, hw:2 * hw] = (kc * lax.rsqrt(k_ss + NORM_EPS)).astype(BF16)
    qkv_ref[0, :, 2 * hw:] = y[:, 2 * hw:].astype(BF16)


def _even_weights(w_in, q_norm, w_q_up, kv_norm, w_kv_up):
    d = w_in.shape[0]
    o0 = EVEN_QKV_W + GDN_V_W
    n_ab = 4 * GDN_HEADS
    o1 = o0 + n_ab
    o2 = o1 + MLA_Q_LORA
    o3 = o2 + MLA_KV_LORA
    zeros = lambda n, rows=d: jnp.zeros((rows, n), F32)
    w_cat = jnp.concatenate([w_in[:, :o0], w_in[:, o1:o2], w_in[:, o2:o3],
                             w_in[:, o0:o1], zeros(LANES - n_ab),
                             w_in[:, o3:], zeros(LANES - MLA_ROPE)], 1).astype(BF16)
    qh = MLA_NOPE + MLA_ROPE
    cols = []
    for h in range(MLA_HEADS):
        nope = w_q_up[:, h * qh:h * qh + MLA_NOPE]
        rp = w_q_up[:, h * qh + MLA_NOPE:(h + 1) * qh]
        z = zeros(HEAD_W, MLA_Q_LORA)
        cols += ([nope, z] if h % 2 == 0 else [z, nope]) + [rp, zeros(LANES - MLA_ROPE, MLA_Q_LORA)]
    wq = jnp.concatenate(cols, 1).astype(BF16)
    kvh = MLA_NOPE + MLA_V
    wk = jnp.concatenate([w_kv_up[:, h * kvh:h * kvh + MLA_NOPE] for h in range(MLA_HEADS)], 1).astype(BF16)
    vcols, ones = [], []
    for h in range(MLA_HEADS):
        vh = w_kv_up[:, h * kvh + MLA_NOPE:(h + 1) * kvh]
        z = zeros(HEAD_W, MLA_KV_LORA)
        vcols += [vh, z] if h % 2 == 0 else [z, vh]
        ones += [0.0, 1.0] if h % 2 == 0 else [1.0, 0.0]
    wv = jnp.concatenate(vcols, 1).astype(BF16)
    vone = jnp.asarray(np.repeat(np.asarray(ones, np.float32), HEAD_W)[None, :])
    q_gain = (q_norm * ((MLA_NOPE + MLA_ROPE) ** -0.5 * LOG2E)).reshape(1, -1).astype(F32)
    return w_cat, q_gain, wq, kv_norm.reshape(1, -1).astype(F32), wk, wv, vone


def _in_even_call(x, g, shift, scale, weights, conv_w, cos, sin, tm, name):
    b, t, d = x.shape
    w_cat, q_gain, wq, kv_gain, wk, wv, vone = weights
    n_tap = conv_w.shape[0]
    cw = jnp.concatenate([conv_w.astype(F32), jnp.zeros((8 - n_tap, conv_w.shape[1]), F32)], 0)
    ones = _seg_ones(GDN_QK_W)
    per = tm // CONV_HALO
    n_halo = t // CONV_HALO
    const = lambda a: pl.BlockSpec(a.shape, lambda i, j: (0,) * a.ndim)
    (sh_arr, sh_spec), (sc_arr, sc_spec) = _mod_operand(shift, d), _mod_operand(scale, d)
    tab = pl.BlockSpec((tm, LANES), lambda i, j: (j, 0))
    widths = (EVEN_QKV_W, GDN_V_W, LANES, MLA_HEADS * MLA_QC, MLA_PAIRS * MLA_QC, MLA_HEADS * LANES)
    dts = (BF16, BF16, F32, BF16, BF16, BF16)
    g2 = g.reshape(1, d).astype(F32)
    return pl.pallas_call(
        functools.partial(_in_even_kernel, n_tap=n_tap),
        grid=(b, t // tm),
        in_specs=[pl.BlockSpec((1, tm, d), lambda i, j: (i, j, 0)),
                  pl.BlockSpec((1, CONV_HALO, d), lambda i, j: (i, jnp.maximum(j * per - 1, 0), 0)),
                  pl.BlockSpec((1, CONV_HALO, d), lambda i, j: (i, jnp.minimum((j + 1) * per, n_halo - 1), 0)),
                  const(g2), sh_spec, sc_spec, const(w_cat), const(q_gain), const(wq), const(kv_gain), const(wk),
                  const(wv), const(vone), tab, tab, const(cw), const(ones)],
        out_specs=[pl.BlockSpec((1, tm, n), lambda i, j: (i, j, 0)) for n in widths],
        out_shape=[jax.ShapeDtypeStruct((b, t, n), dt) for n, dt in zip(widths, dts)],
        scratch_shapes=[pltpu.VMEM((tm + 2 * CONV_HALO, EVEN_QKV_W), F32)],
        compiler_params=_params(2),
        name=name,
    )(x, x, x, g2, sh_arr, sc_arr, w_cat, q_gain, wq, kv_gain, wk, wv, vone,
      cos, sin, cw, ones)


GQA_KV_W = GQA_KV_HEADS * GQA_DIM
PROJ_SUB_ROWS = 256


def _in_odd_kernel(x_ref, g_ref, sh_ref, sc_ref, w_ref, qg_ref, kg_ref, ones_q_ref, ones_k_ref, cos_ref, sin_ref,
                   naq_ref, nak_ref, nav_ref, gq_ref, gk_ref, gv_ref):
    tm = x_ref.shape[1]
    n_sub = max(tm // PROJ_SUB_ROWS, 1)
    subs = [slice(i * (tm // n_sub), (i + 1) * (tm // n_sub)) for i in range(n_sub)]
    us = [(_rms(x_ref[0, r, :]) * g_ref[...] * (1.0 + sc_ref[0]) + sh_ref[0]).astype(BF16) for r in subs]
    o_q = 3 * NA_W
    o_k = o_q + GQA_HEADS * GQA_DIM
    qs = [_dotf(u, w_ref[:, o_q:o_k]) for u in us]
    ks = [_dotf(u, w_ref[:, o_k:o_k + GQA_KV_W]) for u in us]
    vs = [_dotf(u, w_ref[:, o_k + GQA_KV_W:o_k + 2 * GQA_KV_W]) for u in us]
    low = lax.broadcasted_iota(jnp.int32, (tm // n_sub, LANES), 1) < HEAD_W
    for r, u in zip(subs, us):
        naq_ref[0, r, :] = (_dotf(u, w_ref[:, 0:NA_W]) * (NA_DIM ** -0.5 * LOG2E)).astype(BF16)
        nak_ref[0, r, :] = _dotf(u, w_ref[:, NA_W:2 * NA_W]).astype(BF16)
        na_v = _dotf(u, w_ref[:, 2 * NA_W:3 * NA_W])
        for p in range(NA_HEADS // 2):
            tile = na_v[:, p * LANES:(p + 1) * LANES]
            nav_ref[0, r, 2 * p * LANES:(2 * p + 1) * LANES] = tile.astype(BF16)
            nav_ref[0, r, (2 * p + 1) * LANES:(2 * p + 2) * LANES] = jnp.ones(tile.shape, BF16)
    q_ms = [_seg_mean_sq(q, ones_q_ref) for q in qs]
    k_ms = [_seg_mean_sq(k, ones_k_ref) for k in ks]
    for i, r in enumerate(subs):
        cos, sin = cos_ref[r, :], sin_ref[r, :]
        qn = qs[i] * lax.rsqrt(q_ms[i] + NORM_EPS) * qg_ref[...]
        for p in range(GQA_HEADS // 2):
            cols = slice(p * LANES, (p + 1) * LANES)
            gq_ref[0, r, cols] = _rotate(qn[:, cols], cos, sin, GQA_DIM // 2).astype(BF16)
        kr = _rotate(ks[i] * lax.rsqrt(k_ms[i] + NORM_EPS) * kg_ref[...], cos, sin, GQA_DIM // 2)
        k_sw = pltpu.roll(kr, HEAD_W, 1)
        gk_ref[0, r, 0:LANES] = jnp.where(low, kr, k_sw).astype(BF16)
        gk_ref[0, r, LANES:2 * LANES] = jnp.where(low, k_sw, kr).astype(BF16)
        v = vs[i]
        v_sw = pltpu.roll(v, HEAD_W, 1)
        slots = (jnp.where(low, v, 1.0), jnp.where(low, 1.0, v_sw),
                 jnp.where(low, v_sw, 1.0), jnp.where(low, 1.0, v))
        for j, s in enumerate(slots):
            gv_ref[0, r, j * LANES:(j + 1) * LANES] = s.astype(BF16)


def _seg_ones(width):
    head = np.arange(width) // HEAD_W
    m = (head[:, None] == head[None, :]).astype(np.float32)
    return jnp.asarray(np.concatenate([m, m], 0), BF16)


def _in_odd_call(x, g, shift, scale, w, q_gain, k_gain, cos, sin, tm, name):
    b, t, d = x.shape
    assert GQA_KV_W == LANES
    ones_q = _seg_ones(GQA_HEADS * GQA_DIM)
    ones_k = _seg_ones(GQA_KV_W)
    const = lambda a: pl.BlockSpec(a.shape, lambda i, j: (0,) * a.ndim)
    (sh_arr, sh_spec), (sc_arr, sc_spec) = _mod_operand(shift, d), _mod_operand(scale, d)
    tab = pl.BlockSpec((tm, LANES), lambda i, j: (j, 0))
    widths = (NA_W, NA_W, 2 * NA_W, GQA_HEADS * GQA_DIM, 2 * LANES, 4 * LANES)
    g2 = g.reshape(1, d).astype(F32)
    return pl.pallas_call(
        _in_odd_kernel,
        grid=(b, t // tm),
        in_specs=[pl.BlockSpec((1, tm, d), lambda i, j: (i, j, 0)), const(g2), sh_spec, sc_spec, const(w),
                  const(q_gain), const(k_gain), const(ones_q), const(ones_k), tab, tab],
        out_specs=[pl.BlockSpec((1, tm, n), lambda i, j: (i, j, 0)) for n in widths],
        out_shape=[jax.ShapeDtypeStruct((b, t, n), BF16) for n in widths],
        compiler_params=_params(2),
        name=name,
    )(x, g2, sh_arr, sc_arr, w, q_gain, k_gain, ones_q, ones_k, cos, sin)


def _attn_kernel(*refs, dc, chunks, n_src, masked_q):
    q_ref, kv_refs, o_ref = refs[0], refs[1:1 + 2 * n_src], refs[-1]
    tq = q_ref.shape[1]
    lane = lax.broadcasted_iota(jnp.int32, (tq, LANES), 1)
    qs = []
    for hh in range(2):
        if masked_q:
            qp = q_ref[0]
            qs.append(jnp.where((lane < HEAD_W) if hh == 0 else (lane >= HEAD_W), qp, jnp.zeros_like(qp)))
        else:
            qs.append(q_ref[0, :, hh * dc:(hh + 1) * dc])
    ms, accs = [None, None], [None, None]

    def scores(hh, src, s0, n):
        return lax.dot_general(qs[hh], kv_refs[2 * src][0, s0:s0 + n, :], NT_DIMS, preferred_element_type=F32)

    def consume(s, hh, src, s0, n):
        vv = kv_refs[2 * src + 1][0, s0:s0 + n, hh * LANES:(hh + 1) * LANES]
        mc = jnp.max(s, axis=-1, keepdims=True)
        m_new = mc if ms[hh] is None else jnp.maximum(ms[hh], mc)
        pv = _dotf(jnp.exp2(s - m_new).astype(BF16), vv)
        accs[hh] = pv if ms[hh] is None else jnp.exp2(ms[hh] - m_new) * accs[hh] + pv
        ms[hh] = m_new

    items = [(hh,) + tuple(c) for c in chunks for hh in range(2)]
    pending = []
    for it in items:
        pending.append((scores(*it), it))
        if len(pending) > ATTN_LOOKAHEAD:
            s, it0 = pending.pop(0)
            consume(s, *it0)
    for s, it0 in pending:
        consume(s, *it0)
    low = lane < HEAD_W
    num = jnp.where(low, accs[0], accs[1])
    den = pltpu.roll(jnp.where(low, accs[1], accs[0]), HEAD_W, 1)
    o_ref[0] = (num / den).astype(o_ref.dtype)


def _attn_call(q, kvs, dc, n_pairs, n_groups, chunks, tq, masked_q, name):
    b, t_q, _ = q.shape
    per = n_pairs // n_groups
    q_w = LANES if masked_q else 2 * dc
    in_specs = [pl.BlockSpec((1, tq, q_w), lambda i, p, j: (i, j, p))]
    args = [q]
    for k, v in kvs:
        in_specs += [pl.BlockSpec((1, k.shape[1], dc), lambda i, p, j: (i, 0, p // per)),
                     pl.BlockSpec((1, v.shape[1], 2 * LANES), lambda i, p, j: (i, 0, p // per))]
        args += [k, v]
    return pl.pallas_call(
        functools.partial(_attn_kernel, dc=dc, chunks=tuple(chunks), n_src=len(kvs), masked_q=masked_q),
        grid=(b, n_pairs, t_q // tq),
        in_specs=in_specs,
        out_specs=pl.BlockSpec((1, tq, LANES), lambda i, p, j: (i, j, p)),
        out_shape=jax.ShapeDtypeStruct((b, t_q, n_pairs * LANES), BF16),
        compiler_params=_params(3),
        name=name,
    )(*args)


NA_QROWS = 4
NA_KROWS = 12
NA_LOOKAHEAD = 1


def _na_kernel(q_ref, k_ref, v_ref, kc_ref, vc_ref, bias_ref, o_ref):
    rb = pl.program_id(1)
    n_rows = k_ref.shape[1] // GRID_W
    base = jnp.clip(rb * NA_QROWS - NA_WIN_H // 2, 0, n_rows - NA_KROWS) * GRID_W
    base = pl.multiple_of(base, GRID_W)
    nq = NA_QROWS * GRID_W
    nk = NA_KROWS * GRID_W
    lane = lax.broadcasted_iota(jnp.int32, (nq, LANES), 1)
    low = lane < HEAD_W

    def scores(p):
        cols = slice(p * LANES, (p + 1) * LANES)
        qp = q_ref[0, :, cols]
        zero = jnp.zeros_like(qp)
        q2 = jnp.concatenate([jnp.where(low, qp, zero), jnp.where(low, zero, qp)], axis=0)
        s_loc = lax.dot_general(q2, k_ref[0, pl.ds(base, nk), cols], NT_DIMS, preferred_element_type=F32)
        s_ctx = lax.dot_general(q2, kc_ref[0, :, cols], NT_DIMS, preferred_element_type=F32)
        return s_loc, s_ctx

    def consume(p, s_loc, s_ctx):
        vcols = slice(2 * p * LANES, (2 * p + 2) * LANES)
        s_loc = s_loc + bias_ref[0, p]
        m = jnp.maximum(jnp.max(s_loc, axis=-1, keepdims=True), jnp.max(s_ctx, axis=-1, keepdims=True))
        acc = (_dotf(jnp.exp2(s_loc - m).astype(BF16), v_ref[0, pl.ds(base, nk), vcols])
               + _dotf(jnp.exp2(s_ctx - m).astype(BF16), vc_ref[0, :, vcols]))
        even, odd = acc[0:nq], acc[nq:2 * nq]
        num = jnp.where(low, even[:, 0:LANES], odd[:, 0:LANES])
        den = jnp.where(low, even[:, LANES:2 * LANES], odd[:, LANES:2 * LANES])
        o_ref[0, :, p * LANES:(p + 1) * LANES] = (num / den).astype(o_ref.dtype)

    pending = []
    for p in range(NA_HEADS // 2):
        pending.append((p,) + scores(p))
        if len(pending) > NA_LOOKAHEAD:
            consume(*pending.pop(0))
    for item in pending:
        consume(*item)


def _na_bias_table(rpb, rows):
    n_blocks = rows // NA_QROWS
    n_h = rpb.shape[0]
    c = np.arange(GRID_W)[:, None]
    kc = np.arange(GRID_W)[None, :]
    cs = np.clip(c - NA_WIN_W // 2, 0, GRID_W - NA_WIN_W)
    col_ok = (kc >= cs) & (kc < cs + NA_WIN_W)
    dc = np.clip(kc - c + (NA_WIN_W - 1), 0, 2 * NA_WIN_W - 2)
    pick = (np.arange(2 * NA_WIN_W - 1)[:, None, None] == dc[None]).astype(np.float32)
    by_col = jnp.einsum("hdm,mck->hcdk", rpb.astype(F32) * LOG2E, jnp.asarray(pick),
                        precision=lax.Precision.HIGHEST)
    variants = []
    for r0 in (0, NA_QROWS * (n_blocks // 2), rows - NA_QROWS):
        base = int(np.clip(r0 - NA_WIN_H // 2, 0, rows - NA_KROWS))
        r = r0 + np.arange(NA_QROWS)[:, None]
        kr = base + np.arange(NA_KROWS)[None, :]
        rs = np.clip(r - NA_WIN_H // 2, 0, rows - NA_WIN_H)
        row_ok = (kr >= rs) & (kr < rs + NA_WIN_H)
        dr = np.clip(kr - r + (NA_WIN_H - 1), 0, 2 * NA_WIN_H - 2)
        full = jnp.stack([by_col[:, :, dr[i], :] for i in range(NA_QROWS)], 1)
        valid = row_ok[:, None, :, None] & col_ok[None, :, None, :]
        full = jnp.where(valid[None], full, MASK_VALUE)
        variants.append(full.reshape(n_h, NA_QROWS * GRID_W, NA_KROWS * GRID_W))
    return jnp.stack(variants, 0).reshape(3, n_h // 2, 2 * NA_QROWS * GRID_W, NA_KROWS * GRID_W)


def _na_call(q, k, v, kc, vc, bias):
    b, t, w = q.shape
    n_blocks = t // (NA_QROWS * GRID_W)
    nq = NA_QROWS * GRID_W
    tc = kc.shape[1]

    def bias_map(i, r):
        return (jnp.where(r == 0, 0, jnp.where(r == n_blocks - 1, 2, 1)), 0, 0, 0)

    return pl.pallas_call(
        _na_kernel,
        grid=(b, n_blocks),
        in_specs=[pl.BlockSpec((1, nq, w), lambda i, r: (i, r, 0)),
                  pl.BlockSpec((1, t, w), lambda i, r: (i, 0, 0)),
                  pl.BlockSpec((1, t, 2 * w), lambda i, r: (i, 0, 0)),
                  pl.BlockSpec((1, tc, w), lambda i, r: (i, 0, 0)),
                  pl.BlockSpec((1, tc, 2 * w), lambda i, r: (i, 0, 0)),
                  pl.BlockSpec((1, NA_HEADS // 2, 2 * nq, NA_KROWS * GRID_W), bias_map)],
        out_specs=pl.BlockSpec((1, nq, w), lambda i, r: (i, r, 0)),
        out_shape=jax.ShapeDtypeStruct((b, t, w), BF16),
        compiler_params=_params(2),
        name="na_attn",
    )(q, k, v, kc, vc, bias)


GDN_MAX_BLOCK = 16 * GDN_CHUNK
GDN_GROUP_CHUNKS = 4
GDN_PAIRS = GDN_HEADS // 2
TN_DIMS = (((0,), (0,)), ((), ()))


def _split3(x):
    hi = x.astype(BF16)
    r = x - hi.astype(F32)
    mid = r.astype(BF16)
    lo = (r - mid.astype(F32)).astype(BF16)
    return hi, mid, lo


def _bd(x):
    lane = lax.broadcasted_iota(jnp.int32, x.shape, 1)
    z = jnp.zeros_like(x)
    return jnp.concatenate([jnp.where(lane < HEAD_W, x, z), jnp.where(lane >= HEAD_W, x, z)], axis=0)


def _dotb(a, b):
    return jnp.dot(a.astype(BF16), b.astype(BF16), preferred_element_type=F32)


def _mm_pair(x, y):
    return _dotb(x, _bd(y))


def _softplus(x):
    return jnp.maximum(x, 0.0) + jnp.log(1.0 + jnp.exp(-jnp.abs(x)))


def _gdn_kernel(qkv_ref, misc_ref, alog_ref, dtb_ref, eg_ref, eb_ref, tri3_ref, s0_ref, o_ref, sfin_ref, s_ref, *,
                rev, n_blk):
    step = pl.program_id(1)
    c_len = GDN_CHUNK
    hw = GDN_HEADS * HEAD_W

    @pl.when(step == 0)
    def _():
        s_ref[...] = s0_ref[0]

    qn = qkv_ref[0, :, 0:hw].astype(F32)
    kn = qkv_ref[0, :, hw:2 * hw].astype(F32)
    v = qkv_ref[0, :, 2 * hw:3 * hw].astype(F32)

    misc = misc_ref[0]
    g_all = -jnp.exp(alog_ref[...]) * _softplus(misc + dtb_ref[...])
    b_all = jax.nn.sigmoid(misc)
    gx = jnp.dot(jnp.concatenate(_split3(g_all), axis=1), eg_ref[...], preferred_element_type=F32)
    bx = jnp.dot(jnp.concatenate(_split3(b_all), axis=1), eb_ref[...], preferred_element_type=F32)

    row = lax.broadcasted_iota(jnp.int32, (c_len, LANES), 0)
    colp = lax.broadcasted_iota(jnp.int32, (c_len, LANES), 1) & (HEAD_W - 1)
    tri = (row <= colp) if rev else (row >= colp)
    strict = (row < colp) if rev else (row > colp)
    eye_f = (row == colp).astype(F32)
    row_w = lax.broadcasted_iota(jnp.int32, (c_len, hw), 0)
    col_w = lax.broadcasted_iota(jnp.int32, (c_len, hw), 1) & (HEAD_W - 1)
    eye_w = row_w == col_w
    bd_row = lax.broadcasted_iota(jnp.int32, (LANES, LANES), 0)
    bd_col = lax.broadcasted_iota(jnp.int32, (LANES, LANES), 1)
    bd_mask = (bd_row < HEAD_W) == (bd_col < HEAD_W)

    n_chunks = qkv_ref.shape[1] // c_len
    pairs = range(GDN_PAIRS)
    sl = [slice(p * LANES, (p + 1) * LANES) for p in pairs]
    rows_of = lambda c: slice(c * c_len, (c + 1) * c_len)
    gcs, kbs, aqs, grs, gtot, vb, kbg, qd, kd, ms, qks, xs, inner, uws = ({} for _ in range(14))
    state = [s_ref[p] for p in pairs]

    def prep_stages(chunks):
        keys = [(c, p) for c in chunks for p in pairs]

        def decay_sums():
            for c in chunks:
                gcs[c] = jnp.dot(tri3_ref[...], jnp.concatenate(_split3(gx[rows_of(c)]), axis=0),
                                 preferred_element_type=F32)
                kbs[c] = kn[rows_of(c)] * bx[rows_of(c)]

        def gram():
            for c, p in keys:
                r = rows_of(c)
                lhs = jnp.concatenate([kbs[c][:, sl[p]], qn[r][:, sl[p]]], axis=0)
                aqs[c, p] = lax.dot_general(lhs.astype(BF16), _bd(kn[r][:, sl[p]]).astype(BF16), NT_DIMS,
                                            preferred_element_type=F32)

        def decay_rows():
            for c in chunks:
                grs[c] = jnp.sum(jnp.where(eye_w, gcs[c], 0.0), axis=0, keepdims=True)

        def masks():
            for c in chunks:
                r, gc = rows_of(c), gcs[c]
                gtot[c] = gc[0:1] if rev else gc[c_len - 1:c_len]
                egc = jnp.exp(gc)
                vb[c] = v[r] * bx[r]
                kbg[c] = kbs[c] * egc
                qd[c] = qn[r] * egc
                kd[c] = kn[r] * jnp.exp(gtot[c] - gc)
                for p in pairs:
                    diff = gc[:, sl[p]] - grs[c][:, sl[p]]
                    decay = jnp.where(tri, jnp.exp(jnp.where(tri, diff, 0.0)), 0.0)
                    ms[c, p] = jnp.where(strict, aqs[c, p][0:c_len] * decay, 0.0)
                    qks[c, p] = aqs[c, p][c_len:2 * c_len] * decay

        def join_mask(lvl):
            rblk, cblk = row >> lvl, colp >> lvl
            sib = (cblk == rblk + 1) if rev else (cblk == rblk - 1)
            return sib & ((rblk & 1) == (0 if rev else 1))

        def level0():
            for key in keys:
                xs[key] = eye_f - jnp.where(join_mask(0), ms[key], 0.0)

        def first_product(lvl):
            def run():
                for key in keys:
                    inner[key] = _mm_pair(jnp.where(join_mask(lvl), ms[key], 0.0), xs[key])
            return run

        def second_product():
            for key in keys:
                xs[key] = xs[key] - _mm_pair(xs[key], inner[key])

        def apply_inverse():
            for c, p in keys:
                rhs = jnp.concatenate([_bd(vb[c][:, sl[p]]), _bd(kbg[c][:, sl[p]])], axis=1)
                uws[c, p] = _dotb(xs[c, p], rhs)

        stages = [decay_sums, gram, decay_rows, masks, level0]
        for lvl in range(1, 6):
            stages += [first_product(lvl), second_product]
        return stages + [apply_inverse]

    def scan_stages(chunks):
        v_new = {}
        wq = {}

        def read_state(c):
            def run():
                for p in pairs:
                    wq[p] = _dotb(jnp.concatenate([uws[c, p][:, LANES:2 * LANES], qd[c][:, sl[p]]], axis=0), state[p])
                    v_new[p] = uws[c, p][:, 0:LANES] - wq[p][0:c_len]
            return run

        def write_state(c):
            def run():
                upd = [lax.dot_general(kd[c][:, sl[p]].astype(BF16), v_new[p].astype(BF16), TN_DIMS,
                                       preferred_element_type=F32) for p in pairs]
                for p in pairs:
                    o_ref[0, rows_of(c), sl[p]] = wq[p][c_len:2 * c_len] + _dotb(qks[c, p], _bd(v_new[p]))
                    state[p] = state[p] * jnp.exp(gtot[c][:, sl[p]]) + jnp.where(bd_mask, upd[p], 0.0)
            return run

        stages = []
        for c in chunks:
            stages += [read_state(c), write_state(c)]
        return stages

    order = list(range(n_chunks - 1, -1, -1) if rev else range(n_chunks))
    halves = [order[i:i + GDN_GROUP_CHUNKS] for i in range(0, n_chunks, GDN_GROUP_CHUNKS)]
    for stage in prep_stages(halves[0]):
        stage()
    for half, nxt in zip(halves, halves[1:] + [None]):
        scan = scan_stages(half)
        prep = prep_stages(nxt) if nxt else []
        done = 0
        for i, stage in enumerate(prep):
            stage()
            while done < len(scan) and done * len(prep) < (i + 1) * len(scan):
                scan[done]()
                done += 1
        for stage in scan[done:]:
            stage()
    for p in range(GDN_PAIRS):
        s_ref[p] = state[p]

    @pl.when(step == n_blk - 1)
    def _():
        for p in range(GDN_PAIRS):
            sfin_ref[0, p] = state[p]


def _gdn_consts(a_log, dt_bias, d, rev):
    pad = jnp.zeros((LANES - 2 * GDN_HEADS,), F32)
    alog = jnp.concatenate([a_log.reshape(-1).astype(F32), pad]).reshape(1, LANES)
    dtb = jnp.concatenate([dt_bias.reshape(-1).astype(F32), pad]).reshape(1, LANES)
    hw = GDN_HEADS * HEAD_W
    head_of_lane = np.arange(hw) // HEAD_W
    src = np.arange(LANES)[:, None]
    eg = (src == d * GDN_HEADS + head_of_lane[None, :]).astype(np.float32)
    eb = (src == 2 * GDN_HEADS + d * GDN_HEADS + head_of_lane[None, :]).astype(np.float32)
    i = np.arange(GDN_CHUNK)
    tri = (i[:, None] <= i[None, :]) if rev else (i[:, None] >= i[None, :])
    as_bf = lambda a, reps, ax: jnp.asarray(np.concatenate([a] * reps, axis=ax), BF16)
    return alog, dtb, as_bf(eg, 3, 0), as_bf(eb, 3, 0), as_bf(tri.astype(np.float32), 3, 1)


def _gdn_call(qkv, misc, consts, s0, rev, name):
    b, t, w = qkv.shape
    blk = min(t, GDN_MAX_BLOCK)
    assert t % blk == 0 and blk % GDN_CHUNK == 0
    n_blk = t // blk
    hw = GDN_HEADS * HEAD_W
    blk_of = (lambda s: n_blk - 1 - s) if rev else (lambda s: s)
    const = lambda a: pl.BlockSpec(a.shape, lambda i, s: (0,) * a.ndim)
    s_spec = pl.BlockSpec((1, GDN_PAIRS, LANES, LANES), lambda i, s: (i, 0, 0, 0))
    return pl.pallas_call(
        functools.partial(_gdn_kernel, rev=rev, n_blk=n_blk),
        grid=(b, n_blk),
        in_specs=[pl.BlockSpec((1, blk, w), lambda i, s: (i, blk_of(s), 0)),
                  pl.BlockSpec((1, blk, LANES), lambda i, s: (i, blk_of(s), 0))]
                 + [const(a) for a in consts] + [s_spec],
        out_specs=[pl.BlockSpec((1, blk, hw), lambda i, s: (i, blk_of(s), 0)), s_spec],
        out_shape=[jax.ShapeDtypeStruct((b, t, hw), F32),
                   jax.ShapeDtypeStruct((b, GDN_PAIRS, LANES, LANES), F32)],
        scratch_shapes=[pltpu.VMEM((GDN_PAIRS, LANES, LANES), F32)],
        compiler_params=_params(2),
        name=name,
    )(qkv, misc, *consts, s0)


MIX_SUB_ROWS = 256


def _mix_ffn_kernel(*refs, ff_chunks, gdn_inputs, n_sub):
    tm = refs[0].shape[1]
    rows = [slice(i * (tm // n_sub), (i + 1) * (tm // n_sub)) for i in range(n_sub)]
    if gdn_inputs:
        (h_ref, of_ref, ob_ref, gate_ref, att_ref, on_ref, ones_ref, gtm_ref, shf_ref, scf_ref, gtf_ref, gpm_ref,
         gpf_ref, gqf_ref, wo_ref, wg_ref, wu_ref, wd_ref, o_ref) = refs
        mixes = []
        for r in rows:
            o = of_ref[0, r, :] + ob_ref[0, r, :]
            gate = gate_ref[0, r, :].astype(F32)
            a = o * lax.rsqrt(_seg_mean_sq(o, ones_ref) + NORM_EPS) * on_ref[...] * (gate * jax.nn.sigmoid(gate))
            mixes.append(jnp.concatenate([a.astype(BF16), att_ref[0, r, :]], axis=1))
    else:
        (h_ref, a_ref, b_ref, gtm_ref, shf_ref, scf_ref, gtf_ref, gpm_ref, gpf_ref, gqf_ref, wo_ref, wg_ref,
         wu_ref, wd_ref, o_ref) = refs
        mixes = [jnp.concatenate([a_ref[0, r, :], b_ref[0, r, :]], axis=1) for r in rows]
    ys = [_dotf(mix, wo_ref[...]) for mix in mixes]
    h1s = [h_ref[0, r, :] + gtm_ref[0] * (_rms(y) * gpm_ref[...]) for r, y in zip(rows, ys)]
    us = [(_rms(h1) * gpf_ref[...] * (1.0 + scf_ref[0]) + shf_ref[0]).astype(BF16) for h1 in h1s]
    acts = []
    for o, n in ff_chunks:
        for u in us:
            gg = _dotf(u, wg_ref[:, o:o + n])
            uu = _dotf(u, wu_ref[:, o:o + n])
            acts.append((gg * jax.nn.sigmoid(gg) * uu).astype(BF16))
    fs = [None] * n_sub
    for ci, (o, n) in enumerate(ff_chunks):
        for i in range(n_sub):
            part = _dotf(acts[ci * n_sub + i], wd_ref[o:o + n, :])
            fs[i] = part if fs[i] is None else fs[i] + part
    for r, h1, f in zip(rows, h1s, fs):
        o_ref[0, r, :] = h1 + gtf_ref[0] * (_rms(f) * gqf_ref[...])


def _mix_ffn_call(h, mix_inputs, out_norm, gt_m, sh_f, sc_f, gt_f, g_post_mix, g_pre_ffn, g_post_ffn, wo, wg, wu,
                  wd, tm, name):
    b, t, d = h.shape
    dm = wo.shape[0]
    ff = wg.shape[1]
    gdn_inputs = out_norm is not None
    half = pl.cdiv(ff // 2, MXU_TILE) * MXU_TILE
    assert 0 < half < ff and ff % LANES == 0
    ff_chunks = ((0, half), (half, ff - half))
    tok = lambda n: pl.BlockSpec((1, tm, n), lambda i, j: (i, j, 0))
    mods = [_mod_operand(m, d) for m in (gt_m, sh_f, sc_f, gt_f)]
    vec = pl.BlockSpec((1, d), lambda i, j: (0, 0))
    const = lambda shape: pl.BlockSpec(shape, lambda i, j: (0, 0), pipeline_mode=pl.Buffered(1))
    r2 = lambda a: a.reshape(1, d).astype(F32)
    mix_specs = [tok(m.shape[-1]) for m in mix_inputs]
    mix_args = list(mix_inputs)
    if gdn_inputs:
        hw = GDN_HEADS * GDN_DV
        on = jnp.tile(out_norm.astype(F32), GDN_HEADS).reshape(1, hw)
        ones = _seg_ones(hw)
        mix_specs += [pl.BlockSpec(on.shape, lambda i, j: (0, 0)), pl.BlockSpec(ones.shape, lambda i, j: (0, 0))]
        mix_args += [on, ones]
    return pl.pallas_call(
        functools.partial(_mix_ffn_kernel, ff_chunks=ff_chunks, gdn_inputs=gdn_inputs,
                          n_sub=tm // MIX_SUB_ROWS),
        grid=(b, t // tm),
        in_specs=[tok(d)] + mix_specs + [spec for _, spec in mods]
                 + [vec, vec, vec, const((dm, d)), const((d, ff)), const((d, ff)), const((ff, d))],
        out_specs=tok(d),
        out_shape=jax.ShapeDtypeStruct((b, t, d), F32),
        compiler_params=_params(2),
        name=name,
    )(h, *mix_args, *[arr for arr, _ in mods], r2(g_post_mix), r2(g_pre_ffn), r2(g_post_ffn), wo, wg, wu, wd)


def _rope_tables(n_tokens, rot_dim):
    t = jnp.arange(n_tokens, dtype=jnp.int32)
    row = (t // GRID_W).astype(F32)
    col = (t % GRID_W).astype(F32)
    n_freq = rot_dim // 4
    inv_freq = ROPE_THETA ** (-jnp.arange(n_freq, dtype=F32) / n_freq)
    ang = jnp.concatenate([row[:, None] * inv_freq, col[:, None] * inv_freq], -1)
    cos = jnp.concatenate([jnp.cos(ang), jnp.cos(ang)], -1)
    sin = jnp.concatenate([-jnp.sin(ang), jnp.sin(ang)], -1)
    if rot_dim == HEAD_W:
        return jnp.tile(cos, (1, LANES // rot_dim)), jnp.tile(sin, (1, LANES // rot_dim))
    pad = LANES - rot_dim
    return (jnp.concatenate([cos, jnp.ones((n_tokens, pad), F32)], -1),
            jnp.concatenate([sin, jnp.zeros((n_tokens, pad), F32)], -1))


def _no_rope_tables(n_tokens):
    return jnp.ones((n_tokens, LANES), F32), jnp.zeros((n_tokens, LANES), F32)


def _gdn_bidirectional(lat_qkv, lat_misc, ctx_qkv, ctx_misc, a_log, dt_bias):
    bsz = lat_qkv.shape[0]
    s_zero = jnp.zeros((bsz, GDN_PAIRS, LANES, LANES), F32)
    lat, ctx = [], []
    for d, rev in ((0, False), (1, True)):
        consts = _gdn_consts(a_log, dt_bias, d, rev)
        o_c, s_c = _gdn_call(ctx_qkv, ctx_misc, consts, s_zero, rev, "gdn_ctx_%d" % d)
        o_l, _ = _gdn_call(lat_qkv, lat_misc, consts, s_c, rev, "gdn_lat_%d" % d)
        lat.append(o_l)
        ctx.append(o_c)
    return lat, ctx


LOG2E = 1.4426950408889634
ATTN_TQ = 1024
ATTN_LOOKAHEAD = 2


def _attn_chunks(l_len, s_len, size):
    return [(0, 0, l_len)] + [(1, o, size) for o in range(0, s_len, size)]


def _even_mixer(h, hc, mods, mods_c, g_pre, w_in, conv_w, a_log, dt_bias, q_norm, w_q_up, kv_norm, w_kv_up):
    s_len, l_len = h.shape[1], hc.shape[1]
    weights = _even_weights(w_in, q_norm, w_q_up, kv_norm, w_kv_up)
    qkv_l, gate_l, misc_l, q_l, k_l, v_l = _in_even_call(h, g_pre, *mods, weights, conv_w,
                                                         *_rope_tables(s_len, MLA_ROPE), 512, "in_even_lat")
    qkv_c, gate_c, misc_c, q_c, k_c, v_c = _in_even_call(hc, g_pre, *mods_c, weights, conv_w,
                                                         *_no_rope_tables(l_len), l_len, "in_even_ctx")
    o_lat, o_ctx = _gdn_bidirectional(qkv_l, misc_l, qkv_c, misc_c, a_log, dt_bias)
    b_lat = _attn_call(q_l, [(k_c, v_c), (k_l, v_l)], MLA_QC, MLA_PAIRS, MLA_PAIRS, _attn_chunks(l_len, s_len, 512),
                       ATTN_TQ, False, "mla_attn_lat")
    b_ctx = _attn_call(q_c, [(k_c, v_c)], MLA_QC, MLA_PAIRS, MLA_PAIRS, [(0, 0, l_len)], l_len, False,
                       "mla_attn_ctx")
    return (o_lat[0], o_lat[1], gate_l, b_lat), (o_ctx[0], o_ctx[1], gate_c, b_ctx)


def _odd_mixer_last(h, hc, mods, mods_c, g_pre, w_in, rpb, q_norm, k_norm):
    s_len, l_len = h.shape[1], hc.shape[1]
    w = w_in.astype(BF16)
    q_gain = (jnp.tile(q_norm.astype(F32), GQA_HEADS) * (GQA_DIM ** -0.5 * LOG2E)).reshape(1, -1)
    k_gain = jnp.tile(k_norm.astype(F32), GQA_KV_HEADS).reshape(1, -1)
    naq_l, nak_l, nav_l, gq_l, gk_l, gv_l = _in_odd_call(h, g_pre, *mods, w, q_gain, k_gain,
                                                         *_rope_tables(s_len, GQA_DIM), 512, "in_odd_lat")
    _, nak_c, nav_c, _, gk_c, gv_c = _in_odd_call(hc, g_pre, *mods_c, w, q_gain, k_gain, *_no_rope_tables(l_len),
                                                  l_len, "in_odd_ctx")
    c_lat = _na_call(naq_l, nak_l, nav_l, nak_c, nav_c, _na_bias_table(rpb, s_len // GRID_W))
    d_lat = _attn_call(gq_l, [(gk_c, gv_c), (gk_l, gv_l)], LANES, GQA_HEADS // 2, GQA_KV_HEADS,
                       _attn_chunks(l_len, s_len, 256), ATTN_TQ, True, "gqa_attn")
    return c_lat, d_lat


def kernel(x, c, ctx, c_ctx, w_mod, b_mod, g_pre_mix, g_post_mix, g_pre_ffn, g_post_ffn, w_ffn_gate, w_ffn_up,
           w_ffn_down, w_in_even, w_out_even, gdn_conv, gdn_a_log, gdn_dt_bias, gdn_out_norm, mla_q_norm,
           mla_w_q_up, mla_kv_norm, mla_w_kv_up, w_in_odd, w_out_odd, na_rpb, gqa_q_norm, gqa_k_norm):
    bsz, s_len, d = x.shape
    depth = w_mod.shape[0]
    assert depth == 2, "layer 0 = even mixer with context update, layer 1 = odd mixer (last)"
    c_rows = jnp.concatenate([c, c_ctx[None, :], jnp.zeros((16 - bsz - 1, d), F32)], 0)
    h, hc = x, ctx
    for i in range(depth):
        mod_tab = _mod_call(c_rows, w_mod[i].astype(BF16), b_mod[i]).reshape(c_rows.shape[0] * N_MOD, 1, d)
        sh_m, sc_m, gt_m, sh_f, sc_f, gt_f = (_Mod(mod_tab, k, None) for k in range(N_MOD))
        csh_m, csc_m, cgt_m, csh_f, csc_f, cgt_f = (_Mod(mod_tab, k, bsz) for k in range(N_MOD))
        wg, wu, wd = w_ffn_gate[i].astype(BF16), w_ffn_up[i].astype(BF16), w_ffn_down[i].astype(BF16)
        if i == 0:
            mix, mix_c = _even_mixer(h, hc, (sh_m, sc_m), (csh_m, csc_m), g_pre_mix[i], w_in_even[0], gdn_conv[0],
                                     gdn_a_log[0], gdn_dt_bias[0], mla_q_norm[0], mla_w_q_up[0], mla_kv_norm[0],
                                     mla_w_kv_up[0])
            wo, out_norm = w_out_even[0].astype(BF16), gdn_out_norm[0]
            hc = _mix_ffn_call(hc, mix_c, out_norm, cgt_m, csh_f, csc_f, cgt_f, g_post_mix[i], g_pre_ffn[i],
                               g_post_ffn[i], wo, wg, wu, wd, 256, "mix_ffn_ctx")
        else:
            mix = _odd_mixer_last(h, hc, (sh_m, sc_m), (csh_m, csc_m), g_pre_mix[i], w_in_odd[0], na_rpb[0],
                                  gqa_q_norm[0], gqa_k_norm[0])
            wo, out_norm = w_out_odd[0].astype(BF16), None
        h = _mix_ffn_call(h, mix, out_norm, gt_m, sh_f, sc_f, gt_f, g_post_mix[i], g_pre_ffn[i], g_post_ffn[i],
                          wo, wg, wu, wd, 2 * MIX_SUB_ROWS, "mix_ffn_lat%d" % i)
    return h
```

```python
import functools
from typing import NamedTuple, Optional

import numpy as np
import jax
import jax.numpy as jnp
from jax import lax
from jax.experimental import pallas as pl
from jax.experimental.pallas import tpu as pltpu

F32 = jnp.float32
BF16 = jnp.bfloat16

GRID_W = 64
NORM_EPS = 1e-6
ROPE_THETA = 10000.0

GDN_HEADS = 8
GDN_DK = 64
GDN_DV = 64
GDN_CHUNK = 64
GDN_QK_W = GDN_HEADS * GDN_DK
GDN_V_W = GDN_HEADS * GDN_DV

MLA_HEADS = 8
MLA_NOPE = 64
MLA_ROPE = 32
MLA_V = 64
MLA_Q_LORA = 256
MLA_KV_LORA = 128

NA_HEADS = 8
NA_DIM = 64
NA_WIN_H = 8
NA_WIN_W = 16
NA_W = NA_HEADS * NA_DIM

GQA_HEADS = 8
GQA_KV_HEADS = 2
GQA_DIM = 64

LANES = 128
MXU_TILE = 256
HEAD_W = 64
MASK_VALUE = -1e30
VMEM_LIMIT = 56 << 20

NT_DIMS = (((1,), (1,)), ((), ()))


def _params(n_grid, vmem=VMEM_LIMIT):
    return pltpu.CompilerParams(dimension_semantics=("arbitrary",) * n_grid, vmem_limit_bytes=vmem)


def _rms(x):
    return x * lax.rsqrt(jnp.mean(x * x, axis=-1, keepdims=True) + NORM_EPS)


def _mod_kernel(c_ref, w_ref, b_ref, o_ref):
    c = c_ref[...]
    a = (c * jax.nn.sigmoid(c)).astype(BF16)
    o_ref[...] = jnp.dot(a, w_ref[...], preferred_element_type=F32) + b_ref[...]


def _mod_call(c_rows, w, b):
    r, d = c_rows.shape
    n = w.shape[1]
    tn = 1024
    return pl.pallas_call(
        _mod_kernel,
        grid=(n // tn,),
        in_specs=[pl.BlockSpec((r, d), lambda j: (0, 0)),
                  pl.BlockSpec((d, tn), lambda j: (0, j)),
                  pl.BlockSpec((1, tn), lambda j: (0, j))],
        out_specs=pl.BlockSpec((r, tn), lambda j: (0, j)),
        out_shape=jax.ShapeDtypeStruct((r, n), F32),
        compiler_params=_params(1),
        name="mod",
    )(c_rows, w, b.reshape(1, n))


class _Mod(NamedTuple):
    table: jax.Array
    comp: int
    row: Optional[int]


N_MOD = 6


def _mod_operand(m, d):
    if m.row is None:
        return m.table, pl.BlockSpec((1, 1, d), lambda i, j: (i * N_MOD + m.comp, 0, 0))
    return m.table, pl.BlockSpec((1, 1, d), lambda i, j: (m.row * N_MOD + m.comp, 0, 0))


def _norm_mod(x_ref, g_ref, sh_ref, sc_ref):
    return (_rms(x_ref[0]) * g_ref[...] * (1.0 + sc_ref[0]) + sh_ref[0]).astype(BF16)


def _dotf(a, b):
    return jnp.dot(a, b, preferred_element_type=F32)


def _seg_mean_sq(x, ones_ref):
    return _dotf((x * x).astype(BF16), ones_ref[...]) * (1.0 / HEAD_W)


def _rotate(t, cos, sin_signed, half):
    lane = lax.broadcasted_iota(jnp.int32, t.shape, 1)
    n = t.shape[1]
    partner = jnp.where((lane & (2 * half - 1)) < half, pltpu.roll(t, n - half, 1), pltpu.roll(t, half, 1))
    return t * cos + partner * sin_signed


EVEN_QKV_W = 2 * GDN_QK_W + GDN_V_W
EVEN_GROUPS = (EVEN_QKV_W, GDN_V_W, MLA_Q_LORA, MLA_KV_LORA, LANES, LANES)
MLA_QC = 2 * LANES
MLA_PAIRS = MLA_HEADS // 2


CONV_HALO = 16


def _in_even_kernel(x_ref, xp_ref, xn_ref, g_ref, sh_ref, sc_ref, w_ref, qg_ref, wq_ref, kvg_ref, wk_ref, wv_ref,
                    vone_ref, cos_ref, sin_ref, cw_ref, ones_ref, qkv_ref, gate_ref, misc_ref, q_ref, k_ref, v_ref,
                    zs_ref, *, n_tap):
    j = pl.program_id(1)
    tm = x_ref.shape[1]
    u = _norm_mod(x_ref, g_ref, sh_ref, sc_ref)
    offs = np.cumsum((0,) + EVEN_GROUPS)
    grp = lambda i: _dotf(u, w_ref[:, int(offs[i]):int(offs[i + 1])])
    u_ext = jnp.concatenate([_norm_mod(xp_ref, g_ref, sh_ref, sc_ref), u, _norm_mod(xn_ref, g_ref, sh_ref, sc_ref)],
                            axis=0)
    z = _dotf(u_ext, w_ref[:, 0:EVEN_QKV_W])
    zs_ref[0:CONV_HALO, :] = z[0:CONV_HALO] * (j > 0).astype(F32)
    zs_ref[CONV_HALO:CONV_HALO + tm, :] = z[CONV_HALO:CONV_HALO + tm]
    zs_ref[CONV_HALO + tm:, :] = z[CONV_HALO + tm:] * (j < pl.num_programs(1) - 1).astype(F32)
    q_down, pe_raw = grp(2), grp(5)
    kv_misc = _dotf(u, w_ref[:, int(offs[3]):int(offs[5])])
    kv_down, misc = kv_misc[:, 0:MLA_KV_LORA], kv_misc[:, MLA_KV_LORA:]
    gate_ref[0] = grp(1).astype(BF16)
    misc_ref[0] = misc
    cos, sin = cos_ref[...], sin_ref[...]
    qn = (_rms(q_down) * qg_ref[...]).astype(BF16)
    kvn = (_rms(kv_down) * kvg_ref[...]).astype(BF16)
    for h in range(MLA_HEADS):
        blk = _dotf(qn, wq_ref[:, h * MLA_QC:(h + 1) * MLA_QC])
        q_ref[0, :, h * MLA_QC:h * MLA_QC + LANES] = blk[:, 0:LANES].astype(BF16)
        q_ref[0, :, h * MLA_QC + LANES:(h + 1) * MLA_QC] = _rotate(blk[:, LANES:], cos, sin,
                                                                    MLA_ROPE // 2).astype(BF16)
    pe = _rotate(pe_raw, cos, sin, MLA_ROPE // 2).astype(BF16)
    k_nope = _dotf(kvn, wk_ref[...])
    for p in range(MLA_PAIRS):
        k_ref[0, :, p * MLA_QC:p * MLA_QC + LANES] = k_nope[:, p * LANES:(p + 1) * LANES].astype(BF16)
        k_ref[0, :, p * MLA_QC + LANES:(p + 1) * MLA_QC] = pe
    v_ref[0] = (_dotf(kvn, wv_ref[...]) + vone_ref[...]).astype(BF16)
    acc = None
    for t in range(n_tap):
        start = CONV_HALO - n_tap // 2 + t
        term = zs_ref[start:start + tm, :] * cw_ref[t:t + 1, :]
        acc = term if acc is None else acc + term
    y = acc * jax.nn.sigmoid(acc)
    hw = GDN_QK_W
    qc, kc = y[:, 0:hw], y[:, hw:2 * hw]
    q_ss = _seg_mean_sq(qc, ones_ref) * HEAD_W
    k_ss = _seg_mean_sq(kc, ones_ref) * HEAD_W
    qkv_ref[0, :, 0:hw] = (qc * lax.rsqrt(q_ss + NORM_EPS) * (GDN_DK ** -0.5)).astype(BF16)
    qkv_ref[0, :, hw:2 * hw] = (kc * lax.rsqrt(k_ss + NORM_EPS)).astype(BF16)
    qkv_ref[0, :, 2 * hw:] = y[:, 2 * hw:].astype(BF16)


def _even_weights(w_in, q_norm, w_q_up, kv_norm, w_kv_up):
    d = w_in.shape[0]
    o0 = EVEN_QKV_W + GDN_V_W
    n_ab = 4 * GDN_HEADS
    o1 = o0 + n_ab
    o2 = o1 + MLA_Q_LORA
    o3 = o2 + MLA_KV_LORA
    zeros = lambda n, rows=d: jnp.zeros((rows, n), F32)
    w_cat = jnp.concatenate([w_in[:, :o0], w_in[:, o1:o2], w_in[:, o2:o3],
                             w_in[:, o0:o1], zeros(LANES - n_ab),
                             w_in[:, o3:], zeros(LANES - MLA_ROPE)], 1).astype(BF16)
    qh = MLA_NOPE + MLA_ROPE
    cols = []
    for h in range(MLA_HEADS):
        nope = w_q_up[:, h * qh:h * qh + MLA_NOPE]
        rp = w_q_up[:, h * qh + MLA_NOPE:(h + 1) * qh]
        z = zeros(HEAD_W, MLA_Q_LORA)
        cols += ([nope, z] if h % 2 == 0 else [z, nope]) + [rp, zeros(LANES - MLA_ROPE, MLA_Q_LORA)]
    wq = jnp.concatenate(cols, 1).astype(BF16)
    kvh = MLA_NOPE + MLA_V
    wk = jnp.concatenate([w_kv_up[:, h * kvh:h * kvh + MLA_NOPE] for h in range(MLA_HEADS)], 1).astype(BF16)
    vcols, ones = [], []
    for h in range(MLA_HEADS):
        vh = w_kv_up[:, h * kvh + MLA_NOPE:(h + 1) * kvh]
        z = zeros(HEAD_W, MLA_KV_LORA)
        vcols += [vh, z] if h % 2 == 0 else [z, vh]
        ones += [0.0, 1.0] if h % 2 == 0 else [1.0, 0.0]
    wv = jnp.concatenate(vcols, 1).astype(BF16)
    vone = jnp.asarray(np.repeat(np.asarray(ones, np.float32), HEAD_W)[None, :])
    q_gain = (q_norm * ((MLA_NOPE + MLA_ROPE) ** -0.5 * LOG2E)).reshape(1, -1).astype(F32)
    return w_cat, q_gain, wq, kv_norm.reshape(1, -1).astype(F32), wk, wv, vone


def _in_even_call(x, g, shift, scale, weights, conv_w, cos, sin, tm, name):
    b, t, d = x.shape
    w_cat, q_gain, wq, kv_gain, wk, wv, vone = weights
    n_tap = conv_w.shape[0]
    cw = jnp.concatenate([conv_w.astype(F32), jnp.zeros((8 - n_tap, conv_w.shape[1]), F32)], 0)
    ones = _seg_ones(GDN_QK_W)
    per = tm // CONV_HALO
    n_halo = t // CONV_HALO
    const = lambda a: pl.BlockSpec(a.shape, lambda i, j: (0,) * a.ndim)
    (sh_arr, sh_spec), (sc_arr, sc_spec) = _mod_operand(shift, d), _mod_operand(scale, d)
    tab = pl.BlockSpec((tm, LANES), lambda i, j: (j, 0))
    widths = (EVEN_QKV_W, GDN_V_W, LANES, MLA_HEADS * MLA_QC, MLA_PAIRS * MLA_QC, MLA_HEADS * LANES)
    dts = (BF16, BF16, F32, BF16, BF16, BF16)
    g2 = g.reshape(1, d).astype(F32)
    return pl.pallas_call(
        functools.partial(_in_even_kernel, n_tap=n_tap),
        grid=(b, t // tm),
        in_specs=[pl.BlockSpec((1, tm, d), lambda i, j: (i, j, 0)),
                  pl.BlockSpec((1, CONV_HALO, d), lambda i, j: (i, jnp.maximum(j * per - 1, 0), 0)),
                  pl.BlockSpec((1, CONV_HALO, d), lambda i, j: (i, jnp.minimum((j + 1) * per, n_halo - 1), 0)),
                  const(g2), sh_spec, sc_spec, const(w_cat), const(q_gain), const(wq), const(kv_gain), const(wk),
                  const(wv), const(vone), tab, tab, const(cw), const(ones)],
        out_specs=[pl.BlockSpec((1, tm, n), lambda i, j: (i, j, 0)) for n in widths],
        out_shape=[jax.ShapeDtypeStruct((b, t, n), dt) for n, dt in zip(widths, dts)],
        scratch_shapes=[pltpu.VMEM((tm + 2 * CONV_HALO, EVEN_QKV_W), F32)],
        compiler_params=_params(2),
        name=name,
    )(x, x, x, g2, sh_arr, sc_arr, w_cat, q_gain, wq, kv_gain, wk, wv, vone,
      cos, sin, cw, ones)


GQA_KV_W = GQA_KV_HEADS * GQA_DIM
PROJ_SUB_ROWS = 256


def _in_odd_kernel(x_ref, g_ref, sh_ref, sc_ref, w_ref, qg_ref, kg_ref, ones_q_ref, ones_k_ref, cos_ref, sin_ref,
                   naq_ref, nak_ref, nav_ref, gq_ref, gk_ref, gv_ref):
    tm = x_ref.shape[1]
    n_sub = max(tm // PROJ_SUB_ROWS, 1)
    subs = [slice(i * (tm // n_sub), (i + 1) * (tm // n_sub)) for i in range(n_sub)]
    us = [(_rms(x_ref[0, r, :]) * g_ref[...] * (1.0 + sc_ref[0]) + sh_ref[0]).astype(BF16) for r in subs]
    o_q = 3 * NA_W
    o_k = o_q + GQA_HEADS * GQA_DIM
    qs = [_dotf(u, w_ref[:, o_q:o_k]) for u in us]
    kvs = [_dotf(u, w_ref[:, o_k:o_k + 2 * GQA_KV_W]) for u in us]
    ks = [kv[:, 0:GQA_KV_W] for kv in kvs]
    vs = [kv[:, GQA_KV_W:] for kv in kvs]
    low = lax.broadcasted_iota(jnp.int32, (tm // n_sub, LANES), 1) < HEAD_W
    for r, u in zip(subs, us):
        naq_ref[0, r, :] = (_dotf(u, w_ref[:, 0:NA_W]) * (NA_DIM ** -0.5 * LOG2E)).astype(BF16)
        nak_ref[0, r, :] = _dotf(u, w_ref[:, NA_W:2 * NA_W]).astype(BF16)
        na_v = _dotf(u, w_ref[:, 2 * NA_W:3 * NA_W])
        for p in range(NA_HEADS // 2):
            tile = na_v[:, p * LANES:(p + 1) * LANES]
            nav_ref[0, r, 2 * p * LANES:(2 * p + 1) * LANES] = tile.astype(BF16)
            nav_ref[0, r, (2 * p + 1) * LANES:(2 * p + 2) * LANES] = jnp.ones(tile.shape, BF16)
    q_ms = [_seg_mean_sq(q, ones_q_ref) for q in qs]
    k_ms = [_seg_mean_sq(k, ones_k_ref) for k in ks]
    for i, r in enumerate(subs):
        cos, sin = cos_ref[r, :], sin_ref[r, :]
        qn = qs[i] * lax.rsqrt(q_ms[i] + NORM_EPS) * qg_ref[...]
        for p in range(GQA_HEADS // 2):
            cols = slice(p * LANES, (p + 1) * LANES)
            gq_ref[0, r, cols] = _rotate(qn[:, cols], cos, sin, GQA_DIM // 2).astype(BF16)
        kr = _rotate(ks[i] * lax.rsqrt(k_ms[i] + NORM_EPS) * kg_ref[...], cos, sin, GQA_DIM // 2)
        k_sw = pltpu.roll(kr, HEAD_W, 1)
        gk_ref[0, r, 0:LANES] = jnp.where(low, kr, k_sw).astype(BF16)
        gk_ref[0, r, LANES:2 * LANES] = jnp.where(low, k_sw, kr).astype(BF16)
        v = vs[i]
        v_sw = pltpu.roll(v, HEAD_W, 1)
        slots = (jnp.where(low, v, 1.0), jnp.where(low, 1.0, v_sw),
                 jnp.where(low, v_sw, 1.0), jnp.where(low, 1.0, v))
        for j, s in enumerate(slots):
            gv_ref[0, r, j * LANES:(j + 1) * LANES] = s.astype(BF16)


def _seg_ones(width):
    head = np.arange(width) // HEAD_W
    return jnp.asarray((head[:, None] == head[None, :]).astype(np.float32), BF16)


def _in_odd_call(x, g, shift, scale, w, q_gain, k_gain, cos, sin, tm, name):
    b, t, d = x.shape
    assert GQA_KV_W == LANES
    ones_q = _seg_ones(GQA_HEADS * GQA_DIM)
    ones_k = _seg_ones(GQA_KV_W)
    const = lambda a: pl.BlockSpec(a.shape, lambda i, j: (0,) * a.ndim)
    (sh_arr, sh_spec), (sc_arr, sc_spec) = _mod_operand(shift, d), _mod_operand(scale, d)
    tab = pl.BlockSpec((tm, LANES), lambda i, j: (j, 0))
    widths = (NA_W, NA_W, 2 * NA_W, GQA_HEADS * GQA_DIM, 2 * LANES, 4 * LANES)
    g2 = g.reshape(1, d).astype(F32)
    return pl.pallas_call(
        _in_odd_kernel,
        grid=(b, t // tm),
        in_specs=[pl.BlockSpec((1, tm, d), lambda i, j: (i, j, 0)), const(g2), sh_spec, sc_spec, const(w),
                  const(q_gain), const(k_gain), const(ones_q), const(ones_k), tab, tab],
        out_specs=[pl.BlockSpec((1, tm, n), lambda i, j: (i, j, 0)) for n in widths],
        out_shape=[jax.ShapeDtypeStruct((b, t, n), BF16) for n in widths],
        compiler_params=_params(2),
        name=name,
    )(x, g2, sh_arr, sc_arr, w, q_gain, k_gain, ones_q, ones_k, cos, sin)


def _attn_kernel(*refs, dc, chunks, n_src, masked_q):
    q_ref, kv_refs, o_ref = refs[0], refs[1:1 + 2 * n_src], refs[-1]
    tq = q_ref.shape[1]
    lane = lax.broadcasted_iota(jnp.int32, (tq, LANES), 1)
    qs = []
    for hh in range(2):
        if masked_q:
            qp = q_ref[0]
            qs.append(jnp.where((lane < HEAD_W) if hh == 0 else (lane >= HEAD_W), qp, jnp.zeros_like(qp)))
        else:
            qs.append(q_ref[0, :, hh * dc:(hh + 1) * dc])
    ms, accs = [None, None], [None, None]

    def scores(hh, src, s0, n):
        return lax.dot_general(qs[hh], kv_refs[2 * src][0, s0:s0 + n, :], NT_DIMS, preferred_element_type=F32)

    def consume(s, hh, src, s0, n):
        vv = kv_refs[2 * src + 1][0, s0:s0 + n, hh * LANES:(hh + 1) * LANES]
        mc = jnp.max(s, axis=-1, keepdims=True)
        m_new = mc if ms[hh] is None else jnp.maximum(ms[hh], mc)
        pv = _dotf(jnp.exp2(s - m_new).astype(BF16), vv)
        accs[hh] = pv if ms[hh] is None else jnp.exp2(ms[hh] - m_new) * accs[hh] + pv
        ms[hh] = m_new

    items = [(hh,) + tuple(c) for c in chunks for hh in range(2)]
    pending = []
    for it in items:
        pending.append((scores(*it), it))
        if len(pending) > ATTN_LOOKAHEAD:
            s, it0 = pending.pop(0)
            consume(s, *it0)
    for s, it0 in pending:
        consume(s, *it0)
    low = lane < HEAD_W
    num = jnp.where(low, accs[0], accs[1])
    den = pltpu.roll(jnp.where(low, accs[1], accs[0]), HEAD_W, 1)
    o_ref[0] = (num / den).astype(o_ref.dtype)


def _attn_call(q, kvs, dc, n_pairs, n_groups, chunks, tq, masked_q, name):
    b, t_q, _ = q.shape
    per = n_pairs // n_groups
    q_w = LANES if masked_q else 2 * dc
    in_specs = [pl.BlockSpec((1, tq, q_w), lambda i, p, j: (i, j, p))]
    args = [q]
    for k, v in kvs:
        in_specs += [pl.BlockSpec((1, k.shape[1], dc), lambda i, p, j: (i, 0, p // per)),
                     pl.BlockSpec((1, v.shape[1], 2 * LANES), lambda i, p, j: (i, 0, p // per))]
        args += [k, v]
    return pl.pallas_call(
        functools.partial(_attn_kernel, dc=dc, chunks=tuple(chunks), n_src=len(kvs), masked_q=masked_q),
        grid=(b, n_pairs, t_q // tq),
        in_specs=in_specs,
        out_specs=pl.BlockSpec((1, tq, LANES), lambda i, p, j: (i, j, p)),
        out_shape=jax.ShapeDtypeStruct((b, t_q, n_pairs * LANES), BF16),
        compiler_params=_params(3),
        name=name,
    )(*args)


NA_QROWS = 4
NA_KROWS = 12
NA_LOOKAHEAD = 1


def _na_kernel(q_ref, k_ref, v_ref, kc_ref, vc_ref, bias_ref, o_ref):
    rb = pl.program_id(1)
    n_rows = k_ref.shape[1] // GRID_W
    base = jnp.clip(rb * NA_QROWS - NA_WIN_H // 2, 0, n_rows - NA_KROWS) * GRID_W
    base = pl.multiple_of(base, GRID_W)
    nq = NA_QROWS * GRID_W
    nk = NA_KROWS * GRID_W
    lane = lax.broadcasted_iota(jnp.int32, (nq, LANES), 1)
    low = lane < HEAD_W

    def scores(p):
        cols = slice(p * LANES, (p + 1) * LANES)
        qp = q_ref[0, :, cols]
        zero = jnp.zeros_like(qp)
        q2 = jnp.concatenate([jnp.where(low, qp, zero), jnp.where(low, zero, qp)], axis=0)
        s_loc = lax.dot_general(q2, k_ref[0, pl.ds(base, nk), cols], NT_DIMS, preferred_element_type=F32)
        s_ctx = lax.dot_general(q2, kc_ref[0, :, cols], NT_DIMS, preferred_element_type=F32)
        return s_loc, s_ctx

    def consume(p, s_loc, s_ctx):
        vcols = slice(2 * p * LANES, (2 * p + 2) * LANES)
        s_loc = s_loc + bias_ref[0, p]
        m = jnp.maximum(jnp.max(s_loc, axis=-1, keepdims=True), jnp.max(s_ctx, axis=-1, keepdims=True))
        acc = (_dotf(jnp.exp2(s_loc - m).astype(BF16), v_ref[0, pl.ds(base, nk), vcols])
               + _dotf(jnp.exp2(s_ctx - m).astype(BF16), vc_ref[0, :, vcols]))
        even, odd = acc[0:nq], acc[nq:2 * nq]
        num = jnp.where(low, even[:, 0:LANES], odd[:, 0:LANES])
        den = jnp.where(low, even[:, LANES:2 * LANES], odd[:, LANES:2 * LANES])
        o_ref[0, :, p * LANES:(p + 1) * LANES] = (num / den).astype(o_ref.dtype)

    pending = []
    for p in range(NA_HEADS // 2):
        pending.append((p,) + scores(p))
        if len(pending) > NA_LOOKAHEAD:
            consume(*pending.pop(0))
    for item in pending:
        consume(*item)


def _na_bias_table(rpb, rows):
    n_blocks = rows // NA_QROWS
    n_h = rpb.shape[0]
    c = np.arange(GRID_W)[:, None]
    kc = np.arange(GRID_W)[None, :]
    cs = np.clip(c - NA_WIN_W // 2, 0, GRID_W - NA_WIN_W)
    col_ok = (kc >= cs) & (kc < cs + NA_WIN_W)
    dc = np.clip(kc - c + (NA_WIN_W - 1), 0, 2 * NA_WIN_W - 2)
    pick = (np.arange(2 * NA_WIN_W - 1)[:, None, None] == dc[None]).astype(np.float32)
    by_col = jnp.einsum("hdm,mck->hcdk", rpb.astype(F32) * LOG2E, jnp.asarray(pick),
                        precision=lax.Precision.HIGHEST)
    variants = []
    for r0 in (0, NA_QROWS * (n_blocks // 2), rows - NA_QROWS):
        base = int(np.clip(r0 - NA_WIN_H // 2, 0, rows - NA_KROWS))
        r = r0 + np.arange(NA_QROWS)[:, None]
        kr = base + np.arange(NA_KROWS)[None, :]
        rs = np.clip(r - NA_WIN_H // 2, 0, rows - NA_WIN_H)
        row_ok = (kr >= rs) & (kr < rs + NA_WIN_H)
        dr = np.clip(kr - r + (NA_WIN_H - 1), 0, 2 * NA_WIN_H - 2)
        full = jnp.stack([by_col[:, :, dr[i], :] for i in range(NA_QROWS)], 1)
        valid = row_ok[:, None, :, None] & col_ok[None, :, None, :]
        full = jnp.where(valid[None], full, MASK_VALUE)
        variants.append(full.reshape(n_h, NA_QROWS * GRID_W, NA_KROWS * GRID_W))
    return jnp.stack(variants, 0).reshape(3, n_h // 2, 2 * NA_QROWS * GRID_W, NA_KROWS * GRID_W)


def _na_call(q, k, v, kc, vc, bias):
    b, t, w = q.shape
    n_blocks = t // (NA_QROWS * GRID_W)
    nq = NA_QROWS * GRID_W
    tc = kc.shape[1]

    def bias_map(i, r):
        return (jnp.where(r == 0, 0, jnp.where(r == n_blocks - 1, 2, 1)), 0, 0, 0)

    return pl.pallas_call(
        _na_kernel,
        grid=(b, n_blocks),
        in_specs=[pl.BlockSpec((1, nq, w), lambda i, r: (i, r, 0)),
                  pl.BlockSpec((1, t, w), lambda i, r: (i, 0, 0)),
                  pl.BlockSpec((1, t, 2 * w), lambda i, r: (i, 0, 0)),
                  pl.BlockSpec((1, tc, w), lambda i, r: (i, 0, 0)),
                  pl.BlockSpec((1, tc, 2 * w), lambda i, r: (i, 0, 0)),
                  pl.BlockSpec((1, NA_HEADS // 2, 2 * nq, NA_KROWS * GRID_W), bias_map)],
        out_specs=pl.BlockSpec((1, nq, w), lambda i, r: (i, r, 0)),
        out_shape=jax.ShapeDtypeStruct((b, t, w), BF16),
        compiler_params=_params(2),
        name="na_attn",
    )(q, k, v, kc, vc, bias)


GDN_MAX_BLOCK = 16 * GDN_CHUNK
GDN_GROUP_CHUNKS = 4
GDN_PAIRS = GDN_HEADS // 2
TN_DIMS = (((0,), (0,)), ((), ()))


def _split3(x):
    hi = x.astype(BF16)
    r = x - hi.astype(F32)
    mid = r.astype(BF16)
    lo = (r - mid.astype(F32)).astype(BF16)
    return hi, mid, lo


def _bd(x):
    lane = lax.broadcasted_iota(jnp.int32, x.shape, 1)
    z = jnp.zeros_like(x)
    return jnp.concatenate([jnp.where(lane < HEAD_W, x, z), jnp.where(lane >= HEAD_W, x, z)], axis=0)


def _dotb(a, b):
    return jnp.dot(a.astype(BF16), b.astype(BF16), preferred_element_type=F32)


def _mm_pair(x, y):
    return _dotb(x, _bd(y))


def _softplus(x):
    return jnp.maximum(x, 0.0) + jnp.log(1.0 + jnp.exp(-jnp.abs(x)))


def _gdn_kernel(qkv_ref, misc_ref, alog_ref, dtb_ref, eg_ref, eb_ref, tri3_ref, s0_ref, o_ref, sfin_ref, s_ref, *,
                rev, n_blk):
    step = pl.program_id(1)
    c_len = GDN_CHUNK
    hw = GDN_HEADS * HEAD_W

    @pl.when(step == 0)
    def _():
        s_ref[...] = s0_ref[0]

    qn = qkv_ref[0, :, 0:hw].astype(F32)
    kn = qkv_ref[0, :, hw:2 * hw].astype(F32)
    v = qkv_ref[0, :, 2 * hw:3 * hw].astype(F32)

    misc = misc_ref[0]
    g_all = -jnp.exp(alog_ref[...]) * _softplus(misc + dtb_ref[...])
    b_all = jax.nn.sigmoid(misc)
    gx = jnp.dot(jnp.concatenate(_split3(g_all), axis=1), eg_ref[...], preferred_element_type=F32)
    bx = jnp.dot(jnp.concatenate(_split3(b_all), axis=1), eb_ref[...], preferred_element_type=F32)

    row = lax.broadcasted_iota(jnp.int32, (c_len, LANES), 0)
    colp = lax.broadcasted_iota(jnp.int32, (c_len, LANES), 1) & (HEAD_W - 1)
    tri = (row <= colp) if rev else (row >= colp)
    strict = (row < colp) if rev else (row > colp)
    eye_f = (row == colp).astype(F32)
    row_w = lax.broadcasted_iota(jnp.int32, (c_len, hw), 0)
    col_w = lax.broadcasted_iota(jnp.int32, (c_len, hw), 1) & (HEAD_W - 1)
    eye_w = row_w == col_w
    bd_row = lax.broadcasted_iota(jnp.int32, (LANES, LANES), 0)
    bd_col = lax.broadcasted_iota(jnp.int32, (LANES, LANES), 1)
    bd_mask = (bd_row < HEAD_W) == (bd_col < HEAD_W)

    n_chunks = qkv_ref.shape[1] // c_len
    pairs = range(GDN_PAIRS)
    sl = [slice(p * LANES, (p + 1) * LANES) for p in pairs]
    rows_of = lambda c: slice(c * c_len, (c + 1) * c_len)
    gcs, kbs, aqs, grs, gtot, vb, kbg, qd, kd, ms, qks, xs, inner, uws = ({} for _ in range(14))
    state = [s_ref[p] for p in pairs]

    def prep_stages(chunks):
        keys = [(c, p) for c in chunks for p in pairs]

        def decay_sums():
            for c in chunks:
                gcs[c] = jnp.dot(tri3_ref[...], jnp.concatenate(_split3(gx[rows_of(c)]), axis=0),
                                 preferred_element_type=F32)
                kbs[c] = kn[rows_of(c)] * bx[rows_of(c)]

        def gram():
            for c, p in keys:
                r = rows_of(c)
                lhs = jnp.concatenate([kbs[c][:, sl[p]], qn[r][:, sl[p]]], axis=0)
                aqs[c, p] = lax.dot_general(lhs.astype(BF16), _bd(kn[r][:, sl[p]]).astype(BF16), NT_DIMS,
                                            preferred_element_type=F32)

        def decay_rows():
            for c in chunks:
                grs[c] = jnp.sum(jnp.where(eye_w, gcs[c], 0.0), axis=0, keepdims=True)

        def masks():
            for c in chunks:
                r, gc = rows_of(c), gcs[c]
                gtot[c] = gc[0:1] if rev else gc[c_len - 1:c_len]
                egc = jnp.exp(gc)
                vb[c] = v[r] * bx[r]
                kbg[c] = kbs[c] * egc
                qd[c] = qn[r] * egc
                kd[c] = kn[r] * jnp.exp(gtot[c] - gc)
                for p in pairs:
                    diff = gc[:, sl[p]] - grs[c][:, sl[p]]
                    decay = jnp.where(tri, jnp.exp(jnp.where(tri, diff, 0.0)), 0.0)
                    ms[c, p] = jnp.where(strict, aqs[c, p][0:c_len] * decay, 0.0)
                    qks[c, p] = aqs[c, p][c_len:2 * c_len] * decay

        def join_mask(lvl):
            rblk, cblk = row >> lvl, colp >> lvl
            sib = (cblk == rblk + 1) if rev else (cblk == rblk - 1)
            return sib & ((rblk & 1) == (0 if rev else 1))

        def level0():
            for key in keys:
                xs[key] = eye_f - jnp.where(join_mask(0), ms[key], 0.0)

        def first_product(lvl):
            def run():
                for key in keys:
                    inner[key] = _mm_pair(jnp.where(join_mask(lvl), ms[key], 0.0), xs[key])
            return run

        def second_product():
            for key in keys:
                xs[key] = xs[key] - _mm_pair(xs[key], inner[key])

        def apply_inverse():
            for c, p in keys:
                rhs = jnp.concatenate([_bd(vb[c][:, sl[p]]), _bd(kbg[c][:, sl[p]])], axis=1)
                uws[c, p] = _dotb(xs[c, p], rhs)

        stages = [decay_sums, gram, decay_rows, masks, level0]
        for lvl in range(1, 6):
            stages += [first_product(lvl), second_product]
        return stages + [apply_inverse]

    def scan_stages(chunks):
        v_new = {}
        wq = {}

        def read_state(c):
            def run():
                for p in pairs:
                    wq[p] = _dotb(jnp.concatenate([uws[c, p][:, LANES:2 * LANES], qd[c][:, sl[p]]], axis=0), state[p])
                    v_new[p] = uws[c, p][:, 0:LANES] - wq[p][0:c_len]
            return run

        def write_state(c):
            def run():
                upd = [lax.dot_general(kd[c][:, sl[p]].astype(BF16), v_new[p].astype(BF16), TN_DIMS,
                                       preferred_element_type=F32) for p in pairs]
                for p in pairs:
                    o_ref[0, rows_of(c), sl[p]] = wq[p][c_len:2 * c_len] + _dotb(qks[c, p], _bd(v_new[p]))
                    state[p] = state[p] * jnp.exp(gtot[c][:, sl[p]]) + jnp.where(bd_mask, upd[p], 0.0)
            return run

        stages = []
        for c in chunks:
            stages += [read_state(c), write_state(c)]
        return stages

    order = list(range(n_chunks - 1, -1, -1) if rev else range(n_chunks))
    halves = [order[i:i + GDN_GROUP_CHUNKS] for i in range(0, n_chunks, GDN_GROUP_CHUNKS)]
    for stage in prep_stages(halves[0]):
        stage()
    for half, nxt in zip(halves, halves[1:] + [None]):
        scan = scan_stages(half)
        prep = prep_stages(nxt) if nxt else []
        done = 0
        for i, stage in enumerate(prep):
            stage()
            while done < len(scan) and done * len(prep) < (i + 1) * len(scan):
                scan[done]()
                done += 1
        for stage in scan[done:]:
            stage()
    for p in range(GDN_PAIRS):
        s_ref[p] = state[p]

    @pl.when(step == n_blk - 1)
    def _():
        for p in range(GDN_PAIRS):
            sfin_ref[0, p] = state[p]


def _gdn_consts(a_log, dt_bias, d, rev):
    pad = jnp.zeros((LANES - 2 * GDN_HEADS,), F32)
    alog = jnp.concatenate([a_log.reshape(-1).astype(F32), pad]).reshape(1, LANES)
    dtb = jnp.concatenate([dt_bias.reshape(-1).astype(F32), pad]).reshape(1, LANES)
    hw = GDN_HEADS * HEAD_W
    head_of_lane = np.arange(hw) // HEAD_W
    src = np.arange(LANES)[:, None]
    eg = (src == d * GDN_HEADS + head_of_lane[None, :]).astype(np.float32)
    eb = (src == 2 * GDN_HEADS + d * GDN_HEADS + head_of_lane[None, :]).astype(np.float32)
    i = np.arange(GDN_CHUNK)
    tri = (i[:, None] <= i[None, :]) if rev else (i[:, None] >= i[None, :])
    as_bf = lambda a, reps, ax: jnp.asarray(np.concatenate([a] * reps, axis=ax), BF16)
    return alog, dtb, as_bf(eg, 3, 0), as_bf(eb, 3, 0), as_bf(tri.astype(np.float32), 3, 1)


def _gdn_call(qkv, misc, consts, s0, rev, name):
    b, t, w = qkv.shape
    blk = min(t, GDN_MAX_BLOCK)
    assert t % blk == 0 and blk % GDN_CHUNK == 0
    n_blk = t // blk
    hw = GDN_HEADS * HEAD_W
    blk_of = (lambda s: n_blk - 1 - s) if rev else (lambda s: s)
    const = lambda a: pl.BlockSpec(a.shape, lambda i, s: (0,) * a.ndim)
    s_spec = pl.BlockSpec((1, GDN_PAIRS, LANES, LANES), lambda i, s: (i, 0, 0, 0))
    return pl.pallas_call(
        functools.partial(_gdn_kernel, rev=rev, n_blk=n_blk),
        grid=(b, n_blk),
        in_specs=[pl.BlockSpec((1, blk, w), lambda i, s: (i, blk_of(s), 0)),
                  pl.BlockSpec((1, blk, LANES), lambda i, s: (i, blk_of(s), 0))]
                 + [const(a) for a in consts] + [s_spec],
        out_specs=[pl.BlockSpec((1, blk, hw), lambda i, s: (i, blk_of(s), 0)), s_spec],
        out_shape=[jax.ShapeDtypeStruct((b, t, hw), F32),
                   jax.ShapeDtypeStruct((b, GDN_PAIRS, LANES, LANES), F32)],
        scratch_shapes=[pltpu.VMEM((GDN_PAIRS, LANES, LANES), F32)],
        compiler_params=_params(2),
        name=name,
    )(qkv, misc, *consts, s0)


MIX_SUB_ROWS = 256


def _mix_ffn_kernel(*refs, ff_chunks, gdn_inputs, n_sub):
    tm = refs[0].shape[1]
    rows = [slice(i * (tm // n_sub), (i + 1) * (tm // n_sub)) for i in range(n_sub)]
    if gdn_inputs:
        (h_ref, of_ref, ob_ref, gate_ref, att_ref, on_ref, ones_ref, gtm_ref, shf_ref, scf_ref, gtf_ref, gpm_ref,
         gpf_ref, gqf_ref, wo_ref, wg_ref, wu_ref, wd_ref, o_ref) = refs
        mixes = []
        for r in rows:
            o = of_ref[0, r, :] + ob_ref[0, r, :]
            gate = gate_ref[0, r, :].astype(F32)
            a = o * lax.rsqrt(_seg_mean_sq(o, ones_ref) + NORM_EPS) * on_ref[...] * (gate * jax.nn.sigmoid(gate))
            mixes.append(jnp.concatenate([a.astype(BF16), att_ref[0, r, :]], axis=1))
    else:
        (h_ref, a_ref, b_ref, gtm_ref, shf_ref, scf_ref, gtf_ref, gpm_ref, gpf_ref, gqf_ref, wo_ref, wg_ref,
         wu_ref, wd_ref, o_ref) = refs
        mixes = [jnp.concatenate([a_ref[0, r, :], b_ref[0, r, :]], axis=1) for r in rows]
    ys = [_dotf(mix, wo_ref[...]) for mix in mixes]
    h1s = [h_ref[0, r, :] + gtm_ref[0] * (_rms(y) * gpm_ref[...]) for r, y in zip(rows, ys)]
    us = [(_rms(h1) * gpf_ref[...] * (1.0 + scf_ref[0]) + shf_ref[0]).astype(BF16) for h1 in h1s]
    acts = []
    for o, n in ff_chunks:
        for u in us:
            gg = _dotf(u, wg_ref[:, o:o + n])
            uu = _dotf(u, wu_ref[:, o:o + n])
            acts.append((gg * jax.nn.sigmoid(gg) * uu).astype(BF16))
    fs = [None] * n_sub
    for ci, (o, n) in enumerate(ff_chunks):
        for i in range(n_sub):
            part = _dotf(acts[ci * n_sub + i], wd_ref[o:o + n, :])
            fs[i] = part if fs[i] is None else fs[i] + part
    for r, h1, f in zip(rows, h1s, fs):
        o_ref[0, r, :] = h1 + gtf_ref[0] * (_rms(f) * gqf_ref[...])


def _mix_ffn_call(h, mix_inputs, out_norm, gt_m, sh_f, sc_f, gt_f, g_post_mix, g_pre_ffn, g_post_ffn, wo, wg, wu,
                  wd, tm, name):
    b, t, d = h.shape
    dm = wo.shape[0]
    ff = wg.shape[1]
    gdn_inputs = out_norm is not None
    half = pl.cdiv(ff // 2, MXU_TILE) * MXU_TILE
    assert 0 < half < ff and ff % LANES == 0
    ff_chunks = ((0, half), (half, ff - half))
    tok = lambda n: pl.BlockSpec((1, tm, n), lambda i, j: (i, j, 0))
    mods = [_mod_operand(m, d) for m in (gt_m, sh_f, sc_f, gt_f)]
    vec = pl.BlockSpec((1, d), lambda i, j: (0, 0))
    const = lambda shape: pl.BlockSpec(shape, lambda i, j: (0, 0), pipeline_mode=pl.Buffered(1))
    r2 = lambda a: a.reshape(1, d).astype(F32)
    mix_specs = [tok(m.shape[-1]) for m in mix_inputs]
    mix_args = list(mix_inputs)
    if gdn_inputs:
        hw = GDN_HEADS * GDN_DV
        on = jnp.tile(out_norm.astype(F32), GDN_HEADS).reshape(1, hw)
        ones = _seg_ones(hw)
        mix_specs += [pl.BlockSpec(on.shape, lambda i, j: (0, 0)), pl.BlockSpec(ones.shape, lambda i, j: (0, 0))]
        mix_args += [on, ones]
    return pl.pallas_call(
        functools.partial(_mix_ffn_kernel, ff_chunks=ff_chunks, gdn_inputs=gdn_inputs,
                          n_sub=tm // MIX_SUB_ROWS),
        grid=(b, t // tm),
        in_specs=[tok(d)] + mix_specs + [spec for _, spec in mods]
                 + [vec, vec, vec, const((dm, d)), const((d, ff)), const((d, ff)), const((ff, d))],
        out_specs=tok(d),
        out_shape=jax.ShapeDtypeStruct((b, t, d), F32),
        compiler_params=_params(2),
        name=name,
    )(h, *mix_args, *[arr for arr, _ in mods], r2(g_post_mix), r2(g_pre_ffn), r2(g_post_ffn), wo, wg, wu, wd)


def _rope_tables(n_tokens, rot_dim):
    t = jnp.arange(n_tokens, dtype=jnp.int32)
    row = (t // GRID_W).astype(F32)
    col = (t % GRID_W).astype(F32)
    n_freq = rot_dim // 4
    inv_freq = ROPE_THETA ** (-jnp.arange(n_freq, dtype=F32) / n_freq)
    ang = jnp.concatenate([row[:, None] * inv_freq, col[:, None] * inv_freq], -1)
    cos = jnp.concatenate([jnp.cos(ang), jnp.cos(ang)], -1)
    sin = jnp.concatenate([-jnp.sin(ang), jnp.sin(ang)], -1)
    if rot_dim == HEAD_W:
        return jnp.tile(cos, (1, LANES // rot_dim)), jnp.tile(sin, (1, LANES // rot_dim))
    pad = LANES - rot_dim
    return (jnp.concatenate([cos, jnp.ones((n_tokens, pad), F32)], -1),
            jnp.concatenate([sin, jnp.zeros((n_tokens, pad), F32)], -1))


def _no_rope_tables(n_tokens):
    return jnp.ones((n_tokens, LANES), F32), jnp.zeros((n_tokens, LANES), F32)


def _gdn_bidirectional(lat_qkv, lat_misc, ctx_qkv, ctx_misc, a_log, dt_bias):
    bsz = lat_qkv.shape[0]
    s_zero = jnp.zeros((bsz, GDN_PAIRS, LANES, LANES), F32)
    lat, ctx = [], []
    for d, rev in ((0, False), (1, True)):
        consts = _gdn_consts(a_log, dt_bias, d, rev)
        o_c, s_c = _gdn_call(ctx_qkv, ctx_misc, consts, s_zero, rev, "gdn_ctx_%d" % d)
        o_l, _ = _gdn_call(lat_qkv, lat_misc, consts, s_c, rev, "gdn_lat_%d" % d)
        lat.append(o_l)
        ctx.append(o_c)
    return lat, ctx


LOG2E = 1.4426950408889634
ATTN_TQ = 1024
ATTN_LOOKAHEAD = 2


def _attn_chunks(l_len, s_len, size):
    return [(0, 0, l_len)] + [(1, o, size) for o in range(0, s_len, size)]


def _even_mixer(h, hc, mods, mods_c, g_pre, w_in, conv_w, a_log, dt_bias, q_norm, w_q_up, kv_norm, w_kv_up):
    s_len, l_len = h.shape[1], hc.shape[1]
    weights = _even_weights(w_in, q_norm, w_q_up, kv_norm, w_kv_up)
    qkv_l, gate_l, misc_l, q_l, k_l, v_l = _in_even_call(h, g_pre, *mods, weights, conv_w,
                                                         *_rope_tables(s_len, MLA_ROPE), 512, "in_even_lat")
    qkv_c, gate_c, misc_c, q_c, k_c, v_c = _in_even_call(hc, g_pre, *mods_c, weights, conv_w,
                                                         *_no_rope_tables(l_len), l_len, "in_even_ctx")
    o_lat, o_ctx = _gdn_bidirectional(qkv_l, misc_l, qkv_c, misc_c, a_log, dt_bias)
    b_lat = _attn_call(q_l, [(k_c, v_c), (k_l, v_l)], MLA_QC, MLA_PAIRS, MLA_PAIRS, _attn_chunks(l_len, s_len, 512),
                       ATTN_TQ, False, "mla_attn_lat")
    b_ctx = _attn_call(q_c, [(k_c, v_c)], MLA_QC, MLA_PAIRS, MLA_PAIRS, [(0, 0, l_len)], l_len, False,
                       "mla_attn_ctx")
    return (o_lat[0], o_lat[1], gate_l, b_lat), (o_ctx[0], o_ctx[1], gate_c, b_ctx)


def _odd_mixer_last(h, hc, mods, mods_c, g_pre, w_in, rpb, q_norm, k_norm):
    s_len, l_len = h.shape[1], hc.shape[1]
    w = w_in.astype(BF16)
    q_gain = (jnp.tile(q_norm.astype(F32), GQA_HEADS) * (GQA_DIM ** -0.5 * LOG2E)).reshape(1, -1)
    k_gain = jnp.tile(k_norm.astype(F32), GQA_KV_HEADS).reshape(1, -1)
    naq_l, nak_l, nav_l, gq_l, gk_l, gv_l = _in_odd_call(h, g_pre, *mods, w, q_gain, k_gain,
                                                         *_rope_tables(s_len, GQA_DIM), 512, "in_odd_lat")
    _, nak_c, nav_c, _, gk_c, gv_c = _in_odd_call(hc, g_pre, *mods_c, w, q_gain, k_gain, *_no_rope_tables(l_len),
                                                  l_len, "in_odd_ctx")
    c_lat = _na_call(naq_l, nak_l, nav_l, nak_c, nav_c, _na_bias_table(rpb, s_len // GRID_W))
    d_lat = _attn_call(gq_l, [(gk_c, gv_c), (gk_l, gv_l)], LANES, GQA_HEADS // 2, GQA_KV_HEADS,
                       _attn_chunks(l_len, s_len, 256), ATTN_TQ, True, "gqa_attn")
    return c_lat, d_lat


def kernel(x, c, ctx, c_ctx, w_mod, b_mod, g_pre_mix, g_post_mix, g_pre_ffn, g_post_ffn, w_ffn_gate, w_ffn_up,
           w_ffn_down, w_in_even, w_out_even, gdn_conv, gdn_a_log, gdn_dt_bias, gdn_out_norm, mla_q_norm,
           mla_w_q_up, mla_kv_norm, mla_w_kv_up, w_in_odd, w_out_odd, na_rpb, gqa_q_norm, gqa_k_norm):
    bsz, s_len, d = x.shape
    depth = w_mod.shape[0]
    assert depth == 2, "layer 0 = even mixer with context update, layer 1 = odd mixer (last)"
    c_rows = jnp.concatenate([c, c_ctx[None, :], jnp.zeros((16 - bsz - 1, d), F32)], 0)
    h, hc = x, ctx
    for i in range(depth):
        mod_tab = _mod_call(c_rows, w_mod[i].astype(BF16), b_mod[i]).reshape(c_rows.shape[0] * N_MOD, 1, d)
        sh_m, sc_m, gt_m, sh_f, sc_f, gt_f = (_Mod(mod_tab, k, None) for k in range(N_MOD))
        csh_m, csc_m, cgt_m, csh_f, csc_f, cgt_f = (_Mod(mod_tab, k, bsz) for k in range(N_MOD))
        wg, wu, wd = w_ffn_gate[i].astype(BF16), w_ffn_up[i].astype(BF16), w_ffn_down[i].astype(BF16)
        if i == 0:
            mix, mix_c = _even_mixer(h, hc, (sh_m, sc_m), (csh_m, csc_m), g_pre_mix[i], w_in_even[0], gdn_conv[0],
                                     gdn_a_log[0], gdn_dt_bias[0], mla_q_norm[0], mla_w_q_up[0], mla_kv_norm[0],
                                     mla_w_kv_up[0])
            wo, out_norm = w_out_even[0].astype(BF16), gdn_out_norm[0]
            hc = _mix_ffn_call(hc, mix_c, out_norm, cgt_m, csh_f, csc_f, cgt_f, g_post_mix[i], g_pre_ffn[i],
                               g_post_ffn[i], wo, wg, wu, wd, 256, "mix_ffn_ctx")
        else:
            mix = _odd_mixer_last(h, hc, (sh_m, sc_m), (csh_m, csc_m), g_pre_mix[i], w_in_odd[0], na_rpb[0],
                                  gqa_q_norm[0], gqa_k_norm[0])
            wo, out_norm = w_out_odd[0].astype(BF16), None
        h = _mix_ffn_call(h, mix, out_norm, gt_m, sh_f, sc_f, gt_f, g_post_mix[i], g_pre_ffn[i], g_post_ffn[i],
                          wo, wg, wu, wd, 2 * MIX_SUB_ROWS, "mix_ffn_lat%d" % i)
    return h
```

```python
import functools
from typing import NamedTuple, Optional

import numpy as np
import jax
import jax.numpy as jnp
from jax import lax
from jax.experimental import pallas as pl
from jax.experimental.pallas import tpu as pltpu

F32 = jnp.float32
BF16 = jnp.bfloat16

GRID_W = 64
NORM_EPS = 1e-6
ROPE_THETA = 10000.0

GDN_HEADS = 8
GDN_DK = 64
GDN_DV = 64
GDN_CHUNK = 64
GDN_QK_W = GDN_HEADS * GDN_DK
GDN_V_W = GDN_HEADS * GDN_DV

MLA_HEADS = 8
MLA_NOPE = 64
MLA_ROPE = 32
MLA_V = 64
MLA_Q_LORA = 256
MLA_KV_LORA = 128

NA_HEADS = 8
NA_DIM = 64
NA_WIN_H = 8
NA_WIN_W = 16
NA_W = NA_HEADS * NA_DIM

GQA_HEADS = 8
GQA_KV_HEADS = 2
GQA_DIM = 64

LANES = 128
MXU_TILE = 256
HEAD_W = 64
MASK_VALUE = -1e30
VMEM_LIMIT = 56 << 20

NT_DIMS = (((1,), (1,)), ((), ()))


def _params(n_grid, vmem=VMEM_LIMIT):
    return pltpu.CompilerParams(dimension_semantics=("arbitrary",) * n_grid, vmem_limit_bytes=vmem)


def _rms(x):
    return x * lax.rsqrt(jnp.mean(x * x, axis=-1, keepdims=True) + NORM_EPS)


def _mod_kernel(c_ref, w_ref, b_ref, o_ref):
    c = c_ref[...]
    a = (c * jax.nn.sigmoid(c)).astype(BF16)
    o_ref[...] = jnp.dot(a, w_ref[...], preferred_element_type=F32) + b_ref[...]


def _mod_call(c_rows, w, b):
    r, d = c_rows.shape
    n = w.shape[1]
    tn = 1024
    return pl.pallas_call(
        _mod_kernel,
        grid=(n // tn,),
        in_specs=[pl.BlockSpec((r, d), lambda j: (0, 0)),
                  pl.BlockSpec((d, tn), lambda j: (0, j)),
                  pl.BlockSpec((1, tn), lambda j: (0, j))],
        out_specs=pl.BlockSpec((r, tn), lambda j: (0, j)),
        out_shape=jax.ShapeDtypeStruct((r, n), F32),
        compiler_params=_params(1),
        name="mod",
    )(c_rows, w, b.reshape(1, n))


class _Mod(NamedTuple):
    table: jax.Array
    comp: int
    row: Optional[int]


N_MOD = 6


def _mod_operand(m, d):
    if m.row is None:
        return m.table, pl.BlockSpec((1, 1, d), lambda i, j: (i * N_MOD + m.comp, 0, 0))
    return m.table, pl.BlockSpec((1, 1, d), lambda i, j: (m.row * N_MOD + m.comp, 0, 0))


def _norm_mod(x_ref, g_ref, sh_ref, sc_ref):
    return (_rms(x_ref[0]) * g_ref[...] * (1.0 + sc_ref[0]) + sh_ref[0]).astype(BF16)


def _dotf(a, b):
    return jnp.dot(a, b, preferred_element_type=F32)


def _seg_mean_sq(x, ones_ref):
    return _dotf((x * x).astype(BF16), ones_ref[...]) * (1.0 / HEAD_W)


def _rotate(t, cos, sin_signed, half):
    lane = lax.broadcasted_iota(jnp.int32, t.shape, 1)
    n = t.shape[1]
    partner = jnp.where((lane & (2 * half - 1)) < half, pltpu.roll(t, n - half, 1), pltpu.roll(t, half, 1))
    return t * cos + partner * sin_signed


EVEN_QKV_W = 2 * GDN_QK_W + GDN_V_W
EVEN_GROUPS = (EVEN_QKV_W, GDN_V_W, MLA_Q_LORA, MLA_KV_LORA, LANES, LANES)
MLA_QC = 2 * LANES
MLA_PAIRS = MLA_HEADS // 2


CONV_HALO = 16


def _in_even_kernel(x_ref, xp_ref, xn_ref, g_ref, sh_ref, sc_ref, w_ref, qg_ref, wq_ref, kvg_ref, wk_ref, wv_ref,
                    vone_ref, cos_ref, sin_ref, cw_ref, ones_ref, qkv_ref, gate_ref, misc_ref, q_ref, k_ref, v_ref,
                    zs_ref, *, n_tap):
    j = pl.program_id(1)
    tm = x_ref.shape[1]
    u = _norm_mod(x_ref, g_ref, sh_ref, sc_ref)
    offs = np.cumsum((0,) + EVEN_GROUPS)
    grp = lambda i: _dotf(u, w_ref[:, int(offs[i]):int(offs[i + 1])])
    u_ext = jnp.concatenate([_norm_mod(xp_ref, g_ref, sh_ref, sc_ref), u, _norm_mod(xn_ref, g_ref, sh_ref, sc_ref)],
                            axis=0)
    z = _dotf(u_ext, w_ref[:, 0:EVEN_QKV_W])
    zs_ref[0:CONV_HALO, :] = z[0:CONV_HALO] * (j > 0).astype(F32)
    zs_ref[CONV_HALO:CONV_HALO + tm, :] = z[CONV_HALO:CONV_HALO + tm]
    zs_ref[CONV_HALO + tm:, :] = z[CONV_HALO + tm:] * (j < pl.num_programs(1) - 1).astype(F32)
    q_down, pe_raw = grp(2), grp(5)
    kv_misc = _dotf(u, w_ref[:, int(offs[3]):int(offs[5])])
    kv_down, misc = kv_misc[:, 0:MLA_KV_LORA], kv_misc[:, MLA_KV_LORA:]
    gate_ref[0] = grp(1).astype(BF16)
    misc_ref[0] = misc
    cos, sin = cos_ref[...], sin_ref[...]
    qn = (_rms(q_down) * qg_ref[...]).astype(BF16)
    kvn = (_rms(kv_down) * kvg_ref[...]).astype(BF16)
    for h in range(MLA_HEADS):
        blk = _dotf(qn, wq_ref[:, h * MLA_QC:(h + 1) * MLA_QC])
        q_ref[0, :, h * MLA_QC:h * MLA_QC + LANES] = blk[:, 0:LANES].astype(BF16)
        q_ref[0, :, h * MLA_QC + LANES:(h + 1) * MLA_QC] = _rotate(blk[:, LANES:], cos, sin,
                                                                    MLA_ROPE // 2).astype(BF16)
    pe = _rotate(pe_raw, cos, sin, MLA_ROPE // 2).astype(BF16)
    k_nope = _dotf(kvn, wk_ref[...])
    for p in range(MLA_PAIRS):
        k_ref[0, :, p * MLA_QC:p * MLA_QC + LANES] = k_nope[:, p * LANES:(p + 1) * LANES].astype(BF16)
        k_ref[0, :, p * MLA_QC + LANES:(p + 1) * MLA_QC] = pe
    v_ref[0] = (_dotf(kvn, wv_ref[...]) + vone_ref[...]).astype(BF16)
    acc = None
    for t in range(n_tap):
        start = CONV_HALO - n_tap // 2 + t
        term = zs_ref[start:start + tm, :] * cw_ref[t:t + 1, :]
        acc = term if acc is None else acc + term
    y = acc * jax.nn.sigmoid(acc)
    hw = GDN_QK_W
    qc, kc = y[:, 0:hw], y[:, hw:2 * hw]
    q_ss = _seg_mean_sq(qc, ones_ref) * HEAD_W
    k_ss = _seg_mean_sq(kc, ones_ref) * HEAD_W
    qkv_ref[0, :, 0:hw] = (qc * lax.rsqrt(q_ss + NORM_EPS) * (GDN_DK ** -0.5)).astype(BF16)
    qkv_ref[0, :, hw:2 * hw] = (kc * lax.rsqrt(k_ss + NORM_EPS)).astype(BF16)
    qkv_ref[0, :, 2 * hw:] = y[:, 2 * hw:].astype(BF16)


def _even_weights(w_in, q_norm, w_q_up, kv_norm, w_kv_up):
    d = w_in.shape[0]
    o0 = EVEN_QKV_W + GDN_V_W
    n_ab = 4 * GDN_HEADS
    o1 = o0 + n_ab
    o2 = o1 + MLA_Q_LORA
    o3 = o2 + MLA_KV_LORA
    zeros = lambda n, rows=d: jnp.zeros((rows, n), F32)
    w_cat = jnp.concatenate([w_in[:, :o0], w_in[:, o1:o2], w_in[:, o2:o3],
                             w_in[:, o0:o1], zeros(LANES - n_ab),
                             w_in[:, o3:], zeros(LANES - MLA_ROPE)], 1).astype(BF16)
    qh = MLA_NOPE + MLA_ROPE
    cols = []
    for h in range(MLA_HEADS):
        nope = w_q_up[:, h * qh:h * qh + MLA_NOPE]
        rp = w_q_up[:, h * qh + MLA_NOPE:(h + 1) * qh]
        z = zeros(HEAD_W, MLA_Q_LORA)
        cols += ([nope, z] if h % 2 == 0 else [z, nope]) + [rp, zeros(LANES - MLA_ROPE, MLA_Q_LORA)]
    wq = jnp.concatenate(cols, 1).astype(BF16)
    kvh = MLA_NOPE + MLA_V
    wk = jnp.concatenate([w_kv_up[:, h * kvh:h * kvh + MLA_NOPE] for h in range(MLA_HEADS)], 1).astype(BF16)
    vcols, ones = [], []
    for h in range(MLA_HEADS):
        vh = w_kv_up[:, h * kvh + MLA_NOPE:(h + 1) * kvh]
        z = zeros(HEAD_W, MLA_KV_LORA)
        vcols += [vh, z] if h % 2 == 0 else [z, vh]
        ones += [0.0, 1.0] if h % 2 == 0 else [1.0, 0.0]
    wv = jnp.concatenate(vcols, 1).astype(BF16)
    vone = jnp.asarray(np.repeat(np.asarray(ones, np.float32), HEAD_W)[None, :])
    q_gain = (q_norm * ((MLA_NOPE + MLA_ROPE) ** -0.5 * LOG2E)).reshape(1, -1).astype(F32)
    return w_cat, q_gain, wq, kv_norm.reshape(1, -1).astype(F32), wk, wv, vone


def _in_even_call(x, g, shift, scale, weights, conv_w, cos, sin, tm, name):
    b, t, d = x.shape
    w_cat, q_gain, wq, kv_gain, wk, wv, vone = weights
    n_tap = conv_w.shape[0]
    cw = jnp.concatenate([conv_w.astype(F32), jnp.zeros((8 - n_tap, conv_w.shape[1]), F32)], 0)
    ones = _seg_ones(GDN_QK_W)
    per = tm // CONV_HALO
    n_halo = t // CONV_HALO
    const = lambda a: pl.BlockSpec(a.shape, lambda i, j: (0,) * a.ndim)
    (sh_arr, sh_spec), (sc_arr, sc_spec) = _mod_operand(shift, d), _mod_operand(scale, d)
    tab = pl.BlockSpec((tm, LANES), lambda i, j: (j, 0))
    widths = (EVEN_QKV_W, GDN_V_W, LANES, MLA_HEADS * MLA_QC, MLA_PAIRS * MLA_QC, MLA_HEADS * LANES)
    dts = (BF16, BF16, F32, BF16, BF16, BF16)
    g2 = g.reshape(1, d).astype(F32)
    return pl.pallas_call(
        functools.partial(_in_even_kernel, n_tap=n_tap),
        grid=(b, t // tm),
        in_specs=[pl.BlockSpec((1, tm, d), lambda i, j: (i, j, 0)),
                  pl.BlockSpec((1, CONV_HALO, d), lambda i, j: (i, jnp.maximum(j * per - 1, 0), 0)),
                  pl.BlockSpec((1, CONV_HALO, d), lambda i, j: (i, jnp.minimum((j + 1) * per, n_halo - 1), 0)),
                  const(g2), sh_spec, sc_spec, const(w_cat), const(q_gain), const(wq), const(kv_gain), const(wk),
                  const(wv), const(vone), tab, tab, const(cw), const(ones)],
        out_specs=[pl.BlockSpec((1, tm, n), lambda i, j: (i, j, 0)) for n in widths],
        out_shape=[jax.ShapeDtypeStruct((b, t, n), dt) for n, dt in zip(widths, dts)],
        scratch_shapes=[pltpu.VMEM((tm + 2 * CONV_HALO, EVEN_QKV_W), F32)],
        compiler_params=_params(2),
        name=name,
    )(x, x, x, g2, sh_arr, sc_arr, w_cat, q_gain, wq, kv_gain, wk, wv, vone,
      cos, sin, cw, ones)


GQA_KV_W = GQA_KV_HEADS * GQA_DIM
PROJ_SUB_ROWS = 256


def _in_odd_kernel(x_ref, g_ref, sh_ref, sc_ref, w_ref, qg_ref, kg_ref, ones_q_ref, ones_k_ref, cos_ref, sin_ref,
                   naq_ref, nak_ref, nav_ref, gq_ref, gk_ref, gv_ref):
    tm = x_ref.shape[1]
    n_sub = max(tm // PROJ_SUB_ROWS, 1)
    subs = [slice(i * (tm // n_sub), (i + 1) * (tm // n_sub)) for i in range(n_sub)]
    us = [(_rms(x_ref[0, r, :]) * g_ref[...] * (1.0 + sc_ref[0]) + sh_ref[0]).astype(BF16) for r in subs]
    o_q = 3 * NA_W
    o_k = o_q + GQA_HEADS * GQA_DIM
    qs = [_dotf(u, w_ref[:, o_q:o_k]) for u in us]
    kvs = [_dotf(u, w_ref[:, o_k:o_k + 2 * GQA_KV_W]) for u in us]
    ks = [kv[:, 0:GQA_KV_W] for kv in kvs]
    vs = [kv[:, GQA_KV_W:] for kv in kvs]
    low = lax.broadcasted_iota(jnp.int32, (tm // n_sub, LANES), 1) < HEAD_W
    for r, u in zip(subs, us):
        naq_ref[0, r, :] = (_dotf(u, w_ref[:, 0:NA_W]) * (NA_DIM ** -0.5 * LOG2E)).astype(BF16)
        nak_ref[0, r, :] = _dotf(u, w_ref[:, NA_W:2 * NA_W]).astype(BF16)
        na_v = _dotf(u, w_ref[:, 2 * NA_W:3 * NA_W])
        for p in range(NA_HEADS // 2):
            tile = na_v[:, p * LANES:(p + 1) * LANES]
            nav_ref[0, r, 2 * p * LANES:(2 * p + 1) * LANES] = tile.astype(BF16)
            nav_ref[0, r, (2 * p + 1) * LANES:(2 * p + 2) * LANES] = jnp.ones(tile.shape, BF16)
    q_ms = [_seg_mean_sq(q, ones_q_ref) for q in qs]
    k_ms = [_seg_mean_sq(k, ones_k_ref) for k in ks]
    for i, r in enumerate(subs):
        cos, sin = cos_ref[r, :], sin_ref[r, :]
        qn = qs[i] * lax.rsqrt(q_ms[i] + NORM_EPS) * qg_ref[...]
        for p in range(GQA_HEADS // 2):
            cols = slice(p * LANES, (p + 1) * LANES)
            gq_ref[0, r, cols] = _rotate(qn[:, cols], cos, sin, GQA_DIM // 2).astype(BF16)
        kr = _rotate(ks[i] * lax.rsqrt(k_ms[i] + NORM_EPS) * kg_ref[...], cos, sin, GQA_DIM // 2)
        k_sw = pltpu.roll(kr, HEAD_W, 1)
        gk_ref[0, r, 0:LANES] = jnp.where(low, kr, k_sw).astype(BF16)
        gk_ref[0, r, LANES:2 * LANES] = jnp.where(low, k_sw, kr).astype(BF16)
        v = vs[i]
        v_sw = pltpu.roll(v, HEAD_W, 1)
        slots = (jnp.where(low, v, 1.0), jnp.where(low, 1.0, v_sw),
                 jnp.where(low, v_sw, 1.0), jnp.where(low, 1.0, v))
        for j, s in enumerate(slots):
            gv_ref[0, r, j * LANES:(j + 1) * LANES] = s.astype(BF16)


def _seg_ones(width):
    head = np.arange(width) // HEAD_W
    return jnp.asarray((head[:, None] == head[None, :]).astype(np.float32), BF16)


def _in_odd_call(x, g, shift, scale, w, q_gain, k_gain, cos, sin, tm, name):
    b, t, d = x.shape
    assert GQA_KV_W == LANES
    ones_q = _seg_ones(GQA_HEADS * GQA_DIM)
    ones_k = _seg_ones(GQA_KV_W)
    const = lambda a: pl.BlockSpec(a.shape, lambda i, j: (0,) * a.ndim)
    (sh_arr, sh_spec), (sc_arr, sc_spec) = _mod_operand(shift, d), _mod_operand(scale, d)
    tab = pl.BlockSpec((tm, LANES), lambda i, j: (j, 0))
    widths = (NA_W, NA_W, 2 * NA_W, GQA_HEADS * GQA_DIM, 2 * LANES, 4 * LANES)
    g2 = g.reshape(1, d).astype(F32)
    return pl.pallas_call(
        _in_odd_kernel,
        grid=(b, t // tm),
        in_specs=[pl.BlockSpec((1, tm, d), lambda i, j: (i, j, 0)), const(g2), sh_spec, sc_spec, const(w),
                  const(q_gain), const(k_gain), const(ones_q), const(ones_k), tab, tab],
        out_specs=[pl.BlockSpec((1, tm, n), lambda i, j: (i, j, 0)) for n in widths],
        out_shape=[jax.ShapeDtypeStruct((b, t, n), BF16) for n in widths],
        compiler_params=_params(2),
        name=name,
    )(x, g2, sh_arr, sc_arr, w, q_gain, k_gain, ones_q, ones_k, cos, sin)


def _attn_kernel(*refs, dc, chunks, n_src, masked_q):
    q_ref, kv_refs, o_ref = refs[0], refs[1:1 + 2 * n_src], refs[-1]
    tq = q_ref.shape[1]
    lane = lax.broadcasted_iota(jnp.int32, (tq, LANES), 1)
    qs = []
    for hh in range(2):
        if masked_q:
            qp = q_ref[0]
            qs.append(jnp.where((lane < HEAD_W) if hh == 0 else (lane >= HEAD_W), qp, jnp.zeros_like(qp)))
        else:
            qs.append(q_ref[0, :, hh * dc:(hh + 1) * dc])
    ms, accs = [None, None], [None, None]

    def scores(hh, src, s0, n):
        return lax.dot_general(qs[hh], kv_refs[2 * src][0, s0:s0 + n, :], NT_DIMS, preferred_element_type=F32)

    def consume(s, hh, src, s0, n):
        vv = kv_refs[2 * src + 1][0, s0:s0 + n, hh * LANES:(hh + 1) * LANES]
        mc = jnp.max(s, axis=-1, keepdims=True)
        m_new = mc if ms[hh] is None else jnp.maximum(ms[hh], mc)
        pv = _dotf(jnp.exp2(s - m_new).astype(BF16), vv)
        accs[hh] = pv if ms[hh] is None else jnp.exp2(ms[hh] - m_new) * accs[hh] + pv
        ms[hh] = m_new

    items = [(hh,) + tuple(c) for c in chunks for hh in range(2)]
    pending = []
    for it in items:
        pending.append((scores(*it), it))
        if len(pending) > ATTN_LOOKAHEAD:
            s, it0 = pending.pop(0)
            consume(s, *it0)
    for s, it0 in pending:
        consume(s, *it0)
    low = lane < HEAD_W
    num = jnp.where(low, accs[0], accs[1])
    den = pltpu.roll(jnp.where(low, accs[1], accs[0]), HEAD_W, 1)
    o_ref[0] = (num / den).astype(o_ref.dtype)


def _attn_call(q, kvs, dc, n_pairs, n_groups, chunks, tq, masked_q, name):
    b, t_q, _ = q.shape
    per = n_pairs // n_groups
    q_w = LANES if masked_q else 2 * dc
    in_specs = [pl.BlockSpec((1, tq, q_w), lambda i, p, j: (i, j, p))]
    args = [q]
    for k, v in kvs:
        in_specs += [pl.BlockSpec((1, k.shape[1], dc), lambda i, p, j: (i, 0, p // per)),
                     pl.BlockSpec((1, v.shape[1], 2 * LANES), lambda i, p, j: (i, 0, p // per))]
        args += [k, v]
    return pl.pallas_call(
        functools.partial(_attn_kernel, dc=dc, chunks=tuple(chunks), n_src=len(kvs), masked_q=masked_q),
        grid=(b, n_pairs, t_q // tq),
        in_specs=in_specs,
        out_specs=pl.BlockSpec((1, tq, LANES), lambda i, p, j: (i, j, p)),
        out_shape=jax.ShapeDtypeStruct((b, t_q, n_pairs * LANES), BF16),
        compiler_params=_params(3),
        name=name,
    )(*args)


NA_QROWS = 4
NA_KROWS = 12
NA_LOOKAHEAD = 1


def _na_kernel(q_ref, k_ref, v_ref, kc_ref, vc_ref, bias_ref, o_ref):
    rb = pl.program_id(1)
    n_rows = k_ref.shape[1] // GRID_W
    base = jnp.clip(rb * NA_QROWS - NA_WIN_H // 2, 0, n_rows - NA_KROWS) * GRID_W
    base = pl.multiple_of(base, GRID_W)
    nq = NA_QROWS * GRID_W
    nk = NA_KROWS * GRID_W
    lane = lax.broadcasted_iota(jnp.int32, (nq, LANES), 1)
    low = lane < HEAD_W

    def scores(p):
        cols = slice(p * LANES, (p + 1) * LANES)
        qp = q_ref[0, :, cols]
        zero = jnp.zeros_like(qp)
        q2 = jnp.concatenate([jnp.where(low, qp, zero), jnp.where(low, zero, qp)], axis=0)
        s_loc = lax.dot_general(q2, k_ref[0, pl.ds(base, nk), cols], NT_DIMS, preferred_element_type=F32)
        s_ctx = lax.dot_general(q2, kc_ref[0, :, cols], NT_DIMS, preferred_element_type=F32)
        return s_loc, s_ctx

    def consume(p, s_loc, s_ctx):
        vcols = slice(2 * p * LANES, (2 * p + 2) * LANES)
        s_loc = s_loc + bias_ref[0, p]
        m = jnp.maximum(jnp.max(s_loc, axis=-1, keepdims=True), jnp.max(s_ctx, axis=-1, keepdims=True))
        acc = (_dotf(jnp.exp2(s_loc - m).astype(BF16), v_ref[0, pl.ds(base, nk), vcols])
               + _dotf(jnp.exp2(s_ctx - m).astype(BF16), vc_ref[0, :, vcols]))
        even, odd = acc[0:nq], acc[nq:2 * nq]
        num = jnp.where(low, even[:, 0:LANES], odd[:, 0:LANES])
        den = jnp.where(low, even[:, LANES:2 * LANES], odd[:, LANES:2 * LANES])
        o_ref[0, :, p * LANES:(p + 1) * LANES] = (num / den).astype(o_ref.dtype)

    pending = []
    for p in range(NA_HEADS // 2):
        pending.append((p,) + scores(p))
        if len(pending) > NA_LOOKAHEAD:
            consume(*pending.pop(0))
    for item in pending:
        consume(*item)


def _na_bias_table(rpb, rows):
    n_blocks = rows // NA_QROWS
    n_h = rpb.shape[0]
    c = np.arange(GRID_W)[:, None]
    kc = np.arange(GRID_W)[None, :]
    cs = np.clip(c - NA_WIN_W // 2, 0, GRID_W - NA_WIN_W)
    col_ok = (kc >= cs) & (kc < cs + NA_WIN_W)
    dc = np.clip(kc - c + (NA_WIN_W - 1), 0, 2 * NA_WIN_W - 2)
    pick = (np.arange(2 * NA_WIN_W - 1)[:, None, None] == dc[None]).astype(np.float32)
    by_col = jnp.einsum("hdm,mck->hcdk", rpb.astype(F32) * LOG2E, jnp.asarray(pick),
                        precision=lax.Precision.HIGHEST)
    variants = []
    for r0 in (0, NA_QROWS * (n_blocks // 2), rows - NA_QROWS):
        base = int(np.clip(r0 - NA_WIN_H // 2, 0, rows - NA_KROWS))
        r = r0 + np.arange(NA_QROWS)[:, None]
        kr = base + np.arange(NA_KROWS)[None, :]
        rs = np.clip(r - NA_WIN_H // 2, 0, rows - NA_WIN_H)
        row_ok = (kr >= rs) & (kr < rs + NA_WIN_H)
        dr = np.clip(kr - r + (NA_WIN_H - 1), 0, 2 * NA_WIN_H - 2)
        full = jnp.stack([by_col[:, :, dr[i], :] for i in range(NA_QROWS)], 1)
        valid = row_ok[:, None, :, None] & col_ok[None, :, None, :]
        full = jnp.where(valid[None], full, MASK_VALUE)
        variants.append(full.reshape(n_h, NA_QROWS * GRID_W, NA_KROWS * GRID_W))
    return jnp.stack(variants, 0).reshape(3, n_h // 2, 2 * NA_QROWS * GRID_W, NA_KROWS * GRID_W)


def _na_call(q, k, v, kc, vc, bias):
    b, t, w = q.shape
    n_blocks = t // (NA_QROWS * GRID_W)
    nq = NA_QROWS * GRID_W
    tc = kc.shape[1]

    def bias_map(i, r):
        return (jnp.where(r == 0, 0, jnp.where(r == n_blocks - 1, 2, 1)), 0, 0, 0)

    return pl.pallas_call(
        _na_kernel,
        grid=(b, n_blocks),
        in_specs=[pl.BlockSpec((1, nq, w), lambda i, r: (i, r, 0)),
                  pl.BlockSpec((1, t, w), lambda i, r: (i, 0, 0)),
                  pl.BlockSpec((1, t, 2 * w), lambda i, r: (i, 0, 0)),
                  pl.BlockSpec((1, tc, w), lambda i, r: (i, 0, 0)),
                  pl.BlockSpec((1, tc, 2 * w), lambda i, r: (i, 0, 0)),
                  pl.BlockSpec((1, NA_HEADS // 2, 2 * nq, NA_KROWS * GRID_W), bias_map)],
        out_specs=pl.BlockSpec((1, nq, w), lambda i, r: (i, r, 0)),
        out_shape=jax.ShapeDtypeStruct((b, t, w), BF16),
        compiler_params=_params(2),
        name="na_attn",
    )(q, k, v, kc, vc, bias)


GDN_MAX_BLOCK = 16 * GDN_CHUNK
GDN_GROUP_CHUNKS = 4
GDN_PAIRS = GDN_HEADS // 2
TN_DIMS = (((0,), (0,)), ((), ()))


def _split3(x):
    hi = x.astype(BF16)
    r = x - hi.astype(F32)
    mid = r.astype(BF16)
    lo = (r - mid.astype(F32)).astype(BF16)
    return hi, mid, lo


def _bd(x):
    lane = lax.broadcasted_iota(jnp.int32, x.shape, 1)
    z = jnp.zeros_like(x)
    return jnp.concatenate([jnp.where(lane < HEAD_W, x, z), jnp.where(lane >= HEAD_W, x, z)], axis=0)


def _dotb(a, b):
    return jnp.dot(a.astype(BF16), b.astype(BF16), preferred_element_type=F32)


def _mm_pair(x, y):
    return _dotb(x, _bd(y))


def _softplus(x):
    return jnp.maximum(x, 0.0) + jnp.log(1.0 + jnp.exp(-jnp.abs(x)))


def _gdn_kernel(qkv_ref, misc_ref, alog_ref, dtb_ref, eg_ref, eb_ref, tri3_ref, s0_ref, o_ref, sfin_ref, s_ref, *,
                rev, n_blk):
    step = pl.program_id(1)
    c_len = GDN_CHUNK
    hw = GDN_HEADS * HEAD_W

    @pl.when(step == 0)
    def _():
        s_ref[...] = s0_ref[0]

    qn = qkv_ref[0, :, 0:hw].astype(F32)
    kn = qkv_ref[0, :, hw:2 * hw].astype(F32)
    v = qkv_ref[0, :, 2 * hw:3 * hw].astype(F32)

    misc = misc_ref[0]
    g_all = -jnp.exp(alog_ref[...]) * _softplus(misc + dtb_ref[...])
    b_all = jax.nn.sigmoid(misc)
    gx = jnp.dot(jnp.concatenate(_split3(g_all), axis=1), eg_ref[...], preferred_element_type=F32)
    bx = jnp.dot(jnp.concatenate(_split3(b_all), axis=1), eb_ref[...], preferred_element_type=F32)

    row = lax.broadcasted_iota(jnp.int32, (c_len, LANES), 0)
    colp = lax.broadcasted_iota(jnp.int32, (c_len, LANES), 1) & (HEAD_W - 1)
    tri = (row <= colp) if rev else (row >= colp)
    strict = (row < colp) if rev else (row > colp)
    eye_f = (row == colp).astype(F32)
    row_w = lax.broadcasted_iota(jnp.int32, (c_len, hw), 0)
    col_w = lax.broadcasted_iota(jnp.int32, (c_len, hw), 1) & (HEAD_W - 1)
    eye_w = row_w == col_w
    bd_row = lax.broadcasted_iota(jnp.int32, (LANES, LANES), 0)
    bd_col = lax.broadcasted_iota(jnp.int32, (LANES, LANES), 1)
    bd_mask = (bd_row < HEAD_W) == (bd_col < HEAD_W)

    n_chunks = qkv_ref.shape[1] // c_len
    pairs = range(GDN_PAIRS)
    sl = [slice(p * LANES, (p + 1) * LANES) for p in pairs]
    rows_of = lambda c: slice(c * c_len, (c + 1) * c_len)
    gcs, kbs, aqs, grs, gtot, vb, kbg, qd, kd, ms, qks, xs, inner, uws = ({} for _ in range(14))
    state = [s_ref[p] for p in pairs]

    def prep_stages(chunks):
        keys = [(c, p) for c in chunks for p in pairs]

        def decay_sums():
            for c in chunks:
                gcs[c] = jnp.dot(tri3_ref[...], jnp.concatenate(_split3(gx[rows_of(c)]), axis=0),
                                 preferred_element_type=F32)
                kbs[c] = kn[rows_of(c)] * bx[rows_of(c)]

        def gram():
            for c, p in keys:
                r = rows_of(c)
                lhs = jnp.concatenate([kbs[c][:, sl[p]], qn[r][:, sl[p]]], axis=0)
                aqs[c, p] = lax.dot_general(lhs.astype(BF16), _bd(kn[r][:, sl[p]]).astype(BF16), NT_DIMS,
                                            preferred_element_type=F32)

        def decay_rows():
            for c in chunks:
                grs[c] = jnp.sum(jnp.where(eye_w, gcs[c], 0.0), axis=0, keepdims=True)

        def masks():
            for c in chunks:
                r, gc = rows_of(c), gcs[c]
                gtot[c] = gc[0:1] if rev else gc[c_len - 1:c_len]
                egc = jnp.exp(gc)
                vb[c] = v[r] * bx[r]
                kbg[c] = kbs[c] * egc
                qd[c] = qn[r] * egc
                kd[c] = kn[r] * jnp.exp(gtot[c] - gc)
                for p in pairs:
                    diff = gc[:, sl[p]] - grs[c][:, sl[p]]
                    decay = jnp.where(tri, jnp.exp(jnp.where(tri, diff, 0.0)), 0.0)
                    ms[c, p] = jnp.where(strict, aqs[c, p][0:c_len] * decay, 0.0)
                    qks[c, p] = aqs[c, p][c_len:2 * c_len] * decay

        def join_mask(lvl):
            rblk, cblk = row >> lvl, colp >> lvl
            sib = (cblk == rblk + 1) if rev else (cblk == rblk - 1)
            return sib & ((rblk & 1) == (0 if rev else 1))

        def level0():
            for key in keys:
                xs[key] = eye_f - jnp.where(join_mask(0), ms[key], 0.0)

        def first_product(lvl):
            def run():
                for key in keys:
                    inner[key] = _mm_pair(jnp.where(join_mask(lvl), ms[key], 0.0), xs[key])
            return run

        def second_product():
            for key in keys:
                xs[key] = xs[key] - _mm_pair(xs[key], inner[key])

        def apply_inverse():
            for c, p in keys:
                rhs = jnp.concatenate([_bd(vb[c][:, sl[p]]), _bd(kbg[c][:, sl[p]])], axis=1)
                uws[c, p] = _dotb(xs[c, p], rhs)

        stages = [decay_sums, gram, decay_rows, masks, level0]
        for lvl in range(1, 6):
            stages += [first_product(lvl), second_product]
        return stages + [apply_inverse]

    def scan_stages(chunks):
        v_new = {}
        wq = {}

        def read_state(c):
            def run():
                for p in pairs:
                    wq[p] = _dotb(jnp.concatenate([uws[c, p][:, LANES:2 * LANES], qd[c][:, sl[p]]], axis=0), state[p])
                    v_new[p] = uws[c, p][:, 0:LANES] - wq[p][0:c_len]
            return run

        def write_state(c):
            def run():
                upd = [lax.dot_general(kd[c][:, sl[p]].astype(BF16), v_new[p].astype(BF16), TN_DIMS,
                                       preferred_element_type=F32) for p in pairs]
                for p in pairs:
                    o_ref[0, rows_of(c), sl[p]] = wq[p][c_len:2 * c_len] + _dotb(qks[c, p], _bd(v_new[p]))
                    state[p] = state[p] * jnp.exp(gtot[c][:, sl[p]]) + jnp.where(bd_mask, upd[p], 0.0)
            return run

        stages = []
        for c in chunks:
            stages += [read_state(c), write_state(c)]
        return stages

    order = list(range(n_chunks - 1, -1, -1) if rev else range(n_chunks))
    halves = [order[i:i + GDN_GROUP_CHUNKS] for i in range(0, n_chunks, GDN_GROUP_CHUNKS)]
    for stage in prep_stages(halves[0]):
        stage()
    for half, nxt in zip(halves, halves[1:] + [None]):
        scan = scan_stages(half)
        prep = prep_stages(nxt) if nxt else []
        done = 0
        for i, stage in enumerate(prep):
            stage()
            while done < len(scan) and done * len(prep) < (i + 1) * len(scan):
                scan[done]()
                done += 1
        for stage in scan[done:]:
            stage()
    for p in range(GDN_PAIRS):
        s_ref[p] = state[p]

    @pl.when(step == n_blk - 1)
    def _():
        for p in range(GDN_PAIRS):
            sfin_ref[0, p] = state[p]


def _gdn_consts(a_log, dt_bias, d, rev):
    pad = jnp.zeros((LANES - 2 * GDN_HEADS,), F32)
    alog = jnp.concatenate([a_log.reshape(-1).astype(F32), pad]).reshape(1, LANES)
    dtb = jnp.concatenate([dt_bias.reshape(-1).astype(F32), pad]).reshape(1, LANES)
    hw = GDN_HEADS * HEAD_W
    head_of_lane = np.arange(hw) // HEAD_W
    src = np.arange(LANES)[:, None]
    eg = (src == d * GDN_HEADS + head_of_lane[None, :]).astype(np.float32)
    eb = (src == 2 * GDN_HEADS + d * GDN_HEADS + head_of_lane[None, :]).astype(np.float32)
    i = np.arange(GDN_CHUNK)
    tri = (i[:, None] <= i[None, :]) if rev else (i[:, None] >= i[None, :])
    as_bf = lambda a, reps, ax: jnp.asarray(np.concatenate([a] * reps, axis=ax), BF16)
    return alog, dtb, as_bf(eg, 3, 0), as_bf(eb, 3, 0), as_bf(tri.astype(np.float32), 3, 1)


def _gdn_call(qkv, misc, consts, s0, rev, name):
    b, t, w = qkv.shape
    blk = min(t, GDN_MAX_BLOCK)
    assert t % blk == 0 and blk % GDN_CHUNK == 0
    n_blk = t // blk
    hw = GDN_HEADS * HEAD_W
    blk_of = (lambda s: n_blk - 1 - s) if rev else (lambda s: s)
    const = lambda a: pl.BlockSpec(a.shape, lambda i, s: (0,) * a.ndim)
    s_spec = pl.BlockSpec((1, GDN_PAIRS, LANES, LANES), lambda i, s: (i, 0, 0, 0))
    return pl.pallas_call(
        functools.partial(_gdn_kernel, rev=rev, n_blk=n_blk),
        grid=(b, n_blk),
        in_specs=[pl.BlockSpec((1, blk, w), lambda i, s: (i, blk_of(s), 0)),
                  pl.BlockSpec((1, blk, LANES), lambda i, s: (i, blk_of(s), 0))]
                 + [const(a) for a in consts] + [s_spec],
        out_specs=[pl.BlockSpec((1, blk, hw), lambda i, s: (i, blk_of(s), 0)), s_spec],
        out_shape=[jax.ShapeDtypeStruct((b, t, hw), F32),
                   jax.ShapeDtypeStruct((b, GDN_PAIRS, LANES, LANES), F32)],
        scratch_shapes=[pltpu.VMEM((GDN_PAIRS, LANES, LANES), F32)],
        compiler_params=_params(2),
        name=name,
    )(qkv, misc, *consts, s0)


MIX_SUB_ROWS = 256
FFN_CHUNKS = 2


def _mix_ffn_kernel(*refs, ff_chunks, gdn_inputs, n_sub):
    tm = refs[0].shape[1]
    rows = [slice(i * (tm // n_sub), (i + 1) * (tm // n_sub)) for i in range(n_sub)]
    if gdn_inputs:
        (h_ref, of_ref, ob_ref, gate_ref, att_ref, on_ref, ones_ref, gtm_ref, shf_ref, scf_ref, gtf_ref, gpm_ref,
         gpf_ref, gqf_ref, wo_ref, wg_ref, wu_ref, wd_ref, o_ref) = refs
        mixes = []
        for r in rows:
            o = of_ref[0, r, :] + ob_ref[0, r, :]
            gate = gate_ref[0, r, :].astype(F32)
            a = o * lax.rsqrt(_seg_mean_sq(o, ones_ref) + NORM_EPS) * on_ref[...] * (gate * jax.nn.sigmoid(gate))
            mixes.append(jnp.concatenate([a.astype(BF16), att_ref[0, r, :]], axis=1))
    else:
        (h_ref, a_ref, b_ref, gtm_ref, shf_ref, scf_ref, gtf_ref, gpm_ref, gpf_ref, gqf_ref, wo_ref, wg_ref,
         wu_ref, wd_ref, o_ref) = refs
        mixes = [jnp.concatenate([a_ref[0, r, :], b_ref[0, r, :]], axis=1) for r in rows]
    ys = [_dotf(mix, wo_ref[...]) for mix in mixes]
    h1s = [h_ref[0, r, :] + gtm_ref[0] * (_rms(y) * gpm_ref[...]) for r, y in zip(rows, ys)]
    us = [(_rms(h1) * gpf_ref[...] * (1.0 + scf_ref[0]) + shf_ref[0]).astype(BF16) for h1 in h1s]
    acts = []
    for o, n in ff_chunks:
        for u in us:
            gg = _dotf(u, wg_ref[:, o:o + n])
            uu = _dotf(u, wu_ref[:, o:o + n])
            acts.append((gg * jax.nn.sigmoid(gg) * uu).astype(BF16))
    fs = [None] * n_sub
    for ci, (o, n) in enumerate(ff_chunks):
        for i in range(n_sub):
            part = _dotf(acts[ci * n_sub + i], wd_ref[o:o + n, :])
            fs[i] = part if fs[i] is None else fs[i] + part
    for r, h1, f in zip(rows, h1s, fs):
        o_ref[0, r, :] = h1 + gtf_ref[0] * (_rms(f) * gqf_ref[...])


def _mix_ffn_call(h, mix_inputs, out_norm, gt_m, sh_f, sc_f, gt_f, g_post_mix, g_pre_ffn, g_post_ffn, wo, wg, wu,
                  wd, tm, name):
    b, t, d = h.shape
    dm = wo.shape[0]
    ff = wg.shape[1]
    gdn_inputs = out_norm is not None
    step = pl.cdiv(pl.cdiv(ff, FFN_CHUNKS), MXU_TILE) * MXU_TILE
    assert ff % LANES == 0
    ff_chunks = tuple((o, min(step, ff - o)) for o in range(0, ff, step))
    tok = lambda n: pl.BlockSpec((1, tm, n), lambda i, j: (i, j, 0))
    mods = [_mod_operand(m, d) for m in (gt_m, sh_f, sc_f, gt_f)]
    vec = pl.BlockSpec((1, d), lambda i, j: (0, 0))
    const = lambda shape: pl.BlockSpec(shape, lambda i, j: (0, 0), pipeline_mode=pl.Buffered(1))
    r2 = lambda a: a.reshape(1, d).astype(F32)
    mix_specs = [tok(m.shape[-1]) for m in mix_inputs]
    mix_args = list(mix_inputs)
    if gdn_inputs:
        hw = GDN_HEADS * GDN_DV
        on = jnp.tile(out_norm.astype(F32), GDN_HEADS).reshape(1, hw)
        ones = _seg_ones(hw)
        mix_specs += [pl.BlockSpec(on.shape, lambda i, j: (0, 0)), pl.BlockSpec(ones.shape, lambda i, j: (0, 0))]
        mix_args += [on, ones]
    return pl.pallas_call(
        functools.partial(_mix_ffn_kernel, ff_chunks=ff_chunks, gdn_inputs=gdn_inputs,
                          n_sub=tm // MIX_SUB_ROWS),
        grid=(b, t // tm),
        in_specs=[tok(d)] + mix_specs + [spec for _, spec in mods]
                 + [vec, vec, vec, const((dm, d)), const((d, ff)), const((d, ff)), const((ff, d))],
        out_specs=tok(d),
        out_shape=jax.ShapeDtypeStruct((b, t, d), F32),
        compiler_params=_params(2),
        name=name,
    )(h, *mix_args, *[arr for arr, _ in mods], r2(g_post_mix), r2(g_pre_ffn), r2(g_post_ffn), wo, wg, wu, wd)


def _rope_tables(n_tokens, rot_dim):
    t = jnp.arange(n_tokens, dtype=jnp.int32)
    row = (t // GRID_W).astype(F32)
    col = (t % GRID_W).astype(F32)
    n_freq = rot_dim // 4
    inv_freq = ROPE_THETA ** (-jnp.arange(n_freq, dtype=F32) / n_freq)
    ang = jnp.concatenate([row[:, None] * inv_freq, col[:, None] * inv_freq], -1)
    cos = jnp.concatenate([jnp.cos(ang), jnp.cos(ang)], -1)
    sin = jnp.concatenate([-jnp.sin(ang), jnp.sin(ang)], -1)
    if rot_dim == HEAD_W:
        return jnp.tile(cos, (1, LANES // rot_dim)), jnp.tile(sin, (1, LANES // rot_dim))
    pad = LANES - rot_dim
    return (jnp.concatenate([cos, jnp.ones((n_tokens, pad), F32)], -1),
            jnp.concatenate([sin, jnp.zeros((n_tokens, pad), F32)], -1))


def _no_rope_tables(n_tokens):
    return jnp.ones((n_tokens, LANES), F32), jnp.zeros((n_tokens, LANES), F32)


def _gdn_bidirectional(lat_qkv, lat_misc, ctx_qkv, ctx_misc, a_log, dt_bias):
    bsz = lat_qkv.shape[0]
    s_zero = jnp.zeros((bsz, GDN_PAIRS, LANES, LANES), F32)
    lat, ctx = [], []
    for d, rev in ((0, False), (1, True)):
        consts = _gdn_consts(a_log, dt_bias, d, rev)
        o_c, s_c = _gdn_call(ctx_qkv, ctx_misc, consts, s_zero, rev, "gdn_ctx_%d" % d)
        o_l, _ = _gdn_call(lat_qkv, lat_misc, consts, s_c, rev, "gdn_lat_%d" % d)
        lat.append(o_l)
        ctx.append(o_c)
    return lat, ctx


LOG2E = 1.4426950408889634
ATTN_TQ = 1024
ATTN_LOOKAHEAD = 2


def _attn_chunks(l_len, s_len, size):
    return [(0, 0, l_len)] + [(1, o, size) for o in range(0, s_len, size)]


def _even_mixer(h, hc, mods, mods_c, g_pre, w_in, conv_w, a_log, dt_bias, q_norm, w_q_up, kv_norm, w_kv_up):
    s_len, l_len = h.shape[1], hc.shape[1]
    weights = _even_weights(w_in, q_norm, w_q_up, kv_norm, w_kv_up)
    qkv_l, gate_l, misc_l, q_l, k_l, v_l = _in_even_call(h, g_pre, *mods, weights, conv_w,
                                                         *_rope_tables(s_len, MLA_ROPE), 512, "in_even_lat")
    qkv_c, gate_c, misc_c, q_c, k_c, v_c = _in_even_call(hc, g_pre, *mods_c, weights, conv_w,
                                                         *_no_rope_tables(l_len), l_len, "in_even_ctx")
    o_lat, o_ctx = _gdn_bidirectional(qkv_l, misc_l, qkv_c, misc_c, a_log, dt_bias)
    b_lat = _attn_call(q_l, [(k_c, v_c), (k_l, v_l)], MLA_QC, MLA_PAIRS, MLA_PAIRS, _attn_chunks(l_len, s_len, 512),
                       ATTN_TQ, False, "mla_attn_lat")
    b_ctx = _attn_call(q_c, [(k_c, v_c)], MLA_QC, MLA_PAIRS, MLA_PAIRS, [(0, 0, l_len)], l_len, False,
                       "mla_attn_ctx")
    return (o_lat[0], o_lat[1], gate_l, b_lat), (o_ctx[0], o_ctx[1], gate_c, b_ctx)


def _odd_mixer_last(h, hc, mods, mods_c, g_pre, w_in, rpb, q_norm, k_norm):
    s_len, l_len = h.shape[1], hc.shape[1]
    w = w_in.astype(BF16)
    q_gain = (jnp.tile(q_norm.astype(F32), GQA_HEADS) * (GQA_DIM ** -0.5 * LOG2E)).reshape(1, -1)
    k_gain = jnp.tile(k_norm.astype(F32), GQA_KV_HEADS).reshape(1, -1)
    naq_l, nak_l, nav_l, gq_l, gk_l, gv_l = _in_odd_call(h, g_pre, *mods, w, q_gain, k_gain,
                                                         *_rope_tables(s_len, GQA_DIM), 512, "in_odd_lat")
    _, nak_c, nav_c, _, gk_c, gv_c = _in_odd_call(hc, g_pre, *mods_c, w, q_gain, k_gain, *_no_rope_tables(l_len),
                                                  l_len, "in_odd_ctx")
    c_lat = _na_call(naq_l, nak_l, nav_l, nak_c, nav_c, _na_bias_table(rpb, s_len // GRID_W))
    d_lat = _attn_call(gq_l, [(gk_c, gv_c), (gk_l, gv_l)], LANES, GQA_HEADS // 2, GQA_KV_HEADS,
                       _attn_chunks(l_len, s_len, 256), 2 * ATTN_TQ, True, "gqa_attn")
    return c_lat, d_lat


def kernel(x, c, ctx, c_ctx, w_mod, b_mod, g_pre_mix, g_post_mix, g_pre_ffn, g_post_ffn, w_ffn_gate, w_ffn_up,
           w_ffn_down, w_in_even, w_out_even, gdn_conv, gdn_a_log, gdn_dt_bias, gdn_out_norm, mla_q_norm,
           mla_w_q_up, mla_kv_norm, mla_w_kv_up, w_in_odd, w_out_odd, na_rpb, gqa_q_norm, gqa_k_norm):
    bsz, s_len, d = x.shape
    depth = w_mod.shape[0]
    assert depth == 2, "layer 0 = even mixer with context update, layer 1 = odd mixer (last)"
    c_rows = jnp.concatenate([c, c_ctx[None, :], jnp.zeros((16 - bsz - 1, d), F32)], 0)
    h, hc = x, ctx
    for i in range(depth):
        mod_tab = _mod_call(c_rows, w_mod[i].astype(BF16), b_mod[i]).reshape(c_rows.shape[0] * N_MOD, 1, d)
        sh_m, sc_m, gt_m, sh_f, sc_f, gt_f = (_Mod(mod_tab, k, None) for k in range(N_MOD))
        csh_m, csc_m, cgt_m, csh_f, csc_f, cgt_f = (_Mod(mod_tab, k, bsz) for k in range(N_MOD))
        wg, wu, wd = w_ffn_gate[i].astype(BF16), w_ffn_up[i].astype(BF16), w_ffn_down[i].astype(BF16)
        if i == 0:
            mix, mix_c = _even_mixer(h, hc, (sh_m, sc_m), (csh_m, csc_m), g_pre_mix[i], w_in_even[0], gdn_conv[0],
                                     gdn_a_log[0], gdn_dt_bias[0], mla_q_norm[0], mla_w_q_up[0], mla_kv_norm[0],
                                     mla_w_kv_up[0])
            wo, out_norm = w_out_even[0].astype(BF16), gdn_out_norm[0]
            hc = _mix_ffn_call(hc, mix_c, out_norm, cgt_m, csh_f, csc_f, cgt_f, g_post_mix[i], g_pre_ffn[i],
                               g_post_ffn[i], wo, wg, wu, wd, 256, "mix_ffn_ctx")
        else:
            mix = _odd_mixer_last(h, hc, (sh_m, sc_m), (csh_m, csc_m), g_pre_mix[i], w_in_odd[0], na_rpb[0],
                                  gqa_q_norm[0], gqa_k_norm[0])
            wo, out_norm = w_out_odd[0].astype(BF16), None
        h = _mix_ffn_call(h, mix, out_norm, gt_m, sh_f, sc_f, gt_f, g_post_mix[i], g_pre_ffn[i], g_post_ffn[i],
                          wo, wg, wu, wd, 2 * MIX_SUB_ROWS, "mix_ffn_lat%d" % i)
    return h
```

```python
import functools
from typing import NamedTuple, Optional

import numpy as np
import jax
import jax.numpy as jnp
from jax import lax
from jax.experimental import pallas as pl
from jax.experimental.pallas import tpu as pltpu

F32 = jnp.float32
BF16 = jnp.bfloat16

GRID_W = 64
NORM_EPS = 1e-6
ROPE_THETA = 10000.0

GDN_HEADS = 8
GDN_DK = 64
GDN_DV = 64
GDN_CHUNK = 64
GDN_QK_W = GDN_HEADS * GDN_DK
GDN_V_W = GDN_HEADS * GDN_DV

MLA_HEADS = 8
MLA_NOPE = 64
MLA_ROPE = 32
MLA_V = 64
MLA_Q_LORA = 256
MLA_KV_LORA = 128

NA_HEADS = 8
NA_DIM = 64
NA_WIN_H = 8
NA_WIN_W = 16
NA_W = NA_HEADS * NA_DIM

GQA_HEADS = 8
GQA_KV_HEADS = 2
GQA_DIM = 64

LANES = 128
MXU_TILE = 256
HEAD_W = 64
MASK_VALUE = -1e30
VMEM_LIMIT = 56 << 20

NT_DIMS = (((1,), (1,)), ((), ()))


def _params(n_grid, vmem=VMEM_LIMIT):
    return pltpu.CompilerParams(dimension_semantics=("arbitrary",) * n_grid, vmem_limit_bytes=vmem)


def _rms(x):
    return x * lax.rsqrt(jnp.mean(x * x, axis=-1, keepdims=True) + NORM_EPS)


def _mod_kernel(c_ref, w_ref, b_ref, o_ref):
    c = c_ref[...]
    a = (c * jax.nn.sigmoid(c)).astype(BF16)
    o_ref[...] = jnp.dot(a, w_ref[...], preferred_element_type=F32) + b_ref[...]


def _mod_call(c_rows, w, b):
    r, d = c_rows.shape
    n = w.shape[1]
    tn = 1024
    return pl.pallas_call(
        _mod_kernel,
        grid=(n // tn,),
        in_specs=[pl.BlockSpec((r, d), lambda j: (0, 0)),
                  pl.BlockSpec((d, tn), lambda j: (0, j)),
                  pl.BlockSpec((1, tn), lambda j: (0, j))],
        out_specs=pl.BlockSpec((r, tn), lambda j: (0, j)),
        out_shape=jax.ShapeDtypeStruct((r, n), F32),
        compiler_params=_params(1),
        name="mod",
    )(c_rows, w, b.reshape(1, n))


class _Mod(NamedTuple):
    table: jax.Array
    comp: int
    row: Optional[int]


N_MOD = 6


def _mod_operand(m, d):
    if m.row is None:
        return m.table, pl.BlockSpec((1, 1, d), lambda i, j: (i * N_MOD + m.comp, 0, 0))
    return m.table, pl.BlockSpec((1, 1, d), lambda i, j: (m.row * N_MOD + m.comp, 0, 0))


def _norm_mod(x_ref, g_ref, sh_ref, sc_ref):
    return (_rms(x_ref[0]) * g_ref[...] * (1.0 + sc_ref[0]) + sh_ref[0]).astype(BF16)


def _dotf(a, b):
    return jnp.dot(a, b, preferred_element_type=F32)


def _seg_mean_sq(x, ones_ref):
    return _dotf((x * x).astype(BF16), ones_ref[...]) * (1.0 / HEAD_W)


def _rotate(t, cos, sin_signed, half):
    lane = lax.broadcasted_iota(jnp.int32, t.shape, 1)
    n = t.shape[1]
    partner = jnp.where((lane & (2 * half - 1)) < half, pltpu.roll(t, n - half, 1), pltpu.roll(t, half, 1))
    return t * cos + partner * sin_signed


EVEN_QKV_W = 2 * GDN_QK_W + GDN_V_W
EVEN_GROUPS = (EVEN_QKV_W, GDN_V_W, MLA_Q_LORA, MLA_KV_LORA, LANES, LANES)
MLA_QC = 2 * LANES
MLA_PAIRS = MLA_HEADS // 2


CONV_HALO = 16


def _in_even_kernel(x_ref, xp_ref, xn_ref, g_ref, sh_ref, sc_ref, w_ref, qg_ref, wq_ref, kvg_ref, wk_ref, wv_ref,
                    vone_ref, cos_ref, sin_ref, cw_ref, ones_ref, qkv_ref, gate_ref, misc_ref, q_ref, k_ref, v_ref,
                    zs_ref, *, n_tap):
    j = pl.program_id(1)
    tm = x_ref.shape[1]
    u = _norm_mod(x_ref, g_ref, sh_ref, sc_ref)
    offs = np.cumsum((0,) + EVEN_GROUPS)
    grp = lambda i: _dotf(u, w_ref[:, int(offs[i]):int(offs[i + 1])])
    u_ext = jnp.concatenate([_norm_mod(xp_ref, g_ref, sh_ref, sc_ref), u, _norm_mod(xn_ref, g_ref, sh_ref, sc_ref)],
                            axis=0)
    z = _dotf(u_ext, w_ref[:, 0:EVEN_QKV_W])
    zs_ref[0:CONV_HALO, :] = z[0:CONV_HALO] * (j > 0).astype(F32)
    zs_ref[CONV_HALO:CONV_HALO + tm, :] = z[CONV_HALO:CONV_HALO + tm]
    zs_ref[CONV_HALO + tm:, :] = z[CONV_HALO + tm:] * (j < pl.num_programs(1) - 1).astype(F32)
    q_down, pe_raw = grp(2), grp(5)
    kv_misc = _dotf(u, w_ref[:, int(offs[3]):int(offs[5])])
    kv_down, misc = kv_misc[:, 0:MLA_KV_LORA], kv_misc[:, MLA_KV_LORA:]
    gate_ref[0] = grp(1).astype(BF16)
    misc_ref[0] = misc
    cos, sin = cos_ref[...], sin_ref[...]
    qn = (_rms(q_down) * qg_ref[...]).astype(BF16)
    kvn = (_rms(kv_down) * kvg_ref[...]).astype(BF16)
    for h in range(MLA_HEADS):
        blk = _dotf(qn, wq_ref[:, h * MLA_QC:(h + 1) * MLA_QC])
        q_ref[0, :, h * MLA_QC:h * MLA_QC + LANES] = blk[:, 0:LANES].astype(BF16)
        q_ref[0, :, h * MLA_QC + LANES:(h + 1) * MLA_QC] = _rotate(blk[:, LANES:], cos, sin,
                                                                    MLA_ROPE // 2).astype(BF16)
    pe = _rotate(pe_raw, cos, sin, MLA_ROPE // 2).astype(BF16)
    k_nope = _dotf(kvn, wk_ref[...])
    for p in range(MLA_PAIRS):
        k_ref[0, :, p * MLA_QC:p * MLA_QC + LANES] = k_nope[:, p * LANES:(p + 1) * LANES].astype(BF16)
        k_ref[0, :, p * MLA_QC + LANES:(p + 1) * MLA_QC] = pe
    v_ref[0] = (_dotf(kvn, wv_ref[...]) + vone_ref[...]).astype(BF16)
    acc = None
    for t in range(n_tap):
        start = CONV_HALO - n_tap // 2 + t
        term = zs_ref[start:start + tm, :] * cw_ref[t:t + 1, :]
        acc = term if acc is None else acc + term
    y = acc * jax.nn.sigmoid(acc)
    hw = GDN_QK_W
    qc, kc = y[:, 0:hw], y[:, hw:2 * hw]
    q_ss = _seg_mean_sq(qc, ones_ref) * HEAD_W
    k_ss = _seg_mean_sq(kc, ones_ref) * HEAD_W
    qkv_ref[0, :, 0:hw] = (qc * lax.rsqrt(q_ss + NORM_EPS) * (GDN_DK ** -0.5)).astype(BF16)
    qkv_ref[0, :, hw:2 * hw] = (kc * lax.rsqrt(k_ss + NORM_EPS)).astype(BF16)
    qkv_ref[0, :, 2 * hw:] = y[:, 2 * hw:].astype(BF16)


def _even_weights(w_in, q_norm, w_q_up, kv_norm, w_kv_up):
    d = w_in.shape[0]
    o0 = EVEN_QKV_W + GDN_V_W
    n_ab = 4 * GDN_HEADS
    o1 = o0 + n_ab
    o2 = o1 + MLA_Q_LORA
    o3 = o2 + MLA_KV_LORA
    zeros = lambda n, rows=d: jnp.zeros((rows, n), F32)
    w_cat = jnp.concatenate([w_in[:, :o0], w_in[:, o1:o2], w_in[:, o2:o3],
                             w_in[:, o0:o1], zeros(LANES - n_ab),
                             w_in[:, o3:], zeros(LANES - MLA_ROPE)], 1).astype(BF16)
    qh = MLA_NOPE + MLA_ROPE
    cols = []
    for h in range(MLA_HEADS):
        nope = w_q_up[:, h * qh:h * qh + MLA_NOPE]
        rp = w_q_up[:, h * qh + MLA_NOPE:(h + 1) * qh]
        z = zeros(HEAD_W, MLA_Q_LORA)
        cols += ([nope, z] if h % 2 == 0 else [z, nope]) + [rp, zeros(LANES - MLA_ROPE, MLA_Q_LORA)]
    wq = jnp.concatenate(cols, 1).astype(BF16)
    kvh = MLA_NOPE + MLA_V
    wk = jnp.concatenate([w_kv_up[:, h * kvh:h * kvh + MLA_NOPE] for h in range(MLA_HEADS)], 1).astype(BF16)
    vcols, ones = [], []
    for h in range(MLA_HEADS):
        vh = w_kv_up[:, h * kvh + MLA_NOPE:(h + 1) * kvh]
        z = zeros(HEAD_W, MLA_KV_LORA)
        vcols += [vh, z] if h % 2 == 0 else [z, vh]
        ones += [0.0, 1.0] if h % 2 == 0 else [1.0, 0.0]
    wv = jnp.concatenate(vcols, 1).astype(BF16)
    vone = jnp.asarray(np.repeat(np.asarray(ones, np.float32), HEAD_W)[None, :])
    q_gain = (q_norm * ((MLA_NOPE + MLA_ROPE) ** -0.5 * LOG2E)).reshape(1, -1).astype(F32)
    return w_cat, q_gain, wq, kv_norm.reshape(1, -1).astype(F32), wk, wv, vone


def _in_even_call(x, g, shift, scale, weights, conv_w, cos, sin, tm, name):
    b, t, d = x.shape
    w_cat, q_gain, wq, kv_gain, wk, wv, vone = weights
    n_tap = conv_w.shape[0]
    cw = jnp.concatenate([conv_w.astype(F32), jnp.zeros((8 - n_tap, conv_w.shape[1]), F32)], 0)
    ones = _seg_ones(GDN_QK_W)
    per = tm // CONV_HALO
    n_halo = t // CONV_HALO
    const = lambda a: pl.BlockSpec(a.shape, lambda i, j: (0,) * a.ndim)
    (sh_arr, sh_spec), (sc_arr, sc_spec) = _mod_operand(shift, d), _mod_operand(scale, d)
    tab = pl.BlockSpec((tm, LANES), lambda i, j: (j, 0))
    widths = (EVEN_QKV_W, GDN_V_W, LANES, MLA_HEADS * MLA_QC, MLA_PAIRS * MLA_QC, MLA_HEADS * LANES)
    dts = (BF16, BF16, F32, BF16, BF16, BF16)
    g2 = g.reshape(1, d).astype(F32)
    return pl.pallas_call(
        functools.partial(_in_even_kernel, n_tap=n_tap),
        grid=(b, t // tm),
        in_specs=[pl.BlockSpec((1, tm, d), lambda i, j: (i, j, 0)),
                  pl.BlockSpec((1, CONV_HALO, d), lambda i, j: (i, jnp.maximum(j * per - 1, 0), 0)),
                  pl.BlockSpec((1, CONV_HALO, d), lambda i, j: (i, jnp.minimum((j + 1) * per, n_halo - 1), 0)),
                  const(g2), sh_spec, sc_spec, const(w_cat), const(q_gain), const(wq), const(kv_gain), const(wk),
                  const(wv), const(vone), tab, tab, const(cw), const(ones)],
        out_specs=[pl.BlockSpec((1, tm, n), lambda i, j: (i, j, 0)) for n in widths],
        out_shape=[jax.ShapeDtypeStruct((b, t, n), dt) for n, dt in zip(widths, dts)],
        scratch_shapes=[pltpu.VMEM((tm + 2 * CONV_HALO, EVEN_QKV_W), F32)],
        compiler_params=_params(2),
        name=name,
    )(x, x, x, g2, sh_arr, sc_arr, w_cat, q_gain, wq, kv_gain, wk, wv, vone,
      cos, sin, cw, ones)


GQA_KV_W = GQA_KV_HEADS * GQA_DIM
PROJ_SUB_ROWS = 256


def _in_odd_kernel(x_ref, g_ref, sh_ref, sc_ref, w_ref, qg_ref, kg_ref, ones_q_ref, ones_k_ref, cos_ref, sin_ref,
                   naq_ref, nak_ref, nav_ref, gq_ref, gk_ref, gv_ref):
    tm = x_ref.shape[1]
    n_sub = max(tm // PROJ_SUB_ROWS, 1)
    subs = [slice(i * (tm // n_sub), (i + 1) * (tm // n_sub)) for i in range(n_sub)]
    us = [(_rms(x_ref[0, r, :]) * g_ref[...] * (1.0 + sc_ref[0]) + sh_ref[0]).astype(BF16) for r in subs]
    o_q = 3 * NA_W
    o_k = o_q + GQA_HEADS * GQA_DIM
    qs = [_dotf(u, w_ref[:, o_q:o_k]) for u in us]
    kvs = [_dotf(u, w_ref[:, o_k:o_k + 2 * GQA_KV_W]) for u in us]
    ks = [kv[:, 0:GQA_KV_W] for kv in kvs]
    vs = [kv[:, GQA_KV_W:] for kv in kvs]
    low = lax.broadcasted_iota(jnp.int32, (tm // n_sub, LANES), 1) < HEAD_W
    for r, u in zip(subs, us):
        naq_ref[0, r, :] = (_dotf(u, w_ref[:, 0:NA_W]) * (NA_DIM ** -0.5 * LOG2E)).astype(BF16)
        nak_ref[0, r, :] = _dotf(u, w_ref[:, NA_W:2 * NA_W]).astype(BF16)
        na_v = _dotf(u, w_ref[:, 2 * NA_W:3 * NA_W])
        for p in range(NA_HEADS // 2):
            tile = na_v[:, p * LANES:(p + 1) * LANES]
            nav_ref[0, r, 2 * p * LANES:(2 * p + 1) * LANES] = tile.astype(BF16)
            nav_ref[0, r, (2 * p + 1) * LANES:(2 * p + 2) * LANES] = jnp.ones(tile.shape, BF16)
    q_ms = [_seg_mean_sq(q, ones_q_ref) for q in qs]
    k_ms = [_seg_mean_sq(k, ones_k_ref) for k in ks]
    for i, r in enumerate(subs):
        cos, sin = cos_ref[r, :], sin_ref[r, :]
        qn = qs[i] * lax.rsqrt(q_ms[i] + NORM_EPS) * qg_ref[...]
        for p in range(GQA_HEADS // 2):
            cols = slice(p * LANES, (p + 1) * LANES)
            gq_ref[0, r, cols] = _rotate(qn[:, cols], cos, sin, GQA_DIM // 2).astype(BF16)
        kr = _rotate(ks[i] * lax.rsqrt(k_ms[i] + NORM_EPS) * kg_ref[...], cos, sin, GQA_DIM // 2)
        k_sw = pltpu.roll(kr, HEAD_W, 1)
        gk_ref[0, r, 0:LANES] = jnp.where(low, kr, k_sw).astype(BF16)
        gk_ref[0, r, LANES:2 * LANES] = jnp.where(low, k_sw, kr).astype(BF16)
        v = vs[i]
        v_sw = pltpu.roll(v, HEAD_W, 1)
        slots = (jnp.where(low, v, 1.0), jnp.where(low, 1.0, v_sw),
                 jnp.where(low, v_sw, 1.0), jnp.where(low, 1.0, v))
        for j, s in enumerate(slots):
            gv_ref[0, r, j * LANES:(j + 1) * LANES] = s.astype(BF16)


def _seg_ones(width):
    head = np.arange(width) // HEAD_W
    return jnp.asarray((head[:, None] == head[None, :]).astype(np.float32), BF16)


def _in_odd_call(x, g, shift, scale, w, q_gain, k_gain, cos, sin, tm, name):
    b, t, d = x.shape
    assert GQA_KV_W == LANES
    ones_q = _seg_ones(GQA_HEADS * GQA_DIM)
    ones_k = _seg_ones(GQA_KV_W)
    const = lambda a: pl.BlockSpec(a.shape, lambda i, j: (0,) * a.ndim)
    (sh_arr, sh_spec), (sc_arr, sc_spec) = _mod_operand(shift, d), _mod_operand(scale, d)
    tab = pl.BlockSpec((tm, LANES), lambda i, j: (j, 0))
    widths = (NA_W, NA_W, 2 * NA_W, GQA_HEADS * GQA_DIM, 2 * LANES, 4 * LANES)
    g2 = g.reshape(1, d).astype(F32)
    return pl.pallas_call(
        _in_odd_kernel,
        grid=(b, t // tm),
        in_specs=[pl.BlockSpec((1, tm, d), lambda i, j: (i, j, 0)), const(g2), sh_spec, sc_spec, const(w),
                  const(q_gain), const(k_gain), const(ones_q), const(ones_k), tab, tab],
        out_specs=[pl.BlockSpec((1, tm, n), lambda i, j: (i, j, 0)) for n in widths],
        out_shape=[jax.ShapeDtypeStruct((b, t, n), BF16) for n in widths],
        compiler_params=_params(2),
        name=name,
    )(x, g2, sh_arr, sc_arr, w, q_gain, k_gain, ones_q, ones_k, cos, sin)


def _attn_kernel(*refs, dc, chunks, n_src, masked_q):
    q_ref, kv_refs, o_ref = refs[0], refs[1:1 + 2 * n_src], refs[-1]
    tq = q_ref.shape[1]
    lane = lax.broadcasted_iota(jnp.int32, (tq, LANES), 1)
    qs = []
    for hh in range(2):
        if masked_q:
            qp = q_ref[0]
            qs.append(jnp.where((lane < HEAD_W) if hh == 0 else (lane >= HEAD_W), qp, jnp.zeros_like(qp)))
        else:
            qs.append(q_ref[0, :, hh * dc:(hh + 1) * dc])
    ms, accs = [None, None], [None, None]

    def scores(hh, src, s0, n):
        return lax.dot_general(qs[hh], kv_refs[2 * src][0, s0:s0 + n, :], NT_DIMS, preferred_element_type=F32)

    def consume(s, hh, src, s0, n):
        vv = kv_refs[2 * src + 1][0, s0:s0 + n, hh * LANES:(hh + 1) * LANES]
        mc = jnp.max(s, axis=-1, keepdims=True)
        m_new = mc if ms[hh] is None else jnp.maximum(ms[hh], mc)
        pv = _dotf(jnp.exp2(s - m_new).astype(BF16), vv)
        accs[hh] = pv if ms[hh] is None else jnp.exp2(ms[hh] - m_new) * accs[hh] + pv
        ms[hh] = m_new

    items = [(hh,) + tuple(c) for c in chunks for hh in range(2)]
    pending = []
    for it in items:
        pending.append((scores(*it), it))
        if len(pending) > ATTN_LOOKAHEAD:
            s, it0 = pending.pop(0)
            consume(s, *it0)
    for s, it0 in pending:
        consume(s, *it0)
    low = lane < HEAD_W
    num = jnp.where(low, accs[0], accs[1])
    den = pltpu.roll(jnp.where(low, accs[1], accs[0]), HEAD_W, 1)
    o_ref[0] = (num / den).astype(o_ref.dtype)


def _attn_call(q, kvs, dc, n_pairs, n_groups, chunks, tq, masked_q, name):
    b, t_q, _ = q.shape
    per = n_pairs // n_groups
    q_w = LANES if masked_q else 2 * dc
    in_specs = [pl.BlockSpec((1, tq, q_w), lambda i, p, j: (i, j, p))]
    args = [q]
    for k, v in kvs:
        in_specs += [pl.BlockSpec((1, k.shape[1], dc), lambda i, p, j: (i, 0, p // per)),
                     pl.BlockSpec((1, v.shape[1], 2 * LANES), lambda i, p, j: (i, 0, p // per))]
        args += [k, v]
    return pl.pallas_call(
        functools.partial(_attn_kernel, dc=dc, chunks=tuple(chunks), n_src=len(kvs), masked_q=masked_q),
        grid=(b, n_pairs, t_q // tq),
        in_specs=in_specs,
        out_specs=pl.BlockSpec((1, tq, LANES), lambda i, p, j: (i, j, p)),
        out_shape=jax.ShapeDtypeStruct((b, t_q, n_pairs * LANES), BF16),
        compiler_params=_params(3),
        name=name,
    )(*args)


NA_QROWS = 4
NA_KROWS = 12
NA_LOOKAHEAD = 1


def _na_kernel(q_ref, k_ref, v_ref, kc_ref, vc_ref, bias_ref, o_ref):
    rb = pl.program_id(1)
    n_rows = k_ref.shape[1] // GRID_W
    base = jnp.clip(rb * NA_QROWS - NA_WIN_H // 2, 0, n_rows - NA_KROWS) * GRID_W
    base = pl.multiple_of(base, GRID_W)
    nq = NA_QROWS * GRID_W
    nk = NA_KROWS * GRID_W
    lane = lax.broadcasted_iota(jnp.int32, (nq, LANES), 1)
    low = lane < HEAD_W

    def scores(p):
        cols = slice(p * LANES, (p + 1) * LANES)
        qp = q_ref[0, :, cols]
        zero = jnp.zeros_like(qp)
        q2 = jnp.concatenate([jnp.where(low, qp, zero), jnp.where(low, zero, qp)], axis=0)
        s_loc = lax.dot_general(q2, k_ref[0, pl.ds(base, nk), cols], NT_DIMS, preferred_element_type=F32)
        s_ctx = lax.dot_general(q2, kc_ref[0, :, cols], NT_DIMS, preferred_element_type=F32)
        return s_loc, s_ctx

    def consume(p, s_loc, s_ctx):
        vcols = slice(2 * p * LANES, (2 * p + 2) * LANES)
        s_loc = s_loc + bias_ref[0, p]
        m = jnp.maximum(jnp.max(s_loc, axis=-1, keepdims=True), jnp.max(s_ctx, axis=-1, keepdims=True))
        acc = (_dotf(jnp.exp2(s_loc - m).astype(BF16), v_ref[0, pl.ds(base, nk), vcols])
               + _dotf(jnp.exp2(s_ctx - m).astype(BF16), vc_ref[0, :, vcols]))
        even, odd = acc[0:nq], acc[nq:2 * nq]
        num = jnp.where(low, even[:, 0:LANES], odd[:, 0:LANES])
        den = jnp.where(low, even[:, LANES:2 * LANES], odd[:, LANES:2 * LANES])
        o_ref[0, :, p * LANES:(p + 1) * LANES] = (num / den).astype(o_ref.dtype)

    pending = []
    for p in range(NA_HEADS // 2):
        pending.append((p,) + scores(p))
        if len(pending) > NA_LOOKAHEAD:
            consume(*pending.pop(0))
    for item in pending:
        consume(*item)


def _na_bias_table(rpb, rows):
    n_blocks = rows // NA_QROWS
    n_h = rpb.shape[0]
    c = np.arange(GRID_W)[:, None]
    kc = np.arange(GRID_W)[None, :]
    cs = np.clip(c - NA_WIN_W // 2, 0, GRID_W - NA_WIN_W)
    col_ok = (kc >= cs) & (kc < cs + NA_WIN_W)
    dc = np.clip(kc - c + (NA_WIN_W - 1), 0, 2 * NA_WIN_W - 2)
    pick = (np.arange(2 * NA_WIN_W - 1)[:, None, None] == dc[None]).astype(np.float32)
    by_col = jnp.einsum("hdm,mck->hcdk", rpb.astype(F32) * LOG2E, jnp.asarray(pick),
                        precision=lax.Precision.HIGHEST)
    variants = []
    for r0 in (0, NA_QROWS * (n_blocks // 2), rows - NA_QROWS):
        base = int(np.clip(r0 - NA_WIN_H // 2, 0, rows - NA_KROWS))
        r = r0 + np.arange(NA_QROWS)[:, None]
        kr = base + np.arange(NA_KROWS)[None, :]
        rs = np.clip(r - NA_WIN_H // 2, 0, rows - NA_WIN_H)
        row_ok = (kr >= rs) & (kr < rs + NA_WIN_H)
        dr = np.clip(kr - r + (NA_WIN_H - 1), 0, 2 * NA_WIN_H - 2)
        full = jnp.stack([by_col[:, :, dr[i], :] for i in range(NA_QROWS)], 1)
        valid = row_ok[:, None, :, None] & col_ok[None, :, None, :]
        full = jnp.where(valid[None], full, MASK_VALUE)
        variants.append(full.reshape(n_h, NA_QROWS * GRID_W, NA_KROWS * GRID_W))
    return jnp.stack(variants, 0).reshape(3, n_h // 2, 2 * NA_QROWS * GRID_W, NA_KROWS * GRID_W)


def _na_call(q, k, v, kc, vc, bias):
    b, t, w = q.shape
    n_blocks = t // (NA_QROWS * GRID_W)
    nq = NA_QROWS * GRID_W
    tc = kc.shape[1]

    def bias_map(i, r):
        return (jnp.where(r == 0, 0, jnp.where(r == n_blocks - 1, 2, 1)), 0, 0, 0)

    return pl.pallas_call(
        _na_kernel,
        grid=(b, n_blocks),
        in_specs=[pl.BlockSpec((1, nq, w), lambda i, r: (i, r, 0)),
                  pl.BlockSpec((1, t, w), lambda i, r: (i, 0, 0)),
                  pl.BlockSpec((1, t, 2 * w), lambda i, r: (i, 0, 0)),
                  pl.BlockSpec((1, tc, w), lambda i, r: (i, 0, 0)),
                  pl.BlockSpec((1, tc, 2 * w), lambda i, r: (i, 0, 0)),
                  pl.BlockSpec((1, NA_HEADS // 2, 2 * nq, NA_KROWS * GRID_W), bias_map)],
        out_specs=pl.BlockSpec((1, nq, w), lambda i, r: (i, r, 0)),
        out_shape=jax.ShapeDtypeStruct((b, t, w), BF16),
        compiler_params=_params(2),
        name="na_attn",
    )(q, k, v, kc, vc, bias)


GDN_MAX_BLOCK = 16 * GDN_CHUNK
GDN_GROUP_CHUNKS = 4
GDN_PAIRS = GDN_HEADS // 2
TN_DIMS = (((0,), (0,)), ((), ()))


def _split3(x):
    hi = x.astype(BF16)
    r = x - hi.astype(F32)
    mid = r.astype(BF16)
    lo = (r - mid.astype(F32)).astype(BF16)
    return hi, mid, lo


def _bd(x):
    lane = lax.broadcasted_iota(jnp.int32, x.shape, 1)
    z = jnp.zeros_like(x)
    return jnp.concatenate([jnp.where(lane < HEAD_W, x, z), jnp.where(lane >= HEAD_W, x, z)], axis=0)


def _dotb(a, b):
    return jnp.dot(a.astype(BF16), b.astype(BF16), preferred_element_type=F32)


def _mm_pair(x, y):
    return _dotb(x, _bd(y))


def _softplus(x):
    return jnp.maximum(x, 0.0) + jnp.log(1.0 + jnp.exp(-jnp.abs(x)))


def _gdn_kernel(qkv_ref, misc_ref, alog_ref, dtb_ref, eg_ref, eb_ref, tri3_ref, s0_ref, o_ref, sfin_ref, s_ref, *,
                rev, n_blk):
    step = pl.program_id(1)
    c_len = GDN_CHUNK
    hw = GDN_HEADS * HEAD_W

    @pl.when(step == 0)
    def _():
        s_ref[...] = s0_ref[0]

    qn = qkv_ref[0, :, 0:hw].astype(F32)
    kn = qkv_ref[0, :, hw:2 * hw].astype(F32)
    v = qkv_ref[0, :, 2 * hw:3 * hw].astype(F32)

    misc = misc_ref[0]
    g_all = -jnp.exp(alog_ref[...]) * _softplus(misc + dtb_ref[...])
    b_all = jax.nn.sigmoid(misc)
    gx = jnp.dot(jnp.concatenate(_split3(g_all), axis=1), eg_ref[...], preferred_element_type=F32)
    bx = jnp.dot(jnp.concatenate(_split3(b_all), axis=1), eb_ref[...], preferred_element_type=F32)

    row = lax.broadcasted_iota(jnp.int32, (c_len, LANES), 0)
    colp = lax.broadcasted_iota(jnp.int32, (c_len, LANES), 1) & (HEAD_W - 1)
    tri = (row <= colp) if rev else (row >= colp)
    strict = (row < colp) if rev else (row > colp)
    eye_f = (row == colp).astype(F32)
    row_w = lax.broadcasted_iota(jnp.int32, (c_len, hw), 0)
    col_w = lax.broadcasted_iota(jnp.int32, (c_len, hw), 1) & (HEAD_W - 1)
    eye_w = row_w == col_w
    bd_row = lax.broadcasted_iota(jnp.int32, (LANES, LANES), 0)
    bd_col = lax.broadcasted_iota(jnp.int32, (LANES, LANES), 1)
    bd_mask = (bd_row < HEAD_W) == (bd_col < HEAD_W)

    n_chunks = qkv_ref.shape[1] // c_len
    pairs = range(GDN_PAIRS)
    sl = [slice(p * LANES, (p + 1) * LANES) for p in pairs]
    rows_of = lambda c: slice(c * c_len, (c + 1) * c_len)
    gcs, kbs, aqs, grs, gtot, vb, kbg, qd, kd, ms, qks, xs, inner, uws = ({} for _ in range(14))
    state = [s_ref[p] for p in pairs]

    def prep_stages(chunks):
        keys = [(c, p) for c in chunks for p in pairs]

        def decay_sums():
            for c in chunks:
                gcs[c] = jnp.dot(tri3_ref[...], jnp.concatenate(_split3(gx[rows_of(c)]), axis=0),
                                 preferred_element_type=F32)
                kbs[c] = kn[rows_of(c)] * bx[rows_of(c)]

        def gram():
            for c, p in keys:
                r = rows_of(c)
                lhs = jnp.concatenate([kbs[c][:, sl[p]], qn[r][:, sl[p]]], axis=0)
                aqs[c, p] = lax.dot_general(lhs.astype(BF16), _bd(kn[r][:, sl[p]]).astype(BF16), NT_DIMS,
                                            preferred_element_type=F32)

        def decay_rows():
            for c in chunks:
                grs[c] = jnp.sum(jnp.where(eye_w, gcs[c], 0.0), axis=0, keepdims=True)

        def masks():
            for c in chunks:
                r, gc = rows_of(c), gcs[c]
                gtot[c] = gc[0:1] if rev else gc[c_len - 1:c_len]
                egc = jnp.exp(gc)
                vb[c] = v[r] * bx[r]
                kbg[c] = kbs[c] * egc
                qd[c] = qn[r] * egc
                kd[c] = kn[r] * jnp.exp(gtot[c] - gc)
                for p in pairs:
                    diff = gc[:, sl[p]] - grs[c][:, sl[p]]
                    decay = jnp.where(tri, jnp.exp(jnp.where(tri, diff, 0.0)), 0.0)
                    ms[c, p] = jnp.where(strict, aqs[c, p][0:c_len] * decay, 0.0)
                    qks[c, p] = aqs[c, p][c_len:2 * c_len] * decay

        def join_mask(lvl):
            rblk, cblk = row >> lvl, colp >> lvl
            sib = (cblk == rblk + 1) if rev else (cblk == rblk - 1)
            return sib & ((rblk & 1) == (0 if rev else 1))

        def level0():
            for key in keys:
                xs[key] = eye_f - jnp.where(join_mask(0), ms[key], 0.0)

        def first_product(lvl):
            def run():
                for key in keys:
                    inner[key] = _mm_pair(jnp.where(join_mask(lvl), ms[key], 0.0), xs[key])
            return run

        def second_product():
            for key in keys:
                xs[key] = xs[key] - _mm_pair(xs[key], inner[key])

        def apply_inverse():
            for c, p in keys:
                rhs = jnp.concatenate([_bd(vb[c][:, sl[p]]), _bd(kbg[c][:, sl[p]])], axis=1)
                uws[c, p] = _dotb(xs[c, p], rhs)

        stages = [decay_sums, gram, decay_rows, masks, level0]
        for lvl in range(1, 6):
            stages += [first_product(lvl), second_product]
        return stages + [apply_inverse]

    def scan_stages(chunks):
        v_new = {}
        wq = {}

        def read_state(c):
            def run():
                for p in pairs:
                    wq[p] = _dotb(jnp.concatenate([uws[c, p][:, LANES:2 * LANES], qd[c][:, sl[p]]], axis=0), state[p])
                    v_new[p] = uws[c, p][:, 0:LANES] - wq[p][0:c_len]
            return run

        def write_state(c):
            def run():
                upd = [lax.dot_general(kd[c][:, sl[p]].astype(BF16), v_new[p].astype(BF16), TN_DIMS,
                                       preferred_element_type=F32) for p in pairs]
                for p in pairs:
                    o_ref[0, rows_of(c), sl[p]] = wq[p][c_len:2 * c_len] + _dotb(qks[c, p], _bd(v_new[p]))
                    state[p] = state[p] * jnp.exp(gtot[c][:, sl[p]]) + jnp.where(bd_mask, upd[p], 0.0)
            return run

        stages = []
        for c in chunks:
            stages += [read_state(c), write_state(c)]
        return stages

    order = list(range(n_chunks - 1, -1, -1) if rev else range(n_chunks))
    halves = [order[i:i + GDN_GROUP_CHUNKS] for i in range(0, n_chunks, GDN_GROUP_CHUNKS)]
    for stage in prep_stages(halves[0]):
        stage()
    for half, nxt in zip(halves, halves[1:] + [None]):
        scan = scan_stages(half)
        prep = prep_stages(nxt) if nxt else []
        done = 0
        for i, stage in enumerate(prep):
            stage()
            while done < len(scan) and done * len(prep) < (i + 1) * len(scan):
                scan[done]()
                done += 1
        for stage in scan[done:]:
            stage()
    for p in range(GDN_PAIRS):
        s_ref[p] = state[p]

    @pl.when(step == n_blk - 1)
    def _():
        for p in range(GDN_PAIRS):
            sfin_ref[0, p] = state[p]


def _gdn_consts(a_log, dt_bias, d, rev):
    pad = jnp.zeros((LANES - 2 * GDN_HEADS,), F32)
    alog = jnp.concatenate([a_log.reshape(-1).astype(F32), pad]).reshape(1, LANES)
    dtb = jnp.concatenate([dt_bias.reshape(-1).astype(F32), pad]).reshape(1, LANES)
    hw = GDN_HEADS * HEAD_W
    head_of_lane = np.arange(hw) // HEAD_W
    src = np.arange(LANES)[:, None]
    eg = (src == d * GDN_HEADS + head_of_lane[None, :]).astype(np.float32)
    eb = (src == 2 * GDN_HEADS + d * GDN_HEADS + head_of_lane[None, :]).astype(np.float32)
    i = np.arange(GDN_CHUNK)
    tri = (i[:, None] <= i[None, :]) if rev else (i[:, None] >= i[None, :])
    as_bf = lambda a, reps, ax: jnp.asarray(np.concatenate([a] * reps, axis=ax), BF16)
    return alog, dtb, as_bf(eg, 3, 0), as_bf(eb, 3, 0), as_bf(tri.astype(np.float32), 3, 1)


def _gdn_call(qkv, misc, consts, s0, rev, name):
    b, t, w = qkv.shape
    blk = min(t, GDN_MAX_BLOCK)
    assert t % blk == 0 and blk % GDN_CHUNK == 0
    n_blk = t // blk
    hw = GDN_HEADS * HEAD_W
    blk_of = (lambda s: n_blk - 1 - s) if rev else (lambda s: s)
    const = lambda a: pl.BlockSpec(a.shape, lambda i, s: (0,) * a.ndim)
    s_spec = pl.BlockSpec((1, GDN_PAIRS, LANES, LANES), lambda i, s: (i, 0, 0, 0))
    return pl.pallas_call(
        functools.partial(_gdn_kernel, rev=rev, n_blk=n_blk),
        grid=(b, n_blk),
        in_specs=[pl.BlockSpec((1, blk, w), lambda i, s: (i, blk_of(s), 0)),
                  pl.BlockSpec((1, blk, LANES), lambda i, s: (i, blk_of(s), 0))]
                 + [const(a) for a in consts] + [s_spec],
        out_specs=[pl.BlockSpec((1, blk, hw), lambda i, s: (i, blk_of(s), 0)), s_spec],
        out_shape=[jax.ShapeDtypeStruct((b, t, hw), F32),
                   jax.ShapeDtypeStruct((b, GDN_PAIRS, LANES, LANES), F32)],
        scratch_shapes=[pltpu.VMEM((GDN_PAIRS, LANES, LANES), F32)],
        compiler_params=_params(2),
        name=name,
    )(qkv, misc, *consts, s0)


MIX_SUB_ROWS = 256
FFN_CHUNKS = 2


def _mix_ffn_kernel(*refs, ff_chunks, gdn_inputs, n_sub):
    tm = refs[0].shape[1]
    rows = [slice(i * (tm // n_sub), (i + 1) * (tm // n_sub)) for i in range(n_sub)]
    if gdn_inputs:
        (h_ref, of_ref, ob_ref, gate_ref, att_ref, on_ref, ones_ref, gtm_ref, shf_ref, scf_ref, gtf_ref, gpm_ref,
         gpf_ref, gqf_ref, wo_ref, wg_ref, wu_ref, wd_ref, o_ref) = refs
        mixes = []
        for r in rows:
            o = of_ref[0, r, :] + ob_ref[0, r, :]
            gate = gate_ref[0, r, :].astype(F32)
            a = o * lax.rsqrt(_seg_mean_sq(o, ones_ref) + NORM_EPS) * on_ref[...] * (gate * jax.nn.sigmoid(gate))
            mixes.append(jnp.concatenate([a.astype(BF16), att_ref[0, r, :]], axis=1))
    else:
        (h_ref, a_ref, b_ref, gtm_ref, shf_ref, scf_ref, gtf_ref, gpm_ref, gpf_ref, gqf_ref, wo_ref, wg_ref,
         wu_ref, wd_ref, o_ref) = refs
        mixes = [jnp.concatenate([a_ref[0, r, :], b_ref[0, r, :]], axis=1) for r in rows]
    ys = [_dotf(mix, wo_ref[...]) for mix in mixes]
    h1s = [h_ref[0, r, :] + gtm_ref[0] * (_rms(y) * gpm_ref[...]) for r, y in zip(rows, ys)]
    us = [(_rms(h1) * gpf_ref[...] * (1.0 + scf_ref[0]) + shf_ref[0]).astype(BF16) for h1 in h1s]
    acts = []
    for o, n in ff_chunks:
        for u in us:
            gg = _dotf(u, wg_ref[:, o:o + n])
            uu = _dotf(u, wu_ref[:, o:o + n])
            acts.append((gg * jax.nn.sigmoid(gg) * uu).astype(BF16))
    fs = [None] * n_sub
    for ci, (o, n) in enumerate(ff_chunks):
        for i in range(n_sub):
            part = _dotf(acts[ci * n_sub + i], wd_ref[o:o + n, :])
            fs[i] = part if fs[i] is None else fs[i] + part
    for r, h1, f in zip(rows, h1s, fs):
        o_ref[0, r, :] = h1 + gtf_ref[0] * (_rms(f) * gqf_ref[...])


def _mix_ffn_call(h, mix_inputs, out_norm, gt_m, sh_f, sc_f, gt_f, g_post_mix, g_pre_ffn, g_post_ffn, wo, wg, wu,
                  wd, tm, name):
    b, t, d = h.shape
    dm = wo.shape[0]
    ff = wg.shape[1]
    gdn_inputs = out_norm is not None
    step = pl.cdiv(pl.cdiv(ff, FFN_CHUNKS), MXU_TILE) * MXU_TILE
    assert ff % LANES == 0
    ff_chunks = tuple((o, min(step, ff - o)) for o in range(0, ff, step))
    tok = lambda n: pl.BlockSpec((1, tm, n), lambda i, j: (i, j, 0))
    mods = [_mod_operand(m, d) for m in (gt_m, sh_f, sc_f, gt_f)]
    vec = pl.BlockSpec((1, d), lambda i, j: (0, 0))
    const = lambda shape: pl.BlockSpec(shape, lambda i, j: (0, 0), pipeline_mode=pl.Buffered(1))
    r2 = lambda a: a.reshape(1, d).astype(F32)
    mix_specs = [tok(m.shape[-1]) for m in mix_inputs]
    mix_args = list(mix_inputs)
    if gdn_inputs:
        hw = GDN_HEADS * GDN_DV
        on = jnp.tile(out_norm.astype(F32), GDN_HEADS).reshape(1, hw)
        ones = _seg_ones(hw)
        mix_specs += [pl.BlockSpec(on.shape, lambda i, j: (0, 0)), pl.BlockSpec(ones.shape, lambda i, j: (0, 0))]
        mix_args += [on, ones]
    return pl.pallas_call(
        functools.partial(_mix_ffn_kernel, ff_chunks=ff_chunks, gdn_inputs=gdn_inputs,
                          n_sub=tm // MIX_SUB_ROWS),
        grid=(b, t // tm),
        in_specs=[tok(d)] + mix_specs + [spec for _, spec in mods]
                 + [vec, vec, vec, const((dm, d)), const((d, ff)), const((d, ff)), const((ff, d))],
        out_specs=tok(d),
        out_shape=jax.ShapeDtypeStruct((b, t, d), F32),
        compiler_params=_params(2),
        name=name,
    )(h, *mix_args, *[arr for arr, _ in mods], r2(g_post_mix), r2(g_pre_ffn), r2(g_post_ffn), wo, wg, wu, wd)


def _rope_tables(n_tokens, rot_dim):
    t = jnp.arange(n_tokens, dtype=jnp.int32)
    row = (t // GRID_W).astype(F32)
    col = (t % GRID_W).astype(F32)
    n_freq = rot_dim // 4
    inv_freq = ROPE_THETA ** (-jnp.arange(n_freq, dtype=F32) / n_freq)
    ang = jnp.concatenate([row[:, None] * inv_freq, col[:, None] * inv_freq], -1)
    cos = jnp.concatenate([jnp.cos(ang), jnp.cos(ang)], -1)
    sin = jnp.concatenate([-jnp.sin(ang), jnp.sin(ang)], -1)
    if rot_dim == HEAD_W:
        return jnp.tile(cos, (1, LANES // rot_dim)), jnp.tile(sin, (1, LANES // rot_dim))
    pad = LANES - rot_dim
    return (jnp.concatenate([cos, jnp.ones((n_tokens, pad), F32)], -1),
            jnp.concatenate([sin, jnp.zeros((n_tokens, pad), F32)], -1))


def _no_rope_tables(n_tokens):
    return jnp.ones((n_tokens, LANES), F32), jnp.zeros((n_tokens, LANES), F32)


def _gdn_bidirectional(lat_qkv, lat_misc, ctx_qkv, ctx_misc, a_log, dt_bias):
    bsz = lat_qkv.shape[0]
    s_zero = jnp.zeros((bsz, GDN_PAIRS, LANES, LANES), F32)
    lat, ctx = [], []
    for d, rev in ((0, False), (1, True)):
        consts = _gdn_consts(a_log, dt_bias, d, rev)
        o_c, s_c = _gdn_call(ctx_qkv, ctx_misc, consts, s_zero, rev, "gdn_ctx_%d" % d)
        o_l, _ = _gdn_call(lat_qkv, lat_misc, consts, s_c, rev, "gdn_lat_%d" % d)
        lat.append(o_l)
        ctx.append(o_c)
    return lat, ctx


LOG2E = 1.4426950408889634
ATTN_TQ = 1024
ATTN_LOOKAHEAD = 2


def _attn_chunks(l_len, s_len, size):
    return [(0, 0, l_len)] + [(1, o, size) for o in range(0, s_len, size)]


def _even_mixer(h, hc, mods, mods_c, g_pre, w_in, conv_w, a_log, dt_bias, q_norm, w_q_up, kv_norm, w_kv_up):
    s_len, l_len = h.shape[1], hc.shape[1]
    weights = _even_weights(w_in, q_norm, w_q_up, kv_norm, w_kv_up)
    qkv_l, gate_l, misc_l, q_l, k_l, v_l = _in_even_call(h, g_pre, *mods, weights, conv_w,
                                                         *_rope_tables(s_len, MLA_ROPE), 512, "in_even_lat")
    qkv_c, gate_c, misc_c, q_c, k_c, v_c = _in_even_call(hc, g_pre, *mods_c, weights, conv_w,
                                                         *_no_rope_tables(l_len), l_len, "in_even_ctx")
    o_lat, o_ctx = _gdn_bidirectional(qkv_l, misc_l, qkv_c, misc_c, a_log, dt_bias)
    b_lat = _attn_call(q_l, [(k_c, v_c), (k_l, v_l)], MLA_QC, MLA_PAIRS, MLA_PAIRS, _attn_chunks(l_len, s_len, 256),
                       2 * ATTN_TQ, False, "mla_attn_lat")
    b_ctx = _attn_call(q_c, [(k_c, v_c)], MLA_QC, MLA_PAIRS, MLA_PAIRS, [(0, 0, l_len)], l_len, False,
                       "mla_attn_ctx")
    return (o_lat[0], o_lat[1], gate_l, b_lat), (o_ctx[0], o_ctx[1], gate_c, b_ctx)


def _odd_mixer_last(h, hc, mods, mods_c, g_pre, w_in, rpb, q_norm, k_norm):
    s_len, l_len = h.shape[1], hc.shape[1]
    w = w_in.astype(BF16)
    q_gain = (jnp.tile(q_norm.astype(F32), GQA_HEADS) * (GQA_DIM ** -0.5 * LOG2E)).reshape(1, -1)
    k_gain = jnp.tile(k_norm.astype(F32), GQA_KV_HEADS).reshape(1, -1)
    naq_l, nak_l, nav_l, gq_l, gk_l, gv_l = _in_odd_call(h, g_pre, *mods, w, q_gain, k_gain,
                                                         *_rope_tables(s_len, GQA_DIM), 512, "in_odd_lat")
    _, nak_c, nav_c, _, gk_c, gv_c = _in_odd_call(hc, g_pre, *mods_c, w, q_gain, k_gain, *_no_rope_tables(l_len),
                                                  l_len, "in_odd_ctx")
    c_lat = _na_call(naq_l, nak_l, nav_l, nak_c, nav_c, _na_bias_table(rpb, s_len // GRID_W))
    d_lat = _attn_call(gq_l, [(gk_c, gv_c), (gk_l, gv_l)], LANES, GQA_HEADS // 2, GQA_KV_HEADS,
                       _attn_chunks(l_len, s_len, 256), 2 * ATTN_TQ, True, "gqa_attn")
    return c_lat, d_lat


def kernel(x, c, ctx, c_ctx, w_mod, b_mod, g_pre_mix, g_post_mix, g_pre_ffn, g_post_ffn, w_ffn_gate, w_ffn_up,
           w_ffn_down, w_in_even, w_out_even, gdn_conv, gdn_a_log, gdn_dt_bias, gdn_out_norm, mla_q_norm,
           mla_w_q_up, mla_kv_norm, mla_w_kv_up, w_in_odd, w_out_odd, na_rpb, gqa_q_norm, gqa_k_norm):
    bsz, s_len, d = x.shape
    depth = w_mod.shape[0]
    assert depth == 2, "layer 0 = even mixer with context update, layer 1 = odd mixer (last)"
    c_rows = jnp.concatenate([c, c_ctx[None, :], jnp.zeros((16 - bsz - 1, d), F32)], 0)
    h, hc = x, ctx
    for i in range(depth):
        mod_tab = _mod_call(c_rows, w_mod[i].astype(BF16), b_mod[i]).reshape(c_rows.shape[0] * N_MOD, 1, d)
        sh_m, sc_m, gt_m, sh_f, sc_f, gt_f = (_Mod(mod_tab, k, None) for k in range(N_MOD))
        csh_m, csc_m, cgt_m, csh_f, csc_f, cgt_f = (_Mod(mod_tab, k, bsz) for k in range(N_MOD))
        wg, wu, wd = w_ffn_gate[i].astype(BF16), w_ffn_up[i].astype(BF16), w_ffn_down[i].astype(BF16)
        if i == 0:
            mix, mix_c = _even_mixer(h, hc, (sh_m, sc_m), (csh_m, csc_m), g_pre_mix[i], w_in_even[0], gdn_conv[0],
                                     gdn_a_log[0], gdn_dt_bias[0], mla_q_norm[0], mla_w_q_up[0], mla_kv_norm[0],
                                     mla_w_kv_up[0])
            wo, out_norm = w_out_even[0].astype(BF16), gdn_out_norm[0]
            hc = _mix_ffn_call(hc, mix_c, out_norm, cgt_m, csh_f, csc_f, cgt_f, g_post_mix[i], g_pre_ffn[i],
                               g_post_ffn[i], wo, wg, wu, wd, 256, "mix_ffn_ctx")
        else:
            mix = _odd_mixer_last(h, hc, (sh_m, sc_m), (csh_m, csc_m), g_pre_mix[i], w_in_odd[0], na_rpb[0],
                                  gqa_q_norm[0], gqa_k_norm[0])
            wo, out_norm = w_out_odd[0].astype(BF16), None
        h = _mix_ffn_call(h, mix, out_norm, gt_m, sh_f, sc_f, gt_f, g_post_mix[i], g_pre_ffn[i], g_post_ffn[i],
                          wo, wg, wu, wd, 2 * MIX_SUB_ROWS, "mix_ffn_lat%d" % i)
    return h
```
